```python
import math
import jax
import jax.numpy as jnp
from jax import lax
import numpy as np

D_MODEL = 1024
BATCH = 4
SEQ = 8192
DEPTH = 1

GRID_W = 64
CTX_LEN = 256
N_MOD = 6
MLSTM_HEADS = 4
MLSTM_DQK = 128
MLSTM_DV = 256
MLSTM_CHUNK = 64
CONV_K = 5
DIFF_HEADS = 8
DIFF_HEAD_DIM = 64
Q_BLOCK = 128
ROPE_BASE = 10000.0
ROPE_AXIS_DIM = DIFF_HEAD_DIM // 2
ROPE_FREQS = ROPE_AXIS_DIM // 2
FFN_HIDDEN = -(-8 * D_MODEL // (3 * 256)) * 256

M_QK_W = MLSTM_HEADS * MLSTM_DQK
M_V_W = MLSTM_HEADS * MLSTM_DV
M_GATES = 4 * MLSTM_HEADS
DA_W = DIFF_HEADS * 2 * DIFF_HEAD_DIM
IN_SIZES = (M_QK_W, M_QK_W, M_V_W, M_V_W, M_GATES, DA_W, DA_W, DA_W, D_MODEL, D_MODEL)
IN_WIDTH = sum(IN_SIZES)
IN_SPLIT_POINTS = tuple(int(s) for s in np.cumsum(IN_SIZES)[:-1])

kernel_name = 'hybrid_mlstm_diffattn_dit_block'


def rms_norm(x, g, eps=1e-6):
    xf = x.astype(jnp.float32)
    y = xf * lax.rsqrt(jnp.mean(xf * xf, axis=-1, keepdims=True) + eps)
    return (y * g.astype(jnp.float32)).astype(x.dtype)


def modulated_norm(x, g, shift, scale):
    return rms_norm(x, g) * (1 + scale) + shift


def centred_dwconv(x, w, b):
    y = lax.conv_general_dilated(x, w[:, None, :], (1,), [(CONV_K // 2, CONV_K // 2)],
                                 dimension_numbers=('NWC', 'WIO', 'NWC'),
                                 feature_group_count=x.shape[-1])
    return y + b


def to_heads(a, n_heads):
    B, T, _ = a.shape
    return a.reshape(B, T, n_heads, -1).transpose(0, 2, 1, 3)


def flip_t(a):
    return jnp.flip(a, axis=2)


def axial_rope_tables(n_tokens):
    rows = n_tokens // GRID_W
    row = jnp.repeat(jnp.arange(rows, dtype=jnp.float32), GRID_W)
    col = jnp.tile(jnp.arange(GRID_W, dtype=jnp.float32), rows)
    inv = ROPE_BASE ** (-jnp.arange(ROPE_FREQS, dtype=jnp.float32) / ROPE_FREQS)
    ang = jnp.concatenate([row[:, None] * inv, col[:, None] * inv], axis=-1)
    return jnp.cos(ang), jnp.sin(ang)


def rotate_pairs(xh, cos, sin):
    x1, x2 = xh[..., :ROPE_FREQS], xh[..., ROPE_FREQS:]
    return jnp.concatenate([x1 * cos - x2 * sin, x2 * cos + x1 * sin], axis=-1)


def apply_axial_rope(x, cos, sin):
    xf = x.astype(jnp.float32)
    y = jnp.concatenate([
        rotate_pairs(xf[..., :ROPE_AXIS_DIM], cos[:, :ROPE_FREQS], sin[:, :ROPE_FREQS]),
        rotate_pairs(xf[..., ROPE_AXIS_DIM:], cos[:, ROPE_FREQS:], sin[:, ROPE_FREQS:])], axis=-1)
    return y.astype(x.dtype)


def mlstm_chunkwise(q, k, v, ig, lf, C0, n0, m0):
    B, H, T, DK = q.shape
    DV = v.shape[-1]
    L = MLSTM_CHUNK
    NC = T // L
    lower = jnp.tril(jnp.ones((L, L), dtype=bool))

    def chunks(a):
        return jnp.moveaxis(a.reshape(a.shape[:2] + (NC, L) + a.shape[3:]), 2, 0)

    def step(carry, inp):
        C, n, m = carry
        qc, kc, vc, igc, lfc = inp
        b = jnp.cumsum(lfc, axis=-1)
        logD = jnp.where(lower, b[..., :, None] - b[..., None, :] + igc[..., None, :], -jnp.inf)
        m_inter = b + m[..., None]
        m_j = jnp.maximum(m_inter, jnp.max(logD, axis=-1))
        dmat = jnp.exp(logD - m_j[..., None])
        inter = jnp.exp(m_inter - m_j)
        s = jnp.einsum('bhjd,bhsd->bhjs', qc, kc) * dmat
        num = jnp.einsum('bhjs,bhse->bhje', s, vc) + inter[..., None] * jnp.einsum('bhjd,bhde->bhje', qc, C)
        den = jnp.sum(s, axis=-1) + inter * jnp.einsum('bhjd,bhd->bhj', qc, n)
        h = num / jnp.maximum(jnp.abs(den), jnp.exp(-m_j))[..., None]
        bL = b[..., -1]
        logw = bL[..., None] - b + igc
        m_new = jnp.maximum(bL + m, jnp.max(logw, axis=-1))
        w = jnp.exp(logw - m_new[..., None])
        dec = jnp.exp(bL + m - m_new)
        C_new = dec[..., None, None] * C + jnp.einsum('bhs,bhsd,bhse->bhde', w, kc, vc)
        n_new = dec[..., None] * n + jnp.einsum('bhs,bhsd->bhd', w, kc)
        return (C_new, n_new, m_new), h

    state, h = lax.scan(step, (C0, n0, m0), (chunks(q), chunks(k), chunks(v), chunks(ig), chunks(lf)))
    h = jnp.moveaxis(h, 0, 2).reshape(B, H, T, DV)
    return h, state


def project_stream(xn, w_in, b_gate, conv_w, conv_b, q_norm, k_norm):
    f32 = jnp.float32
    p = xn @ w_in
    mq, mk, mv, mo, mg, dq, dk, dv, ga, gb = jnp.split(p, IN_SPLIT_POINTS, axis=-1)
    qk = jax.nn.silu(centred_dwconv(jnp.concatenate([mq, mk], axis=-1), conv_w, conv_b))
    mq = to_heads(qk[..., :M_QK_W], MLSTM_HEADS).astype(f32) * MLSTM_DQK ** -0.5
    mk = to_heads(qk[..., M_QK_W:], MLSTM_HEADS).astype(f32)
    mv = to_heads(mv, MLSTM_HEADS).astype(f32)
    g = (mg + b_gate).astype(f32)
    B, T, _ = g.shape
    g = g.reshape(B, T, 4, MLSTM_HEADS).transpose(2, 0, 3, 1)
    gates = (g[0], jax.nn.log_sigmoid(g[1]), g[2], jax.nn.log_sigmoid(g[3]))
    dq = to_heads(dq, DIFF_HEADS)
    dk = to_heads(dk, DIFF_HEADS)
    q1 = rms_norm(dq[..., :DIFF_HEAD_DIM], q_norm)
    q2 = rms_norm(dq[..., DIFF_HEAD_DIM:], q_norm)
    k1 = rms_norm(dk[..., :DIFF_HEAD_DIM], k_norm)
    k2 = rms_norm(dk[..., DIFF_HEAD_DIM:], k_norm)
    dv = to_heads(dv, DIFF_HEADS)
    return mq, mk, mv, mo, gates, q1, q2, k1, k2, dv, ga, gb


def mlstm_output(h, o, g):
    B, H, T, DV = h.shape
    hn = rms_norm(h.transpose(0, 2, 1, 3), g.reshape(H, DV)).reshape(B, T, H * DV)
    return jax.nn.sigmoid(o.astype(jnp.float32)) * hn


def diff_weights(q1, q2, k1, k2, lam):
    s1 = jnp.einsum('bhqd,bhkd->bhqk', q1, k1, preferred_element_type=jnp.float32)
    s2 = jnp.einsum('bhqd,bhkd->bhqk', q2, k2, preferred_element_type=jnp.float32)
    return jax.nn.softmax(s1, axis=-1) - lam * jax.nn.softmax(s2, axis=-1)


def diff_attention_blocks(q1, q2, k1, k2, v, lam):
    B, H, T, d = q1.shape
    nb = T // Q_BLOCK
    vf = v.astype(jnp.float32)

    def blocks(a):
        return jnp.moveaxis(a.reshape(B, H, nb, Q_BLOCK, d), 2, 0)

    def one_block(qs):
        w = diff_weights(qs[0], qs[1], k1, k2, lam)
        return jnp.einsum('bhqk,bhke->bhqe', w, vf)

    out = lax.map(one_block, (blocks(q1), blocks(q2)))
    return jnp.moveaxis(out, 0, 2).reshape(B, H, T, v.shape[-1])


def diff_output(o, g, lam_init):
    B, H, T, E = o.shape
    return (rms_norm(o, g) * (1 - lam_init)).transpose(0, 2, 1, 3).reshape(B, T, H * E)


def merge_branches(hA, hB, ga, gb, w_a, w_b, w_o):
    y = jax.nn.sigmoid(ga) * (hA @ w_a) + jax.nn.sigmoid(gb) * (hB @ w_b)
    return y @ w_o


def swiglu_ffn(xn, w_in, w_out):
    a, b = jnp.split(xn @ w_in, 2, axis=-1)
    return (jax.nn.silu(a) * b) @ w_out


def setup_inputs(seed: int = 0) -> dict:
    key = jax.random.key(seed)
    ks = jax.random.split(key, 32)

    def nrm(k, shape, s):
        return jax.random.normal(k, shape, jnp.float32) * s

    fg_base = jnp.linspace(3.0, 6.0, MLSTM_HEADS, dtype=jnp.float32)[None, :]
    b_gate = jnp.concatenate([
        nrm(ks[9], (DEPTH, MLSTM_HEADS), 0.1),
        fg_base + nrm(ks[10], (DEPTH, MLSTM_HEADS), 0.1),
        nrm(ks[11], (DEPTH, MLSTM_HEADS), 0.1),
        fg_base + nrm(ks[12], (DEPTH, MLSTM_HEADS), 0.1)], axis=-1)
    return {
        'x': nrm(ks[0], (BATCH, SEQ, D_MODEL), 1.0),
        'c': nrm(ks[1], (BATCH, D_MODEL), 1.0),
        'ctx': nrm(ks[2], (BATCH, CTX_LEN, D_MODEL), 1.0),
        'c_ctx': nrm(ks[3], (D_MODEL,), 1.0),
        'w_mod': nrm(ks[4], (DEPTH, D_MODEL, N_MOD * D_MODEL), D_MODEL ** -0.5),
        'b_mod': nrm(ks[5], (DEPTH, N_MOD * D_MODEL), 0.02),
        'norm1': 1.0 + nrm(ks[6], (DEPTH, D_MODEL), 0.02),
        'norm2': 1.0 + nrm(ks[7], (DEPTH, D_MODEL), 0.02),
        'w_in': nrm(ks[8], (DEPTH, D_MODEL, IN_WIDTH), D_MODEL ** -0.5),
        'b_gate': b_gate,
        'conv_w': nrm(ks[13], (DEPTH, CONV_K, 2 * M_QK_W), CONV_K ** -0.5),
        'conv_b': nrm(ks[14], (DEPTH, 2 * M_QK_W), 0.02),
        'mlstm_norm': 1.0 + nrm(ks[15], (DEPTH, M_V_W), 0.02),
        'q_norm': 1.0 + nrm(ks[16], (DEPTH, DIFF_HEAD_DIM), 0.02),
        'k_norm': 1.0 + nrm(ks[17], (DEPTH, DIFF_HEAD_DIM), 0.02),
        'lam_vecs': nrm(ks[18], (DEPTH, 4, DIFF_HEAD_DIM), 0.1),
        'diff_norm': 1.0 + nrm(ks[19], (DEPTH, 2 * DIFF_HEAD_DIM), 0.02),
        'w_branch_a': nrm(ks[20], (DEPTH, M_V_W, D_MODEL), M_V_W ** -0.5),
        'w_branch_b': nrm(ks[21], (DEPTH, DA_W, D_MODEL), DA_W ** -0.5),
        'w_out': nrm(ks[22], (DEPTH, D_MODEL, D_MODEL), D_MODEL ** -0.5),
        'w_ffn_in': nrm(ks[23], (DEPTH, D_MODEL, 2 * FFN_HIDDEN), D_MODEL ** -0.5),
        'w_ffn_out': nrm(ks[24], (DEPTH, FFN_HIDDEN, D_MODEL), FFN_HIDDEN ** -0.5),
    }


def reference(x, c, ctx, c_ctx, w_mod, b_mod, norm1, norm2, w_in, b_gate, conv_w, conv_b, mlstm_norm,
              q_norm, k_norm, lam_vecs, diff_norm, w_branch_a, w_branch_b, w_out, w_ffn_in, w_ffn_out):
    f32 = jnp.float32
    B, S, _ = x.shape
    cos, sin = axial_rope_tables(S)
    scale_q = DIFF_HEAD_DIM ** -0.5
    for l in range(DEPTH):
        last = l == DEPTH - 1
        mod = jax.nn.silu(c) @ w_mod[l] + b_mod[l]
        sh1, sc1, g1, sh2, sc2, g2 = [m[:, None, :] for m in jnp.split(mod, N_MOD, axis=-1)]
        mod_c = jax.nn.silu(c_ctx) @ w_mod[l] + b_mod[l]
        csh1, csc1, cg1, csh2, csc2, cg2 = jnp.split(mod_c, N_MOD)

        xn = modulated_norm(x, norm1[l], sh1, sc1)
        cn = modulated_norm(ctx, norm1[l], csh1, csc1)
        (mq, mk, mv, mo, (ig_f, lf_f, ig_b, lf_b), q1, q2, k1, k2, dv, ga, gb) = project_stream(
            xn, w_in[l], b_gate[l], conv_w[l], conv_b[l], q_norm[l], k_norm[l])
        (cmq, cmk, cmv, cmo, (cig_f, clf_f, cig_b, clf_b), cq1, cq2, ck1, ck2, cdv, cga, cgb) = project_stream(
            cn, w_in[l], b_gate[l], conv_w[l], conv_b[l], q_norm[l], k_norm[l])

        zero = (jnp.zeros((B, MLSTM_HEADS, MLSTM_DQK, MLSTM_DV), f32),
                jnp.zeros((B, MLSTM_HEADS, MLSTM_DQK), f32),
                jnp.zeros((B, MLSTM_HEADS), f32))
        hc_f, st_f = mlstm_chunkwise(cmq, cmk, cmv, cig_f, clf_f, *zero)
        hc_b, st_b = mlstm_chunkwise(flip_t(cmq), flip_t(cmk), flip_t(cmv), flip_t(cig_b), flip_t(clf_b), *zero)
        h_f, _ = mlstm_chunkwise(mq, mk, mv, ig_f, lf_f, *st_f)
        h_b, _ = mlstm_chunkwise(flip_t(mq), flip_t(mk), flip_t(mv), flip_t(ig_b), flip_t(lf_b), *st_b)
        hA = mlstm_output(h_f + flip_t(h_b), mo, mlstm_norm[l])

        lam_init = 0.8 - 0.6 * math.exp(-0.3 * l)
        lv = lam_vecs[l].astype(f32)
        lam = jnp.exp(jnp.sum(lv[0] * lv[1])) - jnp.exp(jnp.sum(lv[2] * lv[3])) + lam_init
        q1r = apply_axial_rope(q1, cos, sin) * scale_q
        q2r = apply_axial_rope(q2, cos, sin) * scale_q
        k1_all = jnp.concatenate([apply_axial_rope(k1, cos, sin), ck1], axis=2)
        k2_all = jnp.concatenate([apply_axial_rope(k2, cos, sin), ck2], axis=2)
        v_all = jnp.concatenate([dv, cdv], axis=2)
        o = diff_attention_blocks(q1r, q2r, k1_all, k2_all, v_all, lam)
        hB = diff_output(o, diff_norm[l], lam_init)

        y = merge_branches(hA, hB, ga, gb, w_branch_a[l], w_branch_b[l], w_out[l])
        x = x + (g1 * y).astype(x.dtype)
        x = x + (g2 * swiglu_ffn(modulated_norm(x, norm2[l], sh2, sc2), w_ffn_in[l], w_ffn_out[l])).astype(x.dtype)

        if not last:
            hcA = mlstm_output(hc_f + flip_t(hc_b), cmo, mlstm_norm[l])
            wc = diff_weights(cq1 * scale_q, cq2 * scale_q, ck1, ck2, lam)
            hcB = diff_output(jnp.einsum('bhqk,bhke->bhqe', wc, cdv.astype(f32)), diff_norm[l], lam_init)
            yc = merge_branches(hcA, hcB, cga, cgb, w_branch_a[l], w_branch_b[l], w_out[l])
            ctx = ctx + (cg1 * yc).astype(ctx.dtype)
            ctx = ctx + (cg2 * swiglu_ffn(modulated_norm(ctx, norm2[l], csh2, csc2), w_ffn_in[l], w_ffn_out[l])).astype(ctx.dtype)
    return x
```

```python
import functools
import math

import jax
import jax.numpy as jnp
import numpy as np
from jax import lax
from jax.experimental import pallas as pl
from jax.experimental.pallas import tpu as pltpu

F32 = jnp.float32
BF16 = jnp.bfloat16
HIGHEST = lax.Precision.HIGHEST

GRID_W = 64
N_MOD = 6
MLSTM_HEADS = 4
MLSTM_DQK = 128
MLSTM_DV = 256
CONV_K = 5
DIFF_HEADS = 8
DIFF_HEAD_DIM = 64
ROPE_BASE = 10000.0
ROPE_FREQS = DIFF_HEAD_DIM // 4
RMS_EPS = 1e-6
LAM_INIT_L0 = 0.8 - 0.6 * math.exp(-0.3 * 0)

LANES = 128
SUBLANES = 8
VMEM_LIMIT = 56 * 1024 * 1024

TOK_TILE = 256
CHUNK = 256
ATT_TQ = 256
ATT_TK = 256


def _cparams(sem):
    return pltpu.CompilerParams(dimension_semantics=sem, vmem_limit_bytes=VMEM_LIMIT)


def _sigmoid(x):
    return 1.0 / (1.0 + jnp.exp(-x))


def _silu(x):
    return x * _sigmoid(x)


def _log_sigmoid(x):
    return jnp.minimum(x, 0.0) - jnp.log(1.0 + jnp.exp(-jnp.abs(x)))


def _mod_kernel(c_ref, w_ref, b_ref, o_ref):
    a = _silu(c_ref[...])
    o_ref[...] = jnp.dot(a, w_ref[...], preferred_element_type=F32, precision=HIGHEST) + b_ref[...]


def _modulation(cc, w_mod, b_mod):
    rows, d = cc.shape
    n = w_mod.shape[1]
    tn = 1024
    return pl.pallas_call(
        _mod_kernel,
        grid=(n // tn,),
        in_specs=[pl.BlockSpec((rows, d), lambda j: (0, 0)),
                  pl.BlockSpec((d, tn), lambda j: (0, j)),
                  pl.BlockSpec((1, tn), lambda j: (0, j))],
        out_specs=pl.BlockSpec((rows, tn), lambda j: (0, j)),
        out_shape=jax.ShapeDtypeStruct((rows, n), F32),
        compiler_params=_cparams(("arbitrary",)),
        name="mod",
    )(cc, w_mod, b_mod.reshape(1, n))


def _rms_mod(xf, g, shift, scale):
    ms = jnp.mean(xf * xf, axis=-1, keepdims=True)
    return xf * lax.rsqrt(ms + RMS_EPS) * g * (1.0 + scale) + shift


def _proj_kernel(n_lat_tiles, d, x_ref, ctx_ref, mod_ref, g_ref, w_ref, wg_ref, p_ref, gates_ref):
    i = pl.program_id(1)
    xf = jnp.where(i < n_lat_tiles, x_ref[0], ctx_ref[0])
    mod = mod_ref[0]
    xn = _rms_mod(xf, g_ref[...], mod[:, 0:d], mod[:, d:2 * d])
    gates_ref[0] = jnp.dot(xn, wg_ref[...], preferred_element_type=F32, precision=HIGHEST)
    xb = xn.astype(BF16)
    n_out = w_ref.shape[1]
    for j in range(n_out // d):
        p_ref[0, :, j * d:(j + 1) * d] = jnp.dot(
            xb, w_ref[:, j * d:(j + 1) * d], preferred_element_type=F32).astype(BF16)


def _projection(x, ctx, mods3, norm1, wbig, wg):
    b, s, d = x.shape
    n_ctx = ctx.shape[1]
    tm = TOK_TILE
    n_lat_tiles = s // tm
    n_tiles = (s + n_ctx) // tm
    n_out = wbig.shape[1]
    return pl.pallas_call(
        functools.partial(_proj_kernel, n_lat_tiles, d),
        grid=(b, n_tiles),
        in_specs=[
            pl.BlockSpec((1, tm, d), lambda bi, i: (bi, jnp.minimum(i, n_lat_tiles - 1), 0)),
            pl.BlockSpec((1, tm, d), lambda bi, i: (bi, jnp.maximum(i - n_lat_tiles, 0), 0)),
            pl.BlockSpec((1, 1, N_MOD * d), lambda bi, i: (jnp.where(i < n_lat_tiles, bi, b), 0, 0)),
            pl.BlockSpec((1, d), lambda bi, i: (0, 0)),
            pl.BlockSpec((d, n_out), lambda bi, i: (0, 0), pipeline_mode=pl.Buffered(1)),
            pl.BlockSpec((d, LANES), lambda bi, i: (0, 0)),
        ],
        out_specs=[pl.BlockSpec((1, tm, n_out), lambda bi, i: (bi, i, 0)),
                   pl.BlockSpec((1, tm, LANES), lambda bi, i: (bi, i, 0))],
        out_shape=[jax.ShapeDtypeStruct((b, s + n_ctx, n_out), BF16),
                   jax.ShapeDtypeStruct((b, s + n_ctx, LANES), F32)],
        compiler_params=_cparams(("parallel", "arbitrary")),
        name="proj",
    )(x, ctx, mods3, norm1, wbig, wg)


def _conv_kernel(n_lat, n_ctx, n_qcols, x_ref, w_ref, b_ref, o_ref, buf_ref):
    j = pl.program_id(1)
    r = TOK_TILE
    gap = 2 * SUBLANES
    lat0 = SUBLANES
    ctx0 = SUBLANES + n_lat + gap
    zeros8 = jnp.zeros((SUBLANES, LANES), F32)
    buf_ref[0:SUBLANES, :] = zeros8
    buf_ref[lat0 + n_lat:lat0 + n_lat + SUBLANES, :] = zeros8
    buf_ref[lat0 + n_lat + SUBLANES:ctx0, :] = zeros8
    buf_ref[ctx0 + n_ctx:ctx0 + n_ctx + SUBLANES, :] = zeros8
    n_lat_tiles = n_lat // r
    n_tiles = (n_lat + n_ctx) // r

    def fill(t, carry):
        src = pl.multiple_of(t * r, r)
        dst = pl.multiple_of(SUBLANES + t * r + jnp.where(t >= n_lat_tiles, gap, 0), SUBLANES)
        buf_ref[pl.ds(dst, r), :] = x_ref[0, pl.ds(src, r), :].astype(F32)
        return carry

    lax.fori_loop(0, n_tiles, fill, 0)
    w = w_ref[...]
    bias = b_ref[...]
    out_scale = jnp.where(j < n_qcols, MLSTM_DQK ** -0.5, 1.0).astype(F32)
    ext_rows = r + 2 * SUBLANES

    def tile(t, carry):
        dst = pl.multiple_of(t * r, r)
        base = pl.multiple_of(t * r + jnp.where(t >= n_lat_tiles, gap, 0), SUBLANES)
        ext = buf_ref[pl.ds(base, ext_rows), :]
        acc = jnp.zeros((r, LANES), F32) + bias
        for k in range(CONV_K):
            shift = (CONV_K // 2 - k) % ext_rows
            sh = ext if shift == 0 else pltpu.roll(ext, shift, axis=0)
            acc = acc + sh[SUBLANES:SUBLANES + r, :] * w[k:k + 1, :]
        o_ref[0, pl.ds(dst, r), :] = (_silu(acc) * out_scale).astype(BF16)
        return carry

    lax.fori_loop(0, n_tiles, tile, 0)


def _conv_silu(p, conv_w8, conv_b, n_lat, n_ctx):
    b, t, _ = p.shape
    n_cols = conv_w8.shape[1] // LANES
    buf_rows = n_lat + n_ctx + 4 * SUBLANES
    return pl.pallas_call(
        functools.partial(_conv_kernel, n_lat, n_ctx, n_cols // 2),
        grid=(b, n_cols),
        in_specs=[pl.BlockSpec((1, t, LANES), lambda bi, j: (bi, 0, j)),
                  pl.BlockSpec((SUBLANES, LANES), lambda bi, j: (0, j)),
                  pl.BlockSpec((1, LANES), lambda bi, j: (0, j))],
        out_specs=pl.BlockSpec((1, t, LANES), lambda bi, j: (bi, 0, j)),
        out_shape=jax.ShapeDtypeStruct((b, t, n_cols * LANES), BF16),
        scratch_shapes=[pltpu.VMEM((buf_rows, LANES), F32)],
        compiler_params=_cparams(("parallel", "arbitrary")),
        name="conv",
    )(p, conv_w8, conv_b)


def _mlstm_direction(reverse, q, k, v_aug, ig_col, ig_row, lf_cum_col, lf_cum_row, lf_total, c_ref, m_ref, hd):
    l = q.shape[0]
    dv = v_aug.shape[1] - LANES
    m_prev = m_ref[hd:hd + 1, 0:1]
    c_aug = c_ref[hd]
    a_col = ig_col - lf_cum_col
    a_row = ig_row - lf_cum_row
    rows = lax.broadcasted_iota(jnp.int32, (l, l), 0)
    cols = lax.broadcasted_iota(jnp.int32, (l, l), 1)
    visible = (cols >= rows) if reverse else (cols <= rows)
    a_mat = jnp.where(visible, a_row, -jnp.inf)
    m_row = jnp.maximum(jnp.max(a_mat, axis=-1, keepdims=True), m_prev)
    decay = jnp.exp(a_mat - m_row)
    s = lax.dot_general(q, k, (((1,), (1,)), ((), ())), preferred_element_type=F32)
    pmat = (s * decay).astype(BF16)
    intra = jnp.dot(pmat, v_aug, preferred_element_type=F32)
    inter = jnp.dot(q, c_aug.astype(BF16), preferred_element_type=F32)
    numden = intra + jnp.exp(m_prev - m_row) * inter
    num = numden[:, :dv]
    den = numden[:, dv:dv + 1]
    floor = jnp.exp(-(lf_cum_col + m_row))
    h = num / jnp.maximum(jnp.abs(den), floor)
    m_last = jnp.maximum(jnp.max(a_col, axis=0, keepdims=True), m_prev)
    w = jnp.exp(a_col - m_last)
    wv = (w * v_aug.astype(F32)).astype(BF16)
    kv = lax.dot_general(k, wv, (((0,), (0,)), ((), ())), preferred_element_type=F32)
    c_ref[hd] = jnp.exp(m_prev - m_last) * c_aug + kv
    m_ref[hd:hd + 1, :] = jnp.broadcast_to(lf_total + m_last, (1, LANES))
    return h


def _mlstm_kernel(qf_ref, kf_ref, vf_ref, gf_ref, qb_ref, kb_ref, vb_ref, gb_ref, bias_ref,
                  hf_ref, hb_ref, c_ref, m_ref):
    s_idx = pl.program_id(1)

    @pl.when(s_idx == 0)
    def _():
        c_ref[...] = jnp.zeros_like(c_ref)
        m_ref[...] = jnp.zeros_like(m_ref)

    l = qf_ref.shape[1]
    nh = MLSTM_HEADS
    rows = lax.broadcasted_iota(jnp.int32, (l, l), 0)
    cols = lax.broadcasted_iota(jnp.int32, (l, l), 1)
    ones_col = jnp.ones((l, LANES), BF16)
    for reverse, q_ref, k_ref, v_ref, g_ref, h_ref in (
            (False, qf_ref, kf_ref, vf_ref, gf_ref, hf_ref),
            (True, qb_ref, kb_ref, vb_ref, gb_ref, hb_ref)):
        g = g_ref[0] + bias_ref[...]
        lf = _log_sigmoid(g)
        tri = jnp.where((cols >= rows) if reverse else (cols <= rows), 1.0, 0.0).astype(F32)
        lf_cum = jnp.dot(tri, lf, preferred_element_type=F32, precision=HIGHEST)
        g_t = g.T
        lf_cum_t = lf_cum.T
        base = 2 * nh if reverse else 0
        for h in range(nh):
            ci, cf = base + h, base + nh + h
            hd = (nh if reverse else 0) + h
            lf_cum_col = lf_cum[:, cf:cf + 1]
            lf_total = lf_cum[0:1, cf:cf + 1] if reverse else lf_cum[l - 1:l, cf:cf + 1]
            q = q_ref[0, :, h * MLSTM_DQK:(h + 1) * MLSTM_DQK]
            k = k_ref[0, :, h * MLSTM_DQK:(h + 1) * MLSTM_DQK]
            v_aug = jnp.concatenate([v_ref[0, :, h * MLSTM_DV:(h + 1) * MLSTM_DV], ones_col], axis=1)
            hval = _mlstm_direction(reverse, q, k, v_aug, g[:, ci:ci + 1], g_t[ci:ci + 1, :],
                                    lf_cum_col, lf_cum_t[cf:cf + 1, :], lf_total, c_ref, m_ref, hd)
            h_ref[0, :, h * MLSTM_DV:(h + 1) * MLSTM_DV] = hval.astype(h_ref.dtype)


def _mlstm(qk, p, gates, gate_bias, n_lat, n_ctx, v_region):
    b, t, _ = qk.shape
    l = CHUNK
    n_lat_c = n_lat // l
    n_ctx_c = n_ctx // l
    n_steps = n_lat_c + n_ctx_c
    qw = MLSTM_HEADS * MLSTM_DQK
    vw = MLSTM_HEADS * MLSTM_DV

    def fwd(si):
        return jnp.where(si < n_ctx_c, n_lat_c + si, si - n_ctx_c)

    def bwd(si):
        return jnp.where(si < n_ctx_c, n_steps - 1 - si, n_steps - 1 - si)

    in_specs = []
    for order in (fwd, bwd):
        in_specs += [
            pl.BlockSpec((1, l, qw), lambda bi, si, o=order: (bi, o(si), 0)),
            pl.BlockSpec((1, l, qw), lambda bi, si, o=order: (bi, o(si), 1)),
            pl.BlockSpec((1, l, vw), lambda bi, si, o=order: (bi, o(si), v_region)),
            pl.BlockSpec((1, l, LANES), lambda bi, si, o=order: (bi, o(si), 0)),
        ]
    in_specs.append(pl.BlockSpec((1, LANES), lambda bi, si: (0, 0)))
    out_specs = [pl.BlockSpec((1, l, vw), lambda bi, si: (bi, fwd(si), 0)),
                 pl.BlockSpec((1, l, vw), lambda bi, si: (bi, bwd(si), 0))]
    return pl.pallas_call(
        _mlstm_kernel,
        grid=(b, n_steps),
        in_specs=in_specs,
        out_specs=out_specs,
        out_shape=[jax.ShapeDtypeStruct((b, t, vw), BF16)] * 2,
        scratch_shapes=[pltpu.VMEM((2 * MLSTM_HEADS, MLSTM_DQK, MLSTM_DV + LANES), F32),
                        pltpu.VMEM((2 * MLSTM_HEADS, LANES), F32)],
        compiler_params=_cparams(("parallel", "arbitrary")),
        name="mlstm",
    )(qk, qk, p, gates, qk, qk, p, gates, gate_bias)


def _qkprep_kernel(q_ref, k_ref, cos_ref, sin_ref, gq_ref, gk_ref, grp_ref, qo_ref, ko_ref):
    cos = cos_ref[...]
    sin = sin_ref[...]
    grp = grp_ref[...]
    hw = 2 * DIFF_HEAD_DIM
    for src, gain_ref, dst, scale in ((q_ref, gq_ref, qo_ref, DIFF_HEAD_DIM ** -0.5), (k_ref, gk_ref, ko_ref, 1.0)):
        gain = gain_ref[...]
        for h in range(DIFF_HEADS):
            xh = src[0, :, h * hw:(h + 1) * hw].astype(F32)
            ss = jnp.dot((xh * xh).astype(BF16), grp, preferred_element_type=F32)
            y = xh * lax.rsqrt(ss * (1.0 / DIFF_HEAD_DIM) + RMS_EPS) * gain
            y = y * cos + pltpu.roll(y, DIFF_HEAD_DIM, axis=1) * sin
            if scale != 1.0:
                y = y * scale
            dst[0, :, h * hw:(h + 1) * hw] = y.astype(BF16)


def _qkprep(p, cos_t, sin_t, gq, gk, grp, q_region, k_region):
    b, t, _ = p.shape
    tm = TOK_TILE
    w = DIFF_HEADS * 2 * DIFF_HEAD_DIM
    return pl.pallas_call(
        _qkprep_kernel,
        grid=(b, t // tm),
        in_specs=[pl.BlockSpec((1, tm, w), lambda bi, i: (bi, i, q_region)),
                  pl.BlockSpec((1, tm, w), lambda bi, i: (bi, i, k_region)),
                  pl.BlockSpec((tm, LANES), lambda bi, i: (i, 0)),
                  pl.BlockSpec((tm, LANES), lambda bi, i: (i, 0)),
                  pl.BlockSpec((1, LANES), lambda bi, i: (0, 0)),
                  pl.BlockSpec((1, LANES), lambda bi, i: (0, 0)),
                  pl.BlockSpec((LANES, LANES), lambda bi, i: (0, 0))],
        out_specs=[pl.BlockSpec((1, tm, w), lambda bi, i: (bi, i, 0)),
                   pl.BlockSpec((1, tm, w), lambda bi, i: (bi, i, 0))],
        out_shape=[jax.ShapeDtypeStruct((b, t, w), BF16)] * 2,
        compiler_params=_cparams(("parallel", "arbitrary")),
        name="qkprep",
    )(p, p, cos_t, sin_t, gq, gk, grp)


def _attn_kernel(n_keys, q_ref, k_ref, v_ref, lam_ref, gain_ref, o_ref, m_ref, l_ref, acc_ref):
    tq = q_ref.shape[1]
    tk = ATT_TK
    q = q_ref[0]
    lane = lax.broadcasted_iota(jnp.int32, q.shape, 1)
    first = (lane % DIFF_HEAD_DIM) < (DIFF_HEAD_DIM // 2)
    zero = jnp.zeros_like(q)
    qq = jnp.concatenate([jnp.where(first, q, zero), jnp.where(first, zero, q)], axis=0)
    m_ref[...] = jnp.full_like(m_ref, -jnp.inf)
    l_ref[...] = jnp.zeros_like(l_ref)
    acc_ref[...] = jnp.zeros_like(acc_ref)

    def step(j, carry):
        off = pl.multiple_of(j * tk, tk)
        k = k_ref[0, pl.ds(off, tk), :]
        v = v_ref[0, pl.ds(off, tk), :]
        s = lax.dot_general(qq, k, (((1,), (1,)), ((), ())), preferred_element_type=F32)
        m_old = m_ref[...]
        m_new = jnp.maximum(m_old, jnp.max(s, axis=-1, keepdims=True))
        alpha = jnp.exp(m_old - m_new)
        e = jnp.exp(s - m_new)
        l_ref[...] = alpha * l_ref[...] + jnp.sum(e, axis=-1, keepdims=True)
        acc_ref[...] = alpha * acc_ref[...] + jnp.dot(e.astype(BF16), v, preferred_element_type=F32)
        m_ref[...] = m_new
        return carry

    lax.fori_loop(0, n_keys // tk, step, 0)
    lv = lam_ref[...]
    lam = (jnp.exp(jnp.sum(lv[0:1] * lv[1:2], axis=-1, keepdims=True))
           - jnp.exp(jnp.sum(lv[2:3] * lv[3:4], axis=-1, keepdims=True)) + LAM_INIT_L0)
    o_all = acc_ref[...] / l_ref[...]
    o = o_all[:tq] - lam * o_all[tq:]
    ms = jnp.mean(o * o, axis=-1, keepdims=True)
    o_ref[0] = (o * lax.rsqrt(ms + RMS_EPS) * gain_ref[...] * (1.0 - LAM_INIT_L0)).astype(o_ref.dtype)


def _attention(qr, kr, p, lam_vecs, diff_gain, n_lat, v_region):
    b, t, _ = kr.shape
    tq = ATT_TQ
    hw = 2 * DIFF_HEAD_DIM
    v_col0 = v_region * DIFF_HEADS
    return pl.pallas_call(
        functools.partial(_attn_kernel, t),
        grid=(b, DIFF_HEADS, n_lat // tq),
        in_specs=[pl.BlockSpec((1, tq, hw), lambda bi, h, i: (bi, i, h)),
                  pl.BlockSpec((1, t, hw), lambda bi, h, i: (bi, 0, h)),
                  pl.BlockSpec((1, t, hw), lambda bi, h, i: (bi, 0, v_col0 + h)),
                  pl.BlockSpec(lam_vecs.shape, lambda bi, h, i: (0, 0)),
                  pl.BlockSpec((1, hw), lambda bi, h, i: (0, 0))],
        out_specs=pl.BlockSpec((1, tq, hw), lambda bi, h, i: (bi, i, h)),
        out_shape=jax.ShapeDtypeStruct((b, n_lat, DIFF_HEADS * hw), BF16),
        scratch_shapes=[pltpu.VMEM((2 * tq, 1), F32), pltpu.VMEM((2 * tq, 1), F32),
                        pltpu.VMEM((2 * tq, hw), F32)],
        compiler_params=_cparams(("parallel", "parallel", "arbitrary")),
        name="attn",
    )(qr, kr, p, lam_vecs, diff_gain)


def _merge_kernel(d, hf_ref, hb_ref, mo_ref, ga_ref, gb_ref, hB_ref, x_ref, mod_ref, gain_ref,
                  wa_ref, wb_ref, wo_ref, o_ref):
    gain = gain_ref[...]
    parts = []
    for h in range(MLSTM_HEADS):
        sl = slice(h * MLSTM_DV, (h + 1) * MLSTM_DV)
        hs = hf_ref[0, :, sl].astype(F32) + hb_ref[0, :, sl].astype(F32)
        ms = jnp.mean(hs * hs, axis=-1, keepdims=True)
        hn = hs * lax.rsqrt(ms + RMS_EPS) * gain[:, sl]
        parts.append((_sigmoid(mo_ref[0, :, sl].astype(F32)) * hn).astype(BF16))
    h_a = jnp.concatenate(parts, axis=1)
    ya = jnp.dot(h_a, wa_ref[...], preferred_element_type=F32)
    yb = jnp.dot(hB_ref[0], wb_ref[...], preferred_element_type=F32)
    y = _sigmoid(ga_ref[0].astype(F32)) * ya + _sigmoid(gb_ref[0].astype(F32)) * yb
    yo = jnp.dot(y.astype(BF16), wo_ref[...], preferred_element_type=F32)
    g1 = mod_ref[0][:, 2 * d:3 * d]
    o_ref[0] = x_ref[0] + g1 * yo


def _merge(hf, hb, p, h_b, x, mods3, mlstm_gain, wa, wb, wo, mo_region, ga_region, gb_region):
    b, s, d = x.shape
    tm = TOK_TILE
    wspec = lambda shape: pl.BlockSpec(shape, lambda bi, i: (0, 0), pipeline_mode=pl.Buffered(1))
    tok = lambda region: pl.BlockSpec((1, tm, d), lambda bi, i, r=region: (bi, i, r))
    return pl.pallas_call(
        functools.partial(_merge_kernel, d),
        grid=(b, s // tm),
        in_specs=[tok(0), tok(0), tok(mo_region), tok(ga_region), tok(gb_region), tok(0), tok(0),
                  pl.BlockSpec((1, 1, N_MOD * d), lambda bi, i: (bi, 0, 0)),
                  pl.BlockSpec((1, d), lambda bi, i: (0, 0)),
                  wspec(wa.shape), wspec(wb.shape), wspec(wo.shape)],
        out_specs=pl.BlockSpec((1, tm, d), lambda bi, i: (bi, i, 0)),
        out_shape=jax.ShapeDtypeStruct((b, s, d), F32),
        compiler_params=_cparams(("parallel", "arbitrary")),
        name="merge",
    )(hf, hb, p, p, p, h_b, x, mods3, mlstm_gain, wa, wb, wo)


def _ffn_kernel(d, hidden, x_ref, mod_ref, g_ref, wi_ref, wo_ref, o_ref):
    x1 = x_ref[0]
    mod = mod_ref[0]
    xn = _rms_mod(x1, g_ref[...], mod[:, 3 * d:4 * d], mod[:, 4 * d:5 * d]).astype(BF16)
    a = jnp.dot(xn, wi_ref[:, :hidden], preferred_element_type=F32)
    bb = jnp.dot(xn, wi_ref[:, hidden:], preferred_element_type=F32)
    hmid = (_silu(a) * bb).astype(BF16)
    y = jnp.dot(hmid, wo_ref[...], preferred_element_type=F32)
    o_ref[0] = x1 + mod[:, 5 * d:6 * d] * y


def _ffn(x1, mods3, norm2, wi, wo):
    b, s, d = x1.shape
    hidden = wo.shape[0]
    tm = TOK_TILE
    wspec = lambda shape: pl.BlockSpec(shape, lambda bi, i: (0, 0), pipeline_mode=pl.Buffered(1))
    return pl.pallas_call(
        functools.partial(_ffn_kernel, d, hidden),
        grid=(b, s // tm),
        in_specs=[pl.BlockSpec((1, tm, d), lambda bi, i: (bi, i, 0)),
                  pl.BlockSpec((1, 1, N_MOD * d), lambda bi, i: (bi, 0, 0)),
                  pl.BlockSpec((1, d), lambda bi, i: (0, 0)),
                  wspec(wi.shape), wspec(wo.shape)],
        out_specs=pl.BlockSpec((1, tm, d), lambda bi, i: (bi, i, 0)),
        out_shape=jax.ShapeDtypeStruct((b, s, d), F32),
        compiler_params=_cparams(("parallel", "arbitrary")),
        name="ffn",
    )(x1, mods3, norm2, wi, wo)


def _diff_lane_perm():
    l = np.arange(2 * DIFF_HEAD_DIM)
    half, r = l // DIFF_HEAD_DIM, l % DIFF_HEAD_DIM
    qsel, axis, f = r // 32, (r % 32) // ROPE_FREQS, r % ROPE_FREQS
    return qsel * DIFF_HEAD_DIM + axis * 2 * ROPE_FREQS + half * ROPE_FREQS + f, qsel, axis, f, half


def _rope_tables(n_lat, n_ctx):
    _, _, axis, f, half = _diff_lane_perm()
    pos = jnp.arange(n_lat, dtype=F32)
    row = jnp.floor(pos / GRID_W)
    col = pos - row * GRID_W
    inv = ROPE_BASE ** (-jnp.arange(ROPE_FREQS, dtype=F32) / ROPE_FREQS)
    ang = jnp.where(axis[None, :] == 0, row[:, None], col[:, None]) * inv[f][None, :]
    cos = jnp.concatenate([jnp.cos(ang), jnp.ones((n_ctx, LANES), F32)], axis=0)
    sin = jnp.sin(ang) * jnp.where(half == 0, -1.0, 1.0)[None, :]
    sin = jnp.concatenate([sin, jnp.zeros((n_ctx, LANES), F32)], axis=0)
    return cos, sin


def kernel(x, c, ctx, c_ctx, w_mod, b_mod, norm1, norm2, w_in, b_gate, conv_w, conv_b, mlstm_norm, q_norm, k_norm,
           lam_vecs, diff_norm, w_branch_a, w_branch_b, w_out, w_ffn_in, w_ffn_out):
    b, s, d = x.shape
    n_ctx = ctx.shape[1]
    assert w_mod.shape[0] == 1, "single-layer block"
    assert s % CHUNK == 0 and n_ctx % CHUNK == 0 and s % ATT_TQ == 0
    qk_w = MLSTM_HEADS * MLSTM_DQK
    v_w = MLSTM_HEADS * MLSTM_DV
    da_w = DIFF_HEADS * 2 * DIFF_HEAD_DIM
    n_gates = 4 * MLSTM_HEADS
    sizes = (qk_w, qk_w, v_w, v_w, n_gates, da_w, da_w, da_w, d, d)
    offs = np.concatenate([[0], np.cumsum(sizes)])
    seg = lambda i: w_in[0][:, offs[i]:offs[i + 1]]

    perm, qsel, _, _, _ = _diff_lane_perm()
    head_perm = (np.arange(DIFF_HEADS)[:, None] * 2 * DIFF_HEAD_DIM + perm[None, :]).reshape(-1)
    wbig = jnp.concatenate([seg(0), seg(1), seg(2), seg(3), seg(5)[:, head_perm], seg(6)[:, head_perm],
                            seg(7), seg(8), seg(9)], axis=1).astype(BF16)
    R_QK, R_V, R_O, R_DQ, R_DK, R_DV, R_GA, R_GB = range(8)
    wg = jnp.pad(seg(4), ((0, 0), (0, LANES - n_gates)))
    gate_bias = jnp.pad(b_gate[0], (0, LANES - n_gates)).reshape(1, LANES)
    conv_w8 = jnp.pad(conv_w[0], ((0, SUBLANES - CONV_K), (0, 0)))
    gq = q_norm[0][perm % DIFF_HEAD_DIM].reshape(1, LANES)
    gk = k_norm[0][perm % DIFF_HEAD_DIM].reshape(1, LANES)
    grp = jnp.asarray(qsel[:, None] == qsel[None, :], BF16)
    cos_t, sin_t = _rope_tables(s, n_ctx)

    cc = jnp.concatenate([c, c_ctx[None, :], jnp.zeros((SUBLANES - b - 1, d), F32)], axis=0)
    mods = _modulation(cc, w_mod[0], b_mod[0])
    mods3 = mods.reshape(SUBLANES, 1, N_MOD * d)

    p, gates = _projection(x, ctx, mods3, norm1, wbig, wg)
    qk = _conv_silu(p, conv_w8, conv_b, s, n_ctx)
    hf, hb = _mlstm(qk, p, gates, gate_bias, s, n_ctx, R_V)
    qr, kr = _qkprep(p, cos_t, sin_t, gq, gk, grp, R_DQ, R_DK)
    h_b = _attention(qr, kr, p, lam_vecs[0], diff_norm, s, R_DV)
    x1 = _merge(hf, hb, p, h_b, x, mods3, mlstm_norm, w_branch_a[0].astype(BF16), w_branch_b[0].astype(BF16),
                w_out[0].astype(BF16), R_O, R_GA, R_GB)
    return _ffn(x1, mods3, norm2, w_ffn_in[0].astype(BF16), w_ffn_out[0].astype(BF16))
```

```python
import functools
import math

import jax
import jax.numpy as jnp
import numpy as np
from jax import lax
from jax.experimental import pallas as pl
from jax.experimental.pallas import tpu as pltpu

F32 = jnp.float32
BF16 = jnp.bfloat16
HIGHEST = lax.Precision.HIGHEST

GRID_W = 64
N_MOD = 6
MLSTM_HEADS = 4
MLSTM_DQK = 128
MLSTM_DV = 256
CONV_K = 5
DIFF_HEADS = 8
DIFF_HEAD_DIM = 64
ROPE_BASE = 10000.0
ROPE_FREQS = DIFF_HEAD_DIM // 4
RMS_EPS = 1e-6
LAM_INIT_L0 = 0.8 - 0.6 * math.exp(-0.3 * 0)
Q_SCALE_LOG2 = DIFF_HEAD_DIM ** -0.5 * math.log2(math.e)

LANES = 128
SUBLANES = 8
VMEM_LIMIT = 56 * 1024 * 1024

TOK_TILE = 256
CHUNK = 256
ATT_TQ = 512
ATT_TK = 768


def _cparams(sem):
    return pltpu.CompilerParams(dimension_semantics=sem, vmem_limit_bytes=VMEM_LIMIT)


def _sigmoid(x):
    return 1.0 / (1.0 + jnp.exp(-x))


def _silu(x):
    return x * _sigmoid(x)


def _log_sigmoid(x):
    return jnp.minimum(x, 0.0) - jnp.log(1.0 + jnp.exp(-jnp.abs(x)))


def _mod_kernel(c_ref, w_ref, b_ref, o_ref):
    a = _silu(c_ref[...])
    o_ref[...] = jnp.dot(a, w_ref[...], preferred_element_type=F32, precision=HIGHEST) + b_ref[...]


def _modulation(cc, w_mod, b_mod):
    rows, d = cc.shape
    n = w_mod.shape[1]
    tn = 1024
    return pl.pallas_call(
        _mod_kernel,
        grid=(n // tn,),
        in_specs=[pl.BlockSpec((rows, d), lambda j: (0, 0)),
                  pl.BlockSpec((d, tn), lambda j: (0, j)),
                  pl.BlockSpec((1, tn), lambda j: (0, j))],
        out_specs=pl.BlockSpec((rows, tn), lambda j: (0, j)),
        out_shape=jax.ShapeDtypeStruct((rows, n), F32),
        compiler_params=_cparams(("arbitrary",)),
        name="mod",
    )(cc, w_mod, b_mod.reshape(1, n))


def _rms_mod(xf, g, shift, scale):
    ms = jnp.mean(xf * xf, axis=-1, keepdims=True)
    return xf * lax.rsqrt(ms + RMS_EPS) * g * (1.0 + scale) + shift


def _proj_kernel(n_lat_tiles, d, x_ref, ctx_ref, mod_ref, g_ref, w_ref, wg_ref, p_ref, gates_ref):
    i = pl.program_id(1)
    xf = jnp.where(i < n_lat_tiles, x_ref[0], ctx_ref[0])
    mod = mod_ref[0]
    xn = _rms_mod(xf, g_ref[...], mod[:, 0:d], mod[:, d:2 * d])
    gates_ref[0] = jnp.dot(xn, wg_ref[...], preferred_element_type=F32, precision=HIGHEST)
    xb = xn.astype(BF16)
    n_out = w_ref.shape[1]
    for j in range(n_out // d):
        p_ref[0, :, j * d:(j + 1) * d] = jnp.dot(
            xb, w_ref[:, j * d:(j + 1) * d], preferred_element_type=F32).astype(BF16)


def _projection(x, ctx, mods3, norm1, wbig, wg):
    b, s, d = x.shape
    n_ctx = ctx.shape[1]
    tm = TOK_TILE
    n_lat_tiles = s // tm
    n_tiles = (s + n_ctx) // tm
    n_out = wbig.shape[1]
    return pl.pallas_call(
        functools.partial(_proj_kernel, n_lat_tiles, d),
        grid=(b, n_tiles),
        in_specs=[
            pl.BlockSpec((1, tm, d), lambda bi, i: (bi, jnp.minimum(i, n_lat_tiles - 1), 0)),
            pl.BlockSpec((1, tm, d), lambda bi, i: (bi, jnp.maximum(i - n_lat_tiles, 0), 0)),
            pl.BlockSpec((1, 1, N_MOD * d), lambda bi, i: (jnp.where(i < n_lat_tiles, bi, b), 0, 0)),
            pl.BlockSpec((1, d), lambda bi, i: (0, 0)),
            pl.BlockSpec((d, n_out), lambda bi, i: (0, 0), pipeline_mode=pl.Buffered(1)),
            pl.BlockSpec((d, LANES), lambda bi, i: (0, 0)),
        ],
        out_specs=[pl.BlockSpec((1, tm, n_out), lambda bi, i: (bi, i, 0)),
                   pl.BlockSpec((1, tm, LANES), lambda bi, i: (bi, i, 0))],
        out_shape=[jax.ShapeDtypeStruct((b, s + n_ctx, n_out), BF16),
                   jax.ShapeDtypeStruct((b, s + n_ctx, LANES), F32)],
        compiler_params=_cparams(("parallel", "arbitrary")),
        name="proj",
    )(x, ctx, mods3, norm1, wbig, wg)


def _conv_kernel(n_lat, n_ctx, n_qcols, x_ref, w_ref, b_ref, o_ref, buf_ref):
    j = pl.program_id(1)
    r = TOK_TILE
    gap = 2 * SUBLANES
    lat0 = SUBLANES
    ctx0 = SUBLANES + n_lat + gap
    zeros8 = jnp.zeros((SUBLANES, LANES), F32)
    buf_ref[0:SUBLANES, :] = zeros8
    buf_ref[lat0 + n_lat:lat0 + n_lat + SUBLANES, :] = zeros8
    buf_ref[lat0 + n_lat + SUBLANES:ctx0, :] = zeros8
    buf_ref[ctx0 + n_ctx:ctx0 + n_ctx + SUBLANES, :] = zeros8
    n_lat_tiles = n_lat // r
    n_tiles = (n_lat + n_ctx) // r

    def fill(t, carry):
        src = pl.multiple_of(t * r, r)
        dst = pl.multiple_of(SUBLANES + t * r + jnp.where(t >= n_lat_tiles, gap, 0), SUBLANES)
        buf_ref[pl.ds(dst, r), :] = x_ref[0, pl.ds(src, r), :].astype(F32)
        return carry

    lax.fori_loop(0, n_tiles, fill, 0)
    w = w_ref[...]
    bias = b_ref[...]
    out_scale = jnp.where(j < n_qcols, MLSTM_DQK ** -0.5, 1.0).astype(F32)
    ext_rows = r + 2 * SUBLANES

    def tile(t, carry):
        dst = pl.multiple_of(t * r, r)
        base = pl.multiple_of(t * r + jnp.where(t >= n_lat_tiles, gap, 0), SUBLANES)
        ext = buf_ref[pl.ds(base, ext_rows), :]
        acc = jnp.zeros((r, LANES), F32) + bias
        for k in range(CONV_K):
            shift = (CONV_K // 2 - k) % ext_rows
            sh = ext if shift == 0 else pltpu.roll(ext, shift, axis=0)
            acc = acc + sh[SUBLANES:SUBLANES + r, :] * w[k:k + 1, :]
        o_ref[0, pl.ds(dst, r), :] = (_silu(acc) * out_scale).astype(BF16)
        return carry

    lax.fori_loop(0, n_tiles, tile, 0)


def _conv_silu(p, conv_w8, conv_b, n_lat, n_ctx):
    b, t, _ = p.shape
    n_cols = conv_w8.shape[1] // LANES
    buf_rows = n_lat + n_ctx + 4 * SUBLANES
    return pl.pallas_call(
        functools.partial(_conv_kernel, n_lat, n_ctx, n_cols // 2),
        grid=(b, n_cols),
        in_specs=[pl.BlockSpec((1, t, LANES), lambda bi, j: (bi, 0, j)),
                  pl.BlockSpec((SUBLANES, LANES), lambda bi, j: (0, j)),
                  pl.BlockSpec((1, LANES), lambda bi, j: (0, j))],
        out_specs=pl.BlockSpec((1, t, LANES), lambda bi, j: (bi, 0, j)),
        out_shape=jax.ShapeDtypeStruct((b, t, n_cols * LANES), BF16),
        scratch_shapes=[pltpu.VMEM((buf_rows, LANES), F32)],
        compiler_params=_cparams(("parallel", "arbitrary")),
        name="conv",
    )(p, conv_w8, conv_b)


def _mlstm_direction(reverse, q, k, v_aug, ig_col, ig_row, lf_cum_col, lf_cum_row, lf_total, c_ref, m_ref, hd):
    l = q.shape[0]
    dv = v_aug.shape[1] - LANES
    m_prev = m_ref[hd:hd + 1, 0:1]
    c_aug = c_ref[hd]
    a_col = ig_col - lf_cum_col
    a_row = ig_row - lf_cum_row
    rows = lax.broadcasted_iota(jnp.int32, (l, l), 0)
    cols = lax.broadcasted_iota(jnp.int32, (l, l), 1)
    visible = (cols >= rows) if reverse else (cols <= rows)
    a_mat = jnp.where(visible, a_row, -jnp.inf)
    m_row = jnp.maximum(jnp.max(a_mat, axis=-1, keepdims=True), m_prev)
    decay = jnp.exp(a_mat - m_row)
    s = lax.dot_general(q, k, (((1,), (1,)), ((), ())), preferred_element_type=F32)
    pmat = (s * decay).astype(BF16)
    intra = jnp.dot(pmat, v_aug, preferred_element_type=F32)
    inter = jnp.dot(q, c_aug.astype(BF16), preferred_element_type=F32)
    numden = intra + jnp.exp(m_prev - m_row) * inter
    num = numden[:, :dv]
    den = numden[:, dv:dv + 1]
    floor = jnp.exp(-(lf_cum_col + m_row))
    h = num / jnp.maximum(jnp.abs(den), floor)
    m_last = jnp.maximum(jnp.max(a_col, axis=0, keepdims=True), m_prev)
    w = jnp.exp(a_col - m_last)
    wv = (w * v_aug.astype(F32)).astype(BF16)
    kv = lax.dot_general(k, wv, (((0,), (0,)), ((), ())), preferred_element_type=F32)
    c_ref[hd] = jnp.exp(m_prev - m_last) * c_aug + kv
    m_ref[hd:hd + 1, :] = jnp.broadcast_to(lf_total + m_last, (1, LANES))
    return h


def _mlstm_kernel(qf_ref, kf_ref, vf_ref, gf_ref, qb_ref, kb_ref, vb_ref, gb_ref, bias_ref,
                  hf_ref, hb_ref, c_ref, m_ref):
    s_idx = pl.program_id(1)

    @pl.when(s_idx == 0)
    def _():
        c_ref[...] = jnp.zeros_like(c_ref)
        m_ref[...] = jnp.zeros_like(m_ref)

    l = qf_ref.shape[1]
    nh = MLSTM_HEADS
    rows = lax.broadcasted_iota(jnp.int32, (l, l), 0)
    cols = lax.broadcasted_iota(jnp.int32, (l, l), 1)
    ones_col = jnp.ones((l, LANES), BF16)
    for reverse, q_ref, k_ref, v_ref, g_ref, h_ref in (
            (False, qf_ref, kf_ref, vf_ref, gf_ref, hf_ref),
            (True, qb_ref, kb_ref, vb_ref, gb_ref, hb_ref)):
        g = g_ref[0] + bias_ref[...]
        lf = _log_sigmoid(g)
        tri = jnp.where((cols >= rows) if reverse else (cols <= rows), 1.0, 0.0).astype(F32)
        lf_cum = jnp.dot(tri, lf, preferred_element_type=F32, precision=HIGHEST)
        g_t = g.T
        lf_cum_t = lf_cum.T
        base = 2 * nh if reverse else 0
        for h in range(nh):
            ci, cf = base + h, base + nh + h
            hd = (nh if reverse else 0) + h
            lf_cum_col = lf_cum[:, cf:cf + 1]
            lf_total = lf_cum[0:1, cf:cf + 1] if reverse else lf_cum[l - 1:l, cf:cf + 1]
            q = q_ref[0, :, h * MLSTM_DQK:(h + 1) * MLSTM_DQK]
            k = k_ref[0, :, h * MLSTM_DQK:(h + 1) * MLSTM_DQK]
            v_aug = jnp.concatenate([v_ref[0, :, h * MLSTM_DV:(h + 1) * MLSTM_DV], ones_col], axis=1)
            hval = _mlstm_direction(reverse, q, k, v_aug, g[:, ci:ci + 1], g_t[ci:ci + 1, :],
                                    lf_cum_col, lf_cum_t[cf:cf + 1, :], lf_total, c_ref, m_ref, hd)
            h_ref[0, :, h * MLSTM_DV:(h + 1) * MLSTM_DV] = hval.astype(h_ref.dtype)


def _mlstm(qk, p, gates, gate_bias, n_lat, n_ctx, v_region):
    b, t, _ = qk.shape
    l = CHUNK
    n_lat_c = n_lat // l
    n_ctx_c = n_ctx // l
    n_steps = n_lat_c + n_ctx_c
    qw = MLSTM_HEADS * MLSTM_DQK
    vw = MLSTM_HEADS * MLSTM_DV

    def fwd(si):
        return jnp.where(si < n_ctx_c, n_lat_c + si, si - n_ctx_c)

    def bwd(si):
        return jnp.where(si < n_ctx_c, n_steps - 1 - si, n_steps - 1 - si)

    in_specs = []
    for order in (fwd, bwd):
        in_specs += [
            pl.BlockSpec((1, l, qw), lambda bi, si, o=order: (bi, o(si), 0)),
            pl.BlockSpec((1, l, qw), lambda bi, si, o=order: (bi, o(si), 1)),
            pl.BlockSpec((1, l, vw), lambda bi, si, o=order: (bi, o(si), v_region)),
            pl.BlockSpec((1, l, LANES), lambda bi, si, o=order: (bi, o(si), 0)),
        ]
    in_specs.append(pl.BlockSpec((1, LANES), lambda bi, si: (0, 0)))
    out_specs = [pl.BlockSpec((1, l, vw), lambda bi, si: (bi, fwd(si), 0)),
                 pl.BlockSpec((1, l, vw), lambda bi, si: (bi, bwd(si), 0))]
    return pl.pallas_call(
        _mlstm_kernel,
        grid=(b, n_steps),
        in_specs=in_specs,
        out_specs=out_specs,
        out_shape=[jax.ShapeDtypeStruct((b, t, vw), BF16)] * 2,
        scratch_shapes=[pltpu.VMEM((2 * MLSTM_HEADS, MLSTM_DQK, MLSTM_DV + LANES), F32),
                        pltpu.VMEM((2 * MLSTM_HEADS, LANES), F32)],
        compiler_params=_cparams(("parallel", "arbitrary")),
        name="mlstm",
    )(qk, qk, p, gates, qk, qk, p, gates, gate_bias)


def _qkprep_kernel(q_ref, k_ref, v_ref, cos_ref, sin_ref, gq_ref, gk_ref, grp_ref, qo_ref, ko_ref, vo_ref):
    cos = cos_ref[...]
    sin = sin_ref[...]
    grp = grp_ref[...]
    hw = 2 * DIFF_HEAD_DIM

    def normed_rotated(xh, gain):
        ss = jnp.dot((xh * xh).astype(BF16), grp, preferred_element_type=F32)
        y = xh * lax.rsqrt(ss * (1.0 / DIFF_HEAD_DIM) + RMS_EPS) * gain
        return y * cos + pltpu.roll(y, DIFF_HEAD_DIM, axis=1) * sin

    gq = gq_ref[...]
    gk = gk_ref[...]
    for h in range(DIFF_HEADS):
        sl = slice(h * hw, (h + 1) * hw)
        qo_ref[0, sl, :] = (normed_rotated(q_ref[0, :, sl].astype(F32), gq) * Q_SCALE_LOG2).T.astype(BF16)
        ko_ref[0, :, sl] = normed_rotated(k_ref[0, :, sl].astype(F32), gk).astype(BF16)
        vo_ref[0, sl, :] = v_ref[0, :, sl].astype(F32).T.astype(BF16)


def _qkprep(p, cos_t, sin_t, gq, gk, grp, q_region, k_region, v_region):
    b, t, _ = p.shape
    tm = TOK_TILE
    w = DIFF_HEADS * 2 * DIFF_HEAD_DIM
    tok = lambda region: pl.BlockSpec((1, tm, w), lambda bi, i, r=region: (bi, i, r))
    tok_t = pl.BlockSpec((1, w, tm), lambda bi, i: (bi, 0, i))
    row = pl.BlockSpec((1, LANES), lambda bi, i: (0, 0))
    return pl.pallas_call(
        _qkprep_kernel,
        grid=(b, t // tm),
        in_specs=[tok(q_region), tok(k_region), tok(v_region),
                  pl.BlockSpec((tm, LANES), lambda bi, i: (i, 0)),
                  pl.BlockSpec((tm, LANES), lambda bi, i: (i, 0)),
                  row, row, pl.BlockSpec((LANES, LANES), lambda bi, i: (0, 0))],
        out_specs=[tok_t, tok(0), tok_t],
        out_shape=[jax.ShapeDtypeStruct((b, w, t), BF16), jax.ShapeDtypeStruct((b, t, w), BF16),
                   jax.ShapeDtypeStruct((b, w, t), BF16)],
        compiler_params=_cparams(("parallel", "arbitrary")),
        name="qkprep",
    )(p, p, p, cos_t, sin_t, gq, gk, grp)


def _attn_kernel(n_keys, qt_ref, k_ref, vt_ref, lam_ref, gain_ref, o_ref,
                 qq_ref, s_ref, pm_ref, e_ref, al_ref, m_ref, l_ref, acc_ref):
    tq = qt_ref.shape[2]
    tk = ATT_TK
    qt = qt_ref[0]
    row = lax.broadcasted_iota(jnp.int32, qt.shape, 0)
    first = (row % DIFF_HEAD_DIM) < (DIFF_HEAD_DIM // 2)
    zero = jnp.zeros_like(qt)
    qq_ref[...] = jnp.concatenate([jnp.where(first, qt, zero), jnp.where(first, zero, qt)], axis=1)
    m_ref[...] = jnp.full_like(m_ref, -jnp.inf)
    l_ref[...] = jnp.zeros_like(l_ref)
    acc_ref[...] = jnp.zeros_like(acc_ref)
    n = n_keys // tk
    width = 2 * tq

    def scores(tile):
        off = pl.multiple_of(tile * tk, tk)
        s = jnp.dot(k_ref[0, pl.ds(off, tk), :], qq_ref[...], preferred_element_type=F32)
        s_ref[...] = s
        pm_ref[...] = jnp.max(s.reshape(tk // SUBLANES, SUBLANES, width), axis=0)

    def softmax():
        m_old = m_ref[...]
        m_new = jnp.maximum(m_old, jnp.max(pm_ref[...], axis=0, keepdims=True))
        alpha = jnp.exp2(m_old - m_new)
        e = jnp.exp2(s_ref[...] - m_new)
        l_ref[...] = alpha * l_ref[...] + jnp.sum(e, axis=0, keepdims=True)
        e_ref[...] = e.astype(BF16)
        al_ref[...] = alpha
        m_ref[...] = m_new

    def accumulate(tile):
        off = pl.multiple_of(tile * tk, tk)
        pv = jnp.dot(vt_ref[0, :, pl.ds(off, tk)], e_ref[...], preferred_element_type=F32)
        acc_ref[...] = al_ref[...] * acc_ref[...] + pv

    scores(0)
    softmax()
    scores(1)

    def iteration(i, carry):
        accumulate(i - 2)
        softmax()
        scores(i)
        return carry

    lax.fori_loop(2, n, iteration, 0)
    accumulate(n - 2)
    softmax()
    accumulate(n - 1)
    lv = lam_ref[...]
    lam = (jnp.exp(jnp.sum(lv[0:1] * lv[1:2], axis=-1, keepdims=True))
           - jnp.exp(jnp.sum(lv[2:3] * lv[3:4], axis=-1, keepdims=True)) + LAM_INIT_L0)
    o_all = acc_ref[...] / l_ref[...]
    o = (o_all[:, :tq] - lam * o_all[:, tq:]).T
    ms = jnp.mean(o * o, axis=-1, keepdims=True)
    o_ref[0] = (o * lax.rsqrt(ms + RMS_EPS) * gain_ref[...] * (1.0 - LAM_INIT_L0)).astype(o_ref.dtype)


def _attention(qt, kr, vt, lam_vecs, diff_gain, n_lat):
    b, t, _ = kr.shape
    tq = ATT_TQ
    hw = 2 * DIFF_HEAD_DIM
    return pl.pallas_call(
        functools.partial(_attn_kernel, t),
        grid=(b, DIFF_HEADS, n_lat // tq),
        in_specs=[pl.BlockSpec((1, hw, tq), lambda bi, h, i: (bi, h, i)),
                  pl.BlockSpec((1, t, hw), lambda bi, h, i: (bi, 0, h)),
                  pl.BlockSpec((1, hw, t), lambda bi, h, i: (bi, h, 0)),
                  pl.BlockSpec(lam_vecs.shape, lambda bi, h, i: (0, 0)),
                  pl.BlockSpec((1, hw), lambda bi, h, i: (0, 0))],
        out_specs=pl.BlockSpec((1, tq, hw), lambda bi, h, i: (bi, i, h)),
        out_shape=jax.ShapeDtypeStruct((b, n_lat, DIFF_HEADS * hw), BF16),
        scratch_shapes=[pltpu.VMEM((hw, 2 * tq), BF16),
                        pltpu.VMEM((ATT_TK, 2 * tq), F32),
                        pltpu.VMEM((SUBLANES, 2 * tq), F32),
                        pltpu.VMEM((ATT_TK, 2 * tq), BF16),
                        pltpu.VMEM((1, 2 * tq), F32),
                        pltpu.VMEM((1, 2 * tq), F32), pltpu.VMEM((1, 2 * tq), F32),
                        pltpu.VMEM((hw, 2 * tq), F32)],
        compiler_params=_cparams(("parallel", "parallel", "arbitrary")),
        name="attn",
    )(qt, kr, vt, lam_vecs, diff_gain)


def _merge_kernel(d, hf_ref, hb_ref, mo_ref, ga_ref, gb_ref, hB_ref, x_ref, mod_ref, gain_ref,
                  wa_ref, wb_ref, wo_ref, o_ref):
    gain = gain_ref[...]
    parts = []
    for h in range(MLSTM_HEADS):
        sl = slice(h * MLSTM_DV, (h + 1) * MLSTM_DV)
        hs = hf_ref[0, :, sl].astype(F32) + hb_ref[0, :, sl].astype(F32)
        ms = jnp.mean(hs * hs, axis=-1, keepdims=True)
        hn = hs * lax.rsqrt(ms + RMS_EPS) * gain[:, sl]
        parts.append((_sigmoid(mo_ref[0, :, sl].astype(F32)) * hn).astype(BF16))
    h_a = jnp.concatenate(parts, axis=1)
    ya = jnp.dot(h_a, wa_ref[...], preferred_element_type=F32)
    yb = jnp.dot(hB_ref[0], wb_ref[...], preferred_element_type=F32)
    y = _sigmoid(ga_ref[0].astype(F32)) * ya + _sigmoid(gb_ref[0].astype(F32)) * yb
    yo = jnp.dot(y.astype(BF16), wo_ref[...], preferred_element_type=F32)
    g1 = mod_ref[0][:, 2 * d:3 * d]
    o_ref[0] = x_ref[0] + g1 * yo


def _merge(hf, hb, p, h_b, x, mods3, mlstm_gain, wa, wb, wo, mo_region, ga_region, gb_region):
    b, s, d = x.shape
    tm = TOK_TILE
    wspec = lambda shape: pl.BlockSpec(shape, lambda bi, i: (0, 0), pipeline_mode=pl.Buffered(1))
    tok = lambda region: pl.BlockSpec((1, tm, d), lambda bi, i, r=region: (bi, i, r))
    return pl.pallas_call(
        functools.partial(_merge_kernel, d),
        grid=(b, s // tm),
        in_specs=[tok(0), tok(0), tok(mo_region), tok(ga_region), tok(gb_region), tok(0), tok(0),
                  pl.BlockSpec((1, 1, N_MOD * d), lambda bi, i: (bi, 0, 0)),
                  pl.BlockSpec((1, d), lambda bi, i: (0, 0)),
                  wspec(wa.shape), wspec(wb.shape), wspec(wo.shape)],
        out_specs=pl.BlockSpec((1, tm, d), lambda bi, i: (bi, i, 0)),
        out_shape=jax.ShapeDtypeStruct((b, s, d), F32),
        compiler_params=_cparams(("parallel", "arbitrary")),
        name="merge",
    )(hf, hb, p, p, p, h_b, x, mods3, mlstm_gain, wa, wb, wo)


def _ffn_kernel(d, hidden, x_ref, mod_ref, g_ref, wi_ref, wo_ref, o_ref):
    x1 = x_ref[0]
    mod = mod_ref[0]
    xn = _rms_mod(x1, g_ref[...], mod[:, 3 * d:4 * d], mod[:, 4 * d:5 * d]).astype(BF16)
    a = jnp.dot(xn, wi_ref[:, :hidden], preferred_element_type=F32)
    bb = jnp.dot(xn, wi_ref[:, hidden:], preferred_element_type=F32)
    hmid = (_silu(a) * bb).astype(BF16)
    y = jnp.dot(hmid, wo_ref[...], preferred_element_type=F32)
    o_ref[0] = x1 + mod[:, 5 * d:6 * d] * y


def _ffn(x1, mods3, norm2, wi, wo):
    b, s, d = x1.shape
    hidden = wo.shape[0]
    tm = TOK_TILE
    wspec = lambda shape: pl.BlockSpec(shape, lambda bi, i: (0, 0), pipeline_mode=pl.Buffered(1))
    return pl.pallas_call(
        functools.partial(_ffn_kernel, d, hidden),
        grid=(b, s // tm),
        in_specs=[pl.BlockSpec((1, tm, d), lambda bi, i: (bi, i, 0)),
                  pl.BlockSpec((1, 1, N_MOD * d), lambda bi, i: (bi, 0, 0)),
                  pl.BlockSpec((1, d), lambda bi, i: (0, 0)),
                  wspec(wi.shape), wspec(wo.shape)],
        out_specs=pl.BlockSpec((1, tm, d), lambda bi, i: (bi, i, 0)),
        out_shape=jax.ShapeDtypeStruct((b, s, d), F32),
        compiler_params=_cparams(("parallel", "arbitrary")),
        name="ffn",
    )(x1, mods3, norm2, wi, wo)


def _diff_lane_perm():
    l = np.arange(2 * DIFF_HEAD_DIM)
    half, r = l // DIFF_HEAD_DIM, l % DIFF_HEAD_DIM
    qsel, axis, f = r // 32, (r % 32) // ROPE_FREQS, r % ROPE_FREQS
    return qsel * DIFF_HEAD_DIM + axis * 2 * ROPE_FREQS + half * ROPE_FREQS + f, qsel, axis, f, half


def _rope_tables(n_lat, n_ctx):
    _, _, axis, f, half = _diff_lane_perm()
    pos = jnp.arange(n_lat, dtype=F32)
    row = jnp.floor(pos / GRID_W)
    col = pos - row * GRID_W
    inv = ROPE_BASE ** (-jnp.arange(ROPE_FREQS, dtype=F32) / ROPE_FREQS)
    ang = jnp.where(axis[None, :] == 0, row[:, None], col[:, None]) * inv[f][None, :]
    cos = jnp.concatenate([jnp.cos(ang), jnp.ones((n_ctx, LANES), F32)], axis=0)
    sin = jnp.sin(ang) * jnp.where(half == 0, -1.0, 1.0)[None, :]
    sin = jnp.concatenate([sin, jnp.zeros((n_ctx, LANES), F32)], axis=0)
    return cos, sin


def kernel(x, c, ctx, c_ctx, w_mod, b_mod, norm1, norm2, w_in, b_gate, conv_w, conv_b, mlstm_norm, q_norm, k_norm,
           lam_vecs, diff_norm, w_branch_a, w_branch_b, w_out, w_ffn_in, w_ffn_out):
    b, s, d = x.shape
    n_ctx = ctx.shape[1]
    assert w_mod.shape[0] == 1, "single-layer block"
    assert s % CHUNK == 0 and n_ctx % CHUNK == 0 and s % ATT_TQ == 0
    qk_w = MLSTM_HEADS * MLSTM_DQK
    v_w = MLSTM_HEADS * MLSTM_DV
    da_w = DIFF_HEADS * 2 * DIFF_HEAD_DIM
    n_gates = 4 * MLSTM_HEADS
    sizes = (qk_w, qk_w, v_w, v_w, n_gates, da_w, da_w, da_w, d, d)
    offs = np.concatenate([[0], np.cumsum(sizes)])
    seg = lambda i: w_in[0][:, offs[i]:offs[i + 1]]

    perm, qsel, _, _, _ = _diff_lane_perm()
    head_perm = (np.arange(DIFF_HEADS)[:, None] * 2 * DIFF_HEAD_DIM + perm[None, :]).reshape(-1)
    wbig = jnp.concatenate([seg(0), seg(1), seg(2), seg(3), seg(5)[:, head_perm], seg(6)[:, head_perm],
                            seg(7), seg(8), seg(9)], axis=1).astype(BF16)
    R_QK, R_V, R_O, R_DQ, R_DK, R_DV, R_GA, R_GB = range(8)
    wg = jnp.pad(seg(4), ((0, 0), (0, LANES - n_gates)))
    gate_bias = jnp.pad(b_gate[0], (0, LANES - n_gates)).reshape(1, LANES)
    conv_w8 = jnp.pad(conv_w[0], ((0, SUBLANES - CONV_K), (0, 0)))
    gq = q_norm[0][perm % DIFF_HEAD_DIM].reshape(1, LANES)
    gk = k_norm[0][perm % DIFF_HEAD_DIM].reshape(1, LANES)
    grp = jnp.asarray(qsel[:, None] == qsel[None, :], BF16)
    cos_t, sin_t = _rope_tables(s, n_ctx)

    cc = jnp.concatenate([c, c_ctx[None, :], jnp.zeros((SUBLANES - b - 1, d), F32)], axis=0)
    mods = _modulation(cc, w_mod[0], b_mod[0])
    mods3 = mods.reshape(SUBLANES, 1, N_MOD * d)

    p, gates = _projection(x, ctx, mods3, norm1, wbig, wg)
    qk = _conv_silu(p, conv_w8, conv_b, s, n_ctx)
    hf, hb = _mlstm(qk, p, gates, gate_bias, s, n_ctx, R_V)
    qt, kr, vt = _qkprep(p, cos_t, sin_t, gq, gk, grp, R_DQ, R_DK, R_DV)
    h_b = _attention(qt, kr, vt, lam_vecs[0], diff_norm, s)
    x1 = _merge(hf, hb, p, h_b, x, mods3, mlstm_norm, w_branch_a[0].astype(BF16), w_branch_b[0].astype(BF16),
                w_out[0].astype(BF16), R_O, R_GA, R_GB)
    return _ffn(x1, mods3, norm2, w_ffn_in[0].astype(BF16), w_ffn_out[0].astype(BF16))
```

```python
import functools
import math

import jax
import jax.numpy as jnp
import numpy as np
from jax import lax
from jax.experimental import pallas as pl
from jax.experimental.pallas import tpu as pltpu

F32 = jnp.float32
BF16 = jnp.bfloat16
HIGHEST = lax.Precision.HIGHEST

GRID_W = 64
N_MOD = 6
MLSTM_HEADS = 4
MLSTM_DQK = 128
MLSTM_DV = 256
CONV_K = 5
DIFF_HEADS = 8
DIFF_HEAD_DIM = 64
ROPE_BASE = 10000.0
ROPE_FREQS = DIFF_HEAD_DIM // 4
RMS_EPS = 1e-6
LAM_INIT_L0 = 0.8 - 0.6 * math.exp(-0.3 * 0)
Q_SCALE_LOG2 = DIFF_HEAD_DIM ** -0.5 * math.log2(math.e)
V_ROWS = 2 * DIFF_HEAD_DIM + 16

LANES = 128
SUBLANES = 8
VMEM_LIMIT = 56 * 1024 * 1024

TOK_TILE = 256
CHUNK = 256
ATT_TQ = 512
ATT_TK = 2816


def _cparams(sem):
    return pltpu.CompilerParams(dimension_semantics=sem, vmem_limit_bytes=VMEM_LIMIT)


def _sigmoid(x):
    return 1.0 / (1.0 + jnp.exp(-x))


def _silu(x):
    return x * _sigmoid(x)


def _log_sigmoid(x):
    return jnp.minimum(x, 0.0) - jnp.log(1.0 + jnp.exp(-jnp.abs(x)))


def _mod_kernel(c_ref, w_ref, b_ref, o_ref):
    a = _silu(c_ref[...])
    o_ref[...] = jnp.dot(a, w_ref[...], preferred_element_type=F32, precision=HIGHEST) + b_ref[...]


def _modulation(cc, w_mod, b_mod):
    rows, d = cc.shape
    n = w_mod.shape[1]
    tn = 1024
    return pl.pallas_call(
        _mod_kernel,
        grid=(n // tn,),
        in_specs=[pl.BlockSpec((rows, d), lambda j: (0, 0)),
                  pl.BlockSpec((d, tn), lambda j: (0, j)),
                  pl.BlockSpec((1, tn), lambda j: (0, j))],
        out_specs=pl.BlockSpec((rows, tn), lambda j: (0, j)),
        out_shape=jax.ShapeDtypeStruct((rows, n), F32),
        compiler_params=_cparams(("arbitrary",)),
        name="mod",
    )(cc, w_mod, b_mod.reshape(1, n))


def _rms_mod(xf, g, shift, scale):
    ms = jnp.mean(xf * xf, axis=-1, keepdims=True)
    return xf * lax.rsqrt(ms + RMS_EPS) * g * (1.0 + scale) + shift


def _proj_kernel(n_lat_tiles, d, x_ref, ctx_ref, mod_ref, g_ref, w_ref, wg_ref, p_ref, gates_ref):
    i = pl.program_id(1)
    xf = jnp.where(i < n_lat_tiles, x_ref[0], ctx_ref[0])
    mod = mod_ref[0]
    xn = _rms_mod(xf, g_ref[...], mod[:, 0:d], mod[:, d:2 * d])
    gates_ref[0] = jnp.dot(xn, wg_ref[...], preferred_element_type=F32, precision=HIGHEST)
    xb = xn.astype(BF16)
    n_out = w_ref.shape[1]
    for j in range(n_out // d):
        p_ref[0, :, j * d:(j + 1) * d] = jnp.dot(
            xb, w_ref[:, j * d:(j + 1) * d], preferred_element_type=F32).astype(BF16)


def _projection(x, ctx, mods3, norm1, wbig, wg):
    b, s, d = x.shape
    n_ctx = ctx.shape[1]
    tm = TOK_TILE
    n_lat_tiles = s // tm
    n_tiles = (s + n_ctx) // tm
    n_out = wbig.shape[1]
    return pl.pallas_call(
        functools.partial(_proj_kernel, n_lat_tiles, d),
        grid=(b, n_tiles),
        in_specs=[
            pl.BlockSpec((1, tm, d), lambda bi, i: (bi, jnp.minimum(i, n_lat_tiles - 1), 0)),
            pl.BlockSpec((1, tm, d), lambda bi, i: (bi, jnp.maximum(i - n_lat_tiles, 0), 0)),
            pl.BlockSpec((1, 1, N_MOD * d), lambda bi, i: (jnp.where(i < n_lat_tiles, bi, b), 0, 0)),
            pl.BlockSpec((1, d), lambda bi, i: (0, 0)),
            pl.BlockSpec((d, n_out), lambda bi, i: (0, 0), pipeline_mode=pl.Buffered(1)),
            pl.BlockSpec((d, LANES), lambda bi, i: (0, 0)),
        ],
        out_specs=[pl.BlockSpec((1, tm, n_out), lambda bi, i: (bi, i, 0)),
                   pl.BlockSpec((1, tm, LANES), lambda bi, i: (bi, i, 0))],
        out_shape=[jax.ShapeDtypeStruct((b, s + n_ctx, n_out), BF16),
                   jax.ShapeDtypeStruct((b, s + n_ctx, LANES), F32)],
        compiler_params=_cparams(("parallel", "arbitrary")),
        name="proj",
    )(x, ctx, mods3, norm1, wbig, wg)


def _conv_kernel(n_lat, n_ctx, n_qcols, x_ref, w_ref, b_ref, o_ref, buf_ref):
    j = pl.program_id(1)
    r = TOK_TILE
    gap = 2 * SUBLANES
    lat0 = SUBLANES
    ctx0 = SUBLANES + n_lat + gap
    zeros8 = jnp.zeros((SUBLANES, LANES), F32)
    buf_ref[0:SUBLANES, :] = zeros8
    buf_ref[lat0 + n_lat:lat0 + n_lat + SUBLANES, :] = zeros8
    buf_ref[lat0 + n_lat + SUBLANES:ctx0, :] = zeros8
    buf_ref[ctx0 + n_ctx:ctx0 + n_ctx + SUBLANES, :] = zeros8
    n_lat_tiles = n_lat // r
    n_tiles = (n_lat + n_ctx) // r

    def fill(t, carry):
        src = pl.multiple_of(t * r, r)
        dst = pl.multiple_of(SUBLANES + t * r + jnp.where(t >= n_lat_tiles, gap, 0), SUBLANES)
        buf_ref[pl.ds(dst, r), :] = x_ref[0, pl.ds(src, r), :].astype(F32)
        return carry

    lax.fori_loop(0, n_tiles, fill, 0)
    w = w_ref[...]
    bias = b_ref[...]
    out_scale = jnp.where(j < n_qcols, MLSTM_DQK ** -0.5, 1.0).astype(F32)
    ext_rows = r + 2 * SUBLANES

    def tile(t, carry):
        dst = pl.multiple_of(t * r, r)
        base = pl.multiple_of(t * r + jnp.where(t >= n_lat_tiles, gap, 0), SUBLANES)
        ext = buf_ref[pl.ds(base, ext_rows), :]
        acc = jnp.zeros((r, LANES), F32) + bias
        for k in range(CONV_K):
            shift = (CONV_K // 2 - k) % ext_rows
            sh = ext if shift == 0 else pltpu.roll(ext, shift, axis=0)
            acc = acc + sh[SUBLANES:SUBLANES + r, :] * w[k:k + 1, :]
        o_ref[0, pl.ds(dst, r), :] = (_silu(acc) * out_scale).astype(BF16)
        return carry

    lax.fori_loop(0, n_tiles, tile, 0)


def _conv_silu(p, conv_w8, conv_b, n_lat, n_ctx):
    b, t, _ = p.shape
    n_cols = conv_w8.shape[1] // LANES
    buf_rows = n_lat + n_ctx + 4 * SUBLANES
    return pl.pallas_call(
        functools.partial(_conv_kernel, n_lat, n_ctx, n_cols // 2),
        grid=(b, n_cols),
        in_specs=[pl.BlockSpec((1, t, LANES), lambda bi, j: (bi, 0, j)),
                  pl.BlockSpec((SUBLANES, LANES), lambda bi, j: (0, j)),
                  pl.BlockSpec((1, LANES), lambda bi, j: (0, j))],
        out_specs=pl.BlockSpec((1, t, LANES), lambda bi, j: (bi, 0, j)),
        out_shape=jax.ShapeDtypeStruct((b, t, n_cols * LANES), BF16),
        scratch_shapes=[pltpu.VMEM((buf_rows, LANES), F32)],
        compiler_params=_cparams(("parallel", "arbitrary")),
        name="conv",
    )(p, conv_w8, conv_b)


def _mlstm_direction(reverse, q, k, v_aug, ig_col, ig_row, lf_cum_col, lf_cum_row, lf_total, c_ref, m_ref, hd):
    l = q.shape[0]
    dv = v_aug.shape[1] - LANES
    m_prev = m_ref[hd:hd + 1, 0:1]
    c_aug = c_ref[hd]
    a_col = ig_col - lf_cum_col
    a_row = ig_row - lf_cum_row
    rows = lax.broadcasted_iota(jnp.int32, (l, l), 0)
    cols = lax.broadcasted_iota(jnp.int32, (l, l), 1)
    visible = (cols >= rows) if reverse else (cols <= rows)
    a_mat = jnp.where(visible, a_row, -jnp.inf)
    m_row = jnp.maximum(jnp.max(a_mat, axis=-1, keepdims=True), m_prev)
    decay = jnp.exp(a_mat - m_row)
    s = lax.dot_general(q, k, (((1,), (1,)), ((), ())), preferred_element_type=F32)
    pmat = (s * decay).astype(BF16)
    intra = jnp.dot(pmat, v_aug, preferred_element_type=F32)
    inter = jnp.dot(q, c_aug.astype(BF16), preferred_element_type=F32)
    numden = intra + jnp.exp(m_prev - m_row) * inter
    num = numden[:, :dv]
    den = numden[:, dv:dv + 1]
    floor = jnp.exp(-(lf_cum_col + m_row))
    h = num / jnp.maximum(jnp.abs(den), floor)
    m_last = jnp.maximum(jnp.max(a_col, axis=0, keepdims=True), m_prev)
    w = jnp.exp(a_col - m_last)
    wv = (w * v_aug.astype(F32)).astype(BF16)
    kv = lax.dot_general(k, wv, (((0,), (0,)), ((), ())), preferred_element_type=F32)
    c_ref[hd] = jnp.exp(m_prev - m_last) * c_aug + kv
    m_ref[hd:hd + 1, :] = jnp.broadcast_to(lf_total + m_last, (1, LANES))
    return h


def _mlstm_kernel(qf_ref, kf_ref, vf_ref, gf_ref, qb_ref, kb_ref, vb_ref, gb_ref, bias_ref,
                  hf_ref, hb_ref, c_ref, m_ref):
    s_idx = pl.program_id(1)

    @pl.when(s_idx == 0)
    def _():
        c_ref[...] = jnp.zeros_like(c_ref)
        m_ref[...] = jnp.zeros_like(m_ref)

    l = qf_ref.shape[1]
    nh = MLSTM_HEADS
    rows = lax.broadcasted_iota(jnp.int32, (l, l), 0)
    cols = lax.broadcasted_iota(jnp.int32, (l, l), 1)
    ones_col = jnp.ones((l, LANES), BF16)
    for reverse, q_ref, k_ref, v_ref, g_ref, h_ref in (
            (False, qf_ref, kf_ref, vf_ref, gf_ref, hf_ref),
            (True, qb_ref, kb_ref, vb_ref, gb_ref, hb_ref)):
        g = g_ref[0] + bias_ref[...]
        lf = _log_sigmoid(g)
        tri = jnp.where((cols >= rows) if reverse else (cols <= rows), 1.0, 0.0).astype(F32)
        lf_cum = jnp.dot(tri, lf, preferred_element_type=F32, precision=HIGHEST)
        g_t = g.T
        lf_cum_t = lf_cum.T
        base = 2 * nh if reverse else 0
        for h in range(nh):
            ci, cf = base + h, base + nh + h
            hd = (nh if reverse else 0) + h
            lf_cum_col = lf_cum[:, cf:cf + 1]
            lf_total = lf_cum[0:1, cf:cf + 1] if reverse else lf_cum[l - 1:l, cf:cf + 1]
            q = q_ref[0, :, h * MLSTM_DQK:(h + 1) * MLSTM_DQK]
            k = k_ref[0, :, h * MLSTM_DQK:(h + 1) * MLSTM_DQK]
            v_aug = jnp.concatenate([v_ref[0, :, h * MLSTM_DV:(h + 1) * MLSTM_DV], ones_col], axis=1)
            hval = _mlstm_direction(reverse, q, k, v_aug, g[:, ci:ci + 1], g_t[ci:ci + 1, :],
                                    lf_cum_col, lf_cum_t[cf:cf + 1, :], lf_total, c_ref, m_ref, hd)
            h_ref[0, :, h * MLSTM_DV:(h + 1) * MLSTM_DV] = hval.astype(h_ref.dtype)


def _mlstm(qk, p, gates, gate_bias, n_lat, n_ctx, v_region):
    b, t, _ = qk.shape
    l = CHUNK
    n_lat_c = n_lat // l
    n_ctx_c = n_ctx // l
    n_steps = n_lat_c + n_ctx_c
    qw = MLSTM_HEADS * MLSTM_DQK
    vw = MLSTM_HEADS * MLSTM_DV

    def fwd(si):
        return jnp.where(si < n_ctx_c, n_lat_c + si, si - n_ctx_c)

    def bwd(si):
        return jnp.where(si < n_ctx_c, n_steps - 1 - si, n_steps - 1 - si)

    in_specs = []
    for order in (fwd, bwd):
        in_specs += [
            pl.BlockSpec((1, l, qw), lambda bi, si, o=order: (bi, o(si), 0)),
            pl.BlockSpec((1, l, qw), lambda bi, si, o=order: (bi, o(si), 1)),
            pl.BlockSpec((1, l, vw), lambda bi, si, o=order: (bi, o(si), v_region)),
            pl.BlockSpec((1, l, LANES), lambda bi, si, o=order: (bi, o(si), 0)),
        ]
    in_specs.append(pl.BlockSpec((1, LANES), lambda bi, si: (0, 0)))
    out_specs = [pl.BlockSpec((1, l, vw), lambda bi, si: (bi, fwd(si), 0)),
                 pl.BlockSpec((1, l, vw), lambda bi, si: (bi, bwd(si), 0))]
    return pl.pallas_call(
        _mlstm_kernel,
        grid=(b, n_steps),
        in_specs=in_specs,
        out_specs=out_specs,
        out_shape=[jax.ShapeDtypeStruct((b, t, vw), BF16)] * 2,
        scratch_shapes=[pltpu.VMEM((2 * MLSTM_HEADS, MLSTM_DQK, MLSTM_DV + LANES), F32),
                        pltpu.VMEM((2 * MLSTM_HEADS, LANES), F32)],
        compiler_params=_cparams(("parallel", "arbitrary")),
        name="mlstm",
    )(qk, qk, p, gates, qk, qk, p, gates, gate_bias)


def _qkprep_kernel(q_ref, k_ref, v_ref, cos_ref, sin_ref, gq_ref, gk_ref, grp_ref, qo_ref, ko_ref, vo_ref):
    cos = cos_ref[...]
    sin = sin_ref[...]
    grp = grp_ref[...]
    hw = 2 * DIFF_HEAD_DIM

    def normed_rotated(xh, gain):
        ss = jnp.dot((xh * xh).astype(BF16), grp, preferred_element_type=F32)
        y = xh * lax.rsqrt(ss * (1.0 / DIFF_HEAD_DIM) + RMS_EPS) * gain
        return y * cos + pltpu.roll(y, DIFF_HEAD_DIM, axis=1) * sin

    gq = gq_ref[...]
    gk = gk_ref[...]
    for h in range(DIFF_HEADS):
        sl = slice(h * hw, (h + 1) * hw)
        qo_ref[0, sl, :] = (normed_rotated(q_ref[0, :, sl].astype(F32), gq) * Q_SCALE_LOG2).T.astype(BF16)
        ko_ref[0, :, sl] = normed_rotated(k_ref[0, :, sl].astype(F32), gk).astype(BF16)
        vo_ref[0, h * V_ROWS:h * V_ROWS + hw, :] = v_ref[0, :, sl].astype(F32).T.astype(BF16)
        vo_ref[0, h * V_ROWS + hw:(h + 1) * V_ROWS, :] = jnp.ones((V_ROWS - hw, v_ref.shape[1]), BF16)


def _qkprep(p, cos_t, sin_t, gq, gk, grp, q_region, k_region, v_region):
    b, t, _ = p.shape
    tm = TOK_TILE
    w = DIFF_HEADS * 2 * DIFF_HEAD_DIM
    tok = lambda region: pl.BlockSpec((1, tm, w), lambda bi, i, r=region: (bi, i, r))
    tok_t = pl.BlockSpec((1, w, tm), lambda bi, i: (bi, 0, i))
    row = pl.BlockSpec((1, LANES), lambda bi, i: (0, 0))
    return pl.pallas_call(
        _qkprep_kernel,
        grid=(b, t // tm),
        in_specs=[tok(q_region), tok(k_region), tok(v_region),
                  pl.BlockSpec((tm, LANES), lambda bi, i: (i, 0)),
                  pl.BlockSpec((tm, LANES), lambda bi, i: (i, 0)),
                  row, row, pl.BlockSpec((LANES, LANES), lambda bi, i: (0, 0))],
        out_specs=[tok_t, tok(0), pl.BlockSpec((1, DIFF_HEADS * V_ROWS, tm), lambda bi, i: (bi, 0, i))],
        out_shape=[jax.ShapeDtypeStruct((b, w, t), BF16), jax.ShapeDtypeStruct((b, t, w), BF16),
                   jax.ShapeDtypeStruct((b, DIFF_HEADS * V_ROWS, t), BF16)],
        compiler_params=_cparams(("parallel", "arbitrary")),
        name="qkprep",
    )(p, p, p, cos_t, sin_t, gq, gk, grp)


def _attn_kernel(n_keys, tq, qt_ref, k_ref, vt_ref, lam_ref, gain_ref, o_ref,
                 qq_ref, s_ref, pm_ref, e_ref, al_ref, m_ref, acc_ref):
    tk = ATT_TK
    n = n_keys // tk
    nq = qt_ref.shape[2] // tq
    width = 2 * tq
    row = lax.broadcasted_iota(jnp.int32, (qt_ref.shape[1], tq), 0)
    first = (row % DIFF_HEAD_DIM) < (DIFF_HEAD_DIM // 2)

    def mask_queries(qb, carry):
        qt = qt_ref[0, :, pl.ds(pl.multiple_of(qb * tq, tq), tq)]
        zero = jnp.zeros_like(qt)
        qq_ref[qb] = jnp.concatenate([jnp.where(first, qt, zero), jnp.where(first, zero, qt)], axis=1)
        return carry

    lax.fori_loop(0, nq, mask_queries, 0)
    acc_ref[...] = jnp.zeros_like(acc_ref)

    def scores(j, qb):
        off = pl.multiple_of(j * tk, tk)
        s = jnp.dot(k_ref[0, pl.ds(off, tk), :], qq_ref[qb], preferred_element_type=F32)
        s_ref[...] = s
        pm_ref[...] = jnp.max(s.reshape(tk // SUBLANES, SUBLANES, width), axis=0)

    def softmax(j, qb):
        m_old = jnp.where(j == 0, -jnp.inf, m_ref[...])
        m_new = jnp.maximum(m_old, jnp.max(pm_ref[...], axis=0, keepdims=True))
        alpha = jnp.exp2(m_old - m_new)
        e = jnp.exp2(s_ref[...] - m_new)
        e_ref[...] = e.astype(BF16)
        al_ref[...] = alpha
        m_ref[...] = m_new

    def accumulate(j):
        off = pl.multiple_of(j * tk, tk)
        pv = jnp.dot(vt_ref[0, :, pl.ds(off, tk)], e_ref[...], preferred_element_type=F32)
        acc_ref[...] = al_ref[...] * acc_ref[...] + pv

    def finalize(qb):
        lv = lam_ref[...]
        lam = (jnp.exp(jnp.sum(lv[0:1] * lv[1:2], axis=-1, keepdims=True))
               - jnp.exp(jnp.sum(lv[2:3] * lv[3:4], axis=-1, keepdims=True)) + LAM_INIT_L0)
        hw = o_ref.shape[2]
        o_all = acc_ref[0:hw, :] / acc_ref[hw:hw + 1, :]
        o = (o_all[:, :tq] - lam * o_all[:, tq:]).T
        ms = jnp.mean(o * o, axis=-1, keepdims=True)
        out = o * lax.rsqrt(ms + RMS_EPS) * gain_ref[...] * (1.0 - LAM_INIT_L0)
        o_ref[0, pl.ds(pl.multiple_of(qb * tq, tq), tq), :] = out.astype(o_ref.dtype)

    def advance(j, qb):
        wrap = j == n - 1
        return jnp.where(wrap, 0, j + 1), jnp.where(wrap, qb + 1, qb)

    zero_i = jnp.int32(0)
    item0 = (zero_i, zero_i)
    item1 = advance(*item0)
    scores(*item0)
    softmax(*item0)
    scores(*item1)

    def iteration(t, carry):
        ja, qa, jm, qm, js, qs = carry
        accumulate(ja)
        softmax(jm, qm)
        scores(js, qs)

        @pl.when(ja == n - 1)
        def _():
            finalize(qa)

        return (jm, qm, js, qs) + advance(js, qs)

    item2 = advance(*item1)
    lax.fori_loop(2, nq * n, iteration, item0 + item1 + item2)
    accumulate(n - 2)
    softmax(n - 1, nq - 1)
    accumulate(n - 1)
    finalize(nq - 1)


def _attention(qt, kr, vt, lam_vecs, diff_gain, n_lat):
    b, t, _ = kr.shape
    tq = ATT_TQ
    hw = 2 * DIFF_HEAD_DIM
    nq = n_lat // tq
    assert t // ATT_TK >= 2
    return pl.pallas_call(
        functools.partial(_attn_kernel, t, tq),
        grid=(b, DIFF_HEADS),
        in_specs=[pl.BlockSpec((1, hw, n_lat), lambda bi, h: (bi, h, 0)),
                  pl.BlockSpec((1, t, hw), lambda bi, h: (bi, 0, h)),
                  pl.BlockSpec((1, V_ROWS, t), lambda bi, h: (bi, h, 0)),
                  pl.BlockSpec(lam_vecs.shape, lambda bi, h: (0, 0)),
                  pl.BlockSpec((1, hw), lambda bi, h: (0, 0))],
        out_specs=pl.BlockSpec((1, n_lat, hw), lambda bi, h: (bi, 0, h)),
        out_shape=jax.ShapeDtypeStruct((b, n_lat, DIFF_HEADS * hw), BF16),
        scratch_shapes=[pltpu.VMEM((nq, hw, 2 * tq), BF16),
                        pltpu.VMEM((ATT_TK, 2 * tq), F32),
                        pltpu.VMEM((SUBLANES, 2 * tq), F32),
                        pltpu.VMEM((ATT_TK, 2 * tq), BF16),
                        pltpu.VMEM((1, 2 * tq), F32),
                        pltpu.VMEM((1, 2 * tq), F32),
                        pltpu.VMEM((V_ROWS, 2 * tq), F32)],
        compiler_params=_cparams(("parallel", "arbitrary")),
        name="attn",
    )(qt, kr, vt, lam_vecs, diff_gain)


def _merge_kernel(d, hf_ref, hb_ref, mo_ref, ga_ref, gb_ref, hB_ref, x_ref, mod_ref, gain_ref,
                  wa_ref, wb_ref, wo_ref, o_ref):
    gain = gain_ref[...]
    parts = []
    for h in range(MLSTM_HEADS):
        sl = slice(h * MLSTM_DV, (h + 1) * MLSTM_DV)
        hs = hf_ref[0, :, sl].astype(F32) + hb_ref[0, :, sl].astype(F32)
        ms = jnp.mean(hs * hs, axis=-1, keepdims=True)
        hn = hs * lax.rsqrt(ms + RMS_EPS) * gain[:, sl]
        parts.append((_sigmoid(mo_ref[0, :, sl].astype(F32)) * hn).astype(BF16))
    h_a = jnp.concatenate(parts, axis=1)
    ya = jnp.dot(h_a, wa_ref[...], preferred_element_type=F32)
    yb = jnp.dot(hB_ref[0], wb_ref[...], preferred_element_type=F32)
    y = _sigmoid(ga_ref[0].astype(F32)) * ya + _sigmoid(gb_ref[0].astype(F32)) * yb
    yo = jnp.dot(y.astype(BF16), wo_ref[...], preferred_element_type=F32)
    g1 = mod_ref[0][:, 2 * d:3 * d]
    o_ref[0] = x_ref[0] + g1 * yo


def _merge(hf, hb, p, h_b, x, mods3, mlstm_gain, wa, wb, wo, mo_region, ga_region, gb_region):
    b, s, d = x.shape
    tm = TOK_TILE
    wspec = lambda shape: pl.BlockSpec(shape, lambda bi, i: (0, 0), pipeline_mode=pl.Buffered(1))
    tok = lambda region: pl.BlockSpec((1, tm, d), lambda bi, i, r=region: (bi, i, r))
    return pl.pallas_call(
        functools.partial(_merge_kernel, d),
        grid=(b, s // tm),
        in_specs=[tok(0), tok(0), tok(mo_region), tok(ga_region), tok(gb_region), tok(0), tok(0),
                  pl.BlockSpec((1, 1, N_MOD * d), lambda bi, i: (bi, 0, 0)),
                  pl.BlockSpec((1, d), lambda bi, i: (0, 0)),
                  wspec(wa.shape), wspec(wb.shape), wspec(wo.shape)],
        out_specs=pl.BlockSpec((1, tm, d), lambda bi, i: (bi, i, 0)),
        out_shape=jax.ShapeDtypeStruct((b, s, d), F32),
        compiler_params=_cparams(("parallel", "arbitrary")),
        name="merge",
    )(hf, hb, p, p, p, h_b, x, mods3, mlstm_gain, wa, wb, wo)


def _ffn_kernel(d, hidden, x_ref, mod_ref, g_ref, wi_ref, wo_ref, o_ref):
    x1 = x_ref[0]
    mod = mod_ref[0]
    xn = _rms_mod(x1, g_ref[...], mod[:, 3 * d:4 * d], mod[:, 4 * d:5 * d]).astype(BF16)
    a = jnp.dot(xn, wi_ref[:, :hidden], preferred_element_type=F32)
    bb = jnp.dot(xn, wi_ref[:, hidden:], preferred_element_type=F32)
    hmid = (_silu(a) * bb).astype(BF16)
    y = jnp.dot(hmid, wo_ref[...], preferred_element_type=F32)
    o_ref[0] = x1 + mod[:, 5 * d:6 * d] * y


def _ffn(x1, mods3, norm2, wi, wo):
    b, s, d = x1.shape
    hidden = wo.shape[0]
    tm = TOK_TILE
    wspec = lambda shape: pl.BlockSpec(shape, lambda bi, i: (0, 0), pipeline_mode=pl.Buffered(1))
    return pl.pallas_call(
        functools.partial(_ffn_kernel, d, hidden),
        grid=(b, s // tm),
        in_specs=[pl.BlockSpec((1, tm, d), lambda bi, i: (bi, i, 0)),
                  pl.BlockSpec((1, 1, N_MOD * d), lambda bi, i: (bi, 0, 0)),
                  pl.BlockSpec((1, d), lambda bi, i: (0, 0)),
                  wspec(wi.shape), wspec(wo.shape)],
        out_specs=pl.BlockSpec((1, tm, d), lambda bi, i: (bi, i, 0)),
        out_shape=jax.ShapeDtypeStruct((b, s, d), F32),
        compiler_params=_cparams(("parallel", "arbitrary")),
        name="ffn",
    )(x1, mods3, norm2, wi, wo)


def _diff_lane_perm():
    l = np.arange(2 * DIFF_HEAD_DIM)
    half, r = l // DIFF_HEAD_DIM, l % DIFF_HEAD_DIM
    qsel, axis, f = r // 32, (r % 32) // ROPE_FREQS, r % ROPE_FREQS
    return qsel * DIFF_HEAD_DIM + axis * 2 * ROPE_FREQS + half * ROPE_FREQS + f, qsel, axis, f, half


def _rope_tables(n_lat, n_ctx):
    _, _, axis, f, half = _diff_lane_perm()
    pos = jnp.arange(n_lat, dtype=F32)
    row = jnp.floor(pos / GRID_W)
    col = pos - row * GRID_W
    inv = ROPE_BASE ** (-jnp.arange(ROPE_FREQS, dtype=F32) / ROPE_FREQS)
    ang = jnp.where(axis[None, :] == 0, row[:, None], col[:, None]) * inv[f][None, :]
    cos = jnp.concatenate([jnp.cos(ang), jnp.ones((n_ctx, LANES), F32)], axis=0)
    sin = jnp.sin(ang) * jnp.where(half == 0, -1.0, 1.0)[None, :]
    sin = jnp.concatenate([sin, jnp.zeros((n_ctx, LANES), F32)], axis=0)
    return cos, sin


def kernel(x, c, ctx, c_ctx, w_mod, b_mod, norm1, norm2, w_in, b_gate, conv_w, conv_b, mlstm_norm, q_norm, k_norm,
           lam_vecs, diff_norm, w_branch_a, w_branch_b, w_out, w_ffn_in, w_ffn_out):
    b, s, d = x.shape
    n_ctx = ctx.shape[1]
    assert w_mod.shape[0] == 1, "single-layer block"
    assert s % CHUNK == 0 and n_ctx % CHUNK == 0 and s % ATT_TQ == 0
    qk_w = MLSTM_HEADS * MLSTM_DQK
    v_w = MLSTM_HEADS * MLSTM_DV
    da_w = DIFF_HEADS * 2 * DIFF_HEAD_DIM
    n_gates = 4 * MLSTM_HEADS
    sizes = (qk_w, qk_w, v_w, v_w, n_gates, da_w, da_w, da_w, d, d)
    offs = np.concatenate([[0], np.cumsum(sizes)])
    seg = lambda i: w_in[0][:, offs[i]:offs[i + 1]]

    perm, qsel, _, _, _ = _diff_lane_perm()
    head_perm = (np.arange(DIFF_HEADS)[:, None] * 2 * DIFF_HEAD_DIM + perm[None, :]).reshape(-1)
    wbig = jnp.concatenate([seg(0), seg(1), seg(2), seg(3), seg(5)[:, head_perm], seg(6)[:, head_perm],
                            seg(7), seg(8), seg(9)], axis=1).astype(BF16)
    R_QK, R_V, R_O, R_DQ, R_DK, R_DV, R_GA, R_GB = range(8)
    wg = jnp.pad(seg(4), ((0, 0), (0, LANES - n_gates)))
    gate_bias = jnp.pad(b_gate[0], (0, LANES - n_gates)).reshape(1, LANES)
    conv_w8 = jnp.pad(conv_w[0], ((0, SUBLANES - CONV_K), (0, 0)))
    gq = q_norm[0][perm % DIFF_HEAD_DIM].reshape(1, LANES)
    gk = k_norm[0][perm % DIFF_HEAD_DIM].reshape(1, LANES)
    grp = jnp.asarray(qsel[:, None] == qsel[None, :], BF16)
    cos_t, sin_t = _rope_tables(s, n_ctx)

    cc = jnp.concatenate([c, c_ctx[None, :], jnp.zeros((SUBLANES - b - 1, d), F32)], axis=0)
    mods = _modulation(cc, w_mod[0], b_mod[0])
    mods3 = mods.reshape(SUBLANES, 1, N_MOD * d)

    p, gates = _projection(x, ctx, mods3, norm1, wbig, wg)
    qk = _conv_silu(p, conv_w8, conv_b, s, n_ctx)
    hf, hb = _mlstm(qk, p, gates, gate_bias, s, n_ctx, R_V)
    qt, kr, vt = _qkprep(p, cos_t, sin_t, gq, gk, grp, R_DQ, R_DK, R_DV)
    h_b = _attention(qt, kr, vt, lam_vecs[0], diff_norm, s)
    x1 = _merge(hf, hb, p, h_b, x, mods3, mlstm_norm, w_branch_a[0].astype(BF16), w_branch_b[0].astype(BF16),
                w_out[0].astype(BF16), R_O, R_GA, R_GB)
    return _ffn(x1, mods3, norm2, w_ffn_in[0].astype(BF16), w_ffn_out[0].astype(BF16))
```

```python
import functools
import math

import jax
import jax.numpy as jnp
import numpy as np
from jax import lax
from jax.experimental import pallas as pl
from jax.experimental.pallas import tpu as pltpu

F32 = jnp.float32
BF16 = jnp.bfloat16
HIGHEST = lax.Precision.HIGHEST

GRID_W = 64
N_MOD = 6
MLSTM_HEADS = 4
MLSTM_DQK = 128
MLSTM_DV = 256
CONV_K = 5
DIFF_HEADS = 8
DIFF_HEAD_DIM = 64
ROPE_BASE = 10000.0
ROPE_FREQS = DIFF_HEAD_DIM // 4
RMS_EPS = 1e-6
LAM_INIT_L0 = 0.8 - 0.6 * math.exp(-0.3 * 0)
Q_SCALE_LOG2 = DIFF_HEAD_DIM ** -0.5 * math.log2(math.e)
V_ROWS = 2 * DIFF_HEAD_DIM + 16

LANES = 128
SUBLANES = 8
VMEM_LIMIT = 56 * 1024 * 1024

TOK_TILE = 256
LAT_TILE = 512
CHUNK = 256
ATT_TQ = 512
ATT_TK = 2816


def _cparams(sem):
    return pltpu.CompilerParams(dimension_semantics=sem, vmem_limit_bytes=VMEM_LIMIT)


def _sigmoid(x):
    return 1.0 / (1.0 + jnp.exp(-x))


def _silu(x):
    return x * _sigmoid(x)


def _log_sigmoid(x):
    return jnp.minimum(x, 0.0) - jnp.log(1.0 + jnp.exp(-jnp.abs(x)))


def _mod_kernel(c_ref, w_ref, b_ref, o_ref):
    a = _silu(c_ref[...])
    o_ref[...] = jnp.dot(a, w_ref[...], preferred_element_type=F32, precision=HIGHEST) + b_ref[...]


def _modulation(cc, w_mod, b_mod):
    rows, d = cc.shape
    n = w_mod.shape[1]
    tn = 1024
    return pl.pallas_call(
        _mod_kernel,
        grid=(n // tn,),
        in_specs=[pl.BlockSpec((rows, d), lambda j: (0, 0)),
                  pl.BlockSpec((d, tn), lambda j: (0, j)),
                  pl.BlockSpec((1, tn), lambda j: (0, j))],
        out_specs=pl.BlockSpec((rows, tn), lambda j: (0, j)),
        out_shape=jax.ShapeDtypeStruct((rows, n), F32),
        compiler_params=_cparams(("arbitrary",)),
        name="mod",
    )(cc, w_mod, b_mod.reshape(1, n))


def _rms_mod(xf, g, shift, scale):
    ms = jnp.mean(xf * xf, axis=-1, keepdims=True)
    return xf * lax.rsqrt(ms + RMS_EPS) * g * (1.0 + scale) + shift


def _proj_kernel(n_lat_tiles, d, x_ref, ctx_ref, mod_ref, g_ref, w_ref, wg_ref, gbias_ref,
                 cos_ref, sin_ref, gq_ref, gk_ref, grp_ref,
                 qk_ref, pm_ref, qt_ref, kr_ref, vt_ref, gates_ref, gates_t_ref):
    i = pl.program_id(1)
    tm = x_ref.shape[1]
    xf = jnp.where(i < n_lat_tiles, x_ref[0], ctx_ref[0])
    mod = mod_ref[0]
    xn = _rms_mod(xf, g_ref[...], mod[:, 0:d], mod[:, d:2 * d])
    xb = xn.astype(BF16)

    def project(c0, width):
        return jnp.dot(xb, w_ref[:, c0:c0 + width], preferred_element_type=F32)

    qk_ref[0] = project(0, d).astype(BF16)
    for r in range(4):
        pm_ref[0, :, r * d:(r + 1) * d] = project((1 + r) * d, d).astype(BF16)

    cos = cos_ref[...]
    sin = sin_ref[...]
    grp = grp_ref[...]
    gq = gq_ref[...]
    gk = gk_ref[...]
    hw = 2 * DIFF_HEAD_DIM

    def scaled_sumsq(x2):
        ss = jnp.dot((x2 * x2).astype(BF16), grp, preferred_element_type=F32)
        return lax.rsqrt(ss * (1.0 / DIFF_HEAD_DIM) + RMS_EPS)

    def rotated(y):
        return y * cos + pltpu.roll(y, DIFF_HEAD_DIM, axis=1) * sin

    ones_rows = jnp.ones((V_ROWS - hw, tm), BF16)
    yq = project(5 * d, d)
    yk = project(6 * d, d)
    yv = project(7 * d, d)
    for pair in range(DIFF_HEADS // 2):
        sl2 = slice(pair * 2 * hw, (pair + 1) * 2 * hw)
        qn = yq[:, sl2] * scaled_sumsq(yq[:, sl2])
        kn = yk[:, sl2] * scaled_sumsq(yk[:, sl2])
        for u in range(2):
            h = 2 * pair + u
            sl = slice(h * hw, (h + 1) * hw)
            su = slice(u * hw, (u + 1) * hw)
            qt_ref[0, sl, :] = (rotated(qn[:, su] * gq) * Q_SCALE_LOG2).T.astype(BF16)
            kr_ref[0, :, sl] = rotated(kn[:, su] * gk).astype(BF16)
            vt_ref[0, h * V_ROWS:h * V_ROWS + hw, :] = yv[:, sl].T.astype(BF16)
            vt_ref[0, h * V_ROWS + hw:(h + 1) * V_ROWS, :] = ones_rows

    x_lo = (xn - xb.astype(F32)).astype(BF16)
    g2 = (jnp.dot(xb, wg_ref[...], preferred_element_type=F32)
          + jnp.dot(x_lo, wg_ref[...], preferred_element_type=F32))
    g = g2[:, :LANES] + g2[:, LANES:] + gbias_ref[...]
    lf = _log_sigmoid(g)
    lf_hi = lf.astype(BF16)
    lf_lo = (lf - lf_hi.astype(F32)).astype(BF16)
    lf2 = jnp.concatenate([lf_hi, lf_lo], axis=1)
    rows = lax.broadcasted_iota(jnp.int32, (tm, tm), 0)
    cols = lax.broadcasted_iota(jnp.int32, (tm, tm), 1)
    lower = jnp.where(cols <= rows, 1.0, 0.0).astype(BF16)
    upper = jnp.where(cols >= rows, 1.0, 0.0).astype(BF16)
    cum_f2 = jnp.dot(lower, lf2, preferred_element_type=F32)
    cum_b2 = jnp.dot(upper, lf2, preferred_element_type=F32)
    cum_f = cum_f2[:, :LANES] + cum_f2[:, LANES:]
    cum_b = cum_b2[:, :LANES] + cum_b2[:, LANES:]
    lane = lax.broadcasted_iota(jnp.int32, (tm, LANES), 1)
    cum = jnp.where(lane < 2 * MLSTM_HEADS, cum_f, cum_b)
    a = g - pltpu.roll(cum, LANES - MLSTM_HEADS, axis=1)
    gates_ref[0] = jnp.where((lane % (2 * MLSTM_HEADS)) < MLSTM_HEADS, a, cum)
    gates_t_ref[0, 0] = a.T[0:4 * MLSTM_HEADS, :]


def _projection(x, ctx, mods3, norm1, wbig, wg, gate_bias, cos_t, sin_t, gq, gk, grp):
    b, s, d = x.shape
    n_ctx = ctx.shape[1]
    tm = TOK_TILE
    assert tm == CHUNK
    n_lat_tiles = s // tm
    n_tiles = (s + n_ctx) // tm
    t = s + n_ctx
    n_out = wbig.shape[1]
    da_w = DIFF_HEADS * 2 * DIFF_HEAD_DIM
    const = lambda shape: pl.BlockSpec(shape, lambda bi, i: (0,) * len(shape))
    tok = lambda width: pl.BlockSpec((1, tm, width), lambda bi, i: (bi, i, 0))
    tok_t = lambda rows: pl.BlockSpec((1, rows, tm), lambda bi, i: (bi, 0, i))
    return pl.pallas_call(
        functools.partial(_proj_kernel, n_lat_tiles, d),
        grid=(b, n_tiles),
        in_specs=[
            pl.BlockSpec((1, tm, d), lambda bi, i: (bi, jnp.minimum(i, n_lat_tiles - 1), 0)),
            pl.BlockSpec((1, tm, d), lambda bi, i: (bi, jnp.maximum(i - n_lat_tiles, 0), 0)),
            pl.BlockSpec((1, 1, N_MOD * d), lambda bi, i: (jnp.where(i < n_lat_tiles, bi, b), 0, 0)),
            const((1, d)),
            pl.BlockSpec((d, n_out), lambda bi, i: (0, 0), pipeline_mode=pl.Buffered(1)),
            const((d, 2 * LANES)), const((1, LANES)),
            pl.BlockSpec((tm, LANES), lambda bi, i: (i, 0)),
            pl.BlockSpec((tm, LANES), lambda bi, i: (i, 0)),
            const((1, LANES)), const((1, LANES)), const((2 * LANES, 2 * LANES)),
        ],
        out_specs=[tok(d), tok(4 * d), tok_t(da_w), tok(da_w), tok_t(DIFF_HEADS * V_ROWS), tok(LANES),
                   pl.BlockSpec((1, 1, 4 * MLSTM_HEADS, tm), lambda bi, i: (bi, i, 0, 0))],
        out_shape=[jax.ShapeDtypeStruct((b, t, d), BF16),
                   jax.ShapeDtypeStruct((b, t, 4 * d), BF16),
                   jax.ShapeDtypeStruct((b, da_w, t), BF16),
                   jax.ShapeDtypeStruct((b, t, da_w), BF16),
                   jax.ShapeDtypeStruct((b, DIFF_HEADS * V_ROWS, t), BF16),
                   jax.ShapeDtypeStruct((b, t, LANES), F32),
                   jax.ShapeDtypeStruct((b, n_tiles, 4 * MLSTM_HEADS, tm), F32)],
        compiler_params=_cparams(("parallel", "arbitrary")),
        name="proj",
    )(x, ctx, mods3, norm1, wbig, wg, gate_bias, cos_t, sin_t, gq, gk, grp)


def _conv_kernel(n_lat, n_ctx, n_qcols, x_ref, w_ref, b_ref, o_ref, buf_ref):
    j = pl.program_id(1)
    r = TOK_TILE
    gap = 2 * SUBLANES
    lat0 = SUBLANES
    ctx0 = SUBLANES + n_lat + gap
    zeros8 = jnp.zeros((SUBLANES, LANES), F32)
    buf_ref[0:SUBLANES, :] = zeros8
    buf_ref[lat0 + n_lat:lat0 + n_lat + SUBLANES, :] = zeros8
    buf_ref[lat0 + n_lat + SUBLANES:ctx0, :] = zeros8
    buf_ref[ctx0 + n_ctx:ctx0 + n_ctx + SUBLANES, :] = zeros8
    n_lat_tiles = n_lat // r
    n_tiles = (n_lat + n_ctx) // r

    def fill(t, carry):
        src = pl.multiple_of(t * r, r)
        dst = pl.multiple_of(SUBLANES + t * r + jnp.where(t >= n_lat_tiles, gap, 0), SUBLANES)
        buf_ref[pl.ds(dst, r), :] = x_ref[0, pl.ds(src, r), :].astype(F32)
        return carry

    lax.fori_loop(0, n_tiles, fill, 0)
    w = w_ref[...]
    bias = b_ref[...]
    out_scale = jnp.where(j < n_qcols, MLSTM_DQK ** -0.5, 1.0).astype(F32)
    ext_rows = r + 2 * SUBLANES

    def tile(t, carry):
        dst = pl.multiple_of(t * r, r)
        base = pl.multiple_of(t * r + jnp.where(t >= n_lat_tiles, gap, 0), SUBLANES)
        ext = buf_ref[pl.ds(base, ext_rows), :]
        acc = jnp.zeros((r, LANES), F32) + bias
        for k in range(CONV_K):
            shift = (CONV_K // 2 - k) % ext_rows
            sh = ext if shift == 0 else pltpu.roll(ext, shift, axis=0)
            acc = acc + sh[SUBLANES:SUBLANES + r, :] * w[k:k + 1, :]
        o_ref[0, pl.ds(dst, r), :] = (_silu(acc) * out_scale).astype(BF16)
        return carry

    lax.fori_loop(0, n_tiles, tile, 0)


def _conv_silu(p, conv_w8, conv_b, n_lat, n_ctx):
    b, t, _ = p.shape
    n_cols = conv_w8.shape[1] // LANES
    buf_rows = n_lat + n_ctx + 4 * SUBLANES
    return pl.pallas_call(
        functools.partial(_conv_kernel, n_lat, n_ctx, n_cols // 2),
        grid=(b, n_cols),
        in_specs=[pl.BlockSpec((1, t, LANES), lambda bi, j: (bi, 0, j)),
                  pl.BlockSpec((SUBLANES, LANES), lambda bi, j: (0, j)),
                  pl.BlockSpec((1, LANES), lambda bi, j: (0, j))],
        out_specs=pl.BlockSpec((1, t, LANES), lambda bi, j: (bi, 0, j)),
        out_shape=jax.ShapeDtypeStruct((b, t, n_cols * LANES), BF16),
        scratch_shapes=[pltpu.VMEM((buf_rows, LANES), F32)],
        compiler_params=_cparams(("parallel", "arbitrary")),
        name="conv",
    )(p, conv_w8, conv_b)


def _mlstm_direction(reverse, q, k, v_aug, a_col, a_row, lf_cum_col, lf_total, c_ref, m_ref, hd):
    l = q.shape[0]
    dv = v_aug.shape[1] - LANES

    def wide(x, n_tiles):
        return jnp.concatenate([x] * n_tiles, axis=1)

    m_prev = m_ref[hd:hd + 1, :]
    c_aug = c_ref[hd]
    lf_cum_rep = jnp.broadcast_to(lf_cum_col, (l, LANES))
    a_rep = jnp.broadcast_to(a_col, (l, LANES))
    rows = lax.broadcasted_iota(jnp.int32, (l, l), 0)
    cols = lax.broadcasted_iota(jnp.int32, (l, l), 1)
    visible = (cols >= rows) if reverse else (cols <= rows)
    a_mat = jnp.where(visible, a_row, -jnp.inf)
    m_rep = jnp.maximum(jnp.broadcast_to(jnp.max(a_mat, axis=-1, keepdims=True), (l, LANES)), m_prev)
    decay = jnp.exp(a_mat - wide(m_rep, l // LANES))
    s = lax.dot_general(q, k, (((1,), (1,)), ((), ())), preferred_element_type=F32)
    pmat = (s * decay).astype(BF16)
    n_wide = v_aug.shape[1] // LANES
    intra = jnp.dot(pmat, v_aug, preferred_element_type=F32)
    inter = jnp.dot(q, c_aug.astype(BF16), preferred_element_type=F32)
    numden = intra + wide(jnp.exp(m_prev - m_rep), n_wide) * inter
    den = numden[:, dv:]
    floor = jnp.exp(-(lf_cum_rep + m_rep))
    h = numden[:, :dv] * wide(1.0 / jnp.maximum(jnp.abs(den), floor), dv // LANES)
    m_last = jnp.maximum(jnp.max(a_rep, axis=0, keepdims=True), m_prev)
    w = jnp.exp(a_rep - m_last)
    wv = (wide(w, n_wide) * v_aug.astype(F32)).astype(BF16)
    kv = lax.dot_general(k, wv, (((0,), (0,)), ((), ())), preferred_element_type=F32)
    c_ref[hd] = wide(jnp.exp(m_prev - m_last), n_wide) * c_aug + kv
    m_ref[hd:hd + 1, :] = lf_total + m_last
    return h


def _mlstm_kernel(qf_ref, kf_ref, vf_ref, gf_ref, gtf_ref, qb_ref, kb_ref, vb_ref, gb_ref, gtb_ref,
                  hf_ref, hb_ref, c_ref, m_ref):
    s_idx = pl.program_id(1)

    @pl.when(s_idx == 0)
    def _():
        c_ref[...] = jnp.zeros_like(c_ref)
        m_ref[...] = jnp.zeros_like(m_ref)

    l = qf_ref.shape[1]
    nh = MLSTM_HEADS
    ones_col = jnp.ones((l, LANES), BF16)
    for reverse, q_ref, k_ref, v_ref, g_ref, gt_ref, h_ref in (
            (False, qf_ref, kf_ref, vf_ref, gf_ref, gtf_ref, hf_ref),
            (True, qb_ref, kb_ref, vb_ref, gb_ref, gtb_ref, hb_ref)):
        g = g_ref[0]
        g_t = gt_ref[0, 0]
        base = 2 * nh if reverse else 0
        for h in range(nh):
            ca, cc = base + h, base + nh + h
            hd = (nh if reverse else 0) + h
            lf_cum_col = g[:, cc:cc + 1]
            lf_total = g[0:1, cc:cc + 1] if reverse else g[l - 1:l, cc:cc + 1]
            q = q_ref[0, :, h * MLSTM_DQK:(h + 1) * MLSTM_DQK]
            k = k_ref[0, :, h * MLSTM_DQK:(h + 1) * MLSTM_DQK]
            v_aug = jnp.concatenate([v_ref[0, :, h * MLSTM_DV:(h + 1) * MLSTM_DV], ones_col], axis=1)
            hval = _mlstm_direction(reverse, q, k, v_aug, g[:, ca:ca + 1], g_t[ca:ca + 1, :],
                                    lf_cum_col, lf_total, c_ref, m_ref, hd)
            h_ref[0, :, h * MLSTM_DV:(h + 1) * MLSTM_DV] = hval.astype(h_ref.dtype)


def _mlstm(qk, pm, gates, gates_t, n_lat, n_ctx, v_region):
    b, t, _ = qk.shape
    l = CHUNK
    n_lat_c = n_lat // l
    n_ctx_c = n_ctx // l
    n_steps = n_lat_c + n_ctx_c
    qw = MLSTM_HEADS * MLSTM_DQK
    vw = MLSTM_HEADS * MLSTM_DV

    def fwd(si):
        return jnp.where(si < n_ctx_c, n_lat_c + si, si - n_ctx_c)

    def bwd(si):
        return jnp.where(si < n_ctx_c, n_steps - 1 - si, n_steps - 1 - si)

    in_specs = []
    for order in (fwd, bwd):
        in_specs += [
            pl.BlockSpec((1, l, qw), lambda bi, si, o=order: (bi, o(si), 0)),
            pl.BlockSpec((1, l, qw), lambda bi, si, o=order: (bi, o(si), 1)),
            pl.BlockSpec((1, l, vw), lambda bi, si, o=order: (bi, o(si), v_region)),
            pl.BlockSpec((1, l, LANES), lambda bi, si, o=order: (bi, o(si), 0)),
            pl.BlockSpec((1, 1, 4 * MLSTM_HEADS, l), lambda bi, si, o=order: (bi, o(si), 0, 0)),
        ]
    out_specs = [pl.BlockSpec((1, l, vw), lambda bi, si: (bi, fwd(si), 0)),
                 pl.BlockSpec((1, l, vw), lambda bi, si: (bi, bwd(si), 0))]
    return pl.pallas_call(
        _mlstm_kernel,
        grid=(b, n_steps),
        in_specs=in_specs,
        out_specs=out_specs,
        out_shape=[jax.ShapeDtypeStruct((b, t, vw), BF16)] * 2,
        scratch_shapes=[pltpu.VMEM((2 * MLSTM_HEADS, MLSTM_DQK, MLSTM_DV + LANES), F32),
                        pltpu.VMEM((2 * MLSTM_HEADS, LANES), F32)],
        compiler_params=_cparams(("parallel", "arbitrary")),
        name="mlstm",
    )(qk, qk, pm, gates, gates_t, qk, qk, pm, gates, gates_t)


def _attn_kernel(n_keys, tq, qt_ref, k_ref, vt_ref, lam_ref, gain_ref, o_ref,
                 qq_ref, s_ref, pm_ref, e_ref, al_ref, m_ref, acc_ref):
    tk = ATT_TK
    n = n_keys // tk
    nq = qt_ref.shape[2] // tq
    width = 2 * tq
    row = lax.broadcasted_iota(jnp.int32, (qt_ref.shape[1], tq), 0)
    first = (row % DIFF_HEAD_DIM) < (DIFF_HEAD_DIM // 2)

    def mask_queries(qb, carry):
        qt = qt_ref[0, :, pl.ds(pl.multiple_of(qb * tq, tq), tq)]
        zero = jnp.zeros_like(qt)
        qq_ref[qb] = jnp.concatenate([jnp.where(first, qt, zero), jnp.where(first, zero, qt)], axis=1)
        return carry

    lax.fori_loop(0, nq, mask_queries, 0)
    acc_ref[...] = jnp.zeros_like(acc_ref)

    def scores(j, qb):
        off = pl.multiple_of(j * tk, tk)
        s = jnp.dot(k_ref[0, pl.ds(off, tk), :], qq_ref[qb], preferred_element_type=F32)
        s_ref[...] = s
        pm_ref[...] = jnp.max(s.reshape(tk // SUBLANES, SUBLANES, width), axis=0)

    def softmax(j, qb):
        m_old = jnp.where(j == 0, -jnp.inf, m_ref[...])
        m_new = jnp.maximum(m_old, jnp.max(pm_ref[...], axis=0, keepdims=True))
        alpha = jnp.exp2(m_old - m_new)
        e = jnp.exp2(s_ref[...] - m_new)
        e_ref[...] = e.astype(BF16)
        al_ref[...] = alpha
        m_ref[...] = m_new

    def accumulate(j):
        off = pl.multiple_of(j * tk, tk)
        pv = jnp.dot(vt_ref[0, :, pl.ds(off, tk)], e_ref[...], preferred_element_type=F32)
        acc_ref[...] = al_ref[...] * acc_ref[...] + pv

    def finalize(qb):
        lv = lam_ref[...]
        lam = (jnp.exp(jnp.sum(lv[0:1] * lv[1:2], axis=-1, keepdims=True))
               - jnp.exp(jnp.sum(lv[2:3] * lv[3:4], axis=-1, keepdims=True)) + LAM_INIT_L0)
        hw = o_ref.shape[2]
        inv_l = 1.0 / acc_ref[hw:hw + 1, :]
        o_all = acc_ref[0:hw, :] * inv_l
        o_t = o_all[:, :tq] - lam * o_all[:, tq:]
        ms = jnp.mean(o_t * o_t, axis=0, keepdims=True)
        out = (o_t * lax.rsqrt(ms + RMS_EPS)).T * (gain_ref[...] * (1.0 - LAM_INIT_L0))
        o_ref[0, pl.ds(pl.multiple_of(qb * tq, tq), tq), :] = out.astype(o_ref.dtype)

    def advance(j, qb):
        wrap = j == n - 1
        return jnp.where(wrap, 0, j + 1), jnp.where(wrap, qb + 1, qb)

    zero_i = jnp.int32(0)
    item0 = (zero_i, zero_i)
    item1 = advance(*item0)
    scores(*item0)
    softmax(*item0)
    scores(*item1)

    def iteration(t, carry):
        ja, qa, jm, qm, js, qs = carry
        accumulate(ja)
        softmax(jm, qm)
        scores(js, qs)

        @pl.when(ja == n - 1)
        def _():
            finalize(qa)

        return (jm, qm, js, qs) + advance(js, qs)

    item2 = advance(*item1)
    lax.fori_loop(2, nq * n, iteration, item0 + item1 + item2)
    accumulate(n - 2)
    softmax(n - 1, nq - 1)
    accumulate(n - 1)
    finalize(nq - 1)


def _attention(qt, kr, vt, lam_vecs, diff_gain, n_lat):
    b, t, _ = kr.shape
    tq = ATT_TQ
    hw = 2 * DIFF_HEAD_DIM
    nq = n_lat // tq
    assert t // ATT_TK >= 2
    return pl.pallas_call(
        functools.partial(_attn_kernel, t, tq),
        grid=(b, DIFF_HEADS),
        in_specs=[pl.BlockSpec((1, hw, n_lat), lambda bi, h: (bi, h, 0)),
                  pl.BlockSpec((1, t, hw), lambda bi, h: (bi, 0, h)),
                  pl.BlockSpec((1, V_ROWS, t), lambda bi, h: (bi, h, 0)),
                  pl.BlockSpec(lam_vecs.shape, lambda bi, h: (0, 0)),
                  pl.BlockSpec((1, hw), lambda bi, h: (0, 0))],
        out_specs=pl.BlockSpec((1, n_lat, hw), lambda bi, h: (bi, 0, h)),
        out_shape=jax.ShapeDtypeStruct((b, n_lat, DIFF_HEADS * hw), BF16),
        scratch_shapes=[pltpu.VMEM((nq, hw, 2 * tq), BF16),
                        pltpu.VMEM((ATT_TK, 2 * tq), F32),
                        pltpu.VMEM((SUBLANES, 2 * tq), F32),
                        pltpu.VMEM((ATT_TK, 2 * tq), BF16),
                        pltpu.VMEM((1, 2 * tq), F32),
                        pltpu.VMEM((1, 2 * tq), F32),
                        pltpu.VMEM((V_ROWS, 2 * tq), F32)],
        compiler_params=_cparams(("parallel", "arbitrary")),
        name="attn",
    )(qt, kr, vt, lam_vecs, diff_gain)


def _merge_kernel(d, hf_ref, hb_ref, mo_ref, ga_ref, gb_ref, hB_ref, x_ref, mod_ref, gain_ref,
                  wa_ref, wb_ref, wo_ref, o_ref):
    gain = gain_ref[...]
    parts = []
    for h in range(MLSTM_HEADS):
        sl = slice(h * MLSTM_DV, (h + 1) * MLSTM_DV)
        hs = hf_ref[0, :, sl].astype(F32) + hb_ref[0, :, sl].astype(F32)
        ms = jnp.mean(hs * hs, axis=-1, keepdims=True)
        hn = hs * lax.rsqrt(ms + RMS_EPS) * gain[:, sl]
        parts.append((_sigmoid(mo_ref[0, :, sl].astype(F32)) * hn).astype(BF16))
    h_a = jnp.concatenate(parts, axis=1)
    ya = jnp.dot(h_a, wa_ref[...], preferred_element_type=F32)
    yb = jnp.dot(hB_ref[0], wb_ref[...], preferred_element_type=F32)
    y = _sigmoid(ga_ref[0].astype(F32)) * ya + _sigmoid(gb_ref[0].astype(F32)) * yb
    yo = jnp.dot(y.astype(BF16), wo_ref[...], preferred_element_type=F32)
    g1 = mod_ref[0][:, 2 * d:3 * d]
    o_ref[0] = x_ref[0] + g1 * yo


def _merge(hf, hb, p, h_b, x, mods3, mlstm_gain, wa, wb, wo, mo_region, ga_region, gb_region):
    b, s, d = x.shape
    tm = LAT_TILE
    wspec = lambda shape: pl.BlockSpec(shape, lambda bi, i: (0, 0), pipeline_mode=pl.Buffered(1))
    tok = lambda region: pl.BlockSpec((1, tm, d), lambda bi, i, r=region: (bi, i, r))
    return pl.pallas_call(
        functools.partial(_merge_kernel, d),
        grid=(b, s // tm),
        in_specs=[tok(0), tok(0), tok(mo_region), tok(ga_region), tok(gb_region), tok(0), tok(0),
                  pl.BlockSpec((1, 1, N_MOD * d), lambda bi, i: (bi, 0, 0)),
                  pl.BlockSpec((1, d), lambda bi, i: (0, 0)),
                  wspec(wa.shape), wspec(wb.shape), wspec(wo.shape)],
        out_specs=pl.BlockSpec((1, tm, d), lambda bi, i: (bi, i, 0)),
        out_shape=jax.ShapeDtypeStruct((b, s, d), F32),
        compiler_params=_cparams(("parallel", "arbitrary")),
        name="merge",
    )(hf, hb, p, p, p, h_b, x, mods3, mlstm_gain, wa, wb, wo)


def _ffn_kernel(d, hidden, x_ref, mod_ref, g_ref, wi_ref, wo_ref, o_ref):
    x1 = x_ref[0]
    mod = mod_ref[0]
    xn = _rms_mod(x1, g_ref[...], mod[:, 3 * d:4 * d], mod[:, 4 * d:5 * d]).astype(BF16)
    a = jnp.dot(xn, wi_ref[:, :hidden], preferred_element_type=F32)
    bb = jnp.dot(xn, wi_ref[:, hidden:], preferred_element_type=F32)
    hmid = (_silu(a) * bb).astype(BF16)
    y = jnp.dot(hmid, wo_ref[...], preferred_element_type=F32)
    o_ref[0] = x1 + mod[:, 5 * d:6 * d] * y


def _ffn(x1, mods3, norm2, wi, wo):
    b, s, d = x1.shape
    hidden = wo.shape[0]
    tm = LAT_TILE
    wspec = lambda shape: pl.BlockSpec(shape, lambda bi, i: (0, 0), pipeline_mode=pl.Buffered(1))
    return pl.pallas_call(
        functools.partial(_ffn_kernel, d, hidden),
        grid=(b, s // tm),
        in_specs=[pl.BlockSpec((1, tm, d), lambda bi, i: (bi, i, 0)),
                  pl.BlockSpec((1, 1, N_MOD * d), lambda bi, i: (bi, 0, 0)),
                  pl.BlockSpec((1, d), lambda bi, i: (0, 0)),
                  wspec(wi.shape), wspec(wo.shape)],
        out_specs=pl.BlockSpec((1, tm, d), lambda bi, i: (bi, i, 0)),
        out_shape=jax.ShapeDtypeStruct((b, s, d), F32),
        compiler_params=_cparams(("parallel", "arbitrary")),
        name="ffn",
    )(x1, mods3, norm2, wi, wo)


def _diff_lane_perm():
    l = np.arange(2 * DIFF_HEAD_DIM)
    half, r = l // DIFF_HEAD_DIM, l % DIFF_HEAD_DIM
    qsel, axis, f = r // 32, (r % 32) // ROPE_FREQS, r % ROPE_FREQS
    return qsel * DIFF_HEAD_DIM + axis * 2 * ROPE_FREQS + half * ROPE_FREQS + f, qsel, axis, f, half


def _rope_tables(n_lat, n_ctx):
    _, _, axis, f, half = _diff_lane_perm()
    pos = jnp.arange(n_lat, dtype=F32)
    row = jnp.floor(pos / GRID_W)
    col = pos - row * GRID_W
    inv = ROPE_BASE ** (-jnp.arange(ROPE_FREQS, dtype=F32) / ROPE_FREQS)
    ang = jnp.where(axis[None, :] == 0, row[:, None], col[:, None]) * inv[f][None, :]
    cos = jnp.concatenate([jnp.cos(ang), jnp.ones((n_ctx, LANES), F32)], axis=0)
    sin = jnp.sin(ang) * jnp.where(half == 0, -1.0, 1.0)[None, :]
    sin = jnp.concatenate([sin, jnp.zeros((n_ctx, LANES), F32)], axis=0)
    return cos, sin


def kernel(x, c, ctx, c_ctx, w_mod, b_mod, norm1, norm2, w_in, b_gate, conv_w, conv_b, mlstm_norm, q_norm, k_norm,
           lam_vecs, diff_norm, w_branch_a, w_branch_b, w_out, w_ffn_in, w_ffn_out):
    b, s, d = x.shape
    n_ctx = ctx.shape[1]
    assert w_mod.shape[0] == 1, "single-layer block"
    assert s % CHUNK == 0 and n_ctx % CHUNK == 0 and s % ATT_TQ == 0 and s % LAT_TILE == 0
    qk_w = MLSTM_HEADS * MLSTM_DQK
    v_w = MLSTM_HEADS * MLSTM_DV
    da_w = DIFF_HEADS * 2 * DIFF_HEAD_DIM
    n_gates = 4 * MLSTM_HEADS
    sizes = (qk_w, qk_w, v_w, v_w, n_gates, da_w, da_w, da_w, d, d)
    offs = np.concatenate([[0], np.cumsum(sizes)])
    seg = lambda i: w_in[0][:, offs[i]:offs[i + 1]]

    perm, qsel, _, _, _ = _diff_lane_perm()
    head_perm = (np.arange(DIFF_HEADS)[:, None] * 2 * DIFF_HEAD_DIM + perm[None, :]).reshape(-1)
    wbig = jnp.concatenate([seg(0), seg(1), seg(2), seg(3), seg(8), seg(9),
                            seg(5)[:, head_perm], seg(6)[:, head_perm], seg(7)], axis=1).astype(BF16)
    R_V, R_O, R_GA, R_GB = range(4)
    wg_f32 = jnp.pad(seg(4), ((0, 0), (0, LANES - n_gates)))
    wg_hi = wg_f32.astype(BF16)
    wg = jnp.concatenate([wg_hi, (wg_f32 - wg_hi.astype(F32)).astype(BF16)], axis=1)
    gate_bias = jnp.pad(b_gate[0], (0, LANES - n_gates)).reshape(1, LANES)
    conv_w8 = jnp.pad(conv_w[0], ((0, SUBLANES - CONV_K), (0, 0)))
    gq = q_norm[0][perm % DIFF_HEAD_DIM].reshape(1, LANES)
    gk = k_norm[0][perm % DIFF_HEAD_DIM].reshape(1, LANES)
    unit = np.concatenate([qsel, 2 + qsel])
    grp = jnp.asarray(unit[:, None] == unit[None, :], BF16)
    cos_t, sin_t = _rope_tables(s, n_ctx)

    cc = jnp.concatenate([c, c_ctx[None, :], jnp.zeros((SUBLANES - b - 1, d), F32)], axis=0)
    mods = _modulation(cc, w_mod[0], b_mod[0])
    mods3 = mods.reshape(SUBLANES, 1, N_MOD * d)

    qk_pre, pm, qt, kr, vt, gates, gates_t = _projection(x, ctx, mods3, norm1, wbig, wg, gate_bias,
                                                         cos_t, sin_t, gq, gk, grp)
    qk = _conv_silu(qk_pre, conv_w8, conv_b, s, n_ctx)
    hf, hb = _mlstm(qk, pm, gates, gates_t, s, n_ctx, R_V)
    h_b = _attention(qt, kr, vt, lam_vecs[0], diff_norm, s)
    x1 = _merge(hf, hb, pm, h_b, x, mods3, mlstm_norm, w_branch_a[0].astype(BF16), w_branch_b[0].astype(BF16),
                w_out[0].astype(BF16), R_O, R_GA, R_GB)
    return _ffn(x1, mods3, norm2, w_ffn_in[0].astype(BF16), w_ffn_out[0].astype(BF16))
```

```python
import functools
import math

import jax
import jax.numpy as jnp
import numpy as np
from jax import lax
from jax.experimental import pallas as pl
from jax.experimental.pallas import tpu as pltpu

F32 = jnp.float32
BF16 = jnp.bfloat16
HIGHEST = lax.Precision.HIGHEST

GRID_W = 64
N_MOD = 6
MLSTM_HEADS = 4
MLSTM_DQK = 128
MLSTM_DV = 256
CONV_K = 5
DIFF_HEADS = 8
DIFF_HEAD_DIM = 64
ROPE_BASE = 10000.0
ROPE_FREQS = DIFF_HEAD_DIM // 4
RMS_EPS = 1e-6
LAM_INIT_L0 = 0.8 - 0.6 * math.exp(-0.3 * 0)
LOG2_E = math.log2(math.e)
Q_SCALE_LOG2 = DIFF_HEAD_DIM ** -0.5 * LOG2_E
V_ROWS = 2 * DIFF_HEAD_DIM + 16

LANES = 128
SUBLANES = 8
VMEM_LIMIT = 56 * 1024 * 1024

TOK_TILE = 256
LAT_TILE = 512
CHUNK = 256
ATT_TQ = 512
ATT_TK = 2816


def _cparams(sem):
    return pltpu.CompilerParams(dimension_semantics=sem, vmem_limit_bytes=VMEM_LIMIT)


def _sigmoid(x):
    return 1.0 / (1.0 + jnp.exp(-x))


def _silu(x):
    return x * _sigmoid(x)


def _log_sigmoid(x):
    return jnp.minimum(x, 0.0) - jnp.log(1.0 + jnp.exp(-jnp.abs(x)))


def _mod_kernel(c_ref, w_ref, b_ref, o_ref):
    a = _silu(c_ref[...])
    o_ref[...] = jnp.dot(a, w_ref[...], preferred_element_type=F32, precision=HIGHEST) + b_ref[...]


def _modulation(cc, w_mod, b_mod):
    rows, d = cc.shape
    n = w_mod.shape[1]
    tn = 1024
    return pl.pallas_call(
        _mod_kernel,
        grid=(n // tn,),
        in_specs=[pl.BlockSpec((rows, d), lambda j: (0, 0)),
                  pl.BlockSpec((d, tn), lambda j: (0, j)),
                  pl.BlockSpec((1, tn), lambda j: (0, j))],
        out_specs=pl.BlockSpec((rows, tn), lambda j: (0, j)),
        out_shape=jax.ShapeDtypeStruct((rows, n), F32),
        compiler_params=_cparams(("arbitrary",)),
        name="mod",
    )(cc, w_mod, b_mod.reshape(1, n))


def _rms_mod(xf, g, shift, scale):
    ms = jnp.mean(xf * xf, axis=-1, keepdims=True)
    return xf * lax.rsqrt(ms + RMS_EPS) * g * (1.0 + scale) + shift


def _proj_kernel(n_lat_tiles, d, x_ref, xprev_ref, xnext_ref, ctx_ref, mod_ref, g_ref, w_ref, wg_ref, gbias_ref,
                 cos_ref, sin_ref, gq_ref, gk_ref, grp_ref, cw_ref, cb_ref,
                 qk_ref, pm_ref, qt_ref, kr_ref, vt_ref, gates_ref, gates_t_ref,
                 ext_ref, yq_ref, yk_ref, yv_ref, gpre_ref, xb_ref, xlo_ref):
    i = pl.program_id(1)
    tm = x_ref.shape[1]

    @pl.when(i == 0)
    def _():
        ext_ref[...] = jnp.zeros_like(ext_ref)
        yq_ref[...] = jnp.zeros_like(yq_ref)
        yk_ref[...] = jnp.zeros_like(yk_ref)
        yv_ref[...] = jnp.zeros_like(yv_ref)
        gpre_ref[...] = jnp.zeros_like(gpre_ref)

    n_tiles = pl.num_programs(1) - 1
    ti = jnp.minimum(i, n_tiles - 1)
    xf = jnp.where(ti < n_lat_tiles, x_ref[0], ctx_ref[0])
    mod = mod_ref[0]
    gain1 = g_ref[...]
    xn = _rms_mod(xf, gain1, mod[:, 0:d], mod[:, d:2 * d])
    xb = xn.astype(BF16)
    halo = jnp.concatenate([xprev_ref[0], xnext_ref[0]], axis=0)
    xb_ref[0:tm, :] = xb
    xb_ref[tm:, :] = _rms_mod(halo, gain1, mod[:, 0:d], mod[:, d:2 * d]).astype(BF16)
    xlo_ref[...] = (xn - xb.astype(F32)).astype(BF16)

    def project(c0, width):
        return jnp.dot(xb_ref[0:tm, :], w_ref[:, c0:c0 + width], preferred_element_type=F32)

    taps = cw_ref[...]
    conv_bias = cb_ref[...]
    n_q_cols = d // 2

    def conv_columns(c_lo, c_hi):
        for c0 in range(c_lo, c_hi, LANES):
            acc = jnp.zeros((tm, LANES), F32) + conv_bias[:, c0:c0 + LANES]
            for k in range(CONV_K):
                r0 = SUBLANES - CONV_K // 2 + k
                acc = acc + ext_ref[r0:r0 + tm, c0:c0 + LANES] * taps[k:k + 1, c0:c0 + LANES]
            out = _silu(acc)
            if c0 < n_q_cols:
                out = out * MLSTM_DQK ** -0.5
            qk_ref[0, :, c0:c0 + LANES] = out.astype(BF16)

    cos = cos_ref[...]
    sin = sin_ref[...]
    grp = grp_ref[...]
    gq = gq_ref[...]
    gk = gk_ref[...]
    hw = 2 * DIFF_HEAD_DIM

    def scaled_sumsq(x2):
        ss = jnp.dot((x2 * x2).astype(BF16), grp, preferred_element_type=F32)
        return lax.rsqrt(ss * (1.0 / DIFF_HEAD_DIM) + RMS_EPS)

    def rotated(y):
        return y * cos + pltpu.roll(y, DIFF_HEAD_DIM, axis=1) * sin

    ones_rows = jnp.ones((V_ROWS - hw, tm), BF16)

    def attention_pair(pair):
        sl2 = slice(pair * 2 * hw, (pair + 1) * 2 * hw)
        yq2 = yq_ref[:, sl2]
        yk2 = yk_ref[:, sl2]
        qn = yq2 * scaled_sumsq(yq2)
        kn = yk2 * scaled_sumsq(yk2)
        for u in range(2):
            h = 2 * pair + u
            sl = slice(h * hw, (h + 1) * hw)
            su = slice(u * hw, (u + 1) * hw)
            qt_ref[0, sl, :] = (rotated(qn[:, su] * gq) * Q_SCALE_LOG2).T.astype(BF16)
            kr_ref[0, :, sl] = rotated(kn[:, su] * gk).astype(BF16)
            vt_ref[0, h * V_ROWS:h * V_ROWS + hw, :] = yv_ref[:, sl].T.astype(BF16)
            vt_ref[0, h * V_ROWS + hw:(h + 1) * V_ROWS, :] = ones_rows

    def gate_epilogue():
        g = gpre_ref[...]
        lf = _log_sigmoid(g)
        lf_hi = lf.astype(BF16)
        lf_lo = (lf - lf_hi.astype(F32)).astype(BF16)
        lf2 = jnp.concatenate([lf_hi, lf_lo], axis=1)
        rows = lax.broadcasted_iota(jnp.int32, (tm, tm), 0)
        cols = lax.broadcasted_iota(jnp.int32, (tm, tm), 1)
        lower = jnp.where(cols <= rows, 1.0, 0.0).astype(BF16)
        upper = jnp.where(cols >= rows, 1.0, 0.0).astype(BF16)
        cum_f2 = jnp.dot(lower, lf2, preferred_element_type=F32)
        cum_b2 = jnp.dot(upper, lf2, preferred_element_type=F32)
        cum_f = cum_f2[:, :LANES] + cum_f2[:, LANES:]
        cum_b = cum_b2[:, :LANES] + cum_b2[:, LANES:]
        lane = lax.broadcasted_iota(jnp.int32, (tm, LANES), 1)
        cum = jnp.where(lane < 2 * MLSTM_HEADS, cum_f, cum_b)
        a = g - pltpu.roll(cum, LANES - MLSTM_HEADS, axis=1)
        a = a * LOG2_E
        gates_ref[0] = jnp.where((lane % (2 * MLSTM_HEADS)) < MLSTM_HEADS, a, cum * LOG2_E)
        gates_t_ref[0, 0] = a.T[0:4 * MLSTM_HEADS, :]

    for r in range(4):
        pm_ref[0, :, r * d:(r + 1) * d] = project((1 + r) * d, d).astype(BF16)
        attention_pair(r)
    yq_ref[...] = project(5 * d, d)
    conv_columns(0, d // 2)
    yk_ref[...] = project(6 * d, d)
    conv_columns(d // 2, d)
    yv_ref[...] = project(7 * d, d)
    gate_epilogue()
    y_qk = jnp.dot(xb_ref[...], w_ref[:, 0:d], preferred_element_type=F32)
    has_prev = jnp.logical_and(ti > 0, ti < n_lat_tiles)
    has_next = ti < n_lat_tiles - 1
    ext_ref[0:SUBLANES, :] = jnp.where(has_prev, y_qk[tm:tm + SUBLANES], 0.0)
    ext_ref[SUBLANES:SUBLANES + tm, :] = y_qk[0:tm]
    ext_ref[SUBLANES + tm:, :] = jnp.where(has_next, y_qk[tm + SUBLANES:], 0.0)
    g2 = (jnp.dot(xb_ref[0:tm, :], wg_ref[...], preferred_element_type=F32)
          + jnp.dot(xlo_ref[...], wg_ref[...], preferred_element_type=F32))
    gpre_ref[...] = g2[:, :LANES] + g2[:, LANES:] + gbias_ref[...]


def _projection(x, ctx, mods3, norm1, wbig, wg, gate_bias, cos_t, sin_t, gq, gk, grp, conv_w8, conv_b):
    b, s, d = x.shape
    n_ctx = ctx.shape[1]
    tm = TOK_TILE
    assert tm == CHUNK
    assert n_ctx == tm
    n_lat_tiles = s // tm
    n_tiles = (s + n_ctx) // tm
    t = s + n_ctx
    n_out = wbig.shape[1]
    da_w = DIFF_HEADS * 2 * DIFF_HEAD_DIM
    halo_stride = tm // SUBLANES
    last_halo = s // SUBLANES - 1
    const = lambda shape: pl.BlockSpec(shape, lambda bi, i: (0,) * len(shape))
    cur = lambda i: jnp.minimum(i, n_tiles - 1)
    late = lambda i: jnp.maximum(i - 1, 0)
    tok = lambda width: pl.BlockSpec((1, tm, width), lambda bi, i: (bi, late(i), 0))
    tok_t = lambda rows: pl.BlockSpec((1, rows, tm), lambda bi, i: (bi, 0, late(i)))
    n_ctx_tiles = n_ctx // tm
    return pl.pallas_call(
        functools.partial(_proj_kernel, n_lat_tiles, d),
        grid=(b, n_tiles + 1),
        in_specs=[
            pl.BlockSpec((1, tm, d), lambda bi, i: (bi, jnp.minimum(i, n_lat_tiles - 1), 0)),
            pl.BlockSpec((1, SUBLANES, d),
                         lambda bi, i: (bi, jnp.clip(cur(i) * halo_stride - 1, 0, last_halo), 0)),
            pl.BlockSpec((1, SUBLANES, d),
                         lambda bi, i: (bi, jnp.clip((cur(i) + 1) * halo_stride, 0, last_halo), 0)),
            pl.BlockSpec((1, tm, d), lambda bi, i: (bi, jnp.clip(i - n_lat_tiles, 0, n_ctx_tiles - 1), 0)),
            pl.BlockSpec((1, 1, N_MOD * d), lambda bi, i: (jnp.where(i < n_lat_tiles, bi, b), 0, 0)),
            const((1, d)),
            pl.BlockSpec((d, n_out), lambda bi, i: (0, 0), pipeline_mode=pl.Buffered(1)),
            const((d, 2 * LANES)), const((1, LANES)),
            pl.BlockSpec((tm, LANES), lambda bi, i: (late(i), 0)),
            pl.BlockSpec((tm, LANES), lambda bi, i: (late(i), 0)),
            const((1, LANES)), const((1, LANES)), const((2 * LANES, 2 * LANES)),
            const((SUBLANES, d)), const((1, d)),
        ],
        out_specs=[tok(d), pl.BlockSpec((1, tm, 4 * d), lambda bi, i: (bi, cur(i), 0)),
                   tok_t(da_w), tok(da_w), tok_t(DIFF_HEADS * V_ROWS), tok(LANES),
                   pl.BlockSpec((1, 1, 4 * MLSTM_HEADS, tm), lambda bi, i: (bi, late(i), 0, 0))],
        out_shape=[jax.ShapeDtypeStruct((b, t, d), BF16),
                   jax.ShapeDtypeStruct((b, t, 4 * d), BF16),
                   jax.ShapeDtypeStruct((b, da_w, t), BF16),
                   jax.ShapeDtypeStruct((b, t, da_w), BF16),
                   jax.ShapeDtypeStruct((b, DIFF_HEADS * V_ROWS, t), BF16),
                   jax.ShapeDtypeStruct((b, t, LANES), F32),
                   jax.ShapeDtypeStruct((b, n_tiles, 4 * MLSTM_HEADS, tm), F32)],
        scratch_shapes=[pltpu.VMEM((tm + 2 * SUBLANES, d), F32),
                        pltpu.VMEM((tm, da_w), F32), pltpu.VMEM((tm, da_w), F32), pltpu.VMEM((tm, da_w), F32),
                        pltpu.VMEM((tm, LANES), F32),
                        pltpu.VMEM((tm + 2 * SUBLANES, d), BF16),
                        pltpu.VMEM((tm, d), BF16)],
        compiler_params=_cparams(("parallel", "arbitrary")),
        name="proj",
    )(x, x, x, ctx, mods3, norm1, wbig, wg, gate_bias, cos_t, sin_t, gq, gk, grp, conv_w8, conv_b)


def _mlstm_direction(reverse, q, k, v_aug, a_col, a_row, lf_cum_col, lf_total, c_ref, m_ref, hd):
    l = q.shape[0]
    dv = v_aug.shape[1] - LANES

    def wide(x, n_tiles):
        return jnp.concatenate([x] * n_tiles, axis=1)

    m_prev = m_ref[hd:hd + 1, :]
    c_aug = c_ref[hd]
    lf_cum_rep = jnp.broadcast_to(lf_cum_col, (l, LANES))
    a_rep = jnp.broadcast_to(a_col, (l, LANES))
    rows = lax.broadcasted_iota(jnp.int32, (l, l), 0)
    cols = lax.broadcasted_iota(jnp.int32, (l, l), 1)
    visible = (cols >= rows) if reverse else (cols <= rows)
    a_mat = jnp.where(visible, a_row, -jnp.inf)
    m_rep = jnp.maximum(jnp.broadcast_to(jnp.max(a_mat, axis=-1, keepdims=True), (l, LANES)), m_prev)
    decay = jnp.exp2(a_mat - wide(m_rep, l // LANES))
    s = lax.dot_general(q, k, (((1,), (1,)), ((), ())), preferred_element_type=F32)
    pmat = (s * decay).astype(BF16)
    n_wide = v_aug.shape[1] // LANES
    intra = jnp.dot(pmat, v_aug, preferred_element_type=F32)
    inter = jnp.dot(q, c_aug.astype(BF16), preferred_element_type=F32)
    numden = intra + wide(jnp.exp2(m_prev - m_rep), n_wide) * inter
    den = numden[:, dv:]
    floor = jnp.exp2(-(lf_cum_rep + m_rep))
    h = numden[:, :dv] * wide(1.0 / jnp.maximum(jnp.abs(den), floor), dv // LANES)
    m_last = jnp.maximum(jnp.max(a_rep, axis=0, keepdims=True), m_prev)
    w = jnp.exp2(a_rep - m_last)
    kw = (w * k.astype(F32)).astype(BF16)
    kv = lax.dot_general(kw, v_aug, (((0,), (0,)), ((), ())), preferred_element_type=F32)
    c_ref[hd] = wide(jnp.exp2(m_prev - m_last), n_wide) * c_aug + kv
    m_ref[hd:hd + 1, :] = lf_total + m_last
    return h


def _mlstm_kernel(qf_ref, kf_ref, vf_ref, gf_ref, gtf_ref, qb_ref, kb_ref, vb_ref, gb_ref, gtb_ref,
                  hf_ref, hb_ref, c_ref, m_ref):
    s_idx = pl.program_id(1)

    @pl.when(s_idx == 0)
    def _():
        c_ref[...] = jnp.zeros_like(c_ref)
        m_ref[...] = jnp.zeros_like(m_ref)

    l = qf_ref.shape[1]
    nh = MLSTM_HEADS
    ones_col = jnp.ones((l, LANES), BF16)
    for reverse, q_ref, k_ref, v_ref, g_ref, gt_ref, h_ref in (
            (False, qf_ref, kf_ref, vf_ref, gf_ref, gtf_ref, hf_ref),
            (True, qb_ref, kb_ref, vb_ref, gb_ref, gtb_ref, hb_ref)):
        g = g_ref[0]
        g_t = gt_ref[0, 0]
        base = 2 * nh if reverse else 0
        for h in range(nh):
            ca, cc = base + h, base + nh + h
            hd = (nh if reverse else 0) + h
            lf_cum_col = g[:, cc:cc + 1]
            lf_total = g[0:1, cc:cc + 1] if reverse else g[l - 1:l, cc:cc + 1]
            q = q_ref[0, :, h * MLSTM_DQK:(h + 1) * MLSTM_DQK]
            k = k_ref[0, :, h * MLSTM_DQK:(h + 1) * MLSTM_DQK]
            v_aug = jnp.concatenate([v_ref[0, :, h * MLSTM_DV:(h + 1) * MLSTM_DV], ones_col], axis=1)
            hval = _mlstm_direction(reverse, q, k, v_aug, g[:, ca:ca + 1], g_t[ca:ca + 1, :],
                                    lf_cum_col, lf_total, c_ref, m_ref, hd)
            h_ref[0, :, h * MLSTM_DV:(h + 1) * MLSTM_DV] = hval.astype(h_ref.dtype)


def _mlstm(qk, pm, gates, gates_t, n_lat, n_ctx, v_region):
    b, t, _ = qk.shape
    l = CHUNK
    n_lat_c = n_lat // l
    n_ctx_c = n_ctx // l
    n_steps = n_lat_c + n_ctx_c
    qw = MLSTM_HEADS * MLSTM_DQK
    vw = MLSTM_HEADS * MLSTM_DV

    def fwd(si):
        return jnp.where(si < n_ctx_c, n_lat_c + si, si - n_ctx_c)

    def bwd(si):
        return jnp.where(si < n_ctx_c, n_steps - 1 - si, n_steps - 1 - si)

    in_specs = []
    for order in (fwd, bwd):
        in_specs += [
            pl.BlockSpec((1, l, qw), lambda bi, si, o=order: (bi, o(si), 0)),
            pl.BlockSpec((1, l, qw), lambda bi, si, o=order: (bi, o(si), 1)),
            pl.BlockSpec((1, l, vw), lambda bi, si, o=order: (bi, o(si), v_region)),
            pl.BlockSpec((1, l, LANES), lambda bi, si, o=order: (bi, o(si), 0)),
            pl.BlockSpec((1, 1, 4 * MLSTM_HEADS, l), lambda bi, si, o=order: (bi, o(si), 0, 0)),
        ]
    out_specs = [pl.BlockSpec((1, l, vw), lambda bi, si: (bi, fwd(si), 0)),
                 pl.BlockSpec((1, l, vw), lambda bi, si: (bi, bwd(si), 0))]
    return pl.pallas_call(
        _mlstm_kernel,
        grid=(b, n_steps),
        in_specs=in_specs,
        out_specs=out_specs,
        out_shape=[jax.ShapeDtypeStruct((b, t, vw), BF16)] * 2,
        scratch_shapes=[pltpu.VMEM((2 * MLSTM_HEADS, MLSTM_DQK, MLSTM_DV + LANES), F32),
                        pltpu.VMEM((2 * MLSTM_HEADS, LANES), F32)],
        compiler_params=_cparams(("parallel", "arbitrary")),
        name="mlstm",
    )(qk, qk, pm, gates, gates_t, qk, qk, pm, gates, gates_t)


def _attn_kernel(n_keys, tq, qt_ref, k_ref, vt_ref, lam_ref, gain_ref, o_ref,
                 qq_ref, s_ref, pm_ref, e_ref, al_ref, m_ref, acc_ref):
    tk = ATT_TK
    n = n_keys // tk
    nq = qt_ref.shape[2] // tq
    width = 2 * tq
    row = lax.broadcasted_iota(jnp.int32, (qt_ref.shape[1], tq), 0)
    first = (row % DIFF_HEAD_DIM) < (DIFF_HEAD_DIM // 2)

    def mask_queries(qb, carry):
        qt = qt_ref[0, :, pl.ds(pl.multiple_of(qb * tq, tq), tq)]
        zero = jnp.zeros_like(qt)
        qq_ref[qb] = jnp.concatenate([jnp.where(first, qt, zero), jnp.where(first, zero, qt)], axis=1)
        return carry

    lax.fori_loop(0, nq, mask_queries, 0)
    acc_ref[...] = jnp.zeros_like(acc_ref)

    def scores(j, qb):
        off = pl.multiple_of(j * tk, tk)
        s = jnp.dot(k_ref[0, pl.ds(off, tk), :], qq_ref[qb], preferred_element_type=F32)
        s_ref[...] = s
        pm_ref[...] = jnp.max(s.reshape(tk // SUBLANES, SUBLANES, width), axis=0)

    def softmax(j, qb):
        m_old = jnp.where(j == 0, -jnp.inf, m_ref[...])
        m_new = jnp.maximum(m_old, jnp.max(pm_ref[...], axis=0, keepdims=True))
        alpha = jnp.exp2(m_old - m_new)
        e = jnp.exp2(s_ref[...] - m_new)
        e_ref[...] = e.astype(BF16)
        al_ref[...] = alpha
        m_ref[...] = m_new

    def accumulate(j):
        off = pl.multiple_of(j * tk, tk)
        pv = jnp.dot(vt_ref[0, :, pl.ds(off, tk)], e_ref[...], preferred_element_type=F32)
        acc_ref[...] = al_ref[...] * acc_ref[...] + pv

    def finalize(qb):
        lv = lam_ref[...]
        lam = (jnp.exp(jnp.sum(lv[0:1] * lv[1:2], axis=-1, keepdims=True))
               - jnp.exp(jnp.sum(lv[2:3] * lv[3:4], axis=-1, keepdims=True)) + LAM_INIT_L0)
        hw = o_ref.shape[2]
        inv_l = 1.0 / acc_ref[hw:hw + 1, :]
        o_all = acc_ref[0:hw, :] * inv_l
        o_t = o_all[:, :tq] - lam * o_all[:, tq:]
        ms = jnp.mean(o_t * o_t, axis=0, keepdims=True)
        out = (o_t * lax.rsqrt(ms + RMS_EPS)).T * (gain_ref[...] * (1.0 - LAM_INIT_L0))
        o_ref[0, pl.ds(pl.multiple_of(qb * tq, tq), tq), :] = out.astype(o_ref.dtype)

    def advance(j, qb):
        wrap = j == n - 1
        return jnp.where(wrap, 0, j + 1), jnp.where(wrap, qb + 1, qb)

    zero_i = jnp.int32(0)
    item0 = (zero_i, zero_i)
    item1 = advance(*item0)
    scores(*item0)
    softmax(*item0)
    scores(*item1)

    def iteration(t, carry):
        ja, qa, jm, qm, js, qs = carry
        accumulate(ja)
        softmax(jm, qm)
        scores(js, qs)

        @pl.when(ja == n - 1)
        def _():
            finalize(qa)

        return (jm, qm, js, qs) + advance(js, qs)

    item2 = advance(*item1)
    lax.fori_loop(2, nq * n, iteration, item0 + item1 + item2)
    accumulate(n - 2)
    softmax(n - 1, nq - 1)
    accumulate(n - 1)
    finalize(nq - 1)


def _attention(qt, kr, vt, lam_vecs, diff_gain, n_lat):
    b, t, _ = kr.shape
    tq = ATT_TQ
    hw = 2 * DIFF_HEAD_DIM
    nq = n_lat // tq
    assert t // ATT_TK >= 2
    return pl.pallas_call(
        functools.partial(_attn_kernel, t, tq),
        grid=(b, DIFF_HEADS),
        in_specs=[pl.BlockSpec((1, hw, n_lat), lambda bi, h: (bi, h, 0)),
                  pl.BlockSpec((1, t, hw), lambda bi, h: (bi, 0, h)),
                  pl.BlockSpec((1, V_ROWS, t), lambda bi, h: (bi, h, 0)),
                  pl.BlockSpec(lam_vecs.shape, lambda bi, h: (0, 0)),
                  pl.BlockSpec((1, hw), lambda bi, h: (0, 0))],
        out_specs=pl.BlockSpec((1, n_lat, hw), lambda bi, h: (bi, 0, h)),
        out_shape=jax.ShapeDtypeStruct((b, n_lat, DIFF_HEADS * hw), BF16),
        scratch_shapes=[pltpu.VMEM((nq, hw, 2 * tq), BF16),
                        pltpu.VMEM((ATT_TK, 2 * tq), F32),
                        pltpu.VMEM((SUBLANES, 2 * tq), F32),
                        pltpu.VMEM((ATT_TK, 2 * tq), BF16),
                        pltpu.VMEM((1, 2 * tq), F32),
                        pltpu.VMEM((1, 2 * tq), F32),
                        pltpu.VMEM((V_ROWS, 2 * tq), F32)],
        compiler_params=_cparams(("parallel", "arbitrary")),
        name="attn",
    )(qt, kr, vt, lam_vecs, diff_gain)


def _merge_kernel(d, hf_ref, hb_ref, mo_ref, ga_ref, gb_ref, hB_ref, x_ref, mod_ref, gain_ref,
                  wa_ref, wb_ref, wo_ref, o_ref):
    gain = gain_ref[...]
    parts = []
    for h in range(MLSTM_HEADS):
        sl = slice(h * MLSTM_DV, (h + 1) * MLSTM_DV)
        hs = hf_ref[0, :, sl].astype(F32) + hb_ref[0, :, sl].astype(F32)
        ms = jnp.mean(hs * hs, axis=-1, keepdims=True)
        hn = hs * lax.rsqrt(ms + RMS_EPS) * gain[:, sl]
        parts.append((_sigmoid(mo_ref[0, :, sl].astype(F32)) * hn).astype(BF16))
    h_a = jnp.concatenate(parts, axis=1)
    ya = jnp.dot(h_a, wa_ref[...], preferred_element_type=F32)
    yb = jnp.dot(hB_ref[0], wb_ref[...], preferred_element_type=F32)
    y = _sigmoid(ga_ref[0].astype(F32)) * ya + _sigmoid(gb_ref[0].astype(F32)) * yb
    yo = jnp.dot(y.astype(BF16), wo_ref[...], preferred_element_type=F32)
    g1 = mod_ref[0][:, 2 * d:3 * d]
    o_ref[0] = x_ref[0] + g1 * yo


def _merge(hf, hb, p, h_b, x, mods3, mlstm_gain, wa, wb, wo, mo_region, ga_region, gb_region):
    b, s, d = x.shape
    tm = LAT_TILE
    wspec = lambda shape: pl.BlockSpec(shape, lambda bi, i: (0, 0), pipeline_mode=pl.Buffered(1))
    tok = lambda region: pl.BlockSpec((1, tm, d), lambda bi, i, r=region: (bi, i, r))
    return pl.pallas_call(
        functools.partial(_merge_kernel, d),
        grid=(b, s // tm),
        in_specs=[tok(0), tok(0), tok(mo_region), tok(ga_region), tok(gb_region), tok(0), tok(0),
                  pl.BlockSpec((1, 1, N_MOD * d), lambda bi, i: (bi, 0, 0)),
                  pl.BlockSpec((1, d), lambda bi, i: (0, 0)),
                  wspec(wa.shape), wspec(wb.shape), wspec(wo.shape)],
        out_specs=pl.BlockSpec((1, tm, d), lambda bi, i: (bi, i, 0)),
        out_shape=jax.ShapeDtypeStruct((b, s, d), F32),
        compiler_params=_cparams(("parallel", "arbitrary")),
        name="merge",
    )(hf, hb, p, p, p, h_b, x, mods3, mlstm_gain, wa, wb, wo)


def _ffn_kernel(d, hidden, x_ref, mod_ref, g_ref, wi_ref, wo_ref, o_ref):
    x1 = x_ref[0]
    mod = mod_ref[0]
    xn = _rms_mod(x1, g_ref[...], mod[:, 3 * d:4 * d], mod[:, 4 * d:5 * d]).astype(BF16)
    a = jnp.dot(xn, wi_ref[:, :hidden], preferred_element_type=F32)
    bb = jnp.dot(xn, wi_ref[:, hidden:], preferred_element_type=F32)
    hmid = (_silu(a) * bb).astype(BF16)
    y = jnp.dot(hmid, wo_ref[...], preferred_element_type=F32)
    o_ref[0] = x1 + mod[:, 5 * d:6 * d] * y


def _ffn(x1, mods3, norm2, wi, wo):
    b, s, d = x1.shape
    hidden = wo.shape[0]
    tm = LAT_TILE
    wspec = lambda shape: pl.BlockSpec(shape, lambda bi, i: (0, 0), pipeline_mode=pl.Buffered(1))
    return pl.pallas_call(
        functools.partial(_ffn_kernel, d, hidden),
        grid=(b, s // tm),
        in_specs=[pl.BlockSpec((1, tm, d), lambda bi, i: (bi, i, 0)),
                  pl.BlockSpec((1, 1, N_MOD * d), lambda bi, i: (bi, 0, 0)),
                  pl.BlockSpec((1, d), lambda bi, i: (0, 0)),
                  wspec(wi.shape), wspec(wo.shape)],
        out_specs=pl.BlockSpec((1, tm, d), lambda bi, i: (bi, i, 0)),
        out_shape=jax.ShapeDtypeStruct((b, s, d), F32),
        compiler_params=_cparams(("parallel", "arbitrary")),
        name="ffn",
    )(x1, mods3, norm2, wi, wo)


def _diff_lane_perm():
    l = np.arange(2 * DIFF_HEAD_DIM)
    half, r = l // DIFF_HEAD_DIM, l % DIFF_HEAD_DIM
    qsel, axis, f = r // 32, (r % 32) // ROPE_FREQS, r % ROPE_FREQS
    return qsel * DIFF_HEAD_DIM + axis * 2 * ROPE_FREQS + half * ROPE_FREQS + f, qsel, axis, f, half


def _rope_tables(n_lat, n_ctx):
    _, _, axis, f, half = _diff_lane_perm()
    pos = jnp.arange(n_lat, dtype=F32)
    row = jnp.floor(pos / GRID_W)
    col = pos - row * GRID_W
    inv = ROPE_BASE ** (-jnp.arange(ROPE_FREQS, dtype=F32) / ROPE_FREQS)
    ang = jnp.where(axis[None, :] == 0, row[:, None], col[:, None]) * inv[f][None, :]
    cos = jnp.concatenate([jnp.cos(ang), jnp.ones((n_ctx, LANES), F32)], axis=0)
    sin = jnp.sin(ang) * jnp.where(half == 0, -1.0, 1.0)[None, :]
    sin = jnp.concatenate([sin, jnp.zeros((n_ctx, LANES), F32)], axis=0)
    return cos, sin


def kernel(x, c, ctx, c_ctx, w_mod, b_mod, norm1, norm2, w_in, b_gate, conv_w, conv_b, mlstm_norm, q_norm, k_norm,
           lam_vecs, diff_norm, w_branch_a, w_branch_b, w_out, w_ffn_in, w_ffn_out):
    b, s, d = x.shape
    n_ctx = ctx.shape[1]
    assert w_mod.shape[0] == 1, "single-layer block"
    assert s % CHUNK == 0 and n_ctx % CHUNK == 0 and s % ATT_TQ == 0 and s % LAT_TILE == 0
    qk_w = MLSTM_HEADS * MLSTM_DQK
    v_w = MLSTM_HEADS * MLSTM_DV
    da_w = DIFF_HEADS * 2 * DIFF_HEAD_DIM
    n_gates = 4 * MLSTM_HEADS
    sizes = (qk_w, qk_w, v_w, v_w, n_gates, da_w, da_w, da_w, d, d)
    offs = np.concatenate([[0], np.cumsum(sizes)])
    seg = lambda i: w_in[0][:, offs[i]:offs[i + 1]]

    perm, qsel, _, _, _ = _diff_lane_perm()
    head_perm = (np.arange(DIFF_HEADS)[:, None] * 2 * DIFF_HEAD_DIM + perm[None, :]).reshape(-1)
    wbig = jnp.concatenate([seg(0), seg(1), seg(2), seg(3), seg(8), seg(9),
                            seg(5)[:, head_perm], seg(6)[:, head_perm], seg(7)], axis=1).astype(BF16)
    R_V, R_O, R_GA, R_GB = range(4)
    wg_f32 = jnp.pad(seg(4), ((0, 0), (0, LANES - n_gates)))
    wg_hi = wg_f32.astype(BF16)
    wg = jnp.concatenate([wg_hi, (wg_f32 - wg_hi.astype(F32)).astype(BF16)], axis=1)
    gate_bias = jnp.pad(b_gate[0], (0, LANES - n_gates)).reshape(1, LANES)
    conv_w8 = jnp.pad(conv_w[0], ((0, SUBLANES - CONV_K), (0, 0)))
    gq = q_norm[0][perm % DIFF_HEAD_DIM].reshape(1, LANES)
    gk = k_norm[0][perm % DIFF_HEAD_DIM].reshape(1, LANES)
    unit = np.concatenate([qsel, 2 + qsel])
    grp = jnp.asarray(unit[:, None] == unit[None, :], BF16)
    cos_t, sin_t = _rope_tables(s, n_ctx)

    cc = jnp.concatenate([c, c_ctx[None, :], jnp.zeros((SUBLANES - b - 1, d), F32)], axis=0)
    mods = _modulation(cc, w_mod[0], b_mod[0])
    mods3 = mods.reshape(SUBLANES, 1, N_MOD * d)

    qk, pm, qt, kr, vt, gates, gates_t = _projection(x, ctx, mods3, norm1, wbig, wg, gate_bias,
                                                     cos_t, sin_t, gq, gk, grp, conv_w8, conv_b)
    hf, hb = _mlstm(qk, pm, gates, gates_t, s, n_ctx, R_V)
    h_b = _attention(qt, kr, vt, lam_vecs[0], diff_norm, s)
    x1 = _merge(hf, hb, pm, h_b, x, mods3, mlstm_norm, w_branch_a[0].astype(BF16), w_branch_b[0].astype(BF16),
                w_out[0].astype(BF16), R_O, R_GA, R_GB)
    return _ffn(x1, mods3, norm2, w_ffn_in[0].astype(BF16), w_ffn_out[0].astype(BF16))
```

```python
import functools
import math

import jax
import jax.numpy as jnp
import numpy as np
from jax import lax
from jax.experimental import pallas as pl
from jax.experimental.pallas import tpu as pltpu

F32 = jnp.float32
BF16 = jnp.bfloat16
HIGHEST = lax.Precision.HIGHEST

GRID_W = 64
N_MOD = 6
MLSTM_HEADS = 4
MLSTM_DQK = 128
MLSTM_DV = 256
CONV_K = 5
DIFF_HEADS = 8
DIFF_HEAD_DIM = 64
ROPE_BASE = 10000.0
ROPE_FREQS = DIFF_HEAD_DIM // 4
RMS_EPS = 1e-6
LAM_INIT_L0 = 0.8 - 0.6 * math.exp(-0.3 * 0)
LOG2_E = math.log2(math.e)
Q_SCALE_LOG2 = DIFF_HEAD_DIM ** -0.5 * LOG2_E
V_ROWS = 2 * DIFF_HEAD_DIM + 16

LANES = 128
SUBLANES = 8
VMEM_LIMIT = 56 * 1024 * 1024

TOK_TILE = 256
LAT_TILE = 512
CHUNK = 256
MLSTM_BATCH = 1
ATT_TQ = 512
ATT_TK = 2816


def _cparams(sem):
    return pltpu.CompilerParams(dimension_semantics=sem, vmem_limit_bytes=VMEM_LIMIT)


def _sigmoid(x):
    return 1.0 / (1.0 + jnp.exp(-x))


def _silu(x):
    return x * _sigmoid(x)


def _log_sigmoid(x):
    return jnp.minimum(x, 0.0) - jnp.log(1.0 + jnp.exp(-jnp.abs(x)))


def _mod_kernel(c_ref, w_ref, b_ref, o_ref):
    a = _silu(c_ref[...])
    o_ref[...] = jnp.dot(a, w_ref[...], preferred_element_type=F32, precision=HIGHEST) + b_ref[...]


def _modulation(cc, w_mod, b_mod):
    rows, d = cc.shape
    n = w_mod.shape[1]
    tn = 1024
    return pl.pallas_call(
        _mod_kernel,
        grid=(n // tn,),
        in_specs=[pl.BlockSpec((rows, d), lambda j: (0, 0)),
                  pl.BlockSpec((d, tn), lambda j: (0, j)),
                  pl.BlockSpec((1, tn), lambda j: (0, j))],
        out_specs=pl.BlockSpec((rows, tn), lambda j: (0, j)),
        out_shape=jax.ShapeDtypeStruct((rows, n), F32),
        compiler_params=_cparams(("arbitrary",)),
        name="mod",
    )(cc, w_mod, b_mod.reshape(1, n))


def _rms_mod(xf, g, shift, scale):
    ms = jnp.mean(xf * xf, axis=-1, keepdims=True)
    return xf * lax.rsqrt(ms + RMS_EPS) * g * (1.0 + scale) + shift


def _proj_kernel(n_lat_tiles, d, x_ref, xprev_ref, xnext_ref, ctx_ref, mod_ref, g_ref, w_ref, wg_ref, gbias_ref,
                 cos_ref, sin_ref, gq_ref, gk_ref, grp_ref, cw_ref, cb_ref,
                 qk_ref, pm_ref, qt_ref, kr_ref, vt_ref, gates_ref, gates_t_ref,
                 ext_ref, yq_ref, yk_ref, yv_ref, gpre_ref, xb_ref, xlo_ref):
    i = pl.program_id(1)
    tm = x_ref.shape[1]

    @pl.when(i == 0)
    def _():
        ext_ref[...] = jnp.zeros_like(ext_ref)
        yq_ref[...] = jnp.zeros_like(yq_ref)
        yk_ref[...] = jnp.zeros_like(yk_ref)
        yv_ref[...] = jnp.zeros_like(yv_ref)
        gpre_ref[...] = jnp.zeros_like(gpre_ref)

    n_tiles = pl.num_programs(1) - 1
    ti = jnp.minimum(i, n_tiles - 1)
    xf = jnp.where(ti < n_lat_tiles, x_ref[0], ctx_ref[0])
    mod = mod_ref[0]
    gain1 = g_ref[...]
    xn = _rms_mod(xf, gain1, mod[:, 0:d], mod[:, d:2 * d])
    xb = xn.astype(BF16)
    halo = jnp.concatenate([xprev_ref[0], xnext_ref[0]], axis=0)
    xb_ref[0:tm, :] = xb
    xb_ref[tm:, :] = _rms_mod(halo, gain1, mod[:, 0:d], mod[:, d:2 * d]).astype(BF16)
    xlo_ref[...] = (xn - xb.astype(F32)).astype(BF16)

    def project(c0, width):
        return jnp.dot(xb_ref[0:tm, :], w_ref[:, c0:c0 + width], preferred_element_type=F32)

    taps = cw_ref[...]
    conv_bias = cb_ref[...]
    n_q_cols = d // 2

    def conv_columns(c_lo, c_hi):
        for c0 in range(c_lo, c_hi, LANES):
            acc = jnp.zeros((tm, LANES), F32) + conv_bias[:, c0:c0 + LANES]
            for k in range(CONV_K):
                r0 = SUBLANES - CONV_K // 2 + k
                acc = acc + ext_ref[r0:r0 + tm, c0:c0 + LANES] * taps[k:k + 1, c0:c0 + LANES]
            out = _silu(acc)
            if c0 < n_q_cols:
                out = out * MLSTM_DQK ** -0.5
            qk_ref[0, :, c0:c0 + LANES] = out.astype(BF16)

    cos = cos_ref[...]
    sin = sin_ref[...]
    grp = grp_ref[...]
    gq = gq_ref[...]
    gk = gk_ref[...]
    hw = 2 * DIFF_HEAD_DIM

    def scaled_sumsq(x2):
        ss = jnp.dot((x2 * x2).astype(BF16), grp, preferred_element_type=F32)
        return lax.rsqrt(ss * (1.0 / DIFF_HEAD_DIM) + RMS_EPS)

    def rotated(y):
        return y * cos + pltpu.roll(y, DIFF_HEAD_DIM, axis=1) * sin

    ones_rows = jnp.ones((V_ROWS - hw, tm), BF16)

    def attention_pair(pair):
        sl2 = slice(pair * 2 * hw, (pair + 1) * 2 * hw)
        yq2 = yq_ref[:, sl2]
        yk2 = yk_ref[:, sl2]
        qn = yq2 * scaled_sumsq(yq2)
        kn = yk2 * scaled_sumsq(yk2)
        for u in range(2):
            h = 2 * pair + u
            sl = slice(h * hw, (h + 1) * hw)
            su = slice(u * hw, (u + 1) * hw)
            qt_ref[0, sl, :] = (rotated(qn[:, su] * gq) * Q_SCALE_LOG2).T.astype(BF16)
            kr_ref[0, :, sl] = rotated(kn[:, su] * gk).astype(BF16)
            vt_ref[0, h * V_ROWS:h * V_ROWS + hw, :] = yv_ref[:, sl].T.astype(BF16)
            vt_ref[0, h * V_ROWS + hw:(h + 1) * V_ROWS, :] = ones_rows

    def gate_epilogue():
        g = gpre_ref[...]
        lf = _log_sigmoid(g)
        lf_hi = lf.astype(BF16)
        lf_lo = (lf - lf_hi.astype(F32)).astype(BF16)
        lf2 = jnp.concatenate([lf_hi, lf_lo], axis=1)
        rows = lax.broadcasted_iota(jnp.int32, (tm, tm), 0)
        cols = lax.broadcasted_iota(jnp.int32, (tm, tm), 1)
        lower = jnp.where(cols <= rows, 1.0, 0.0).astype(BF16)
        upper = jnp.where(cols >= rows, 1.0, 0.0).astype(BF16)
        cum_f2 = jnp.dot(lower, lf2, preferred_element_type=F32)
        cum_b2 = jnp.dot(upper, lf2, preferred_element_type=F32)
        cum_f = cum_f2[:, :LANES] + cum_f2[:, LANES:]
        cum_b = cum_b2[:, :LANES] + cum_b2[:, LANES:]
        lane = lax.broadcasted_iota(jnp.int32, (tm, LANES), 1)
        cum = jnp.where(lane < 2 * MLSTM_HEADS, cum_f, cum_b)
        a = g - pltpu.roll(cum, LANES - MLSTM_HEADS, axis=1)
        a = a * LOG2_E
        gates_ref[0] = jnp.where((lane % (2 * MLSTM_HEADS)) < MLSTM_HEADS, a, cum * LOG2_E)
        gates_t_ref[0, 0] = a.T[0:4 * MLSTM_HEADS, :]

    for r in range(4):
        pm_ref[0, :, r * d:(r + 1) * d] = project((1 + r) * d, d).astype(BF16)
        attention_pair(r)
    yq_ref[...] = project(5 * d, d)
    conv_columns(0, d // 2)
    yk_ref[...] = project(6 * d, d)
    conv_columns(d // 2, d)
    yv_ref[...] = project(7 * d, d)
    gate_epilogue()
    y_qk = jnp.dot(xb_ref[...], w_ref[:, 0:d], preferred_element_type=F32)
    has_prev = jnp.logical_and(ti > 0, ti < n_lat_tiles)
    has_next = ti < n_lat_tiles - 1
    ext_ref[0:SUBLANES, :] = jnp.where(has_prev, y_qk[tm:tm + SUBLANES], 0.0)
    ext_ref[SUBLANES:SUBLANES + tm, :] = y_qk[0:tm]
    ext_ref[SUBLANES + tm:, :] = jnp.where(has_next, y_qk[tm + SUBLANES:], 0.0)
    g2 = (jnp.dot(xb_ref[0:tm, :], wg_ref[...], preferred_element_type=F32)
          + jnp.dot(xlo_ref[...], wg_ref[...], preferred_element_type=F32))
    gpre_ref[...] = g2[:, :LANES] + g2[:, LANES:] + gbias_ref[...]


def _projection(x, ctx, mods3, norm1, wbig, wg, gate_bias, cos_t, sin_t, gq, gk, grp, conv_w8, conv_b):
    b, s, d = x.shape
    n_ctx = ctx.shape[1]
    tm = TOK_TILE
    assert tm == CHUNK
    assert n_ctx == tm
    n_lat_tiles = s // tm
    n_tiles = (s + n_ctx) // tm
    t = s + n_ctx
    n_out = wbig.shape[1]
    da_w = DIFF_HEADS * 2 * DIFF_HEAD_DIM
    halo_stride = tm // SUBLANES
    last_halo = s // SUBLANES - 1
    const = lambda shape: pl.BlockSpec(shape, lambda bi, i: (0,) * len(shape))
    cur = lambda i: jnp.minimum(i, n_tiles - 1)
    late = lambda i: jnp.maximum(i - 1, 0)
    tok = lambda width: pl.BlockSpec((1, tm, width), lambda bi, i: (bi, late(i), 0))
    tok_t = lambda rows: pl.BlockSpec((1, rows, tm), lambda bi, i: (bi, 0, late(i)))
    n_ctx_tiles = n_ctx // tm
    return pl.pallas_call(
        functools.partial(_proj_kernel, n_lat_tiles, d),
        grid=(b, n_tiles + 1),
        in_specs=[
            pl.BlockSpec((1, tm, d), lambda bi, i: (bi, jnp.minimum(i, n_lat_tiles - 1), 0)),
            pl.BlockSpec((1, SUBLANES, d),
                         lambda bi, i: (bi, jnp.clip(cur(i) * halo_stride - 1, 0, last_halo), 0)),
            pl.BlockSpec((1, SUBLANES, d),
                         lambda bi, i: (bi, jnp.clip((cur(i) + 1) * halo_stride, 0, last_halo), 0)),
            pl.BlockSpec((1, tm, d), lambda bi, i: (bi, jnp.clip(i - n_lat_tiles, 0, n_ctx_tiles - 1), 0)),
            pl.BlockSpec((1, 1, N_MOD * d), lambda bi, i: (jnp.where(i < n_lat_tiles, bi, b), 0, 0)),
            const((1, d)),
            pl.BlockSpec((d, n_out), lambda bi, i: (0, 0), pipeline_mode=pl.Buffered(1)),
            const((d, 2 * LANES)), const((1, LANES)),
            pl.BlockSpec((tm, LANES), lambda bi, i: (late(i), 0)),
            pl.BlockSpec((tm, LANES), lambda bi, i: (late(i), 0)),
            const((1, LANES)), const((1, LANES)), const((2 * LANES, 2 * LANES)),
            const((SUBLANES, d)), const((1, d)),
        ],
        out_specs=[tok(d), pl.BlockSpec((1, tm, 4 * d), lambda bi, i: (bi, cur(i), 0)),
                   tok_t(da_w), tok(da_w), tok_t(DIFF_HEADS * V_ROWS), tok(LANES),
                   pl.BlockSpec((1, 1, 4 * MLSTM_HEADS, tm), lambda bi, i: (bi, late(i), 0, 0))],
        out_shape=[jax.ShapeDtypeStruct((b, t, d), BF16),
                   jax.ShapeDtypeStruct((b, t, 4 * d), BF16),
                   jax.ShapeDtypeStruct((b, da_w, t), BF16),
                   jax.ShapeDtypeStruct((b, t, da_w), BF16),
                   jax.ShapeDtypeStruct((b, DIFF_HEADS * V_ROWS, t), BF16),
                   jax.ShapeDtypeStruct((b, t, LANES), F32),
                   jax.ShapeDtypeStruct((b, n_tiles, 4 * MLSTM_HEADS, tm), F32)],
        scratch_shapes=[pltpu.VMEM((tm + 2 * SUBLANES, d), F32),
                        pltpu.VMEM((tm, da_w), F32), pltpu.VMEM((tm, da_w), F32), pltpu.VMEM((tm, da_w), F32),
                        pltpu.VMEM((tm, LANES), F32),
                        pltpu.VMEM((tm + 2 * SUBLANES, d), BF16),
                        pltpu.VMEM((tm, d), BF16)],
        compiler_params=_cparams(("parallel", "arbitrary")),
        name="proj",
    )(x, x, x, ctx, mods3, norm1, wbig, wg, gate_bias, cos_t, sin_t, gq, gk, grp, conv_w8, conv_b)


def _mlstm_direction(reverse, q, k, v_aug, a_col, a_row, lf_cum_col, lf_total, c_ref, m_ref, hd):
    l = q.shape[0]
    dv = v_aug.shape[1] - LANES

    def wide(x, n_tiles):
        return jnp.concatenate([x] * n_tiles, axis=1)

    m_prev = m_ref[hd:hd + 1, :]
    c_aug = c_ref[hd]
    lf_cum_rep = jnp.broadcast_to(lf_cum_col, (l, LANES))
    a_rep = jnp.broadcast_to(a_col, (l, LANES))
    rows = lax.broadcasted_iota(jnp.int32, (l, l), 0)
    cols = lax.broadcasted_iota(jnp.int32, (l, l), 1)
    visible = (cols >= rows) if reverse else (cols <= rows)
    a_mat = jnp.where(visible, a_row, -jnp.inf)
    m_rep = jnp.maximum(jnp.broadcast_to(jnp.max(a_mat, axis=-1, keepdims=True), (l, LANES)), m_prev)
    decay = jnp.exp2(a_mat - wide(m_rep, l // LANES))
    s = lax.dot_general(q, k, (((1,), (1,)), ((), ())), preferred_element_type=F32)
    pmat = (s * decay).astype(BF16)
    n_wide = v_aug.shape[1] // LANES
    intra = jnp.dot(pmat, v_aug, preferred_element_type=F32)
    inter = jnp.dot(q, c_aug.astype(BF16), preferred_element_type=F32)
    numden = intra + wide(jnp.exp2(m_prev - m_rep), n_wide) * inter
    den = numden[:, dv:]
    floor = jnp.exp2(-(lf_cum_rep + m_rep))
    h = numden[:, :dv] * wide(1.0 / jnp.maximum(jnp.abs(den), floor), dv // LANES)
    m_last = jnp.maximum(jnp.max(a_rep, axis=0, keepdims=True), m_prev)
    w = jnp.exp2(a_rep - m_last)
    kw = (w * k.astype(F32)).astype(BF16)
    kv = lax.dot_general(kw, v_aug, (((0,), (0,)), ((), ())), preferred_element_type=F32)
    c_ref[hd] = wide(jnp.exp2(m_prev - m_last), n_wide) * c_aug + kv
    m_ref[hd:hd + 1, :] = lf_total + m_last
    return h


def _mlstm_kernel(qf_ref, kf_ref, vf_ref, gf_ref, gtf_ref, qb_ref, kb_ref, vb_ref, gb_ref, gtb_ref,
                  hf_ref, hb_ref, c_ref, m_ref):
    s_idx = pl.program_id(1)

    @pl.when(s_idx == 0)
    def _():
        c_ref[...] = jnp.zeros_like(c_ref)
        m_ref[...] = jnp.zeros_like(m_ref)

    nb, l = qf_ref.shape[0], qf_ref.shape[1]
    nh = MLSTM_HEADS
    ones_col = jnp.ones((l, LANES), BF16)
    for bb in range(nb):
        for reverse, q_ref, k_ref, v_ref, g_ref, gt_ref, h_ref in (
                (False, qf_ref, kf_ref, vf_ref, gf_ref, gtf_ref, hf_ref),
                (True, qb_ref, kb_ref, vb_ref, gb_ref, gtb_ref, hb_ref)):
            g = g_ref[bb]
            g_t = gt_ref[bb, 0]
            base = 2 * nh if reverse else 0
            for h in range(nh):
                ca, cc = base + h, base + nh + h
                hd = bb * 2 * nh + (nh if reverse else 0) + h
                lf_cum_col = g[:, cc:cc + 1]
                lf_total = g[0:1, cc:cc + 1] if reverse else g[l - 1:l, cc:cc + 1]
                q = q_ref[bb, :, h * MLSTM_DQK:(h + 1) * MLSTM_DQK]
                k = k_ref[bb, :, h * MLSTM_DQK:(h + 1) * MLSTM_DQK]
                v_aug = jnp.concatenate([v_ref[bb, :, h * MLSTM_DV:(h + 1) * MLSTM_DV], ones_col], axis=1)
                hval = _mlstm_direction(reverse, q, k, v_aug, g[:, ca:ca + 1], g_t[ca:ca + 1, :],
                                        lf_cum_col, lf_total, c_ref, m_ref, hd)
                h_ref[bb, :, h * MLSTM_DV:(h + 1) * MLSTM_DV] = hval.astype(h_ref.dtype)


def _mlstm(qk, pm, gates, gates_t, n_lat, n_ctx, v_region):
    b, t, _ = qk.shape
    l = CHUNK
    n_lat_c = n_lat // l
    n_ctx_c = n_ctx // l
    n_steps = n_lat_c + n_ctx_c
    qw = MLSTM_HEADS * MLSTM_DQK
    vw = MLSTM_HEADS * MLSTM_DV

    def fwd(si):
        return jnp.where(si < n_ctx_c, n_lat_c + si, si - n_ctx_c)

    def bwd(si):
        return jnp.where(si < n_ctx_c, n_steps - 1 - si, n_steps - 1 - si)

    nb = MLSTM_BATCH
    assert b % nb == 0
    in_specs = []
    for order in (fwd, bwd):
        in_specs += [
            pl.BlockSpec((nb, l, qw), lambda bi, si, o=order: (bi, o(si), 0)),
            pl.BlockSpec((nb, l, qw), lambda bi, si, o=order: (bi, o(si), 1)),
            pl.BlockSpec((nb, l, vw), lambda bi, si, o=order: (bi, o(si), v_region)),
            pl.BlockSpec((nb, l, LANES), lambda bi, si, o=order: (bi, o(si), 0)),
            pl.BlockSpec((nb, 1, 4 * MLSTM_HEADS, l), lambda bi, si, o=order: (bi, o(si), 0, 0)),
        ]
    out_specs = [pl.BlockSpec((nb, l, vw), lambda bi, si: (bi, fwd(si), 0)),
                 pl.BlockSpec((nb, l, vw), lambda bi, si: (bi, bwd(si), 0))]
    n_scans = nb * 2 * MLSTM_HEADS
    return pl.pallas_call(
        _mlstm_kernel,
        grid=(b // nb, n_steps),
        in_specs=in_specs,
        out_specs=out_specs,
        out_shape=[jax.ShapeDtypeStruct((b, t, vw), BF16)] * 2,
        scratch_shapes=[pltpu.VMEM((n_scans, MLSTM_DQK, MLSTM_DV + LANES), F32),
                        pltpu.VMEM((n_scans, LANES), F32)],
        compiler_params=_cparams(("parallel", "arbitrary")),
        name="mlstm",
    )(qk, qk, pm, gates, gates_t, qk, qk, pm, gates, gates_t)


def _attn_kernel(n_keys, tq, qt_ref, k_ref, vt_ref, lam_ref, gain_ref, o_ref,
                 qq_ref, s_ref, pm_ref, e_ref, al_ref, m_ref, acc_ref):
    tk = ATT_TK
    n = n_keys // tk
    nq = qt_ref.shape[2] // tq
    width = 2 * tq
    row = lax.broadcasted_iota(jnp.int32, (qt_ref.shape[1], tq), 0)
    first = (row % DIFF_HEAD_DIM) < (DIFF_HEAD_DIM // 2)

    def mask_queries(qb, carry):
        qt = qt_ref[0, :, pl.ds(pl.multiple_of(qb * tq, tq), tq)]
        zero = jnp.zeros_like(qt)
        qq_ref[qb] = jnp.concatenate([jnp.where(first, qt, zero), jnp.where(first, zero, qt)], axis=1)
        return carry

    lax.fori_loop(0, nq, mask_queries, 0)
    acc_ref[...] = jnp.zeros_like(acc_ref)

    def scores(j, qb):
        off = pl.multiple_of(j * tk, tk)
        s = jnp.dot(k_ref[0, pl.ds(off, tk), :], qq_ref[qb], preferred_element_type=F32)
        s_ref[...] = s
        pm_ref[...] = jnp.max(s.reshape(tk // SUBLANES, SUBLANES, width), axis=0)

    def softmax(j, qb):
        m_old = jnp.where(j == 0, -jnp.inf, m_ref[...])
        m_new = jnp.maximum(m_old, jnp.max(pm_ref[...], axis=0, keepdims=True))
        alpha = jnp.exp2(m_old - m_new)
        e = jnp.exp2(s_ref[...] - m_new)
        e_ref[...] = e.astype(BF16)
        al_ref[...] = alpha
        m_ref[...] = m_new

    def accumulate(j):
        off = pl.multiple_of(j * tk, tk)
        pv = jnp.dot(vt_ref[0, :, pl.ds(off, tk)], e_ref[...], preferred_element_type=F32)
        acc_ref[...] = al_ref[...] * acc_ref[...] + pv

    def finalize(qb):
        lv = lam_ref[...]
        lam = (jnp.exp(jnp.sum(lv[0:1] * lv[1:2], axis=-1, keepdims=True))
               - jnp.exp(jnp.sum(lv[2:3] * lv[3:4], axis=-1, keepdims=True)) + LAM_INIT_L0)
        hw = o_ref.shape[2]
        inv_l = 1.0 / acc_ref[hw:hw + 1, :]
        o_all = acc_ref[0:hw, :] * inv_l
        o_t = o_all[:, :tq] - lam * o_all[:, tq:]
        ms = jnp.mean(o_t * o_t, axis=0, keepdims=True)
        out = (o_t * lax.rsqrt(ms + RMS_EPS)).T * (gain_ref[...] * (1.0 - LAM_INIT_L0))
        o_ref[0, pl.ds(pl.multiple_of(qb * tq, tq), tq), :] = out.astype(o_ref.dtype)

    def advance(j, qb):
        wrap = j == n - 1
        return jnp.where(wrap, 0, j + 1), jnp.where(wrap, qb + 1, qb)

    zero_i = jnp.int32(0)
    item0 = (zero_i, zero_i)
    item1 = advance(*item0)
    scores(*item0)
    softmax(*item0)
    scores(*item1)

    def iteration(t, carry):
        ja, qa, jm, qm, js, qs = carry
        accumulate(ja)
        softmax(jm, qm)
        scores(js, qs)

        @pl.when(ja == n - 1)
        def _():
            finalize(qa)

        return (jm, qm, js, qs) + advance(js, qs)

    item2 = advance(*item1)
    lax.fori_loop(2, nq * n, iteration, item0 + item1 + item2)
    accumulate(n - 2)
    softmax(n - 1, nq - 1)
    accumulate(n - 1)
    finalize(nq - 1)


def _attention(qt, kr, vt, lam_vecs, diff_gain, n_lat):
    b, t, _ = kr.shape
    tq = ATT_TQ
    hw = 2 * DIFF_HEAD_DIM
    nq = n_lat // tq
    assert t // ATT_TK >= 2
    return pl.pallas_call(
        functools.partial(_attn_kernel, t, tq),
        grid=(b, DIFF_HEADS),
        in_specs=[pl.BlockSpec((1, hw, n_lat), lambda bi, h: (bi, h, 0)),
                  pl.BlockSpec((1, t, hw), lambda bi, h: (bi, 0, h)),
                  pl.BlockSpec((1, V_ROWS, t), lambda bi, h: (bi, h, 0)),
                  pl.BlockSpec(lam_vecs.shape, lambda bi, h: (0, 0)),
                  pl.BlockSpec((1, hw), lambda bi, h: (0, 0))],
        out_specs=pl.BlockSpec((1, n_lat, hw), lambda bi, h: (bi, 0, h)),
        out_shape=jax.ShapeDtypeStruct((b, n_lat, DIFF_HEADS * hw), BF16),
        scratch_shapes=[pltpu.VMEM((nq, hw, 2 * tq), BF16),
                        pltpu.VMEM((ATT_TK, 2 * tq), F32),
                        pltpu.VMEM((SUBLANES, 2 * tq), F32),
                        pltpu.VMEM((ATT_TK, 2 * tq), BF16),
                        pltpu.VMEM((1, 2 * tq), F32),
                        pltpu.VMEM((1, 2 * tq), F32),
                        pltpu.VMEM((V_ROWS, 2 * tq), F32)],
        compiler_params=_cparams(("parallel", "arbitrary")),
        name="attn",
    )(qt, kr, vt, lam_vecs, diff_gain)


def _merge_kernel(d, hf_ref, hb_ref, mo_ref, ga_ref, gb_ref, hB_ref, x_ref, mod_ref, gain_ref,
                  wa_ref, wb_ref, wo_ref, o_ref):
    gain = gain_ref[...]
    parts = []
    for h in range(MLSTM_HEADS):
        sl = slice(h * MLSTM_DV, (h + 1) * MLSTM_DV)
        hs = hf_ref[0, :, sl].astype(F32) + hb_ref[0, :, sl].astype(F32)
        ms = jnp.mean(hs * hs, axis=-1, keepdims=True)
        hn = hs * lax.rsqrt(ms + RMS_EPS) * gain[:, sl]
        parts.append((_sigmoid(mo_ref[0, :, sl].astype(F32)) * hn).astype(BF16))
    h_a = jnp.concatenate(parts, axis=1)
    ya = jnp.dot(h_a, wa_ref[...], preferred_element_type=F32)
    yb = jnp.dot(hB_ref[0], wb_ref[...], preferred_element_type=F32)
    y = _sigmoid(ga_ref[0].astype(F32)) * ya + _sigmoid(gb_ref[0].astype(F32)) * yb
    yo = jnp.dot(y.astype(BF16), wo_ref[...], preferred_element_type=F32)
    g1 = mod_ref[0][:, 2 * d:3 * d]
    o_ref[0] = x_ref[0] + g1 * yo


def _merge(hf, hb, p, h_b, x, mods3, mlstm_gain, wa, wb, wo, mo_region, ga_region, gb_region):
    b, s, d = x.shape
    tm = LAT_TILE
    wspec = lambda shape: pl.BlockSpec(shape, lambda bi, i: (0, 0), pipeline_mode=pl.Buffered(1))
    tok = lambda region: pl.BlockSpec((1, tm, d), lambda bi, i, r=region: (bi, i, r))
    return pl.pallas_call(
        functools.partial(_merge_kernel, d),
        grid=(b, s // tm),
        in_specs=[tok(0), tok(0), tok(mo_region), tok(ga_region), tok(gb_region), tok(0), tok(0),
                  pl.BlockSpec((1, 1, N_MOD * d), lambda bi, i: (bi, 0, 0)),
                  pl.BlockSpec((1, d), lambda bi, i: (0, 0)),
                  wspec(wa.shape), wspec(wb.shape), wspec(wo.shape)],
        out_specs=pl.BlockSpec((1, tm, d), lambda bi, i: (bi, i, 0)),
        out_shape=jax.ShapeDtypeStruct((b, s, d), F32),
        compiler_params=_cparams(("parallel", "arbitrary")),
        name="merge",
    )(hf, hb, p, p, p, h_b, x, mods3, mlstm_gain, wa, wb, wo)


def _ffn_kernel(d, hidden, x_ref, mod_ref, g_ref, wi_ref, wo_ref, o_ref):
    x1 = x_ref[0]
    mod = mod_ref[0]
    xn = _rms_mod(x1, g_ref[...], mod[:, 3 * d:4 * d], mod[:, 4 * d:5 * d]).astype(BF16)
    a = jnp.dot(xn, wi_ref[:, :hidden], preferred_element_type=F32)
    bb = jnp.dot(xn, wi_ref[:, hidden:], preferred_element_type=F32)
    hmid = (_silu(a) * bb).astype(BF16)
    y = jnp.dot(hmid, wo_ref[...], preferred_element_type=F32)
    o_ref[0] = x1 + mod[:, 5 * d:6 * d] * y


def _ffn(x1, mods3, norm2, wi, wo):
    b, s, d = x1.shape
    hidden = wo.shape[0]
    tm = LAT_TILE
    wspec = lambda shape: pl.BlockSpec(shape, lambda bi, i: (0, 0), pipeline_mode=pl.Buffered(1))
    return pl.pallas_call(
        functools.partial(_ffn_kernel, d, hidden),
        grid=(b, s // tm),
        in_specs=[pl.BlockSpec((1, tm, d), lambda bi, i: (bi, i, 0)),
                  pl.BlockSpec((1, 1, N_MOD * d), lambda bi, i: (bi, 0, 0)),
                  pl.BlockSpec((1, d), lambda bi, i: (0, 0)),
                  wspec(wi.shape), wspec(wo.shape)],
        out_specs=pl.BlockSpec((1, tm, d), lambda bi, i: (bi, i, 0)),
        out_shape=jax.ShapeDtypeStruct((b, s, d), F32),
        compiler_params=_cparams(("parallel", "arbitrary")),
        name="ffn",
    )(x1, mods3, norm2, wi, wo)


def _diff_lane_perm():
    l = np.arange(2 * DIFF_HEAD_DIM)
    half, r = l // DIFF_HEAD_DIM, l % DIFF_HEAD_DIM
    qsel, axis, f = r // 32, (r % 32) // ROPE_FREQS, r % ROPE_FREQS
    return qsel * DIFF_HEAD_DIM + axis * 2 * ROPE_FREQS + half * ROPE_FREQS + f, qsel, axis, f, half


def _rope_tables(n_lat, n_ctx):
    _, _, axis, f, half = _diff_lane_perm()
    pos = np.arange(n_lat)
    row, col = pos // GRID_W, pos % GRID_W
    inv = (ROPE_BASE ** (-np.arange(ROPE_FREQS, dtype=np.float32) / ROPE_FREQS)).astype(np.float32)
    ang = (np.where(axis[None, :] == 0, row[:, None], col[:, None]).astype(np.float32) * inv[f][None, :])
    cos = np.concatenate([np.cos(ang), np.ones((n_ctx, LANES))], axis=0)
    sin = np.concatenate([np.sin(ang) * np.where(half == 0, -1.0, 1.0)[None, :], np.zeros((n_ctx, LANES))], axis=0)
    return jnp.asarray(cos, F32), jnp.asarray(sin, F32)


def kernel(x, c, ctx, c_ctx, w_mod, b_mod, norm1, norm2, w_in, b_gate, conv_w, conv_b, mlstm_norm, q_norm, k_norm,
           lam_vecs, diff_norm, w_branch_a, w_branch_b, w_out, w_ffn_in, w_ffn_out):
    b, s, d = x.shape
    n_ctx = ctx.shape[1]
    assert w_mod.shape[0] == 1, "single-layer block"
    assert s % CHUNK == 0 and n_ctx % CHUNK == 0 and s % ATT_TQ == 0 and s % LAT_TILE == 0
    qk_w = MLSTM_HEADS * MLSTM_DQK
    v_w = MLSTM_HEADS * MLSTM_DV
    da_w = DIFF_HEADS * 2 * DIFF_HEAD_DIM
    n_gates = 4 * MLSTM_HEADS
    sizes = (qk_w, qk_w, v_w, v_w, n_gates, da_w, da_w, da_w, d, d)
    offs = np.concatenate([[0], np.cumsum(sizes)])
    seg = lambda i: w_in[0][:, offs[i]:offs[i + 1]]

    perm, qsel, _, _, _ = _diff_lane_perm()
    head_perm = (np.arange(DIFF_HEADS)[:, None] * 2 * DIFF_HEAD_DIM + perm[None, :]).reshape(-1)
    segb = lambda i: seg(i).astype(BF16)
    wbig = jnp.concatenate([segb(0), segb(1), segb(2), segb(3), segb(8), segb(9),
                            segb(5)[:, head_perm], segb(6)[:, head_perm], segb(7)], axis=1)
    R_V, R_O, R_GA, R_GB = range(4)
    wg_f32 = jnp.pad(seg(4), ((0, 0), (0, LANES - n_gates)))
    wg_hi = wg_f32.astype(BF16)
    wg = jnp.concatenate([wg_hi, (wg_f32 - wg_hi.astype(F32)).astype(BF16)], axis=1)
    gate_bias = jnp.pad(b_gate[0], (0, LANES - n_gates)).reshape(1, LANES)
    conv_w8 = jnp.pad(conv_w[0], ((0, SUBLANES - CONV_K), (0, 0)))
    gq = q_norm[0][perm % DIFF_HEAD_DIM].reshape(1, LANES)
    gk = k_norm[0][perm % DIFF_HEAD_DIM].reshape(1, LANES)
    unit = np.concatenate([qsel, 2 + qsel])
    grp = jnp.asarray(unit[:, None] == unit[None, :], BF16)
    cos_t, sin_t = _rope_tables(s, n_ctx)

    cc = jnp.concatenate([c, c_ctx[None, :], jnp.zeros((SUBLANES - b - 1, d), F32)], axis=0)
    mods = _modulation(cc, w_mod[0], b_mod[0])
    mods3 = mods.reshape(SUBLANES, 1, N_MOD * d)

    qk, pm, qt, kr, vt, gates, gates_t = _projection(x, ctx, mods3, norm1, wbig, wg, gate_bias,
                                                     cos_t, sin_t, gq, gk, grp, conv_w8, conv_b)
    hf, hb = _mlstm(qk, pm, gates, gates_t, s, n_ctx, R_V)
    h_b = _attention(qt, kr, vt, lam_vecs[0], diff_norm, s)
    x1 = _merge(hf, hb, pm, h_b, x, mods3, mlstm_norm, w_branch_a[0].astype(BF16), w_branch_b[0].astype(BF16),
                w_out[0].astype(BF16), R_O, R_GA, R_GB)
    return _ffn(x1, mods3, norm2, w_ffn_in[0].astype(BF16), w_ffn_out[0].astype(BF16))
```

```python
import functools
import math

import jax
import jax.numpy as jnp
import numpy as np
from jax import lax
from jax.experimental import pallas as pl
from jax.experimental.pallas import tpu as pltpu

F32 = jnp.float32
BF16 = jnp.bfloat16
HIGHEST = lax.Precision.HIGHEST

GRID_W = 64
N_MOD = 6
MLSTM_HEADS = 4
MLSTM_DQK = 128
MLSTM_DV = 256
CONV_K = 5
DIFF_HEADS = 8
DIFF_HEAD_DIM = 64
ROPE_BASE = 10000.0
ROPE_FREQS = DIFF_HEAD_DIM // 4
RMS_EPS = 1e-6
LAM_INIT_L0 = 0.8 - 0.6 * math.exp(-0.3 * 0)
LOG2_E = math.log2(math.e)
Q_SCALE_LOG2 = DIFF_HEAD_DIM ** -0.5 * LOG2_E
V_ROWS = 2 * DIFF_HEAD_DIM + 16

LANES = 128
SUBLANES = 8
VMEM_LIMIT = 56 * 1024 * 1024

TOK_TILE = 256
LAT_TILE = 512
CHUNK = 256
MLSTM_BATCH = 1
ATT_TQ = 512
ATT_TK = 2816


def _cparams(sem):
    return pltpu.CompilerParams(dimension_semantics=sem, vmem_limit_bytes=VMEM_LIMIT)


def _sigmoid(x):
    return 0.5 + 0.5 * jnp.tanh(0.5 * x)


def _silu(x):
    h = 0.5 * x
    return h + h * jnp.tanh(h)


def _log_sigmoid(x):
    return jnp.minimum(x, 0.0) - jnp.log(1.0 + jnp.exp(-jnp.abs(x)))


def _mod_kernel(c_ref, w_ref, b_ref, o_ref):
    a = _silu(c_ref[...])
    o_ref[...] = jnp.dot(a, w_ref[...], preferred_element_type=F32, precision=HIGHEST) + b_ref[...]


def _modulation(cc, w_mod, b_mod):
    rows, d = cc.shape
    n = w_mod.shape[1]
    tn = 1024
    return pl.pallas_call(
        _mod_kernel,
        grid=(n // tn,),
        in_specs=[pl.BlockSpec((rows, d), lambda j: (0, 0)),
                  pl.BlockSpec((d, tn), lambda j: (0, j)),
                  pl.BlockSpec((1, tn), lambda j: (0, j))],
        out_specs=pl.BlockSpec((rows, tn), lambda j: (0, j)),
        out_shape=jax.ShapeDtypeStruct((rows, n), F32),
        compiler_params=_cparams(("arbitrary",)),
        name="mod",
    )(cc, w_mod, b_mod.reshape(1, n))


def _rms_mod(xf, g, shift, scale):
    ms = jnp.mean(xf * xf, axis=-1, keepdims=True)
    return xf * lax.rsqrt(ms + RMS_EPS) * g * (1.0 + scale) + shift


def _proj_kernel(n_lat_tiles, d, x_ref, xprev_ref, xnext_ref, ctx_ref, mod_ref, g_ref, w_ref, wg_ref, gbias_ref,
                 cos_ref, sin_ref, gq_ref, gk_ref, grp_ref, cw_ref, cb_ref,
                 qk_ref, pm_ref, qt_ref, kr_ref, vt_ref, gates_ref, gates_t_ref,
                 ext_ref, yq_ref, yk_ref, yv_ref, gpre_ref, xb_ref, xlo_ref):
    i = pl.program_id(1)
    tm = x_ref.shape[1]

    @pl.when(i == 0)
    def _():
        ext_ref[...] = jnp.zeros_like(ext_ref)
        yq_ref[...] = jnp.zeros_like(yq_ref)
        yk_ref[...] = jnp.zeros_like(yk_ref)
        yv_ref[...] = jnp.zeros_like(yv_ref)
        gpre_ref[...] = jnp.zeros_like(gpre_ref)

    n_tiles = pl.num_programs(1) - 1
    ti = jnp.minimum(i, n_tiles - 1)
    xf = jnp.where(ti < n_lat_tiles, x_ref[0], ctx_ref[0])
    mod = mod_ref[0]
    gain1 = g_ref[...]
    xn = _rms_mod(xf, gain1, mod[:, 0:d], mod[:, d:2 * d])
    xb = xn.astype(BF16)
    halo = jnp.concatenate([xprev_ref[0], xnext_ref[0]], axis=0)
    xb_ref[0:tm, :] = xb
    xb_ref[tm:, :] = _rms_mod(halo, gain1, mod[:, 0:d], mod[:, d:2 * d]).astype(BF16)
    xlo_ref[...] = (xn - xb.astype(F32)).astype(BF16)

    def project(c0, width):
        return jnp.dot(xb_ref[0:tm, :], w_ref[:, c0:c0 + width], preferred_element_type=F32)

    taps = cw_ref[...]
    conv_bias = cb_ref[...]
    n_q_cols = d // 2

    w2 = 2 * LANES

    def conv_columns(c_lo, c_hi):
        for c0 in range(c_lo, c_hi, w2):
            acc = jnp.zeros((tm, w2), F32) + conv_bias[:, c0:c0 + w2]
            for k in range(CONV_K):
                r0 = SUBLANES - CONV_K // 2 + k
                acc = acc + ext_ref[r0:r0 + tm, c0:c0 + w2] * taps[k:k + 1, c0:c0 + w2]
            out = _silu(acc)
            if c0 < n_q_cols:
                out = out * MLSTM_DQK ** -0.5
            qk_ref[0, :, c0:c0 + w2] = out.astype(BF16)

    cos = cos_ref[...]
    sin = sin_ref[...]
    grp = grp_ref[...]
    gq = gq_ref[...]
    gk = gk_ref[...]
    hw = 2 * DIFF_HEAD_DIM

    def scaled_sumsq(x2):
        ss = jnp.dot((x2 * x2).astype(BF16), grp, preferred_element_type=F32)
        return lax.rsqrt(ss * (1.0 / DIFF_HEAD_DIM) + RMS_EPS)

    def rotated(y):
        return y * cos + pltpu.roll(y, DIFF_HEAD_DIM, axis=1) * sin

    ones_rows = jnp.ones((V_ROWS - hw, tm), BF16)

    def attention_pair(pair):
        sl2 = slice(pair * 2 * hw, (pair + 1) * 2 * hw)
        yq2 = yq_ref[:, sl2]
        yk2 = yk_ref[:, sl2]
        qn = yq2 * scaled_sumsq(yq2)
        kn = yk2 * scaled_sumsq(yk2)
        for u in range(2):
            h = 2 * pair + u
            sl = slice(h * hw, (h + 1) * hw)
            su = slice(u * hw, (u + 1) * hw)
            qt_ref[0, sl, :] = (rotated(qn[:, su] * gq) * Q_SCALE_LOG2).T.astype(BF16)
            kr_ref[0, :, sl] = rotated(kn[:, su] * gk).astype(BF16)
            vt_ref[0, h * V_ROWS:h * V_ROWS + hw, :] = yv_ref[:, sl].T.astype(BF16)
            vt_ref[0, h * V_ROWS + hw:(h + 1) * V_ROWS, :] = ones_rows

    def gate_epilogue():
        g = gpre_ref[...]
        lf = _log_sigmoid(g)
        lf_hi = lf.astype(BF16)
        lf_lo = (lf - lf_hi.astype(F32)).astype(BF16)
        lf2 = jnp.concatenate([lf_hi, lf_lo], axis=1)
        rows = lax.broadcasted_iota(jnp.int32, (tm, tm), 0)
        cols = lax.broadcasted_iota(jnp.int32, (tm, tm), 1)
        lower = jnp.where(cols <= rows, 1.0, 0.0).astype(BF16)
        upper = jnp.where(cols >= rows, 1.0, 0.0).astype(BF16)
        cum_f2 = jnp.dot(lower, lf2, preferred_element_type=F32)
        cum_b2 = jnp.dot(upper, lf2, preferred_element_type=F32)
        cum_f = cum_f2[:, :LANES] + cum_f2[:, LANES:]
        cum_b = cum_b2[:, :LANES] + cum_b2[:, LANES:]
        lane = lax.broadcasted_iota(jnp.int32, (tm, LANES), 1)
        cum = jnp.where(lane < 2 * MLSTM_HEADS, cum_f, cum_b)
        a = g - pltpu.roll(cum, LANES - MLSTM_HEADS, axis=1)
        a = a * LOG2_E
        gates_ref[0] = jnp.where((lane % (2 * MLSTM_HEADS)) < MLSTM_HEADS, a, cum * LOG2_E)
        gates_t_ref[0, 0] = a.T[0:4 * MLSTM_HEADS, :]

    for r in range(4):
        pm_ref[0, :, r * d:(r + 1) * d] = project((1 + r) * d, d).astype(BF16)
        attention_pair(r)
    yq_ref[...] = project(5 * d, d)
    conv_columns(0, d // 2)
    yk_ref[...] = project(6 * d, d)
    conv_columns(d // 2, d)
    yv_ref[...] = project(7 * d, d)
    gate_epilogue()
    y_qk = jnp.dot(xb_ref[...], w_ref[:, 0:d], preferred_element_type=F32)
    has_prev = jnp.logical_and(ti > 0, ti < n_lat_tiles)
    has_next = ti < n_lat_tiles - 1
    ext_ref[0:SUBLANES, :] = jnp.where(has_prev, y_qk[tm:tm + SUBLANES], 0.0)
    ext_ref[SUBLANES:SUBLANES + tm, :] = y_qk[0:tm]
    ext_ref[SUBLANES + tm:, :] = jnp.where(has_next, y_qk[tm + SUBLANES:], 0.0)
    g2 = (jnp.dot(xb_ref[0:tm, :], wg_ref[...], preferred_element_type=F32)
          + jnp.dot(xlo_ref[...], wg_ref[...], preferred_element_type=F32))
    gpre_ref[...] = g2[:, :LANES] + g2[:, LANES:] + gbias_ref[...]


def _projection(x, ctx, mods3, norm1, wbig, wg, gate_bias, cos_t, sin_t, gq, gk, grp, conv_w8, conv_b):
    b, s, d = x.shape
    n_ctx = ctx.shape[1]
    tm = TOK_TILE
    assert tm == CHUNK
    assert n_ctx == tm
    n_lat_tiles = s // tm
    n_tiles = (s + n_ctx) // tm
    t = s + n_ctx
    n_out = wbig.shape[1]
    da_w = DIFF_HEADS * 2 * DIFF_HEAD_DIM
    halo_stride = tm // SUBLANES
    last_halo = s // SUBLANES - 1
    const = lambda shape: pl.BlockSpec(shape, lambda bi, i: (0,) * len(shape))
    cur = lambda i: jnp.minimum(i, n_tiles - 1)
    late = lambda i: jnp.maximum(i - 1, 0)
    tok = lambda width: pl.BlockSpec((1, tm, width), lambda bi, i: (bi, late(i), 0))
    tok_t = lambda rows: pl.BlockSpec((1, rows, tm), lambda bi, i: (bi, 0, late(i)))
    n_ctx_tiles = n_ctx // tm
    return pl.pallas_call(
        functools.partial(_proj_kernel, n_lat_tiles, d),
        grid=(b, n_tiles + 1),
        in_specs=[
            pl.BlockSpec((1, tm, d), lambda bi, i: (bi, jnp.minimum(i, n_lat_tiles - 1), 0)),
            pl.BlockSpec((1, SUBLANES, d),
                         lambda bi, i: (bi, jnp.clip(cur(i) * halo_stride - 1, 0, last_halo), 0)),
            pl.BlockSpec((1, SUBLANES, d),
                         lambda bi, i: (bi, jnp.clip((cur(i) + 1) * halo_stride, 0, last_halo), 0)),
            pl.BlockSpec((1, tm, d), lambda bi, i: (bi, jnp.clip(i - n_lat_tiles, 0, n_ctx_tiles - 1), 0)),
            pl.BlockSpec((1, 1, N_MOD * d), lambda bi, i: (jnp.where(i < n_lat_tiles, bi, b), 0, 0)),
            const((1, d)),
            pl.BlockSpec((d, n_out), lambda bi, i: (0, 0), pipeline_mode=pl.Buffered(1)),
            const((d, 2 * LANES)), const((1, LANES)),
            pl.BlockSpec((tm, LANES), lambda bi, i: (late(i), 0)),
            pl.BlockSpec((tm, LANES), lambda bi, i: (late(i), 0)),
            const((1, LANES)), const((1, LANES)), const((2 * LANES, 2 * LANES)),
            const((SUBLANES, d)), const((1, d)),
        ],
        out_specs=[tok(d), pl.BlockSpec((1, tm, 4 * d), lambda bi, i: (bi, cur(i), 0)),
                   tok_t(da_w), tok(da_w), tok_t(DIFF_HEADS * V_ROWS), tok(LANES),
                   pl.BlockSpec((1, 1, 4 * MLSTM_HEADS, tm), lambda bi, i: (bi, late(i), 0, 0))],
        out_shape=[jax.ShapeDtypeStruct((b, t, d), BF16),
                   jax.ShapeDtypeStruct((b, t, 4 * d), BF16),
                   jax.ShapeDtypeStruct((b, da_w, t), BF16),
                   jax.ShapeDtypeStruct((b, t, da_w), BF16),
                   jax.ShapeDtypeStruct((b, DIFF_HEADS * V_ROWS, t), BF16),
                   jax.ShapeDtypeStruct((b, t, LANES), F32),
                   jax.ShapeDtypeStruct((b, n_tiles, 4 * MLSTM_HEADS, tm), F32)],
        scratch_shapes=[pltpu.VMEM((tm + 2 * SUBLANES, d), F32),
                        pltpu.VMEM((tm, da_w), F32), pltpu.VMEM((tm, da_w), F32), pltpu.VMEM((tm, da_w), F32),
                        pltpu.VMEM((tm, LANES), F32),
                        pltpu.VMEM((tm + 2 * SUBLANES, d), BF16),
                        pltpu.VMEM((tm, d), BF16)],
        compiler_params=_cparams(("parallel", "arbitrary")),
        name="proj",
    )(x, x, x, ctx, mods3, norm1, wbig, wg, gate_bias, cos_t, sin_t, gq, gk, grp, conv_w8, conv_b)


def _mlstm_direction(reverse, q, k, v_aug, a_col, a_row, lf_cum_col, lf_total, c_ref, m_ref, hd):
    l = q.shape[0]
    dv = v_aug.shape[1] - LANES

    def wide(x, n_tiles):
        return jnp.concatenate([x] * n_tiles, axis=1)

    m_prev = m_ref[hd:hd + 1, :]
    c_aug = c_ref[hd]
    lf_cum_rep = jnp.broadcast_to(lf_cum_col, (l, LANES))
    a_rep = jnp.broadcast_to(a_col, (l, LANES))
    rows = lax.broadcasted_iota(jnp.int32, (l, l), 0)
    cols = lax.broadcasted_iota(jnp.int32, (l, l), 1)
    visible = (cols >= rows) if reverse else (cols <= rows)
    a_mat = jnp.where(visible, a_row, -jnp.inf)
    m_rep = jnp.maximum(jnp.broadcast_to(jnp.max(a_mat, axis=-1, keepdims=True), (l, LANES)), m_prev)
    decay = jnp.exp2(a_mat - wide(m_rep, l // LANES))
    s = lax.dot_general(q, k, (((1,), (1,)), ((), ())), preferred_element_type=F32)
    pmat = (s * decay).astype(BF16)
    n_wide = v_aug.shape[1] // LANES
    intra = jnp.dot(pmat, v_aug, preferred_element_type=F32)
    inter = jnp.dot(q, c_aug.astype(BF16), preferred_element_type=F32)
    numden = intra + wide(jnp.exp2(m_prev - m_rep), n_wide) * inter
    den = numden[:, dv:]
    floor = jnp.exp2(-(lf_cum_rep + m_rep))
    h = numden[:, :dv] * wide(1.0 / jnp.maximum(jnp.abs(den), floor), dv // LANES)
    m_last = jnp.maximum(jnp.max(a_rep, axis=0, keepdims=True), m_prev)
    w = jnp.exp2(a_rep - m_last)
    kw = (w * k.astype(F32)).astype(BF16)
    kv = lax.dot_general(kw, v_aug, (((0,), (0,)), ((), ())), preferred_element_type=F32)
    c_ref[hd] = wide(jnp.exp2(m_prev - m_last), n_wide) * c_aug + kv
    m_ref[hd:hd + 1, :] = lf_total + m_last
    return h


def _mlstm_kernel(qf_ref, kf_ref, vf_ref, gf_ref, gtf_ref, qb_ref, kb_ref, vb_ref, gb_ref, gtb_ref,
                  hf_ref, hb_ref, c_ref, m_ref):
    s_idx = pl.program_id(1)

    @pl.when(s_idx == 0)
    def _():
        c_ref[...] = jnp.zeros_like(c_ref)
        m_ref[...] = jnp.zeros_like(m_ref)

    nb, l = qf_ref.shape[0], qf_ref.shape[1]
    nh = MLSTM_HEADS
    ones_col = jnp.ones((l, LANES), BF16)
    for bb in range(nb):
        for reverse, q_ref, k_ref, v_ref, g_ref, gt_ref, h_ref in (
                (False, qf_ref, kf_ref, vf_ref, gf_ref, gtf_ref, hf_ref),
                (True, qb_ref, kb_ref, vb_ref, gb_ref, gtb_ref, hb_ref)):
            g = g_ref[bb]
            g_t = gt_ref[bb, 0]
            base = 2 * nh if reverse else 0
            for h in range(nh):
                ca, cc = base + h, base + nh + h
                hd = bb * 2 * nh + (nh if reverse else 0) + h
                lf_cum_col = g[:, cc:cc + 1]
                lf_total = g[0:1, cc:cc + 1] if reverse else g[l - 1:l, cc:cc + 1]
                q = q_ref[bb, :, h * MLSTM_DQK:(h + 1) * MLSTM_DQK]
                k = k_ref[bb, :, h * MLSTM_DQK:(h + 1) * MLSTM_DQK]
                v_aug = jnp.concatenate([v_ref[bb, :, h * MLSTM_DV:(h + 1) * MLSTM_DV], ones_col], axis=1)
                hval = _mlstm_direction(reverse, q, k, v_aug, g[:, ca:ca + 1], g_t[ca:ca + 1, :],
                                        lf_cum_col, lf_total, c_ref, m_ref, hd)
                h_ref[bb, :, h * MLSTM_DV:(h + 1) * MLSTM_DV] = hval.astype(h_ref.dtype)


def _mlstm(qk, pm, gates, gates_t, n_lat, n_ctx, v_region):
    b, t, _ = qk.shape
    l = CHUNK
    n_lat_c = n_lat // l
    n_ctx_c = n_ctx // l
    n_steps = n_lat_c + n_ctx_c
    qw = MLSTM_HEADS * MLSTM_DQK
    vw = MLSTM_HEADS * MLSTM_DV

    def fwd(si):
        return jnp.where(si < n_ctx_c, n_lat_c + si, si - n_ctx_c)

    def bwd(si):
        return jnp.where(si < n_ctx_c, n_steps - 1 - si, n_steps - 1 - si)

    nb = MLSTM_BATCH
    assert b % nb == 0
    in_specs = []
    for order in (fwd, bwd):
        in_specs += [
            pl.BlockSpec((nb, l, qw), lambda bi, si, o=order: (bi, o(si), 0)),
            pl.BlockSpec((nb, l, qw), lambda bi, si, o=order: (bi, o(si), 1)),
            pl.BlockSpec((nb, l, vw), lambda bi, si, o=order: (bi, o(si), v_region)),
            pl.BlockSpec((nb, l, LANES), lambda bi, si, o=order: (bi, o(si), 0)),
            pl.BlockSpec((nb, 1, 4 * MLSTM_HEADS, l), lambda bi, si, o=order: (bi, o(si), 0, 0)),
        ]
    out_specs = [pl.BlockSpec((nb, l, vw), lambda bi, si: (bi, fwd(si), 0)),
                 pl.BlockSpec((nb, l, vw), lambda bi, si: (bi, bwd(si), 0))]
    n_scans = nb * 2 * MLSTM_HEADS
    return pl.pallas_call(
        _mlstm_kernel,
        grid=(b // nb, n_steps),
        in_specs=in_specs,
        out_specs=out_specs,
        out_shape=[jax.ShapeDtypeStruct((b, t, vw), BF16)] * 2,
        scratch_shapes=[pltpu.VMEM((n_scans, MLSTM_DQK, MLSTM_DV + LANES), F32),
                        pltpu.VMEM((n_scans, LANES), F32)],
        compiler_params=_cparams(("parallel", "arbitrary")),
        name="mlstm",
    )(qk, qk, pm, gates, gates_t, qk, qk, pm, gates, gates_t)


def _attn_kernel(n_keys, tq, qt_ref, k_ref, vt_ref, lam_ref, gain_ref, o_ref,
                 qq_ref, s_ref, pm_ref, e_ref, al_ref, m_ref, acc_ref):
    tk = ATT_TK
    n = n_keys // tk
    nq = qt_ref.shape[2] // tq
    width = 2 * tq
    row = lax.broadcasted_iota(jnp.int32, (qt_ref.shape[1], tq), 0)
    first = (row % DIFF_HEAD_DIM) < (DIFF_HEAD_DIM // 2)

    def mask_queries(qb, carry):
        qt = qt_ref[0, :, pl.ds(pl.multiple_of(qb * tq, tq), tq)]
        zero = jnp.zeros_like(qt)
        qq_ref[qb] = jnp.concatenate([jnp.where(first, qt, zero), jnp.where(first, zero, qt)], axis=1)
        return carry

    lax.fori_loop(0, nq, mask_queries, 0)
    acc_ref[...] = jnp.zeros_like(acc_ref)

    def scores(j, qb):
        off = pl.multiple_of(j * tk, tk)
        s = jnp.dot(k_ref[0, pl.ds(off, tk), :], qq_ref[qb], preferred_element_type=F32)
        s_ref[...] = s
        pm_ref[...] = jnp.max(s.reshape(tk // SUBLANES, SUBLANES, width), axis=0)

    def softmax(j, qb):
        m_old = jnp.where(j == 0, -jnp.inf, m_ref[...])
        m_new = jnp.maximum(m_old, jnp.max(pm_ref[...], axis=0, keepdims=True))
        alpha = jnp.exp2(m_old - m_new)
        e = jnp.exp2(s_ref[...] - m_new)
        e_ref[...] = e.astype(BF16)
        al_ref[...] = alpha
        m_ref[...] = m_new

    def accumulate(j):
        off = pl.multiple_of(j * tk, tk)
        pv = jnp.dot(vt_ref[0, :, pl.ds(off, tk)], e_ref[...], preferred_element_type=F32)
        acc_ref[...] = al_ref[...] * acc_ref[...] + pv

    def finalize(qb):
        lv = lam_ref[...]
        lam = (jnp.exp(jnp.sum(lv[0:1] * lv[1:2], axis=-1, keepdims=True))
               - jnp.exp(jnp.sum(lv[2:3] * lv[3:4], axis=-1, keepdims=True)) + LAM_INIT_L0)
        hw = o_ref.shape[2]
        inv_l = 1.0 / acc_ref[hw:hw + 1, :]
        o_all = acc_ref[0:hw, :] * inv_l
        o_t = o_all[:, :tq] - lam * o_all[:, tq:]
        ms = jnp.mean(o_t * o_t, axis=0, keepdims=True)
        out = (o_t * lax.rsqrt(ms + RMS_EPS)).T * (gain_ref[...] * (1.0 - LAM_INIT_L0))
        o_ref[0, pl.ds(pl.multiple_of(qb * tq, tq), tq), :] = out.astype(o_ref.dtype)

    def advance(j, qb):
        wrap = j == n - 1
        return jnp.where(wrap, 0, j + 1), jnp.where(wrap, qb + 1, qb)

    zero_i = jnp.int32(0)
    item0 = (zero_i, zero_i)
    item1 = advance(*item0)
    scores(*item0)
    softmax(*item0)
    scores(*item1)

    def iteration(t, carry):
        ja, qa, jm, qm, js, qs = carry
        accumulate(ja)
        softmax(jm, qm)
        scores(js, qs)

        @pl.when(ja == n - 1)
        def _():
            finalize(qa)

        return (jm, qm, js, qs) + advance(js, qs)

    item2 = advance(*item1)
    lax.fori_loop(2, nq * n, iteration, item0 + item1 + item2)
    accumulate(n - 2)
    softmax(n - 1, nq - 1)
    accumulate(n - 1)
    finalize(nq - 1)


def _attention(qt, kr, vt, lam_vecs, diff_gain, n_lat):
    b, t, _ = kr.shape
    tq = ATT_TQ
    hw = 2 * DIFF_HEAD_DIM
    nq = n_lat // tq
    assert t // ATT_TK >= 2
    return pl.pallas_call(
        functools.partial(_attn_kernel, t, tq),
        grid=(b, DIFF_HEADS),
        in_specs=[pl.BlockSpec((1, hw, n_lat), lambda bi, h: (bi, h, 0)),
                  pl.BlockSpec((1, t, hw), lambda bi, h: (bi, 0, h)),
                  pl.BlockSpec((1, V_ROWS, t), lambda bi, h: (bi, h, 0)),
                  pl.BlockSpec(lam_vecs.shape, lambda bi, h: (0, 0)),
                  pl.BlockSpec((1, hw), lambda bi, h: (0, 0))],
        out_specs=pl.BlockSpec((1, n_lat, hw), lambda bi, h: (bi, 0, h)),
        out_shape=jax.ShapeDtypeStruct((b, n_lat, DIFF_HEADS * hw), BF16),
        scratch_shapes=[pltpu.VMEM((nq, hw, 2 * tq), BF16),
                        pltpu.VMEM((ATT_TK, 2 * tq), F32),
                        pltpu.VMEM((SUBLANES, 2 * tq), F32),
                        pltpu.VMEM((ATT_TK, 2 * tq), BF16),
                        pltpu.VMEM((1, 2 * tq), F32),
                        pltpu.VMEM((1, 2 * tq), F32),
                        pltpu.VMEM((V_ROWS, 2 * tq), F32)],
        compiler_params=_cparams(("parallel", "arbitrary")),
        name="attn",
    )(qt, kr, vt, lam_vecs, diff_gain)


def _merge_kernel(d, hf_ref, hb_ref, mo_ref, ga_ref, gb_ref, hB_ref, x_ref, mod_ref, gain_ref,
                  wa_ref, wb_ref, wo_ref, o_ref):
    gain = gain_ref[...]
    parts = []
    for h in range(MLSTM_HEADS):
        sl = slice(h * MLSTM_DV, (h + 1) * MLSTM_DV)
        hs = hf_ref[0, :, sl].astype(F32) + hb_ref[0, :, sl].astype(F32)
        ms = jnp.mean(hs * hs, axis=-1, keepdims=True)
        hn = hs * lax.rsqrt(ms + RMS_EPS) * gain[:, sl]
        parts.append((_sigmoid(mo_ref[0, :, sl].astype(F32)) * hn).astype(BF16))
    h_a = jnp.concatenate(parts, axis=1)
    ya = jnp.dot(h_a, wa_ref[...], preferred_element_type=F32)
    yb = jnp.dot(hB_ref[0], wb_ref[...], preferred_element_type=F32)
    y = _sigmoid(ga_ref[0].astype(F32)) * ya + _sigmoid(gb_ref[0].astype(F32)) * yb
    yo = jnp.dot(y.astype(BF16), wo_ref[...], preferred_element_type=F32)
    g1 = mod_ref[0][:, 2 * d:3 * d]
    o_ref[0] = x_ref[0] + g1 * yo


def _merge(hf, hb, p, h_b, x, mods3, mlstm_gain, wa, wb, wo, mo_region, ga_region, gb_region):
    b, s, d = x.shape
    tm = LAT_TILE
    wspec = lambda shape: pl.BlockSpec(shape, lambda bi, i: (0, 0), pipeline_mode=pl.Buffered(1))
    tok = lambda region: pl.BlockSpec((1, tm, d), lambda bi, i, r=region: (bi, i, r))
    return pl.pallas_call(
        functools.partial(_merge_kernel, d),
        grid=(b, s // tm),
        in_specs=[tok(0), tok(0), tok(mo_region), tok(ga_region), tok(gb_region), tok(0), tok(0),
                  pl.BlockSpec((1, 1, N_MOD * d), lambda bi, i: (bi, 0, 0)),
                  pl.BlockSpec((1, d), lambda bi, i: (0, 0)),
                  wspec(wa.shape), wspec(wb.shape), wspec(wo.shape)],
        out_specs=pl.BlockSpec((1, tm, d), lambda bi, i: (bi, i, 0)),
        out_shape=jax.ShapeDtypeStruct((b, s, d), F32),
        compiler_params=_cparams(("parallel", "arbitrary")),
        name="merge",
    )(hf, hb, p, p, p, h_b, x, mods3, mlstm_gain, wa, wb, wo)


def _ffn_kernel(d, hidden, x_ref, mod_ref, g_ref, wi_ref, wo_ref, o_ref):
    x1 = x_ref[0]
    mod = mod_ref[0]
    xn = _rms_mod(x1, g_ref[...], mod[:, 3 * d:4 * d], mod[:, 4 * d:5 * d]).astype(BF16)
    a = jnp.dot(xn, wi_ref[:, :hidden], preferred_element_type=F32)
    bb = jnp.dot(xn, wi_ref[:, hidden:], preferred_element_type=F32)
    hmid = (_silu(a) * bb).astype(BF16)
    y = jnp.dot(hmid, wo_ref[...], preferred_element_type=F32)
    o_ref[0] = x1 + mod[:, 5 * d:6 * d] * y


def _ffn(x1, mods3, norm2, wi, wo):
    b, s, d = x1.shape
    hidden = wo.shape[0]
    tm = LAT_TILE
    wspec = lambda shape: pl.BlockSpec(shape, lambda bi, i: (0, 0), pipeline_mode=pl.Buffered(1))
    return pl.pallas_call(
        functools.partial(_ffn_kernel, d, hidden),
        grid=(b, s // tm),
        in_specs=[pl.BlockSpec((1, tm, d), lambda bi, i: (bi, i, 0)),
                  pl.BlockSpec((1, 1, N_MOD * d), lambda bi, i: (bi, 0, 0)),
                  pl.BlockSpec((1, d), lambda bi, i: (0, 0)),
                  wspec(wi.shape), wspec(wo.shape)],
        out_specs=pl.BlockSpec((1, tm, d), lambda bi, i: (bi, i, 0)),
        out_shape=jax.ShapeDtypeStruct((b, s, d), F32),
        compiler_params=_cparams(("parallel", "arbitrary")),
        name="ffn",
    )(x1, mods3, norm2, wi, wo)


def _diff_lane_perm():
    l = np.arange(2 * DIFF_HEAD_DIM)
    half, r = l // DIFF_HEAD_DIM, l % DIFF_HEAD_DIM
    qsel, axis, f = r // 32, (r % 32) // ROPE_FREQS, r % ROPE_FREQS
    return qsel * DIFF_HEAD_DIM + axis * 2 * ROPE_FREQS + half * ROPE_FREQS + f, qsel, axis, f, half


def _rope_tables(n_lat, n_ctx):
    _, _, axis, f, half = _diff_lane_perm()
    pos = np.arange(n_lat)
    row, col = pos // GRID_W, pos % GRID_W
    inv = (ROPE_BASE ** (-np.arange(ROPE_FREQS, dtype=np.float32) / ROPE_FREQS)).astype(np.float32)
    ang = (np.where(axis[None, :] == 0, row[:, None], col[:, None]).astype(np.float32) * inv[f][None, :])
    cos = np.concatenate([np.cos(ang), np.ones((n_ctx, LANES))], axis=0)
    sin = np.concatenate([np.sin(ang) * np.where(half == 0, -1.0, 1.0)[None, :], np.zeros((n_ctx, LANES))], axis=0)
    return jnp.asarray(cos, F32), jnp.asarray(sin, F32)


def kernel(x, c, ctx, c_ctx, w_mod, b_mod, norm1, norm2, w_in, b_gate, conv_w, conv_b, mlstm_norm, q_norm, k_norm,
           lam_vecs, diff_norm, w_branch_a, w_branch_b, w_out, w_ffn_in, w_ffn_out):
    b, s, d = x.shape
    n_ctx = ctx.shape[1]
    assert w_mod.shape[0] == 1, "single-layer block"
    assert s % CHUNK == 0 and n_ctx % CHUNK == 0 and s % ATT_TQ == 0 and s % LAT_TILE == 0
    qk_w = MLSTM_HEADS * MLSTM_DQK
    v_w = MLSTM_HEADS * MLSTM_DV
    da_w = DIFF_HEADS * 2 * DIFF_HEAD_DIM
    n_gates = 4 * MLSTM_HEADS
    sizes = (qk_w, qk_w, v_w, v_w, n_gates, da_w, da_w, da_w, d, d)
    offs = np.concatenate([[0], np.cumsum(sizes)])
    seg = lambda i: w_in[0][:, offs[i]:offs[i + 1]]

    perm, qsel, _, _, _ = _diff_lane_perm()
    head_perm = (np.arange(DIFF_HEADS)[:, None] * 2 * DIFF_HEAD_DIM + perm[None, :]).reshape(-1)
    segb = lambda i: seg(i).astype(BF16)
    wbig = jnp.concatenate([segb(0), segb(1), segb(2), segb(3), segb(8), segb(9),
                            segb(5)[:, head_perm], segb(6)[:, head_perm], segb(7)], axis=1)
    R_V, R_O, R_GA, R_GB = range(4)
    wg_f32 = jnp.pad(seg(4), ((0, 0), (0, LANES - n_gates)))
    wg_hi = wg_f32.astype(BF16)
    wg = jnp.concatenate([wg_hi, (wg_f32 - wg_hi.astype(F32)).astype(BF16)], axis=1)
    gate_bias = jnp.pad(b_gate[0], (0, LANES - n_gates)).reshape(1, LANES)
    conv_w8 = jnp.pad(conv_w[0], ((0, SUBLANES - CONV_K), (0, 0)))
    gq = q_norm[0][perm % DIFF_HEAD_DIM].reshape(1, LANES)
    gk = k_norm[0][perm % DIFF_HEAD_DIM].reshape(1, LANES)
    unit = np.concatenate([qsel, 2 + qsel])
    grp = jnp.asarray(unit[:, None] == unit[None, :], BF16)
    cos_t, sin_t = _rope_tables(s, n_ctx)

    cc = jnp.concatenate([c, c_ctx[None, :], jnp.zeros((SUBLANES - b - 1, d), F32)], axis=0)
    mods = _modulation(cc, w_mod[0], b_mod[0])
    mods3 = mods.reshape(SUBLANES, 1, N_MOD * d)

    qk, pm, qt, kr, vt, gates, gates_t = _projection(x, ctx, mods3, norm1, wbig, wg, gate_bias,
                                                     cos_t, sin_t, gq, gk, grp, conv_w8, conv_b)
    hf, hb = _mlstm(qk, pm, gates, gates_t, s, n_ctx, R_V)
    h_b = _attention(qt, kr, vt, lam_vecs[0], diff_norm, s)
    x1 = _merge(hf, hb, pm, h_b, x, mods3, mlstm_norm, w_branch_a[0].astype(BF16), w_branch_b[0].astype(BF16),
                w_out[0].astype(BF16), R_O, R_GA, R_GB)
    return _ffn(x1, mods3, norm2, w_ffn_in[0].astype(BF16), w_ffn_out[0].astype(BF16))
```

```python
import functools
import math

import jax
import jax.numpy as jnp
import numpy as np
from jax import lax
from jax.experimental import pallas as pl
from jax.experimental.pallas import tpu as pltpu

F32 = jnp.float32
BF16 = jnp.bfloat16
HIGHEST = lax.Precision.HIGHEST

GRID_W = 64
N_MOD = 6
MLSTM_HEADS = 4
MLSTM_DQK = 128
MLSTM_DV = 256
CONV_K = 5
DIFF_HEADS = 8
DIFF_HEAD_DIM = 64
ROPE_BASE = 10000.0
ROPE_FREQS = DIFF_HEAD_DIM // 4
RMS_EPS = 1e-6
LAM_INIT_L0 = 0.8 - 0.6 * math.exp(-0.3 * 0)
LOG2_E = math.log2(math.e)
Q_SCALE_LOG2 = DIFF_HEAD_DIM ** -0.5 * LOG2_E
V_ROWS = 2 * DIFF_HEAD_DIM + 16

LANES = 128
SUBLANES = 8
VMEM_LIMIT = 56 * 1024 * 1024

TOK_TILE = 256
LAT_TILE = 512
CHUNK = 256
MLSTM_BATCH = 1
ATT_TQ = 1024
ATT_TK = 2816


def _cparams(sem):
    return pltpu.CompilerParams(dimension_semantics=sem, vmem_limit_bytes=VMEM_LIMIT)


def _sigmoid(x):
    return 0.5 + 0.5 * jnp.tanh(0.5 * x)


def _silu(x):
    h = 0.5 * x
    return h + h * jnp.tanh(h)


def _log_sigmoid(x):
    return jnp.minimum(x, 0.0) - jnp.log(1.0 + jnp.exp(-jnp.abs(x)))


def _mod_kernel(c_ref, w_ref, b_ref, o_ref):
    a = _silu(c_ref[...])
    o_ref[...] = jnp.dot(a, w_ref[...], preferred_element_type=F32, precision=HIGHEST) + b_ref[...]


def _modulation(cc, w_mod, b_mod):
    rows, d = cc.shape
    n = w_mod.shape[1]
    tn = 1024
    return pl.pallas_call(
        _mod_kernel,
        grid=(n // tn,),
        in_specs=[pl.BlockSpec((rows, d), lambda j: (0, 0)),
                  pl.BlockSpec((d, tn), lambda j: (0, j)),
                  pl.BlockSpec((1, tn), lambda j: (0, j))],
        out_specs=pl.BlockSpec((rows, tn), lambda j: (0, j)),
        out_shape=jax.ShapeDtypeStruct((rows, n), F32),
        compiler_params=_cparams(("arbitrary",)),
        name="mod",
    )(cc, w_mod, b_mod.reshape(1, n))


def _rms_mod(xf, g, shift, scale):
    ms = jnp.mean(xf * xf, axis=-1, keepdims=True)
    return xf * lax.rsqrt(ms + RMS_EPS) * g * (1.0 + scale) + shift


def _proj_kernel(n_lat_tiles, d, x_ref, xprev_ref, xnext_ref, ctx_ref, mod_ref, g_ref, w_ref, wg_ref, gbias_ref,
                 cos_ref, sin_ref, gq_ref, gk_ref, grp_ref, cw_ref, cb_ref,
                 qk_ref, pm_ref, qt_ref, kr_ref, vt_ref, gates_ref, gates_t_ref,
                 ext_ref, yq_ref, yk_ref, yv_ref, gpre_ref, xb_ref, xlo_ref):
    i = pl.program_id(1)
    tm = x_ref.shape[1]

    @pl.when(i == 0)
    def _():
        ext_ref[...] = jnp.zeros_like(ext_ref)
        yq_ref[...] = jnp.zeros_like(yq_ref)
        yk_ref[...] = jnp.zeros_like(yk_ref)
        yv_ref[...] = jnp.zeros_like(yv_ref)
        gpre_ref[...] = jnp.zeros_like(gpre_ref)

    n_tiles = pl.num_programs(1) - 1
    ti = jnp.minimum(i, n_tiles - 1)
    xf = jnp.where(ti < n_lat_tiles, x_ref[0], ctx_ref[0])
    mod = mod_ref[0]
    gain1 = g_ref[...]
    xn = _rms_mod(xf, gain1, mod[:, 0:d], mod[:, d:2 * d])
    xb = xn.astype(BF16)
    halo = jnp.concatenate([xprev_ref[0], xnext_ref[0]], axis=0)
    xb_ref[0:tm, :] = xb
    xb_ref[tm:, :] = _rms_mod(halo, gain1, mod[:, 0:d], mod[:, d:2 * d]).astype(BF16)
    xlo_ref[...] = (xn - xb.astype(F32)).astype(BF16)

    def project(c0, width):
        return jnp.dot(xb_ref[0:tm, :], w_ref[:, c0:c0 + width], preferred_element_type=F32)

    taps = cw_ref[...]
    conv_bias = cb_ref[...]
    n_q_cols = d // 2

    w2 = 2 * LANES

    def conv_columns(c_lo, c_hi):
        for c0 in range(c_lo, c_hi, w2):
            acc = jnp.zeros((tm, w2), F32) + conv_bias[:, c0:c0 + w2]
            for k in range(CONV_K):
                r0 = SUBLANES - CONV_K // 2 + k
                acc = acc + ext_ref[r0:r0 + tm, c0:c0 + w2] * taps[k:k + 1, c0:c0 + w2]
            out = _silu(acc)
            if c0 < n_q_cols:
                out = out * MLSTM_DQK ** -0.5
            qk_ref[0, :, c0:c0 + w2] = out.astype(BF16)

    cos = cos_ref[...]
    sin = sin_ref[...]
    grp = grp_ref[...]
    gq = gq_ref[...]
    gk = gk_ref[...]
    hw = 2 * DIFF_HEAD_DIM

    def scaled_sumsq(x2):
        ss = jnp.dot((x2 * x2).astype(BF16), grp, preferred_element_type=F32)
        return lax.rsqrt(ss * (1.0 / DIFF_HEAD_DIM) + RMS_EPS)

    def rotated(y):
        return y * cos + pltpu.roll(y, DIFF_HEAD_DIM, axis=1) * sin

    ones_rows = jnp.ones((V_ROWS - hw, tm), BF16)

    def attention_pair(pair):
        sl2 = slice(pair * 2 * hw, (pair + 1) * 2 * hw)
        yq2 = yq_ref[:, sl2]
        yk2 = yk_ref[:, sl2]
        qn = yq2 * scaled_sumsq(yq2)
        kn = yk2 * scaled_sumsq(yk2)
        for u in range(2):
            h = 2 * pair + u
            sl = slice(h * hw, (h + 1) * hw)
            su = slice(u * hw, (u + 1) * hw)
            qt_ref[0, sl, :] = (rotated(qn[:, su] * gq) * Q_SCALE_LOG2).T.astype(BF16)
            kr_ref[0, :, sl] = rotated(kn[:, su] * gk).astype(BF16)
            vt_ref[0, h * V_ROWS:h * V_ROWS + hw, :] = yv_ref[:, sl].T.astype(BF16)
            vt_ref[0, h * V_ROWS + hw:(h + 1) * V_ROWS, :] = ones_rows

    def gate_epilogue():
        g = gpre_ref[...]
        lf = _log_sigmoid(g)
        lf_hi = lf.astype(BF16)
        lf_lo = (lf - lf_hi.astype(F32)).astype(BF16)
        lf2 = jnp.concatenate([lf_hi, lf_lo], axis=1)
        rows = lax.broadcasted_iota(jnp.int32, (tm, tm), 0)
        cols = lax.broadcasted_iota(jnp.int32, (tm, tm), 1)
        lower = jnp.where(cols <= rows, 1.0, 0.0).astype(BF16)
        upper = jnp.where(cols >= rows, 1.0, 0.0).astype(BF16)
        cum_f2 = jnp.dot(lower, lf2, preferred_element_type=F32)
        cum_b2 = jnp.dot(upper, lf2, preferred_element_type=F32)
        cum_f = cum_f2[:, :LANES] + cum_f2[:, LANES:]
        cum_b = cum_b2[:, :LANES] + cum_b2[:, LANES:]
        lane = lax.broadcasted_iota(jnp.int32, (tm, LANES), 1)
        cum = jnp.where(lane < 2 * MLSTM_HEADS, cum_f, cum_b)
        a = g - pltpu.roll(cum, LANES - MLSTM_HEADS, axis=1)
        a = a * LOG2_E
        gates_ref[0] = jnp.where((lane % (2 * MLSTM_HEADS)) < MLSTM_HEADS, a, cum * LOG2_E)
        gates_t_ref[0, 0] = a.T[0:4 * MLSTM_HEADS, :]

    for r in range(4):
        pm_ref[0, :, r * d:(r + 1) * d] = project((1 + r) * d, d).astype(BF16)
        attention_pair(r)
    yq_ref[...] = project(5 * d, d)
    conv_columns(0, d // 2)
    yk_ref[...] = project(6 * d, d)
    conv_columns(d // 2, d)
    yv_ref[...] = project(7 * d, d)
    gate_epilogue()
    y_qk = jnp.dot(xb_ref[...], w_ref[:, 0:d], preferred_element_type=F32)
    has_prev = jnp.logical_and(ti > 0, ti < n_lat_tiles)
    has_next = ti < n_lat_tiles - 1
    ext_ref[0:SUBLANES, :] = jnp.where(has_prev, y_qk[tm:tm + SUBLANES], 0.0)
    ext_ref[SUBLANES:SUBLANES + tm, :] = y_qk[0:tm]
    ext_ref[SUBLANES + tm:, :] = jnp.where(has_next, y_qk[tm + SUBLANES:], 0.0)
    g2 = (jnp.dot(xb_ref[0:tm, :], wg_ref[...], preferred_element_type=F32)
          + jnp.dot(xlo_ref[...], wg_ref[...], preferred_element_type=F32))
    gpre_ref[...] = g2[:, :LANES] + g2[:, LANES:] + gbias_ref[...]


def _projection(x, ctx, mods3, norm1, wbig, wg, gate_bias, cos_t, sin_t, gq, gk, grp, conv_w8, conv_b):
    b, s, d = x.shape
    n_ctx = ctx.shape[1]
    tm = TOK_TILE
    assert tm == CHUNK
    assert n_ctx == tm
    n_lat_tiles = s // tm
    n_tiles = (s + n_ctx) // tm
    t = s + n_ctx
    n_out = wbig.shape[1]
    da_w = DIFF_HEADS * 2 * DIFF_HEAD_DIM
    halo_stride = tm // SUBLANES
    last_halo = s // SUBLANES - 1
    const = lambda shape: pl.BlockSpec(shape, lambda bi, i: (0,) * len(shape))
    cur = lambda i: jnp.minimum(i, n_tiles - 1)
    late = lambda i: jnp.maximum(i - 1, 0)
    tok = lambda width: pl.BlockSpec((1, tm, width), lambda bi, i: (bi, late(i), 0))
    tok_t = lambda rows: pl.BlockSpec((1, rows, tm), lambda bi, i: (bi, 0, late(i)))
    n_ctx_tiles = n_ctx // tm
    return pl.pallas_call(
        functools.partial(_proj_kernel, n_lat_tiles, d),
        grid=(b, n_tiles + 1),
        in_specs=[
            pl.BlockSpec((1, tm, d), lambda bi, i: (bi, jnp.minimum(i, n_lat_tiles - 1), 0)),
            pl.BlockSpec((1, SUBLANES, d),
                         lambda bi, i: (bi, jnp.clip(cur(i) * halo_stride - 1, 0, last_halo), 0)),
            pl.BlockSpec((1, SUBLANES, d),
                         lambda bi, i: (bi, jnp.clip((cur(i) + 1) * halo_stride, 0, last_halo), 0)),
            pl.BlockSpec((1, tm, d), lambda bi, i: (bi, jnp.clip(i - n_lat_tiles, 0, n_ctx_tiles - 1), 0)),
            pl.BlockSpec((1, 1, N_MOD * d), lambda bi, i: (jnp.where(i < n_lat_tiles, bi, b), 0, 0)),
            const((1, d)),
            pl.BlockSpec((d, n_out), lambda bi, i: (0, 0), pipeline_mode=pl.Buffered(1)),
            const((d, 2 * LANES)), const((1, LANES)),
            pl.BlockSpec((tm, LANES), lambda bi, i: (late(i), 0)),
            pl.BlockSpec((tm, LANES), lambda bi, i: (late(i), 0)),
            const((1, LANES)), const((1, LANES)), const((2 * LANES, 2 * LANES)),
            const((SUBLANES, d)), const((1, d)),
        ],
        out_specs=[tok(d), pl.BlockSpec((1, tm, 4 * d), lambda bi, i: (bi, cur(i), 0)),
                   tok_t(da_w), tok(da_w), tok_t(DIFF_HEADS * V_ROWS), tok(LANES),
                   pl.BlockSpec((1, 1, 4 * MLSTM_HEADS, tm), lambda bi, i: (bi, late(i), 0, 0))],
        out_shape=[jax.ShapeDtypeStruct((b, t, d), BF16),
                   jax.ShapeDtypeStruct((b, t, 4 * d), BF16),
                   jax.ShapeDtypeStruct((b, da_w, t), BF16),
                   jax.ShapeDtypeStruct((b, t, da_w), BF16),
                   jax.ShapeDtypeStruct((b, DIFF_HEADS * V_ROWS, t), BF16),
                   jax.ShapeDtypeStruct((b, t, LANES), F32),
                   jax.ShapeDtypeStruct((b, n_tiles, 4 * MLSTM_HEADS, tm), F32)],
        scratch_shapes=[pltpu.VMEM((tm + 2 * SUBLANES, d), F32),
                        pltpu.VMEM((tm, da_w), F32), pltpu.VMEM((tm, da_w), F32), pltpu.VMEM((tm, da_w), F32),
                        pltpu.VMEM((tm, LANES), F32),
                        pltpu.VMEM((tm + 2 * SUBLANES, d), BF16),
                        pltpu.VMEM((tm, d), BF16)],
        compiler_params=_cparams(("parallel", "arbitrary")),
        name="proj",
    )(x, x, x, ctx, mods3, norm1, wbig, wg, gate_bias, cos_t, sin_t, gq, gk, grp, conv_w8, conv_b)


def _mlstm_direction(reverse, q, k, v_aug, a_col, a_row, lf_cum_col, lf_total, c_ref, m_ref, hd):
    l = q.shape[0]
    dv = v_aug.shape[1] - LANES

    def wide(x, n_tiles):
        return jnp.concatenate([x] * n_tiles, axis=1)

    m_prev = m_ref[hd:hd + 1, :]
    c_aug = c_ref[hd]
    lf_cum_rep = jnp.broadcast_to(lf_cum_col, (l, LANES))
    a_rep = jnp.broadcast_to(a_col, (l, LANES))
    rows = lax.broadcasted_iota(jnp.int32, (l, l), 0)
    cols = lax.broadcasted_iota(jnp.int32, (l, l), 1)
    visible = (cols >= rows) if reverse else (cols <= rows)
    a_mat = jnp.where(visible, a_row, -jnp.inf)
    m_rep = jnp.maximum(jnp.broadcast_to(jnp.max(a_mat, axis=-1, keepdims=True), (l, LANES)), m_prev)
    decay = jnp.exp2(a_mat - wide(m_rep, l // LANES))
    s = lax.dot_general(q, k, (((1,), (1,)), ((), ())), preferred_element_type=F32)
    pmat = (s * decay).astype(BF16)
    n_wide = v_aug.shape[1] // LANES
    intra = jnp.dot(pmat, v_aug, preferred_element_type=F32)
    inter = jnp.dot(q, c_aug.astype(BF16), preferred_element_type=F32)
    numden = intra + wide(jnp.exp2(m_prev - m_rep), n_wide) * inter
    den = numden[:, dv:]
    floor = jnp.exp2(-(lf_cum_rep + m_rep))
    h = numden[:, :dv] * wide(1.0 / jnp.maximum(jnp.abs(den), floor), dv // LANES)
    m_last = jnp.maximum(jnp.max(a_rep, axis=0, keepdims=True), m_prev)
    w = jnp.exp2(a_rep - m_last)
    kw = (w * k.astype(F32)).astype(BF16)
    kv = lax.dot_general(kw, v_aug, (((0,), (0,)), ((), ())), preferred_element_type=F32)
    c_ref[hd] = wide(jnp.exp2(m_prev - m_last), n_wide) * c_aug + kv
    m_ref[hd:hd + 1, :] = lf_total + m_last
    return h


def _mlstm_kernel(qf_ref, kf_ref, vf_ref, gf_ref, gtf_ref, qb_ref, kb_ref, vb_ref, gb_ref, gtb_ref,
                  hf_ref, hb_ref, c_ref, m_ref):
    s_idx = pl.program_id(1)

    @pl.when(s_idx == 0)
    def _():
        c_ref[...] = jnp.zeros_like(c_ref)
        m_ref[...] = jnp.zeros_like(m_ref)

    nb, l = qf_ref.shape[0], qf_ref.shape[1]
    nh = MLSTM_HEADS
    ones_col = jnp.ones((l, LANES), BF16)
    for bb in range(nb):
        for reverse, q_ref, k_ref, v_ref, g_ref, gt_ref, h_ref in (
                (False, qf_ref, kf_ref, vf_ref, gf_ref, gtf_ref, hf_ref),
                (True, qb_ref, kb_ref, vb_ref, gb_ref, gtb_ref, hb_ref)):
            g = g_ref[bb]
            g_t = gt_ref[bb, 0]
            base = 2 * nh if reverse else 0
            for h in range(nh):
                ca, cc = base + h, base + nh + h
                hd = bb * 2 * nh + (nh if reverse else 0) + h
                lf_cum_col = g[:, cc:cc + 1]
                lf_total = g[0:1, cc:cc + 1] if reverse else g[l - 1:l, cc:cc + 1]
                q = q_ref[bb, :, h * MLSTM_DQK:(h + 1) * MLSTM_DQK]
                k = k_ref[bb, :, h * MLSTM_DQK:(h + 1) * MLSTM_DQK]
                v_aug = jnp.concatenate([v_ref[bb, :, h * MLSTM_DV:(h + 1) * MLSTM_DV], ones_col], axis=1)
                hval = _mlstm_direction(reverse, q, k, v_aug, g[:, ca:ca + 1], g_t[ca:ca + 1, :],
                                        lf_cum_col, lf_total, c_ref, m_ref, hd)
                h_ref[bb, :, h * MLSTM_DV:(h + 1) * MLSTM_DV] = hval.astype(h_ref.dtype)


def _mlstm(qk, pm, gates, gates_t, n_lat, n_ctx, v_region):
    b, t, _ = qk.shape
    l = CHUNK
    n_lat_c = n_lat // l
    n_ctx_c = n_ctx // l
    n_steps = n_lat_c + n_ctx_c
    qw = MLSTM_HEADS * MLSTM_DQK
    vw = MLSTM_HEADS * MLSTM_DV

    def fwd(si):
        return jnp.where(si < n_ctx_c, n_lat_c + si, si - n_ctx_c)

    def bwd(si):
        return jnp.where(si < n_ctx_c, n_steps - 1 - si, n_steps - 1 - si)

    nb = MLSTM_BATCH
    assert b % nb == 0
    in_specs = []
    for order in (fwd, bwd):
        in_specs += [
            pl.BlockSpec((nb, l, qw), lambda bi, si, o=order: (bi, o(si), 0)),
            pl.BlockSpec((nb, l, qw), lambda bi, si, o=order: (bi, o(si), 1)),
            pl.BlockSpec((nb, l, vw), lambda bi, si, o=order: (bi, o(si), v_region)),
            pl.BlockSpec((nb, l, LANES), lambda bi, si, o=order: (bi, o(si), 0)),
            pl.BlockSpec((nb, 1, 4 * MLSTM_HEADS, l), lambda bi, si, o=order: (bi, o(si), 0, 0)),
        ]
    out_specs = [pl.BlockSpec((nb, l, vw), lambda bi, si: (bi, fwd(si), 0)),
                 pl.BlockSpec((nb, l, vw), lambda bi, si: (bi, bwd(si), 0))]
    n_scans = nb * 2 * MLSTM_HEADS
    return pl.pallas_call(
        _mlstm_kernel,
        grid=(b // nb, n_steps),
        in_specs=in_specs,
        out_specs=out_specs,
        out_shape=[jax.ShapeDtypeStruct((b, t, vw), BF16)] * 2,
        scratch_shapes=[pltpu.VMEM((n_scans, MLSTM_DQK, MLSTM_DV + LANES), F32),
                        pltpu.VMEM((n_scans, LANES), F32)],
        compiler_params=_cparams(("parallel", "arbitrary")),
        name="mlstm",
    )(qk, qk, pm, gates, gates_t, qk, qk, pm, gates, gates_t)


def _attn_kernel(n_keys, tq, qt_ref, k_ref, vt_ref, lam_ref, gain_ref, o_ref,
                 s_ref, pm_ref, e_ref, al_ref, m_ref, acc_ref):
    tk = ATT_TK
    n = n_keys // tk
    nq = qt_ref.shape[2] // tq
    width = 2 * tq
    row = lax.broadcasted_iota(jnp.int32, (qt_ref.shape[1], tq), 0)
    first = (row % DIFF_HEAD_DIM) < (DIFF_HEAD_DIM // 2)

    acc_ref[...] = jnp.zeros_like(acc_ref)

    def scores(j, qb):
        off = pl.multiple_of(j * tk, tk)
        qt = qt_ref[0, :, pl.ds(pl.multiple_of(qb * tq, tq), tq)]
        zero = jnp.zeros_like(qt)
        qq = jnp.concatenate([jnp.where(first, qt, zero), jnp.where(first, zero, qt)], axis=1)
        s = jnp.dot(k_ref[0, pl.ds(off, tk), :], qq, preferred_element_type=F32)
        s_ref[...] = s
        pm_ref[...] = jnp.max(s.reshape(tk // SUBLANES, SUBLANES, width), axis=0)

    def softmax(j, qb):
        m_old = jnp.where(j == 0, -jnp.inf, m_ref[...])
        m_new = jnp.maximum(m_old, jnp.max(pm_ref[...], axis=0, keepdims=True))
        alpha = jnp.exp2(m_old - m_new)
        e = jnp.exp2(s_ref[...] - m_new)
        e_ref[...] = e.astype(BF16)
        al_ref[...] = alpha
        m_ref[...] = m_new

    def accumulate(j):
        off = pl.multiple_of(j * tk, tk)
        pv = jnp.dot(vt_ref[0, :, pl.ds(off, tk)], e_ref[...], preferred_element_type=F32)
        acc_ref[...] = al_ref[...] * acc_ref[...] + pv

    def finalize(qb):
        lv = lam_ref[...]
        lam = (jnp.exp(jnp.sum(lv[0:1] * lv[1:2], axis=-1, keepdims=True))
               - jnp.exp(jnp.sum(lv[2:3] * lv[3:4], axis=-1, keepdims=True)) + LAM_INIT_L0)
        hw = o_ref.shape[2]
        inv_l = 1.0 / acc_ref[hw:hw + 1, :]
        o_all = acc_ref[0:hw, :] * inv_l
        o_t = o_all[:, :tq] - lam * o_all[:, tq:]
        ms = jnp.mean(o_t * o_t, axis=0, keepdims=True)
        out = (o_t * lax.rsqrt(ms + RMS_EPS)).T * (gain_ref[...] * (1.0 - LAM_INIT_L0))
        o_ref[0, pl.ds(pl.multiple_of(qb * tq, tq), tq), :] = out.astype(o_ref.dtype)

    def advance(j, qb):
        wrap = j == n - 1
        return jnp.where(wrap, 0, j + 1), jnp.where(wrap, qb + 1, qb)

    zero_i = jnp.int32(0)
    item0 = (zero_i, zero_i)
    item1 = advance(*item0)
    scores(*item0)
    softmax(*item0)
    scores(*item1)

    def iteration(t, carry):
        ja, qa, jm, qm, js, qs = carry
        accumulate(ja)
        softmax(jm, qm)
        scores(js, qs)

        @pl.when(ja == n - 1)
        def _():
            finalize(qa)

        return (jm, qm, js, qs) + advance(js, qs)

    item2 = advance(*item1)
    lax.fori_loop(2, nq * n, iteration, item0 + item1 + item2)
    accumulate(n - 2)
    softmax(n - 1, nq - 1)
    accumulate(n - 1)
    finalize(nq - 1)


def _attention(qt, kr, vt, lam_vecs, diff_gain, n_lat):
    b, t, _ = kr.shape
    tq = ATT_TQ
    hw = 2 * DIFF_HEAD_DIM
    nq = n_lat // tq
    assert t // ATT_TK >= 2
    return pl.pallas_call(
        functools.partial(_attn_kernel, t, tq),
        grid=(b, DIFF_HEADS),
        in_specs=[pl.BlockSpec((1, hw, n_lat), lambda bi, h: (bi, h, 0)),
                  pl.BlockSpec((1, t, hw), lambda bi, h: (bi, 0, h)),
                  pl.BlockSpec((1, V_ROWS, t), lambda bi, h: (bi, h, 0)),
                  pl.BlockSpec(lam_vecs.shape, lambda bi, h: (0, 0)),
                  pl.BlockSpec((1, hw), lambda bi, h: (0, 0))],
        out_specs=pl.BlockSpec((1, n_lat, hw), lambda bi, h: (bi, 0, h)),
        out_shape=jax.ShapeDtypeStruct((b, n_lat, DIFF_HEADS * hw), BF16),
        scratch_shapes=[pltpu.VMEM((ATT_TK, 2 * tq), F32),
                        pltpu.VMEM((SUBLANES, 2 * tq), F32),
                        pltpu.VMEM((ATT_TK, 2 * tq), BF16),
                        pltpu.VMEM((1, 2 * tq), F32),
                        pltpu.VMEM((1, 2 * tq), F32),
                        pltpu.VMEM((V_ROWS, 2 * tq), F32)],
        compiler_params=_cparams(("parallel", "arbitrary")),
        name="attn",
    )(qt, kr, vt, lam_vecs, diff_gain)


def _merge_kernel(d, hf_ref, hb_ref, mo_ref, ga_ref, gb_ref, hB_ref, x_ref, mod_ref, gain_ref,
                  wa_ref, wb_ref, wo_ref, o_ref):
    gain = gain_ref[...]
    parts = []
    for h in range(MLSTM_HEADS):
        sl = slice(h * MLSTM_DV, (h + 1) * MLSTM_DV)
        hs = hf_ref[0, :, sl].astype(F32) + hb_ref[0, :, sl].astype(F32)
        ms = jnp.mean(hs * hs, axis=-1, keepdims=True)
        hn = hs * lax.rsqrt(ms + RMS_EPS) * gain[:, sl]
        parts.append((_sigmoid(mo_ref[0, :, sl].astype(F32)) * hn).astype(BF16))
    h_a = jnp.concatenate(parts, axis=1)
    ya = jnp.dot(h_a, wa_ref[...], preferred_element_type=F32)
    yb = jnp.dot(hB_ref[0], wb_ref[...], preferred_element_type=F32)
    y = _sigmoid(ga_ref[0].astype(F32)) * ya + _sigmoid(gb_ref[0].astype(F32)) * yb
    yo = jnp.dot(y.astype(BF16), wo_ref[...], preferred_element_type=F32)
    g1 = mod_ref[0][:, 2 * d:3 * d]
    o_ref[0] = x_ref[0] + g1 * yo


def _merge(hf, hb, p, h_b, x, mods3, mlstm_gain, wa, wb, wo, mo_region, ga_region, gb_region):
    b, s, d = x.shape
    tm = LAT_TILE
    wspec = lambda shape: pl.BlockSpec(shape, lambda bi, i: (0, 0), pipeline_mode=pl.Buffered(1))
    tok = lambda region: pl.BlockSpec((1, tm, d), lambda bi, i, r=region: (bi, i, r))
    return pl.pallas_call(
        functools.partial(_merge_kernel, d),
        grid=(b, s // tm),
        in_specs=[tok(0), tok(0), tok(mo_region), tok(ga_region), tok(gb_region), tok(0), tok(0),
                  pl.BlockSpec((1, 1, N_MOD * d), lambda bi, i: (bi, 0, 0)),
                  pl.BlockSpec((1, d), lambda bi, i: (0, 0)),
                  wspec(wa.shape), wspec(wb.shape), wspec(wo.shape)],
        out_specs=pl.BlockSpec((1, tm, d), lambda bi, i: (bi, i, 0)),
        out_shape=jax.ShapeDtypeStruct((b, s, d), F32),
        compiler_params=_cparams(("parallel", "arbitrary")),
        name="merge",
    )(hf, hb, p, p, p, h_b, x, mods3, mlstm_gain, wa, wb, wo)


def _ffn_kernel(d, hidden, x_ref, mod_ref, g_ref, wi_ref, wo_ref, o_ref):
    x1 = x_ref[0]
    mod = mod_ref[0]
    xn = _rms_mod(x1, g_ref[...], mod[:, 3 * d:4 * d], mod[:, 4 * d:5 * d]).astype(BF16)
    a = jnp.dot(xn, wi_ref[:, :hidden], preferred_element_type=F32)
    bb = jnp.dot(xn, wi_ref[:, hidden:], preferred_element_type=F32)
    hmid = (_silu(a) * bb).astype(BF16)
    y = jnp.dot(hmid, wo_ref[...], preferred_element_type=F32)
    o_ref[0] = x1 + mod[:, 5 * d:6 * d] * y


def _ffn(x1, mods3, norm2, wi, wo):
    b, s, d = x1.shape
    hidden = wo.shape[0]
    tm = LAT_TILE
    wspec = lambda shape: pl.BlockSpec(shape, lambda bi, i: (0, 0), pipeline_mode=pl.Buffered(1))
    return pl.pallas_call(
        functools.partial(_ffn_kernel, d, hidden),
        grid=(b, s // tm),
        in_specs=[pl.BlockSpec((1, tm, d), lambda bi, i: (bi, i, 0)),
                  pl.BlockSpec((1, 1, N_MOD * d), lambda bi, i: (bi, 0, 0)),
                  pl.BlockSpec((1, d), lambda bi, i: (0, 0)),
                  wspec(wi.shape), wspec(wo.shape)],
        out_specs=pl.BlockSpec((1, tm, d), lambda bi, i: (bi, i, 0)),
        out_shape=jax.ShapeDtypeStruct((b, s, d), F32),
        compiler_params=_cparams(("parallel", "arbitrary")),
        name="ffn",
    )(x1, mods3, norm2, wi, wo)


def _diff_lane_perm():
    l = np.arange(2 * DIFF_HEAD_DIM)
    half, r = l // DIFF_HEAD_DIM, l % DIFF_HEAD_DIM
    qsel, axis, f = r // 32, (r % 32) // ROPE_FREQS, r % ROPE_FREQS
    return qsel * DIFF_HEAD_DIM + axis * 2 * ROPE_FREQS + half * ROPE_FREQS + f, qsel, axis, f, half


def _rope_tables(n_lat, n_ctx):
    _, _, axis, f, half = _diff_lane_perm()
    pos = np.arange(n_lat)
    row, col = pos // GRID_W, pos % GRID_W
    inv = (ROPE_BASE ** (-np.arange(ROPE_FREQS, dtype=np.float32) / ROPE_FREQS)).astype(np.float32)
    ang = (np.where(axis[None, :] == 0, row[:, None], col[:, None]).astype(np.float32) * inv[f][None, :])
    cos = np.concatenate([np.cos(ang), np.ones((n_ctx, LANES))], axis=0)
    sin = np.concatenate([np.sin(ang) * np.where(half == 0, -1.0, 1.0)[None, :], np.zeros((n_ctx, LANES))], axis=0)
    return jnp.asarray(cos, F32), jnp.asarray(sin, F32)


def kernel(x, c, ctx, c_ctx, w_mod, b_mod, norm1, norm2, w_in, b_gate, conv_w, conv_b, mlstm_norm, q_norm, k_norm,
           lam_vecs, diff_norm, w_branch_a, w_branch_b, w_out, w_ffn_in, w_ffn_out):
    b, s, d = x.shape
    n_ctx = ctx.shape[1]
    assert w_mod.shape[0] == 1, "single-layer block"
    assert s % CHUNK == 0 and n_ctx % CHUNK == 0 and s % ATT_TQ == 0 and s % LAT_TILE == 0
    qk_w = MLSTM_HEADS * MLSTM_DQK
    v_w = MLSTM_HEADS * MLSTM_DV
    da_w = DIFF_HEADS * 2 * DIFF_HEAD_DIM
    n_gates = 4 * MLSTM_HEADS
    sizes = (qk_w, qk_w, v_w, v_w, n_gates, da_w, da_w, da_w, d, d)
    offs = np.concatenate([[0], np.cumsum(sizes)])
    seg = lambda i: w_in[0][:, offs[i]:offs[i + 1]]

    perm, qsel, _, _, _ = _diff_lane_perm()
    head_perm = (np.arange(DIFF_HEADS)[:, None] * 2 * DIFF_HEAD_DIM + perm[None, :]).reshape(-1)
    segb = lambda i: seg(i).astype(BF16)
    wbig = jnp.concatenate([segb(0), segb(1), segb(2), segb(3), segb(8), segb(9),
                            segb(5)[:, head_perm], segb(6)[:, head_perm], segb(7)], axis=1)
    R_V, R_O, R_GA, R_GB = range(4)
    wg_f32 = jnp.pad(seg(4), ((0, 0), (0, LANES - n_gates)))
    wg_hi = wg_f32.astype(BF16)
    wg = jnp.concatenate([wg_hi, (wg_f32 - wg_hi.astype(F32)).astype(BF16)], axis=1)
    gate_bias = jnp.pad(b_gate[0], (0, LANES - n_gates)).reshape(1, LANES)
    conv_w8 = jnp.pad(conv_w[0], ((0, SUBLANES - CONV_K), (0, 0)))
    gq = q_norm[0][perm % DIFF_HEAD_DIM].reshape(1, LANES)
    gk = k_norm[0][perm % DIFF_HEAD_DIM].reshape(1, LANES)
    unit = np.concatenate([qsel, 2 + qsel])
    grp = jnp.asarray(unit[:, None] == unit[None, :], BF16)
    cos_t, sin_t = _rope_tables(s, n_ctx)

    cc = jnp.concatenate([c, c_ctx[None, :], jnp.zeros((SUBLANES - b - 1, d), F32)], axis=0)
    mods = _modulation(cc, w_mod[0], b_mod[0])
    mods3 = mods.reshape(SUBLANES, 1, N_MOD * d)

    qk, pm, qt, kr, vt, gates, gates_t = _projection(x, ctx, mods3, norm1, wbig, wg, gate_bias,
                                                     cos_t, sin_t, gq, gk, grp, conv_w8, conv_b)
    hf, hb = _mlstm(qk, pm, gates, gates_t, s, n_ctx, R_V)
    h_b = _attention(qt, kr, vt, lam_vecs[0], diff_norm, s)
    x1 = _merge(hf, hb, pm, h_b, x, mods3, mlstm_norm, w_branch_a[0].astype(BF16), w_branch_b[0].astype(BF16),
                w_out[0].astype(BF16), R_O, R_GA, R_GB)
    return _ffn(x1, mods3, norm2, w_ffn_in[0].astype(BF16), w_ffn_out[0].astype(BF16))
```

```python
import functools
import math

import jax
import jax.numpy as jnp
import numpy as np
from jax import lax
from jax.experimental import pallas as pl
from jax.experimental.pallas import tpu as pltpu

F32 = jnp.float32
BF16 = jnp.bfloat16
HIGHEST = lax.Precision.HIGHEST

GRID_W = 64
N_MOD = 6
MLSTM_HEADS = 4
MLSTM_DQK = 128
MLSTM_DV = 256
CONV_K = 5
DIFF_HEADS = 8
DIFF_HEAD_DIM = 64
ROPE_BASE = 10000.0
ROPE_FREQS = DIFF_HEAD_DIM // 4
RMS_EPS = 1e-6
LAM_INIT_L0 = 0.8 - 0.6 * math.exp(-0.3 * 0)
LOG2_E = math.log2(math.e)
Q_SCALE_LOG2 = DIFF_HEAD_DIM ** -0.5 * LOG2_E
V_ROWS = 2 * DIFF_HEAD_DIM + 16

LANES = 128
SUBLANES = 8
VMEM_LIMIT = 56 * 1024 * 1024

TOK_TILE = 256
LAT_TILE = 512
CHUNK = 256
MLSTM_BATCH = 4
ATT_TQ = 1024
ATT_TK = 2816


def _cparams(sem):
    return pltpu.CompilerParams(dimension_semantics=sem, vmem_limit_bytes=VMEM_LIMIT)


def _sigmoid(x):
    return 0.5 + 0.5 * jnp.tanh(0.5 * x)


def _silu(x):
    h = 0.5 * x
    return h + h * jnp.tanh(h)


def _log_sigmoid(x):
    return jnp.minimum(x, 0.0) - jnp.log(1.0 + jnp.exp(-jnp.abs(x)))


def _mod_kernel(c_ref, w_ref, b_ref, o_ref):
    a = _silu(c_ref[...])
    o_ref[...] = jnp.dot(a, w_ref[...], preferred_element_type=F32, precision=HIGHEST) + b_ref[...]


def _modulation(cc, w_mod, b_mod):
    rows, d = cc.shape
    n = w_mod.shape[1]
    tn = 1024
    return pl.pallas_call(
        _mod_kernel,
        grid=(n // tn,),
        in_specs=[pl.BlockSpec((rows, d), lambda j: (0, 0)),
                  pl.BlockSpec((d, tn), lambda j: (0, j)),
                  pl.BlockSpec((1, tn), lambda j: (0, j))],
        out_specs=pl.BlockSpec((rows, tn), lambda j: (0, j)),
        out_shape=jax.ShapeDtypeStruct((rows, n), F32),
        compiler_params=_cparams(("arbitrary",)),
        name="mod",
    )(cc, w_mod, b_mod.reshape(1, n))


def _rms_mod(xf, g, shift, scale):
    ms = jnp.mean(xf * xf, axis=-1, keepdims=True)
    return xf * lax.rsqrt(ms + RMS_EPS) * g * (1.0 + scale) + shift


def _proj_kernel(n_lat_tiles, d, x_ref, xprev_ref, xnext_ref, ctx_ref, mod_ref, g_ref, w_ref, wg_ref, gbias_ref,
                 cos_ref, sin_ref, gq_ref, gk_ref, grp_ref, cw_ref, cb_ref,
                 qk_ref, pm_ref, qt_ref, kr_ref, vt_ref, gates_ref, gates_t_ref,
                 ext_ref, yq_ref, yk_ref, yv_ref, gpre_ref, xb_ref, xlo_ref):
    i = pl.program_id(1)
    tm = x_ref.shape[1]

    @pl.when(i == 0)
    def _():
        ext_ref[...] = jnp.zeros_like(ext_ref)
        yq_ref[...] = jnp.zeros_like(yq_ref)
        yk_ref[...] = jnp.zeros_like(yk_ref)
        yv_ref[...] = jnp.zeros_like(yv_ref)
        gpre_ref[...] = jnp.zeros_like(gpre_ref)

    n_tiles = pl.num_programs(1) - 1
    ti = jnp.minimum(i, n_tiles - 1)
    xf = jnp.where(ti < n_lat_tiles, x_ref[0], ctx_ref[0])
    mod = mod_ref[0]
    gain1 = g_ref[...]
    xn = _rms_mod(xf, gain1, mod[:, 0:d], mod[:, d:2 * d])
    xb = xn.astype(BF16)
    halo = jnp.concatenate([xprev_ref[0], xnext_ref[0]], axis=0)
    xb_ref[0:tm, :] = xb
    xb_ref[tm:, :] = _rms_mod(halo, gain1, mod[:, 0:d], mod[:, d:2 * d]).astype(BF16)
    xlo_ref[...] = (xn - xb.astype(F32)).astype(BF16)

    def project(c0, width):
        return jnp.dot(xb_ref[0:tm, :], w_ref[:, c0:c0 + width], preferred_element_type=F32)

    taps = cw_ref[...]
    conv_bias = cb_ref[...]
    n_q_cols = d // 2

    w2 = 2 * LANES

    def conv_columns(c_lo, c_hi):
        for c0 in range(c_lo, c_hi, w2):
            acc = jnp.zeros((tm, w2), F32) + conv_bias[:, c0:c0 + w2]
            for k in range(CONV_K):
                r0 = SUBLANES - CONV_K // 2 + k
                acc = acc + ext_ref[r0:r0 + tm, c0:c0 + w2] * taps[k:k + 1, c0:c0 + w2]
            out = _silu(acc)
            if c0 < n_q_cols:
                out = out * MLSTM_DQK ** -0.5
            qk_ref[0, :, c0:c0 + w2] = out.astype(BF16)

    cos = cos_ref[...]
    sin = sin_ref[...]
    grp = grp_ref[...]
    gq = gq_ref[...]
    gk = gk_ref[...]
    hw = 2 * DIFF_HEAD_DIM

    def scaled_sumsq(x2):
        ss = jnp.dot((x2 * x2).astype(BF16), grp, preferred_element_type=F32)
        return lax.rsqrt(ss * (1.0 / DIFF_HEAD_DIM) + RMS_EPS)

    def rotated(y):
        return y * cos + pltpu.roll(y, DIFF_HEAD_DIM, axis=1) * sin

    ones_rows = jnp.ones((V_ROWS - hw, tm), BF16)

    def attention_pair(pair):
        sl2 = slice(pair * 2 * hw, (pair + 1) * 2 * hw)
        yq2 = yq_ref[:, sl2]
        yk2 = yk_ref[:, sl2]
        qn = yq2 * scaled_sumsq(yq2)
        kn = yk2 * scaled_sumsq(yk2)
        for u in range(2):
            h = 2 * pair + u
            sl = slice(h * hw, (h + 1) * hw)
            su = slice(u * hw, (u + 1) * hw)
            qt_ref[0, sl, :] = (rotated(qn[:, su] * gq) * Q_SCALE_LOG2).T.astype(BF16)
            kr_ref[0, :, sl] = rotated(kn[:, su] * gk).astype(BF16)
            vt_ref[0, h * V_ROWS:h * V_ROWS + hw, :] = yv_ref[:, sl].T.astype(BF16)
            vt_ref[0, h * V_ROWS + hw:(h + 1) * V_ROWS, :] = ones_rows

    def gate_epilogue():
        g = gpre_ref[...]
        lf = _log_sigmoid(g)
        lf_hi = lf.astype(BF16)
        lf_lo = (lf - lf_hi.astype(F32)).astype(BF16)
        lf2 = jnp.concatenate([lf_hi, lf_lo], axis=1)
        rows = lax.broadcasted_iota(jnp.int32, (tm, tm), 0)
        cols = lax.broadcasted_iota(jnp.int32, (tm, tm), 1)
        lower = jnp.where(cols <= rows, 1.0, 0.0).astype(BF16)
        upper = jnp.where(cols >= rows, 1.0, 0.0).astype(BF16)
        cum_f2 = jnp.dot(lower, lf2, preferred_element_type=F32)
        cum_b2 = jnp.dot(upper, lf2, preferred_element_type=F32)
        cum_f = cum_f2[:, :LANES] + cum_f2[:, LANES:]
        cum_b = cum_b2[:, :LANES] + cum_b2[:, LANES:]
        lane = lax.broadcasted_iota(jnp.int32, (tm, LANES), 1)
        cum = jnp.where(lane < 2 * MLSTM_HEADS, cum_f, cum_b)
        a = g - pltpu.roll(cum, LANES - MLSTM_HEADS, axis=1)
        a = a * LOG2_E
        gates_ref[0] = jnp.where((lane % (2 * MLSTM_HEADS)) < MLSTM_HEADS, a, cum * LOG2_E)
        gates_t_ref[0, 0] = a.T[0:4 * MLSTM_HEADS, :]

    for r in range(4):
        pm_ref[0, :, r * d:(r + 1) * d] = project((1 + r) * d, d).astype(BF16)
        attention_pair(r)
    yq_ref[...] = project(5 * d, d)
    conv_columns(0, d // 2)
    yk_ref[...] = project(6 * d, d)
    conv_columns(d // 2, d)
    yv_ref[...] = project(7 * d, d)
    gate_epilogue()
    y_qk = jnp.dot(xb_ref[...], w_ref[:, 0:d], preferred_element_type=F32)
    has_prev = jnp.logical_and(ti > 0, ti < n_lat_tiles)
    has_next = ti < n_lat_tiles - 1
    ext_ref[0:SUBLANES, :] = jnp.where(has_prev, y_qk[tm:tm + SUBLANES], 0.0)
    ext_ref[SUBLANES:SUBLANES + tm, :] = y_qk[0:tm]
    ext_ref[SUBLANES + tm:, :] = jnp.where(has_next, y_qk[tm + SUBLANES:], 0.0)
    g2 = (jnp.dot(xb_ref[0:tm, :], wg_ref[...], preferred_element_type=F32)
          + jnp.dot(xlo_ref[...], wg_ref[...], preferred_element_type=F32))
    gpre_ref[...] = g2[:, :LANES] + g2[:, LANES:] + gbias_ref[...]


def _projection(x, ctx, mods3, norm1, wbig, wg, gate_bias, cos_t, sin_t, gq, gk, grp, conv_w8, conv_b):
    b, s, d = x.shape
    n_ctx = ctx.shape[1]
    tm = TOK_TILE
    assert tm == CHUNK
    assert n_ctx == tm
    n_lat_tiles = s // tm
    n_tiles = (s + n_ctx) // tm
    t = s + n_ctx
    n_out = wbig.shape[1]
    da_w = DIFF_HEADS * 2 * DIFF_HEAD_DIM
    halo_stride = tm // SUBLANES
    last_halo = s // SUBLANES - 1
    const = lambda shape: pl.BlockSpec(shape, lambda bi, i: (0,) * len(shape))
    cur = lambda i: jnp.minimum(i, n_tiles - 1)
    late = lambda i: jnp.maximum(i - 1, 0)
    tok = lambda width: pl.BlockSpec((1, tm, width), lambda bi, i: (bi, late(i), 0))
    tok_t = lambda rows: pl.BlockSpec((1, rows, tm), lambda bi, i: (bi, 0, late(i)))
    n_ctx_tiles = n_ctx // tm
    return pl.pallas_call(
        functools.partial(_proj_kernel, n_lat_tiles, d),
        grid=(b, n_tiles + 1),
        in_specs=[
            pl.BlockSpec((1, tm, d), lambda bi, i: (bi, jnp.minimum(i, n_lat_tiles - 1), 0)),
            pl.BlockSpec((1, SUBLANES, d),
                         lambda bi, i: (bi, jnp.clip(cur(i) * halo_stride - 1, 0, last_halo), 0)),
            pl.BlockSpec((1, SUBLANES, d),
                         lambda bi, i: (bi, jnp.clip((cur(i) + 1) * halo_stride, 0, last_halo), 0)),
            pl.BlockSpec((1, tm, d), lambda bi, i: (bi, jnp.clip(i - n_lat_tiles, 0, n_ctx_tiles - 1), 0)),
            pl.BlockSpec((1, 1, N_MOD * d), lambda bi, i: (jnp.where(i < n_lat_tiles, bi, b), 0, 0)),
            const((1, d)),
            pl.BlockSpec((d, n_out), lambda bi, i: (0, 0), pipeline_mode=pl.Buffered(1)),
            const((d, 2 * LANES)), const((1, LANES)),
            pl.BlockSpec((tm, LANES), lambda bi, i: (late(i), 0)),
            pl.BlockSpec((tm, LANES), lambda bi, i: (late(i), 0)),
            const((1, LANES)), const((1, LANES)), const((2 * LANES, 2 * LANES)),
            const((SUBLANES, d)), const((1, d)),
        ],
        out_specs=[tok(d), pl.BlockSpec((1, tm, 4 * d), lambda bi, i: (bi, cur(i), 0)),
                   tok_t(da_w), tok(da_w), tok_t(DIFF_HEADS * V_ROWS), tok(LANES),
                   pl.BlockSpec((1, 1, 4 * MLSTM_HEADS, tm), lambda bi, i: (bi, late(i), 0, 0))],
        out_shape=[jax.ShapeDtypeStruct((b, t, d), BF16),
                   jax.ShapeDtypeStruct((b, t, 4 * d), BF16),
                   jax.ShapeDtypeStruct((b, da_w, t), BF16),
                   jax.ShapeDtypeStruct((b, t, da_w), BF16),
                   jax.ShapeDtypeStruct((b, DIFF_HEADS * V_ROWS, t), BF16),
                   jax.ShapeDtypeStruct((b, t, LANES), F32),
                   jax.ShapeDtypeStruct((b, n_tiles, 4 * MLSTM_HEADS, tm), F32)],
        scratch_shapes=[pltpu.VMEM((tm + 2 * SUBLANES, d), F32),
                        pltpu.VMEM((tm, da_w), F32), pltpu.VMEM((tm, da_w), F32), pltpu.VMEM((tm, da_w), F32),
                        pltpu.VMEM((tm, LANES), F32),
                        pltpu.VMEM((tm + 2 * SUBLANES, d), BF16),
                        pltpu.VMEM((tm, d), BF16)],
        compiler_params=_cparams(("parallel", "arbitrary")),
        name="proj",
    )(x, x, x, ctx, mods3, norm1, wbig, wg, gate_bias, cos_t, sin_t, gq, gk, grp, conv_w8, conv_b)


def _mlstm_direction(reverse, q, k, v_aug, a_col, a_row, lf_cum_col, lf_total, c_ref, m_ref, hd):
    l = q.shape[0]
    dv = v_aug.shape[1] - LANES

    def wide(x, n_tiles):
        return jnp.concatenate([x] * n_tiles, axis=1)

    m_prev = m_ref[hd:hd + 1, :]
    c_aug = c_ref[hd]
    lf_cum_rep = jnp.broadcast_to(lf_cum_col, (l, LANES))
    a_rep = jnp.broadcast_to(a_col, (l, LANES))
    rows = lax.broadcasted_iota(jnp.int32, (l, l), 0)
    cols = lax.broadcasted_iota(jnp.int32, (l, l), 1)
    visible = (cols >= rows) if reverse else (cols <= rows)
    a_mat = jnp.where(visible, a_row, -jnp.inf)
    m_rep = jnp.maximum(jnp.broadcast_to(jnp.max(a_mat, axis=-1, keepdims=True), (l, LANES)), m_prev)
    decay = jnp.exp2(a_mat - wide(m_rep, l // LANES))
    s = lax.dot_general(q, k, (((1,), (1,)), ((), ())), preferred_element_type=F32)
    pmat = (s * decay).astype(BF16)
    n_wide = v_aug.shape[1] // LANES
    intra = jnp.dot(pmat, v_aug, preferred_element_type=F32)
    inter = jnp.dot(q, c_aug.astype(BF16), preferred_element_type=F32)
    numden = intra + wide(jnp.exp2(m_prev - m_rep), n_wide) * inter
    den = numden[:, dv:]
    floor = jnp.exp2(-(lf_cum_rep + m_rep))
    h = numden[:, :dv] * wide(1.0 / jnp.maximum(jnp.abs(den), floor), dv // LANES)
    m_last = jnp.maximum(jnp.max(a_rep, axis=0, keepdims=True), m_prev)
    w = jnp.exp2(a_rep - m_last)
    kw = (w * k.astype(F32)).astype(BF16)
    kv = lax.dot_general(kw, v_aug, (((0,), (0,)), ((), ())), preferred_element_type=F32)
    c_ref[hd] = wide(jnp.exp2(m_prev - m_last), n_wide) * c_aug + kv
    m_ref[hd:hd + 1, :] = lf_total + m_last
    return h


def _mlstm_kernel(qf_ref, kf_ref, vf_ref, gf_ref, gtf_ref, qb_ref, kb_ref, vb_ref, gb_ref, gtb_ref,
                  hf_ref, hb_ref, c_ref, m_ref):
    s_idx = pl.program_id(1)

    @pl.when(s_idx == 0)
    def _():
        c_ref[...] = jnp.zeros_like(c_ref)
        m_ref[...] = jnp.zeros_like(m_ref)

    nb, l = qf_ref.shape[0], qf_ref.shape[1]
    nh = MLSTM_HEADS
    ones_col = jnp.ones((l, LANES), BF16)
    for bb in range(nb):
        for reverse, q_ref, k_ref, v_ref, g_ref, gt_ref, h_ref in (
                (False, qf_ref, kf_ref, vf_ref, gf_ref, gtf_ref, hf_ref),
                (True, qb_ref, kb_ref, vb_ref, gb_ref, gtb_ref, hb_ref)):
            g = g_ref[bb]
            g_t = gt_ref[bb, 0]
            base = 2 * nh if reverse else 0
            for h in range(nh):
                ca, cc = base + h, base + nh + h
                hd = bb * 2 * nh + (nh if reverse else 0) + h
                lf_cum_col = g[:, cc:cc + 1]
                lf_total = g[0:1, cc:cc + 1] if reverse else g[l - 1:l, cc:cc + 1]
                q = q_ref[bb, :, h * MLSTM_DQK:(h + 1) * MLSTM_DQK]
                k = k_ref[bb, :, h * MLSTM_DQK:(h + 1) * MLSTM_DQK]
                v_aug = jnp.concatenate([v_ref[bb, :, h * MLSTM_DV:(h + 1) * MLSTM_DV], ones_col], axis=1)
                hval = _mlstm_direction(reverse, q, k, v_aug, g[:, ca:ca + 1], g_t[ca:ca + 1, :],
                                        lf_cum_col, lf_total, c_ref, m_ref, hd)
                h_ref[bb, :, h * MLSTM_DV:(h + 1) * MLSTM_DV] = hval.astype(h_ref.dtype)


def _mlstm(qk, pm, gates, gates_t, n_lat, n_ctx, v_region):
    b, t, _ = qk.shape
    l = CHUNK
    n_lat_c = n_lat // l
    n_ctx_c = n_ctx // l
    n_steps = n_lat_c + n_ctx_c
    qw = MLSTM_HEADS * MLSTM_DQK
    vw = MLSTM_HEADS * MLSTM_DV

    def fwd(si):
        return jnp.where(si < n_ctx_c, n_lat_c + si, si - n_ctx_c)

    def bwd(si):
        return jnp.where(si < n_ctx_c, n_steps - 1 - si, n_steps - 1 - si)

    nb = MLSTM_BATCH
    assert b % nb == 0
    in_specs = []
    for order in (fwd, bwd):
        in_specs += [
            pl.BlockSpec((nb, l, qw), lambda bi, si, o=order: (bi, o(si), 0)),
            pl.BlockSpec((nb, l, qw), lambda bi, si, o=order: (bi, o(si), 1)),
            pl.BlockSpec((nb, l, vw), lambda bi, si, o=order: (bi, o(si), v_region)),
            pl.BlockSpec((nb, l, LANES), lambda bi, si, o=order: (bi, o(si), 0)),
            pl.BlockSpec((nb, 1, 4 * MLSTM_HEADS, l), lambda bi, si, o=order: (bi, o(si), 0, 0)),
        ]
    out_specs = [pl.BlockSpec((nb, l, vw), lambda bi, si: (bi, fwd(si), 0)),
                 pl.BlockSpec((nb, l, vw), lambda bi, si: (bi, bwd(si), 0))]
    n_scans = nb * 2 * MLSTM_HEADS
    return pl.pallas_call(
        _mlstm_kernel,
        grid=(b // nb, n_steps),
        in_specs=in_specs,
        out_specs=out_specs,
        out_shape=[jax.ShapeDtypeStruct((b, t, vw), BF16)] * 2,
        scratch_shapes=[pltpu.VMEM((n_scans, MLSTM_DQK, MLSTM_DV + LANES), F32),
                        pltpu.VMEM((n_scans, LANES), F32)],
        compiler_params=_cparams(("parallel", "arbitrary")),
        name="mlstm",
    )(qk, qk, pm, gates, gates_t, qk, qk, pm, gates, gates_t)


def _attn_kernel(n_keys, tq, qt_ref, k_ref, vt_ref, lam_ref, gain_ref, o_ref,
                 s_ref, pm_ref, e_ref, al_ref, m_ref, acc_ref):
    tk = ATT_TK
    n = n_keys // tk
    nq = qt_ref.shape[2] // tq
    width = 2 * tq
    row = lax.broadcasted_iota(jnp.int32, (qt_ref.shape[1], tq), 0)
    first = (row % DIFF_HEAD_DIM) < (DIFF_HEAD_DIM // 2)

    acc_ref[...] = jnp.zeros_like(acc_ref)

    def scores(j, qb):
        off = pl.multiple_of(j * tk, tk)
        qt = qt_ref[0, :, pl.ds(pl.multiple_of(qb * tq, tq), tq)]
        zero = jnp.zeros_like(qt)
        qq = jnp.concatenate([jnp.where(first, qt, zero), jnp.where(first, zero, qt)], axis=1)
        s = jnp.dot(k_ref[0, pl.ds(off, tk), :], qq, preferred_element_type=F32)
        s_ref[...] = s
        pm_ref[...] = jnp.max(s.reshape(tk // SUBLANES, SUBLANES, width), axis=0)

    def softmax(j, qb):
        m_old = jnp.where(j == 0, -jnp.inf, m_ref[...])
        m_new = jnp.maximum(m_old, jnp.max(pm_ref[...], axis=0, keepdims=True))
        alpha = jnp.exp2(m_old - m_new)
        e = jnp.exp2(s_ref[...] - m_new)
        e_ref[...] = e.astype(BF16)
        al_ref[...] = alpha
        m_ref[...] = m_new

    def accumulate(j):
        off = pl.multiple_of(j * tk, tk)
        pv = jnp.dot(vt_ref[0, :, pl.ds(off, tk)], e_ref[...], preferred_element_type=F32)
        acc_ref[...] = al_ref[...] * acc_ref[...] + pv

    def finalize(qb):
        lv = lam_ref[...]
        lam = (jnp.exp(jnp.sum(lv[0:1] * lv[1:2], axis=-1, keepdims=True))
               - jnp.exp(jnp.sum(lv[2:3] * lv[3:4], axis=-1, keepdims=True)) + LAM_INIT_L0)
        hw = o_ref.shape[2]
        inv_l = 1.0 / acc_ref[hw:hw + 1, :]
        o_all = acc_ref[0:hw, :] * inv_l
        o_t = o_all[:, :tq] - lam * o_all[:, tq:]
        ms = jnp.mean(o_t * o_t, axis=0, keepdims=True)
        out = (o_t * lax.rsqrt(ms + RMS_EPS)).T * (gain_ref[...] * (1.0 - LAM_INIT_L0))
        o_ref[0, pl.ds(pl.multiple_of(qb * tq, tq), tq), :] = out.astype(o_ref.dtype)

    def advance(j, qb):
        wrap = j == n - 1
        return jnp.where(wrap, 0, j + 1), jnp.where(wrap, qb + 1, qb)

    zero_i = jnp.int32(0)
    item0 = (zero_i, zero_i)
    item1 = advance(*item0)
    scores(*item0)
    softmax(*item0)
    scores(*item1)

    def iteration(t, carry):
        ja, qa, jm, qm, js, qs = carry
        accumulate(ja)
        softmax(jm, qm)
        scores(js, qs)

        @pl.when(ja == n - 1)
        def _():
            finalize(qa)

        return (jm, qm, js, qs) + advance(js, qs)

    item2 = advance(*item1)
    lax.fori_loop(2, nq * n, iteration, item0 + item1 + item2)
    accumulate(n - 2)
    softmax(n - 1, nq - 1)
    accumulate(n - 1)
    finalize(nq - 1)


def _attention(qt, kr, vt, lam_vecs, diff_gain, n_lat):
    b, t, _ = kr.shape
    tq = ATT_TQ
    hw = 2 * DIFF_HEAD_DIM
    nq = n_lat // tq
    assert t // ATT_TK >= 2
    return pl.pallas_call(
        functools.partial(_attn_kernel, t, tq),
        grid=(b, DIFF_HEADS),
        in_specs=[pl.BlockSpec((1, hw, n_lat), lambda bi, h: (bi, h, 0)),
                  pl.BlockSpec((1, t, hw), lambda bi, h: (bi, 0, h)),
                  pl.BlockSpec((1, V_ROWS, t), lambda bi, h: (bi, h, 0)),
                  pl.BlockSpec(lam_vecs.shape, lambda bi, h: (0, 0)),
                  pl.BlockSpec((1, hw), lambda bi, h: (0, 0))],
        out_specs=pl.BlockSpec((1, n_lat, hw), lambda bi, h: (bi, 0, h)),
        out_shape=jax.ShapeDtypeStruct((b, n_lat, DIFF_HEADS * hw), BF16),
        scratch_shapes=[pltpu.VMEM((ATT_TK, 2 * tq), F32),
                        pltpu.VMEM((SUBLANES, 2 * tq), F32),
                        pltpu.VMEM((ATT_TK, 2 * tq), BF16),
                        pltpu.VMEM((1, 2 * tq), F32),
                        pltpu.VMEM((1, 2 * tq), F32),
                        pltpu.VMEM((V_ROWS, 2 * tq), F32)],
        compiler_params=_cparams(("parallel", "arbitrary")),
        name="attn",
    )(qt, kr, vt, lam_vecs, diff_gain)


def _merge_kernel(d, hf_ref, hb_ref, mo_ref, ga_ref, gb_ref, hB_ref, x_ref, mod_ref, gain_ref,
                  wa_ref, wb_ref, wo_ref, o_ref):
    gain = gain_ref[...]
    parts = []
    for h in range(MLSTM_HEADS):
        sl = slice(h * MLSTM_DV, (h + 1) * MLSTM_DV)
        hs = hf_ref[0, :, sl].astype(F32) + hb_ref[0, :, sl].astype(F32)
        ms = jnp.mean(hs * hs, axis=-1, keepdims=True)
        hn = hs * lax.rsqrt(ms + RMS_EPS) * gain[:, sl]
        parts.append((_sigmoid(mo_ref[0, :, sl].astype(F32)) * hn).astype(BF16))
    h_a = jnp.concatenate(parts, axis=1)
    ya = jnp.dot(h_a, wa_ref[...], preferred_element_type=F32)
    yb = jnp.dot(hB_ref[0], wb_ref[...], preferred_element_type=F32)
    y = _sigmoid(ga_ref[0].astype(F32)) * ya + _sigmoid(gb_ref[0].astype(F32)) * yb
    yo = jnp.dot(y.astype(BF16), wo_ref[...], preferred_element_type=F32)
    g1 = mod_ref[0][:, 2 * d:3 * d]
    o_ref[0] = x_ref[0] + g1 * yo


def _merge(hf, hb, p, h_b, x, mods3, mlstm_gain, wa, wb, wo, mo_region, ga_region, gb_region):
    b, s, d = x.shape
    tm = LAT_TILE
    wspec = lambda shape: pl.BlockSpec(shape, lambda bi, i: (0, 0), pipeline_mode=pl.Buffered(1))
    tok = lambda region: pl.BlockSpec((1, tm, d), lambda bi, i, r=region: (bi, i, r))
    return pl.pallas_call(
        functools.partial(_merge_kernel, d),
        grid=(b, s // tm),
        in_specs=[tok(0), tok(0), tok(mo_region), tok(ga_region), tok(gb_region), tok(0), tok(0),
                  pl.BlockSpec((1, 1, N_MOD * d), lambda bi, i: (bi, 0, 0)),
                  pl.BlockSpec((1, d), lambda bi, i: (0, 0)),
                  wspec(wa.shape), wspec(wb.shape), wspec(wo.shape)],
        out_specs=pl.BlockSpec((1, tm, d), lambda bi, i: (bi, i, 0)),
        out_shape=jax.ShapeDtypeStruct((b, s, d), F32),
        compiler_params=_cparams(("parallel", "arbitrary")),
        name="merge",
    )(hf, hb, p, p, p, h_b, x, mods3, mlstm_gain, wa, wb, wo)


def _ffn_kernel(d, hidden, x_ref, mod_ref, g_ref, wi_ref, wo_ref, o_ref):
    x1 = x_ref[0]
    mod = mod_ref[0]
    xn = _rms_mod(x1, g_ref[...], mod[:, 3 * d:4 * d], mod[:, 4 * d:5 * d]).astype(BF16)
    a = jnp.dot(xn, wi_ref[:, :hidden], preferred_element_type=F32)
    bb = jnp.dot(xn, wi_ref[:, hidden:], preferred_element_type=F32)
    hmid = (_silu(a) * bb).astype(BF16)
    y = jnp.dot(hmid, wo_ref[...], preferred_element_type=F32)
    o_ref[0] = x1 + mod[:, 5 * d:6 * d] * y


def _ffn(x1, mods3, norm2, wi, wo):
    b, s, d = x1.shape
    hidden = wo.shape[0]
    tm = LAT_TILE
    wspec = lambda shape: pl.BlockSpec(shape, lambda bi, i: (0, 0), pipeline_mode=pl.Buffered(1))
    return pl.pallas_call(
        functools.partial(_ffn_kernel, d, hidden),
        grid=(b, s // tm),
        in_specs=[pl.BlockSpec((1, tm, d), lambda bi, i: (bi, i, 0)),
                  pl.BlockSpec((1, 1, N_MOD * d), lambda bi, i: (bi, 0, 0)),
                  pl.BlockSpec((1, d), lambda bi, i: (0, 0)),
                  wspec(wi.shape), wspec(wo.shape)],
        out_specs=pl.BlockSpec((1, tm, d), lambda bi, i: (bi, i, 0)),
        out_shape=jax.ShapeDtypeStruct((b, s, d), F32),
        compiler_params=_cparams(("parallel", "arbitrary")),
        name="ffn",
    )(x1, mods3, norm2, wi, wo)


def _diff_lane_perm():
    l = np.arange(2 * DIFF_HEAD_DIM)
    half, r = l // DIFF_HEAD_DIM, l % DIFF_HEAD_DIM
    qsel, axis, f = r // 32, (r % 32) // ROPE_FREQS, r % ROPE_FREQS
    return qsel * DIFF_HEAD_DIM + axis * 2 * ROPE_FREQS + half * ROPE_FREQS + f, qsel, axis, f, half


def _rope_tables(n_lat, n_ctx):
    _, _, axis, f, half = _diff_lane_perm()
    pos = np.arange(n_lat)
    row, col = pos // GRID_W, pos % GRID_W
    inv = (ROPE_BASE ** (-np.arange(ROPE_FREQS, dtype=np.float32) / ROPE_FREQS)).astype(np.float32)
    ang = (np.where(axis[None, :] == 0, row[:, None], col[:, None]).astype(np.float32) * inv[f][None, :])
    cos = np.concatenate([np.cos(ang), np.ones((n_ctx, LANES))], axis=0)
    sin = np.concatenate([np.sin(ang) * np.where(half == 0, -1.0, 1.0)[None, :], np.zeros((n_ctx, LANES))], axis=0)
    return jnp.asarray(cos, F32), jnp.asarray(sin, F32)


def kernel(x, c, ctx, c_ctx, w_mod, b_mod, norm1, norm2, w_in, b_gate, conv_w, conv_b, mlstm_norm, q_norm, k_norm,
           lam_vecs, diff_norm, w_branch_a, w_branch_b, w_out, w_ffn_in, w_ffn_out):
    b, s, d = x.shape
    n_ctx = ctx.shape[1]
    assert w_mod.shape[0] == 1, "single-layer block"
    assert s % CHUNK == 0 and n_ctx % CHUNK == 0 and s % ATT_TQ == 0 and s % LAT_TILE == 0
    qk_w = MLSTM_HEADS * MLSTM_DQK
    v_w = MLSTM_HEADS * MLSTM_DV
    da_w = DIFF_HEADS * 2 * DIFF_HEAD_DIM
    n_gates = 4 * MLSTM_HEADS
    sizes = (qk_w, qk_w, v_w, v_w, n_gates, da_w, da_w, da_w, d, d)
    offs = np.concatenate([[0], np.cumsum(sizes)])
    seg = lambda i: w_in[0][:, offs[i]:offs[i + 1]]

    perm, qsel, _, _, _ = _diff_lane_perm()
    head_perm = (np.arange(DIFF_HEADS)[:, None] * 2 * DIFF_HEAD_DIM + perm[None, :]).reshape(-1)
    segb = lambda i: seg(i).astype(BF16)
    wbig = jnp.concatenate([segb(0), segb(1), segb(2), segb(3), segb(8), segb(9),
                            segb(5)[:, head_perm], segb(6)[:, head_perm], segb(7)], axis=1)
    R_V, R_O, R_GA, R_GB = range(4)
    wg_f32 = jnp.pad(seg(4), ((0, 0), (0, LANES - n_gates)))
    wg_hi = wg_f32.astype(BF16)
    wg = jnp.concatenate([wg_hi, (wg_f32 - wg_hi.astype(F32)).astype(BF16)], axis=1)
    gate_bias = jnp.pad(b_gate[0], (0, LANES - n_gates)).reshape(1, LANES)
    conv_w8 = jnp.pad(conv_w[0], ((0, SUBLANES - CONV_K), (0, 0)))
    gq = q_norm[0][perm % DIFF_HEAD_DIM].reshape(1, LANES)
    gk = k_norm[0][perm % DIFF_HEAD_DIM].reshape(1, LANES)
    unit = np.concatenate([qsel, 2 + qsel])
    grp = jnp.asarray(unit[:, None] == unit[None, :], BF16)
    cos_t, sin_t = _rope_tables(s, n_ctx)

    cc = jnp.concatenate([c, c_ctx[None, :], jnp.zeros((SUBLANES - b - 1, d), F32)], axis=0)
    mods = _modulation(cc, w_mod[0], b_mod[0])
    mods3 = mods.reshape(SUBLANES, 1, N_MOD * d)

    qk, pm, qt, kr, vt, gates, gates_t = _projection(x, ctx, mods3, norm1, wbig, wg, gate_bias,
                                                     cos_t, sin_t, gq, gk, grp, conv_w8, conv_b)
    hf, hb = _mlstm(qk, pm, gates, gates_t, s, n_ctx, R_V)
    h_b = _attention(qt, kr, vt, lam_vecs[0], diff_norm, s)
    x1 = _merge(hf, hb, pm, h_b, x, mods3, mlstm_norm, w_branch_a[0].astype(BF16), w_branch_b[0].astype(BF16),
                w_out[0].astype(BF16), R_O, R_GA, R_GB)
    return _ffn(x1, mods3, norm2, w_ffn_in[0].astype(BF16), w_ffn_out[0].astype(BF16))
```

```python
import functools
import math

import jax
import jax.numpy as jnp
import numpy as np
from jax import lax
from jax.experimental import pallas as pl
from jax.experimental.pallas import tpu as pltpu

F32 = jnp.float32
BF16 = jnp.bfloat16
HIGHEST = lax.Precision.HIGHEST

GRID_W = 64
N_MOD = 6
MLSTM_HEADS = 4
MLSTM_DQK = 128
MLSTM_DV = 256
CONV_K = 5
DIFF_HEADS = 8
DIFF_HEAD_DIM = 64
ROPE_BASE = 10000.0
ROPE_FREQS = DIFF_HEAD_DIM // 4
RMS_EPS = 1e-6
LAM_INIT_L0 = 0.8 - 0.6 * math.exp(-0.3 * 0)
LOG2_E = math.log2(math.e)
Q_SCALE_LOG2 = DIFF_HEAD_DIM ** -0.5 * LOG2_E
LANES = 128
SUBLANES = 8
BF16_SUBLANES = 16
V_ROWS = 2 * DIFF_HEAD_DIM + BF16_SUBLANES
VMEM_LIMIT = 56 * 1024 * 1024

TOK_TILE = 256
LAT_TILE = 512
CHUNK = 256
MLSTM_BATCH = 4
ATT_TQ = 1024
ATT_TK = 2816


def _cparams(sem):
    return pltpu.CompilerParams(dimension_semantics=sem, vmem_limit_bytes=VMEM_LIMIT)


def _sigmoid(x):
    return 0.5 + 0.5 * jnp.tanh(0.5 * x)


def _silu(x):
    h = 0.5 * x
    return h + h * jnp.tanh(h)


def _log_sigmoid(x):
    return jnp.minimum(x, 0.0) - jnp.log(1.0 + jnp.exp(-jnp.abs(x)))


def _mod_kernel(c_ref, w_ref, b_ref, o_ref):
    a = _silu(c_ref[...])
    o_ref[...] = jnp.dot(a, w_ref[...], preferred_element_type=F32, precision=HIGHEST) + b_ref[...]


def _modulation(cc, w_mod, b_mod):
    rows, d = cc.shape
    n = w_mod.shape[1]
    tn = 1024
    return pl.pallas_call(
        _mod_kernel,
        grid=(n // tn,),
        in_specs=[pl.BlockSpec((rows, d), lambda j: (0, 0)),
                  pl.BlockSpec((d, tn), lambda j: (0, j)),
                  pl.BlockSpec((1, tn), lambda j: (0, j))],
        out_specs=pl.BlockSpec((rows, tn), lambda j: (0, j)),
        out_shape=jax.ShapeDtypeStruct((rows, n), F32),
        compiler_params=_cparams(("arbitrary",)),
        name="mod",
    )(cc, w_mod, b_mod.reshape(1, n))


def _rms_mod(xf, g, shift, scale):
    ms = jnp.mean(xf * xf, axis=-1, keepdims=True)
    return xf * lax.rsqrt(ms + RMS_EPS) * g * (1.0 + scale) + shift


def _proj_kernel(n_lat_tiles, d, x_ref, xprev_ref, xnext_ref, ctx_ref, mod_ref, g_ref,
                 wm_ref, wgab_ref, watt_ref, wg_ref, gbias_ref,
                 cos_ref, sin_ref, gq_ref, gk_ref, grp_ref, cw_ref, cb_ref,
                 qk_ref, pm_ref, qt_ref, kr_ref, vt_ref, gates_ref, gates_t_ref,
                 ext_ref, yq_ref, yk_ref, yv_ref, gpre_ref, xb_ref, xlo_ref):
    i = pl.program_id(1)
    tm = x_ref.shape[1]

    @pl.when(i == 0)
    def _():
        ext_ref[...] = jnp.zeros_like(ext_ref)
        yq_ref[...] = jnp.zeros_like(yq_ref)
        yk_ref[...] = jnp.zeros_like(yk_ref)
        yv_ref[...] = jnp.zeros_like(yv_ref)
        gpre_ref[...] = jnp.zeros_like(gpre_ref)

    n_tiles = pl.num_programs(1) - 1
    ti = jnp.minimum(i, n_tiles - 1)
    xf = jnp.where(ti < n_lat_tiles, x_ref[0], ctx_ref[0])
    mod = mod_ref[0]
    gain1 = g_ref[...]
    xn = _rms_mod(xf, gain1, mod[:, 0:d], mod[:, d:2 * d])
    xb = xn.astype(BF16)
    halo = jnp.concatenate([xprev_ref[0], xnext_ref[0]], axis=0)
    xb_ref[0:tm, :] = xb
    xb_ref[tm:, :] = _rms_mod(halo, gain1, mod[:, 0:d], mod[:, d:2 * d]).astype(BF16)
    xlo_ref[...] = (xn - xb.astype(F32)).astype(BF16)

    def project(w_ref, region):
        return jnp.dot(xb_ref[0:tm, :], w_ref[:, region * d:(region + 1) * d], preferred_element_type=F32)

    taps = cw_ref[...]
    conv_bias = cb_ref[...]
    n_q_cols = d // 2

    w2 = 2 * LANES

    def conv_columns(c_lo, c_hi):
        for c0 in range(c_lo, c_hi, w2):
            acc = jnp.zeros((tm, w2), F32) + conv_bias[:, c0:c0 + w2]
            for k in range(CONV_K):
                r0 = SUBLANES - CONV_K // 2 + k
                acc = acc + ext_ref[r0:r0 + tm, c0:c0 + w2] * taps[k:k + 1, c0:c0 + w2]
            out = _silu(acc)
            if c0 < n_q_cols:
                out = out * MLSTM_DQK ** -0.5
            qk_ref[0, :, c0:c0 + w2] = out.astype(BF16)

    cos = cos_ref[...]
    sin = sin_ref[...]
    grp = grp_ref[...]
    gq = gq_ref[...]
    gk = gk_ref[...]
    hw = 2 * DIFF_HEAD_DIM

    def scaled_sumsq(x2):
        ss = jnp.dot((x2 * x2).astype(BF16), grp, preferred_element_type=F32)
        return lax.rsqrt(ss * (1.0 / DIFF_HEAD_DIM) + RMS_EPS)

    def rotated(y):
        return y * cos + pltpu.roll(y, DIFF_HEAD_DIM, axis=1) * sin

    ones_rows = jnp.ones((V_ROWS - hw, tm), BF16)

    def attention_pair(pair):
        sl2 = slice(pair * 2 * hw, (pair + 1) * 2 * hw)
        yq2 = yq_ref[:, sl2]
        yk2 = yk_ref[:, sl2]
        qn = yq2 * scaled_sumsq(yq2)
        kn = yk2 * scaled_sumsq(yk2)
        for u in range(2):
            h = 2 * pair + u
            sl = slice(h * hw, (h + 1) * hw)
            su = slice(u * hw, (u + 1) * hw)
            qt_ref[0, sl, :] = (rotated(qn[:, su] * gq) * Q_SCALE_LOG2).T.astype(BF16)
            kr_ref[0, :, sl] = rotated(kn[:, su] * gk).astype(BF16)
            vt_ref[0, h * V_ROWS:h * V_ROWS + hw, :] = yv_ref[:, sl].T.astype(BF16)
            vt_ref[0, h * V_ROWS + hw:(h + 1) * V_ROWS, :] = ones_rows

    def gate_epilogue():
        g = gpre_ref[...]
        lf = _log_sigmoid(g)
        lf_hi = lf.astype(BF16)
        lf_lo = (lf - lf_hi.astype(F32)).astype(BF16)
        lf2 = jnp.concatenate([lf_hi, lf_lo], axis=1)
        rows = lax.broadcasted_iota(jnp.int32, (tm, tm), 0)
        cols = lax.broadcasted_iota(jnp.int32, (tm, tm), 1)
        lower = jnp.where(cols <= rows, 1.0, 0.0).astype(BF16)
        upper = jnp.where(cols >= rows, 1.0, 0.0).astype(BF16)
        cum_f2 = jnp.dot(lower, lf2, preferred_element_type=F32)
        cum_b2 = jnp.dot(upper, lf2, preferred_element_type=F32)
        cum_f = cum_f2[:, :LANES] + cum_f2[:, LANES:]
        cum_b = cum_b2[:, :LANES] + cum_b2[:, LANES:]
        lane = lax.broadcasted_iota(jnp.int32, (tm, LANES), 1)
        cum = jnp.where(lane < 2 * MLSTM_HEADS, cum_f, cum_b)
        a = g - pltpu.roll(cum, LANES - MLSTM_HEADS, axis=1)
        a = a * LOG2_E
        gates_ref[0] = jnp.where((lane % (2 * MLSTM_HEADS)) < MLSTM_HEADS, a, cum * LOG2_E)
        gates_t_ref[0, 0] = a.T[0:4 * MLSTM_HEADS, :]

    for r, (w_ref, region) in enumerate(((wm_ref, 1), (wm_ref, 2), (wgab_ref, 0), (wgab_ref, 1))):
        pm_ref[0, :, r * d:(r + 1) * d] = project(w_ref, region).astype(BF16)
        attention_pair(r)
    yq_ref[...] = project(watt_ref, 0)
    conv_columns(0, d // 2)
    yk_ref[...] = project(watt_ref, 1)
    conv_columns(d // 2, d)
    yv_ref[...] = project(watt_ref, 2)
    gate_epilogue()
    y_qk = jnp.dot(xb_ref[...], wm_ref[:, 0:d], preferred_element_type=F32)
    has_prev = jnp.logical_and(ti > 0, ti < n_lat_tiles)
    has_next = ti < n_lat_tiles - 1
    ext_ref[0:SUBLANES, :] = jnp.where(has_prev, y_qk[tm:tm + SUBLANES], 0.0)
    ext_ref[SUBLANES:SUBLANES + tm, :] = y_qk[0:tm]
    ext_ref[SUBLANES + tm:, :] = jnp.where(has_next, y_qk[tm + SUBLANES:], 0.0)
    g2 = (jnp.dot(xb_ref[0:tm, :], wg_ref[...], preferred_element_type=F32)
          + jnp.dot(xlo_ref[...], wg_ref[...], preferred_element_type=F32))
    gpre_ref[...] = g2[:, :LANES] + g2[:, LANES:] + gbias_ref[...]


def _projection(x, ctx, mods3, norm1, weights, wg, gate_bias, cos_t, sin_t, gq, gk, grp, conv_w8, conv_b):
    b, s, d = x.shape
    n_ctx = ctx.shape[1]
    tm = TOK_TILE
    assert tm == CHUNK
    assert n_ctx == tm
    n_lat_tiles = s // tm
    n_tiles = (s + n_ctx) // tm
    t = s + n_ctx
    w_main, w_gab, w_att = weights
    resident = lambda shape: pl.BlockSpec(shape, lambda bi, i: (0, 0), pipeline_mode=pl.Buffered(1))
    da_w = DIFF_HEADS * 2 * DIFF_HEAD_DIM
    halo_stride = tm // SUBLANES
    last_halo = s // SUBLANES - 1
    const = lambda shape: pl.BlockSpec(shape, lambda bi, i: (0,) * len(shape))
    cur = lambda i: jnp.minimum(i, n_tiles - 1)
    late = lambda i: jnp.maximum(i - 1, 0)
    tok = lambda width: pl.BlockSpec((1, tm, width), lambda bi, i: (bi, late(i), 0))
    tok_t = lambda rows: pl.BlockSpec((1, rows, tm), lambda bi, i: (bi, 0, late(i)))
    n_ctx_tiles = n_ctx // tm
    return pl.pallas_call(
        functools.partial(_proj_kernel, n_lat_tiles, d),
        grid=(b, n_tiles + 1),
        in_specs=[
            pl.BlockSpec((1, tm, d), lambda bi, i: (bi, jnp.minimum(i, n_lat_tiles - 1), 0)),
            pl.BlockSpec((1, SUBLANES, d),
                         lambda bi, i: (bi, jnp.clip(cur(i) * halo_stride - 1, 0, last_halo), 0)),
            pl.BlockSpec((1, SUBLANES, d),
                         lambda bi, i: (bi, jnp.clip((cur(i) + 1) * halo_stride, 0, last_halo), 0)),
            pl.BlockSpec((1, tm, d), lambda bi, i: (bi, jnp.clip(i - n_lat_tiles, 0, n_ctx_tiles - 1), 0)),
            pl.BlockSpec((1, 1, N_MOD * d), lambda bi, i: (jnp.where(i < n_lat_tiles, bi, b), 0, 0)),
            const((1, d)),
            resident(w_main.shape), resident(w_gab.shape), resident(w_att.shape),
            const((d, 2 * LANES)), const((1, LANES)),
            pl.BlockSpec((tm, LANES), lambda bi, i: (late(i), 0)),
            pl.BlockSpec((tm, LANES), lambda bi, i: (late(i), 0)),
            const((1, LANES)), const((1, LANES)), const((2 * LANES, 2 * LANES)),
            const((SUBLANES, d)), const((1, d)),
        ],
        out_specs=[tok(d), pl.BlockSpec((1, tm, 4 * d), lambda bi, i: (bi, cur(i), 0)),
                   tok_t(da_w), tok(da_w), tok_t(DIFF_HEADS * V_ROWS), tok(LANES),
                   pl.BlockSpec((1, 1, 4 * MLSTM_HEADS, tm), lambda bi, i: (bi, late(i), 0, 0))],
        out_shape=[jax.ShapeDtypeStruct((b, t, d), BF16),
                   jax.ShapeDtypeStruct((b, t, 4 * d), BF16),
                   jax.ShapeDtypeStruct((b, da_w, t), BF16),
                   jax.ShapeDtypeStruct((b, t, da_w), BF16),
                   jax.ShapeDtypeStruct((b, DIFF_HEADS * V_ROWS, t), BF16),
                   jax.ShapeDtypeStruct((b, t, LANES), F32),
                   jax.ShapeDtypeStruct((b, n_tiles, 4 * MLSTM_HEADS, tm), F32)],
        scratch_shapes=[pltpu.VMEM((tm + 2 * SUBLANES, d), F32),
                        pltpu.VMEM((tm, da_w), F32), pltpu.VMEM((tm, da_w), F32), pltpu.VMEM((tm, da_w), F32),
                        pltpu.VMEM((tm, LANES), F32),
                        pltpu.VMEM((tm + 2 * SUBLANES, d), BF16),
                        pltpu.VMEM((tm, d), BF16)],
        compiler_params=_cparams(("parallel", "arbitrary")),
        name="proj",
    )(x, x, x, ctx, mods3, norm1, w_main, w_gab, w_att, wg, gate_bias, cos_t, sin_t, gq, gk, grp, conv_w8, conv_b)


def _mlstm_direction(reverse, q, k, v_aug, a_col, a_row, lf_cum_col, lf_total, c_ref, m_ref, hd):
    l = q.shape[0]
    dv = v_aug.shape[1] - LANES

    def wide(x, n_tiles):
        return jnp.concatenate([x] * n_tiles, axis=1)

    m_prev = m_ref[hd:hd + 1, :]
    c_aug = c_ref[hd]
    lf_cum_rep = jnp.broadcast_to(lf_cum_col, (l, LANES))
    a_rep = jnp.broadcast_to(a_col, (l, LANES))
    rows = lax.broadcasted_iota(jnp.int32, (l, l), 0)
    cols = lax.broadcasted_iota(jnp.int32, (l, l), 1)
    visible = (cols >= rows) if reverse else (cols <= rows)
    a_mat = jnp.where(visible, a_row, -jnp.inf)
    m_rep = jnp.maximum(jnp.broadcast_to(jnp.max(a_mat, axis=-1, keepdims=True), (l, LANES)), m_prev)
    decay = jnp.exp2(a_mat - wide(m_rep, l // LANES))
    s = lax.dot_general(q, k, (((1,), (1,)), ((), ())), preferred_element_type=F32)
    pmat = (s * decay).astype(BF16)
    n_wide = v_aug.shape[1] // LANES
    intra = jnp.dot(pmat, v_aug, preferred_element_type=F32)
    inter = jnp.dot(q, c_aug.astype(BF16), preferred_element_type=F32)
    numden = intra + wide(jnp.exp2(m_prev - m_rep), n_wide) * inter
    den = numden[:, dv:]
    floor = jnp.exp2(-(lf_cum_rep + m_rep))
    h = numden[:, :dv] * wide(1.0 / jnp.maximum(jnp.abs(den), floor), dv // LANES)
    m_last = jnp.maximum(jnp.max(a_rep, axis=0, keepdims=True), m_prev)
    w = jnp.exp2(a_rep - m_last)
    kw = (w * k.astype(F32)).astype(BF16)
    kv = lax.dot_general(kw, v_aug, (((0,), (0,)), ((), ())), preferred_element_type=F32)
    c_ref[hd] = wide(jnp.exp2(m_prev - m_last), n_wide) * c_aug + kv
    m_ref[hd:hd + 1, :] = lf_total + m_last
    return h


def _mlstm_kernel(qf_ref, kf_ref, vf_ref, gf_ref, gtf_ref, qb_ref, kb_ref, vb_ref, gb_ref, gtb_ref,
                  hf_ref, hb_ref, c_ref, m_ref):
    s_idx = pl.program_id(1)

    @pl.when(s_idx == 0)
    def _():
        c_ref[...] = jnp.zeros_like(c_ref)
        m_ref[...] = jnp.zeros_like(m_ref)

    nb, l = qf_ref.shape[0], qf_ref.shape[1]
    nh = MLSTM_HEADS
    ones_col = jnp.ones((l, LANES), BF16)
    for bb in range(nb):
        for reverse, q_ref, k_ref, v_ref, g_ref, gt_ref, h_ref in (
                (False, qf_ref, kf_ref, vf_ref, gf_ref, gtf_ref, hf_ref),
                (True, qb_ref, kb_ref, vb_ref, gb_ref, gtb_ref, hb_ref)):
            g = g_ref[bb]
            g_t = gt_ref[bb, 0]
            base = 2 * nh if reverse else 0
            for h in range(nh):
                ca, cc = base + h, base + nh + h
                hd = bb * 2 * nh + (nh if reverse else 0) + h
                lf_cum_col = g[:, cc:cc + 1]
                lf_total = g[0:1, cc:cc + 1] if reverse else g[l - 1:l, cc:cc + 1]
                q = q_ref[bb, :, h * MLSTM_DQK:(h + 1) * MLSTM_DQK]
                k = k_ref[bb, :, h * MLSTM_DQK:(h + 1) * MLSTM_DQK]
                v_aug = jnp.concatenate([v_ref[bb, :, h * MLSTM_DV:(h + 1) * MLSTM_DV], ones_col], axis=1)
                hval = _mlstm_direction(reverse, q, k, v_aug, g[:, ca:ca + 1], g_t[ca:ca + 1, :],
                                        lf_cum_col, lf_total, c_ref, m_ref, hd)
                h_ref[bb, :, h * MLSTM_DV:(h + 1) * MLSTM_DV] = hval.astype(h_ref.dtype)


def _mlstm(qk, pm, gates, gates_t, n_lat, n_ctx, v_region):
    b, t, _ = qk.shape
    l = CHUNK
    n_lat_c = n_lat // l
    n_ctx_c = n_ctx // l
    n_steps = n_lat_c + n_ctx_c
    qw = MLSTM_HEADS * MLSTM_DQK
    vw = MLSTM_HEADS * MLSTM_DV

    def fwd(si):
        return jnp.where(si < n_ctx_c, n_lat_c + si, si - n_ctx_c)

    def bwd(si):
        return jnp.where(si < n_ctx_c, n_steps - 1 - si, n_steps - 1 - si)

    nb = MLSTM_BATCH
    assert b % nb == 0
    in_specs = []
    for order in (fwd, bwd):
        in_specs += [
            pl.BlockSpec((nb, l, qw), lambda bi, si, o=order: (bi, o(si), 0)),
            pl.BlockSpec((nb, l, qw), lambda bi, si, o=order: (bi, o(si), 1)),
            pl.BlockSpec((nb, l, vw), lambda bi, si, o=order: (bi, o(si), v_region)),
            pl.BlockSpec((nb, l, LANES), lambda bi, si, o=order: (bi, o(si), 0)),
            pl.BlockSpec((nb, 1, 4 * MLSTM_HEADS, l), lambda bi, si, o=order: (bi, o(si), 0, 0)),
        ]
    out_specs = [pl.BlockSpec((nb, l, vw), lambda bi, si: (bi, fwd(si), 0)),
                 pl.BlockSpec((nb, l, vw), lambda bi, si: (bi, bwd(si), 0))]
    n_scans = nb * 2 * MLSTM_HEADS
    return pl.pallas_call(
        _mlstm_kernel,
        grid=(b // nb, n_steps),
        in_specs=in_specs,
        out_specs=out_specs,
        out_shape=[jax.ShapeDtypeStruct((b, t, vw), BF16)] * 2,
        scratch_shapes=[pltpu.VMEM((n_scans, MLSTM_DQK, MLSTM_DV + LANES), F32),
                        pltpu.VMEM((n_scans, LANES), F32)],
        compiler_params=_cparams(("parallel", "arbitrary")),
        name="mlstm",
    )(qk, qk, pm, gates, gates_t, qk, qk, pm, gates, gates_t)


def _attn_kernel(n_keys, tq, qt_ref, k_ref, vt_ref, lam_ref, gain_ref, o_ref,
                 s_ref, pm_ref, e_ref, al_ref, m_ref, acc_ref):
    tk = ATT_TK
    n = n_keys // tk
    nq = qt_ref.shape[2] // tq
    width = 2 * tq
    row = lax.broadcasted_iota(jnp.int32, (qt_ref.shape[1], tq), 0)
    first = (row % DIFF_HEAD_DIM) < (DIFF_HEAD_DIM // 2)

    acc_ref[...] = jnp.zeros_like(acc_ref)

    def scores(j, qb):
        off = pl.multiple_of(j * tk, tk)
        qt = qt_ref[0, :, pl.ds(pl.multiple_of(qb * tq, tq), tq)]
        zero = jnp.zeros_like(qt)
        qq = jnp.concatenate([jnp.where(first, qt, zero), jnp.where(first, zero, qt)], axis=1)
        s = jnp.dot(k_ref[0, pl.ds(off, tk), :], qq, preferred_element_type=F32)
        s_ref[...] = s
        pm_ref[...] = jnp.max(s.reshape(tk // SUBLANES, SUBLANES, width), axis=0)

    def softmax(j, qb):
        m_old = jnp.where(j == 0, -jnp.inf, m_ref[...])
        m_new = jnp.maximum(m_old, jnp.max(pm_ref[...], axis=0, keepdims=True))
        alpha = jnp.exp2(m_old - m_new)
        e = jnp.exp2(s_ref[...] - m_new)
        e_ref[...] = e.astype(BF16)
        al_ref[...] = alpha
        m_ref[...] = m_new

    def accumulate(j):
        off = pl.multiple_of(j * tk, tk)
        pv = jnp.dot(vt_ref[0, :, pl.ds(off, tk)], e_ref[...], preferred_element_type=F32)
        acc_ref[...] = al_ref[...] * acc_ref[...] + pv

    def finalize(qb):
        lv = lam_ref[...]
        lam = (jnp.exp(jnp.sum(lv[0:1] * lv[1:2], axis=-1, keepdims=True))
               - jnp.exp(jnp.sum(lv[2:3] * lv[3:4], axis=-1, keepdims=True)) + LAM_INIT_L0)
        hw = o_ref.shape[2]
        inv_l = 1.0 / acc_ref[hw:hw + 1, :]
        o_all = acc_ref[0:hw, :] * inv_l
        o_t = o_all[:, :tq] - lam * o_all[:, tq:]
        ms = jnp.mean(o_t * o_t, axis=0, keepdims=True)
        out = (o_t * lax.rsqrt(ms + RMS_EPS)).T * (gain_ref[...] * (1.0 - LAM_INIT_L0))
        o_ref[0, pl.ds(pl.multiple_of(qb * tq, tq), tq), :] = out.astype(o_ref.dtype)

    def advance(j, qb):
        wrap = j == n - 1
        return jnp.where(wrap, 0, j + 1), jnp.where(wrap, qb + 1, qb)

    zero_i = jnp.int32(0)
    item0 = (zero_i, zero_i)
    item1 = advance(*item0)
    scores(*item0)
    softmax(*item0)
    scores(*item1)

    def iteration(t, carry):
        ja, qa, jm, qm, js, qs = carry
        accumulate(ja)
        softmax(jm, qm)
        scores(js, qs)

        @pl.when(ja == n - 1)
        def _():
            finalize(qa)

        return (jm, qm, js, qs) + advance(js, qs)

    item2 = advance(*item1)
    lax.fori_loop(2, nq * n, iteration, item0 + item1 + item2)
    accumulate(n - 2)
    softmax(n - 1, nq - 1)
    accumulate(n - 1)
    finalize(nq - 1)


def _attention(qt, kr, vt, lam_vecs, diff_gain, n_lat):
    b, t, _ = kr.shape
    tq = ATT_TQ
    hw = 2 * DIFF_HEAD_DIM
    nq = n_lat // tq
    assert t % ATT_TK == 0 and t // ATT_TK >= 2
    return pl.pallas_call(
        functools.partial(_attn_kernel, t, tq),
        grid=(b, DIFF_HEADS),
        in_specs=[pl.BlockSpec((1, hw, n_lat), lambda bi, h: (bi, h, 0)),
                  pl.BlockSpec((1, t, hw), lambda bi, h: (bi, 0, h)),
                  pl.BlockSpec((1, V_ROWS, t), lambda bi, h: (bi, h, 0)),
                  pl.BlockSpec(lam_vecs.shape, lambda bi, h: (0, 0)),
                  pl.BlockSpec((1, hw), lambda bi, h: (0, 0))],
        out_specs=pl.BlockSpec((1, n_lat, hw), lambda bi, h: (bi, 0, h)),
        out_shape=jax.ShapeDtypeStruct((b, n_lat, DIFF_HEADS * hw), BF16),
        scratch_shapes=[pltpu.VMEM((ATT_TK, 2 * tq), F32),
                        pltpu.VMEM((SUBLANES, 2 * tq), F32),
                        pltpu.VMEM((ATT_TK, 2 * tq), BF16),
                        pltpu.VMEM((1, 2 * tq), F32),
                        pltpu.VMEM((1, 2 * tq), F32),
                        pltpu.VMEM((V_ROWS, 2 * tq), F32)],
        compiler_params=_cparams(("parallel", "arbitrary")),
        name="attn",
    )(qt, kr, vt, lam_vecs, diff_gain)


def _merge_kernel(d, hf_ref, hb_ref, mo_ref, ga_ref, gb_ref, hB_ref, x_ref, mod_ref, gain_ref,
                  wa_ref, wb_ref, wo_ref, o_ref):
    gain = gain_ref[...]
    parts = []
    for h in range(MLSTM_HEADS):
        sl = slice(h * MLSTM_DV, (h + 1) * MLSTM_DV)
        hs = hf_ref[0, :, sl].astype(F32) + hb_ref[0, :, sl].astype(F32)
        ms = jnp.mean(hs * hs, axis=-1, keepdims=True)
        hn = hs * lax.rsqrt(ms + RMS_EPS) * gain[:, sl]
        parts.append((_sigmoid(mo_ref[0, :, sl].astype(F32)) * hn).astype(BF16))
    h_a = jnp.concatenate(parts, axis=1)
    ya = jnp.dot(h_a, wa_ref[...], preferred_element_type=F32)
    yb = jnp.dot(hB_ref[0], wb_ref[...], preferred_element_type=F32)
    y = _sigmoid(ga_ref[0].astype(F32)) * ya + _sigmoid(gb_ref[0].astype(F32)) * yb
    yo = jnp.dot(y.astype(BF16), wo_ref[...], preferred_element_type=F32)
    g1 = mod_ref[0][:, 2 * d:3 * d]
    o_ref[0] = x_ref[0] + g1 * yo


def _merge(hf, hb, p, h_b, x, mods3, mlstm_gain, wa, wb, wo, mo_region, ga_region, gb_region):
    b, s, d = x.shape
    tm = LAT_TILE
    wspec = lambda shape: pl.BlockSpec(shape, lambda bi, i: (0, 0), pipeline_mode=pl.Buffered(1))
    tok = lambda region: pl.BlockSpec((1, tm, d), lambda bi, i, r=region: (bi, i, r))
    return pl.pallas_call(
        functools.partial(_merge_kernel, d),
        grid=(b, s // tm),
        in_specs=[tok(0), tok(0), tok(mo_region), tok(ga_region), tok(gb_region), tok(0), tok(0),
                  pl.BlockSpec((1, 1, N_MOD * d), lambda bi, i: (bi, 0, 0)),
                  pl.BlockSpec((1, d), lambda bi, i: (0, 0)),
                  wspec(wa.shape), wspec(wb.shape), wspec(wo.shape)],
        out_specs=pl.BlockSpec((1, tm, d), lambda bi, i: (bi, i, 0)),
        out_shape=jax.ShapeDtypeStruct((b, s, d), F32),
        compiler_params=_cparams(("parallel", "arbitrary")),
        name="merge",
    )(hf, hb, p, p, p, h_b, x, mods3, mlstm_gain, wa, wb, wo)


def _ffn_kernel(d, hidden, x_ref, mod_ref, g_ref, wi_ref, wo_ref, o_ref):
    x1 = x_ref[0]
    mod = mod_ref[0]
    xn = _rms_mod(x1, g_ref[...], mod[:, 3 * d:4 * d], mod[:, 4 * d:5 * d]).astype(BF16)
    a = jnp.dot(xn, wi_ref[:, :hidden], preferred_element_type=F32)
    bb = jnp.dot(xn, wi_ref[:, hidden:], preferred_element_type=F32)
    hmid = (_silu(a) * bb).astype(BF16)
    y = jnp.dot(hmid, wo_ref[...], preferred_element_type=F32)
    o_ref[0] = x1 + mod[:, 5 * d:6 * d] * y


def _ffn(x1, mods3, norm2, wi, wo):
    b, s, d = x1.shape
    hidden = wo.shape[0]
    tm = LAT_TILE
    wspec = lambda shape: pl.BlockSpec(shape, lambda bi, i: (0, 0), pipeline_mode=pl.Buffered(1))
    return pl.pallas_call(
        functools.partial(_ffn_kernel, d, hidden),
        grid=(b, s // tm),
        in_specs=[pl.BlockSpec((1, tm, d), lambda bi, i: (bi, i, 0)),
                  pl.BlockSpec((1, 1, N_MOD * d), lambda bi, i: (bi, 0, 0)),
                  pl.BlockSpec((1, d), lambda bi, i: (0, 0)),
                  wspec(wi.shape), wspec(wo.shape)],
        out_specs=pl.BlockSpec((1, tm, d), lambda bi, i: (bi, i, 0)),
        out_shape=jax.ShapeDtypeStruct((b, s, d), F32),
        compiler_params=_cparams(("parallel", "arbitrary")),
        name="ffn",
    )(x1, mods3, norm2, wi, wo)


def _diff_lane_perm():
    l = np.arange(2 * DIFF_HEAD_DIM)
    half, r = l // DIFF_HEAD_DIM, l % DIFF_HEAD_DIM
    qsel, axis, f = r // 32, (r % 32) // ROPE_FREQS, r % ROPE_FREQS
    return qsel * DIFF_HEAD_DIM + axis * 2 * ROPE_FREQS + half * ROPE_FREQS + f, qsel, axis, f, half


def _rope_tables(n_lat, n_ctx):
    _, _, axis, f, half = _diff_lane_perm()
    pos = np.arange(n_lat)
    row, col = pos // GRID_W, pos % GRID_W
    inv = (ROPE_BASE ** (-np.arange(ROPE_FREQS, dtype=np.float32) / ROPE_FREQS)).astype(np.float32)
    ang = (np.where(axis[None, :] == 0, row[:, None], col[:, None]).astype(np.float32) * inv[f][None, :])
    cos = np.concatenate([np.cos(ang), np.ones((n_ctx, LANES))], axis=0)
    sin = np.concatenate([np.sin(ang) * np.where(half == 0, -1.0, 1.0)[None, :], np.zeros((n_ctx, LANES))], axis=0)
    return jnp.asarray(cos, F32), jnp.asarray(sin, F32)


def kernel(x, c, ctx, c_ctx, w_mod, b_mod, norm1, norm2, w_in, b_gate, conv_w, conv_b, mlstm_norm, q_norm, k_norm,
           lam_vecs, diff_norm, w_branch_a, w_branch_b, w_out, w_ffn_in, w_ffn_out):
    b, s, d = x.shape
    n_ctx = ctx.shape[1]
    assert w_mod.shape[0] == 1, "single-layer block"
    assert s % CHUNK == 0 and n_ctx % CHUNK == 0 and s % ATT_TQ == 0 and s % LAT_TILE == 0
    qk_w = MLSTM_HEADS * MLSTM_DQK
    v_w = MLSTM_HEADS * MLSTM_DV
    da_w = DIFF_HEADS * 2 * DIFF_HEAD_DIM
    n_gates = 4 * MLSTM_HEADS
    sizes = (qk_w, qk_w, v_w, v_w, n_gates, da_w, da_w, da_w, d, d)
    offs = np.concatenate([[0], np.cumsum(sizes)])
    seg = lambda i: w_in[0][:, offs[i]:offs[i + 1]]

    perm, qsel, _, _, _ = _diff_lane_perm()
    head_perm = (np.arange(DIFF_HEADS)[:, None] * 2 * DIFF_HEAD_DIM + perm[None, :]).reshape(-1)
    w_main = w_in[0][:, offs[0]:offs[4]].astype(BF16)
    w_gab = w_in[0][:, offs[8]:offs[10]].astype(BF16)
    segb = lambda i: seg(i).astype(BF16)
    w_att = jnp.concatenate([segb(5)[:, head_perm], segb(6)[:, head_perm], segb(7)], axis=1)
    R_V, R_O, R_GA, R_GB = range(4)
    wg_f32 = jnp.pad(seg(4), ((0, 0), (0, LANES - n_gates)))
    wg_hi = wg_f32.astype(BF16)
    wg = jnp.concatenate([wg_hi, (wg_f32 - wg_hi.astype(F32)).astype(BF16)], axis=1)
    gate_bias = jnp.pad(b_gate[0], (0, LANES - n_gates)).reshape(1, LANES)
    conv_w8 = jnp.pad(conv_w[0], ((0, SUBLANES - CONV_K), (0, 0)))
    gq = q_norm[0][perm % DIFF_HEAD_DIM].reshape(1, LANES)
    gk = k_norm[0][perm % DIFF_HEAD_DIM].reshape(1, LANES)
    unit = np.concatenate([qsel, 2 + qsel])
    grp = jnp.asarray(unit[:, None] == unit[None, :], BF16)
    cos_t, sin_t = _rope_tables(s, n_ctx)

    cc = jnp.concatenate([c, c_ctx[None, :], jnp.zeros((SUBLANES - b - 1, d), F32)], axis=0)
    mods = _modulation(cc, w_mod[0], b_mod[0])
    mods3 = mods.reshape(SUBLANES, 1, N_MOD * d)

    qk, pm, qt, kr, vt, gates, gates_t = _projection(x, ctx, mods3, norm1, (w_main, w_gab, w_att), wg, gate_bias,
                                                     cos_t, sin_t, gq, gk, grp, conv_w8, conv_b)
    hf, hb = _mlstm(qk, pm, gates, gates_t, s, n_ctx, R_V)
    h_b = _attention(qt, kr, vt, lam_vecs[0], diff_norm, s)
    x1 = _merge(hf, hb, pm, h_b, x, mods3, mlstm_norm, w_branch_a[0].astype(BF16), w_branch_b[0].astype(BF16),
                w_out[0].astype(BF16), R_O, R_GA, R_GB)
    return _ffn(x1, mods3, norm2, w_ffn_in[0].astype(BF16), w_ffn_out[0].astype(BF16))
```

```python
import functools
import math

import jax
import jax.numpy as jnp
import numpy as np
from jax import lax
from jax.experimental import pallas as pl
from jax.experimental.pallas import tpu as pltpu

F32 = jnp.float32
BF16 = jnp.bfloat16
HIGHEST = lax.Precision.HIGHEST

GRID_W = 64
N_MOD = 6
MLSTM_HEADS = 4
MLSTM_DQK = 128
MLSTM_DV = 256
CONV_K = 5
DIFF_HEADS = 8
DIFF_HEAD_DIM = 64
ROPE_BASE = 10000.0
ROPE_FREQS = DIFF_HEAD_DIM // 4
RMS_EPS = 1e-6
LAM_INIT_L0 = 0.8 - 0.6 * math.exp(-0.3 * 0)
LOG2_E = math.log2(math.e)
Q_SCALE_LOG2 = DIFF_HEAD_DIM ** -0.5 * LOG2_E
LANES = 128
SUBLANES = 8
BF16_SUBLANES = 16
V_ROWS = 2 * DIFF_HEAD_DIM + BF16_SUBLANES
VMEM_LIMIT = 56 * 1024 * 1024

TOK_TILE = 256
LAT_TILE = 512
CHUNK = 256
MLSTM_BATCH = 4
ATT_TQ = 1024
ATT_TK = 2816


def _cparams(sem):
    return pltpu.CompilerParams(dimension_semantics=sem, vmem_limit_bytes=VMEM_LIMIT)


def _sigmoid(x):
    return 0.5 + 0.5 * jnp.tanh(0.5 * x)


def _silu(x):
    h = 0.5 * x
    return h + h * jnp.tanh(h)


def _log_sigmoid(x):
    return jnp.minimum(x, 0.0) - jnp.log(1.0 + jnp.exp(-jnp.abs(x)))


def _mod_kernel(c_ref, w_ref, b_ref, o_ref):
    a = _silu(c_ref[...])
    o_ref[...] = jnp.dot(a, w_ref[...], preferred_element_type=F32, precision=HIGHEST) + b_ref[...]


def _modulation(cc, w_mod, b_mod):
    rows, d = cc.shape
    n = w_mod.shape[1]
    tn = 1024
    return pl.pallas_call(
        _mod_kernel,
        grid=(n // tn,),
        in_specs=[pl.BlockSpec((rows, d), lambda j: (0, 0)),
                  pl.BlockSpec((d, tn), lambda j: (0, j)),
                  pl.BlockSpec((1, tn), lambda j: (0, j))],
        out_specs=pl.BlockSpec((rows, tn), lambda j: (0, j)),
        out_shape=jax.ShapeDtypeStruct((rows, n), F32),
        compiler_params=_cparams(("arbitrary",)),
        name="mod",
    )(cc, w_mod, b_mod.reshape(1, n))


def _rms_mod(xf, g, shift, scale):
    ms = jnp.mean(xf * xf, axis=-1, keepdims=True)
    return xf * lax.rsqrt(ms + RMS_EPS) * g * (1.0 + scale) + shift


def _proj_kernel(n_lat_tiles, n_tiles, d, x_ref, xprev_ref, xnext_ref, ctx_ref, mod_ref, g_ref,
                 wm_ref, wgab_ref, watt_ref, wg_ref, gbias_ref,
                 cos_ref, sin_ref, gq_ref, gk_ref, grp_ref, cw_ref, cb_ref,
                 qk_ref, pm_ref, qt_ref, kr_ref, vt_ref, gates_ref, gates_t_ref,
                 ext_ref, yq_ref, yk_ref, yv_ref, gpre_ref, xb_ref, xlo_ref):
    i = pl.program_id(0)
    tm = x_ref.shape[1]

    @pl.when(i == 0)
    def _():
        ext_ref[...] = jnp.zeros_like(ext_ref)
        yq_ref[...] = jnp.zeros_like(yq_ref)
        yk_ref[...] = jnp.zeros_like(yk_ref)
        yv_ref[...] = jnp.zeros_like(yv_ref)
        gpre_ref[...] = jnp.zeros_like(gpre_ref)

    ti = jnp.minimum(i, pl.num_programs(0) - 2) % n_tiles
    xf = jnp.where(ti < n_lat_tiles, x_ref[0], ctx_ref[0])
    mod = mod_ref[0]
    gain1 = g_ref[...]
    xn = _rms_mod(xf, gain1, mod[:, 0:d], mod[:, d:2 * d])
    xb = xn.astype(BF16)
    halo = jnp.concatenate([xprev_ref[0], xnext_ref[0]], axis=0)
    xb_ref[0:tm, :] = xb
    xb_ref[tm:, :] = _rms_mod(halo, gain1, mod[:, 0:d], mod[:, d:2 * d]).astype(BF16)
    xlo_ref[...] = (xn - xb.astype(F32)).astype(BF16)

    def project(w_ref, region):
        return jnp.dot(xb_ref[0:tm, :], w_ref[:, region * d:(region + 1) * d], preferred_element_type=F32)

    taps = cw_ref[...]
    conv_bias = cb_ref[...]
    n_q_cols = d // 2

    w2 = 2 * LANES

    def conv_columns(c_lo, c_hi):
        for c0 in range(c_lo, c_hi, w2):
            acc = jnp.zeros((tm, w2), F32) + conv_bias[:, c0:c0 + w2]
            for k in range(CONV_K):
                r0 = SUBLANES - CONV_K // 2 + k
                acc = acc + ext_ref[r0:r0 + tm, c0:c0 + w2] * taps[k:k + 1, c0:c0 + w2]
            out = _silu(acc)
            if c0 < n_q_cols:
                out = out * MLSTM_DQK ** -0.5
            qk_ref[0, :, c0:c0 + w2] = out.astype(BF16)

    cos = cos_ref[...]
    sin = sin_ref[...]
    grp = grp_ref[...]
    gq = gq_ref[...]
    gk = gk_ref[...]
    hw = 2 * DIFF_HEAD_DIM

    def scaled_sumsq(x2):
        ss = jnp.dot((x2 * x2).astype(BF16), grp, preferred_element_type=F32)
        return lax.rsqrt(ss * (1.0 / DIFF_HEAD_DIM) + RMS_EPS)

    def rotated(y):
        return y * cos + pltpu.roll(y, DIFF_HEAD_DIM, axis=1) * sin

    ones_rows = jnp.ones((V_ROWS - hw, tm), BF16)

    def attention_pair(pair):
        sl2 = slice(pair * 2 * hw, (pair + 1) * 2 * hw)
        yq2 = yq_ref[:, sl2]
        yk2 = yk_ref[:, sl2]
        qn = yq2 * scaled_sumsq(yq2)
        kn = yk2 * scaled_sumsq(yk2)
        for u in range(2):
            h = 2 * pair + u
            sl = slice(h * hw, (h + 1) * hw)
            su = slice(u * hw, (u + 1) * hw)
            qt_ref[0, sl, :] = (rotated(qn[:, su] * gq) * Q_SCALE_LOG2).T.astype(BF16)
            kr_ref[0, :, sl] = rotated(kn[:, su] * gk).astype(BF16)
            vt_ref[0, h * V_ROWS:h * V_ROWS + hw, :] = yv_ref[:, sl].T.astype(BF16)
            vt_ref[0, h * V_ROWS + hw:(h + 1) * V_ROWS, :] = ones_rows

    def gate_epilogue():
        g = gpre_ref[...]
        lf = _log_sigmoid(g)
        lf_hi = lf.astype(BF16)
        lf_lo = (lf - lf_hi.astype(F32)).astype(BF16)
        lf2 = jnp.concatenate([lf_hi, lf_lo], axis=1)
        rows = lax.broadcasted_iota(jnp.int32, (tm, tm), 0)
        cols = lax.broadcasted_iota(jnp.int32, (tm, tm), 1)
        lower = jnp.where(cols <= rows, 1.0, 0.0).astype(BF16)
        upper = jnp.where(cols >= rows, 1.0, 0.0).astype(BF16)
        cum_f2 = jnp.dot(lower, lf2, preferred_element_type=F32)
        cum_b2 = jnp.dot(upper, lf2, preferred_element_type=F32)
        cum_f = cum_f2[:, :LANES] + cum_f2[:, LANES:]
        cum_b = cum_b2[:, :LANES] + cum_b2[:, LANES:]
        lane = lax.broadcasted_iota(jnp.int32, (tm, LANES), 1)
        cum = jnp.where(lane < 2 * MLSTM_HEADS, cum_f, cum_b)
        a = g - pltpu.roll(cum, LANES - MLSTM_HEADS, axis=1)
        a = a * LOG2_E
        gates_ref[0] = jnp.where((lane % (2 * MLSTM_HEADS)) < MLSTM_HEADS, a, cum * LOG2_E)
        gates_t_ref[0, 0] = a.T[0:4 * MLSTM_HEADS, :]

    for r, (w_ref, region) in enumerate(((wm_ref, 1), (wm_ref, 2), (wgab_ref, 0), (wgab_ref, 1))):
        pm_ref[0, :, r * d:(r + 1) * d] = project(w_ref, region).astype(BF16)
        attention_pair(r)
    yq_ref[...] = project(watt_ref, 0)
    conv_columns(0, d // 2)
    yk_ref[...] = project(watt_ref, 1)
    conv_columns(d // 2, d)
    yv_ref[...] = project(watt_ref, 2)
    gate_epilogue()
    y_qk = jnp.dot(xb_ref[...], wm_ref[:, 0:d], preferred_element_type=F32)
    has_prev = jnp.logical_and(ti > 0, ti < n_lat_tiles)
    has_next = ti < n_lat_tiles - 1
    ext_ref[0:SUBLANES, :] = jnp.where(has_prev, y_qk[tm:tm + SUBLANES], 0.0)
    ext_ref[SUBLANES:SUBLANES + tm, :] = y_qk[0:tm]
    ext_ref[SUBLANES + tm:, :] = jnp.where(has_next, y_qk[tm + SUBLANES:], 0.0)
    g2 = (jnp.dot(xb_ref[0:tm, :], wg_ref[...], preferred_element_type=F32)
          + jnp.dot(xlo_ref[...], wg_ref[...], preferred_element_type=F32))
    gpre_ref[...] = g2[:, :LANES] + g2[:, LANES:] + gbias_ref[...]


def _projection(x, ctx, mods3, norm1, weights, wg, gate_bias, cos_t, sin_t, gq, gk, grp, conv_w8, conv_b):
    b, s, d = x.shape
    n_ctx = ctx.shape[1]
    tm = TOK_TILE
    assert tm == CHUNK
    assert n_ctx == tm
    n_lat_tiles = s // tm
    n_tiles = (s + n_ctx) // tm
    t = s + n_ctx
    w_main, w_gab, w_att = weights
    resident = lambda shape: pl.BlockSpec(shape, lambda i: (0, 0), pipeline_mode=pl.Buffered(1))
    da_w = DIFF_HEADS * 2 * DIFF_HEAD_DIM
    halo_stride = tm // SUBLANES
    last_halo = s // SUBLANES - 1
    const = lambda shape: pl.BlockSpec(shape, lambda i: (0,) * len(shape))
    n_pairs = b * n_tiles

    def cur(i):
        j = jnp.minimum(i, n_pairs - 1)
        return j // n_tiles, j % n_tiles

    def late(i):
        j = jnp.maximum(i - 1, 0)
        return j // n_tiles, j % n_tiles

    tok = lambda width: pl.BlockSpec((1, tm, width), lambda i: (late(i)[0], late(i)[1], 0))
    tok_t = lambda rows: pl.BlockSpec((1, rows, tm), lambda i: (late(i)[0], 0, late(i)[1]))
    n_ctx_tiles = n_ctx // tm
    return pl.pallas_call(
        functools.partial(_proj_kernel, n_lat_tiles, n_tiles, d),
        grid=(n_pairs + 1,),
        in_specs=[
            pl.BlockSpec((1, tm, d), lambda i: (cur(i)[0], jnp.minimum(cur(i)[1], n_lat_tiles - 1), 0)),
            pl.BlockSpec((1, SUBLANES, d),
                         lambda i: (cur(i)[0], jnp.clip(cur(i)[1] * halo_stride - 1, 0, last_halo), 0)),
            pl.BlockSpec((1, SUBLANES, d),
                         lambda i: (cur(i)[0], jnp.clip((cur(i)[1] + 1) * halo_stride, 0, last_halo), 0)),
            pl.BlockSpec((1, tm, d),
                         lambda i: (cur(i)[0], jnp.clip(cur(i)[1] - n_lat_tiles, 0, n_ctx_tiles - 1), 0)),
            pl.BlockSpec((1, 1, N_MOD * d),
                         lambda i: (jnp.where(cur(i)[1] < n_lat_tiles, cur(i)[0], b), 0, 0)),
            const((1, d)),
            resident(w_main.shape), resident(w_gab.shape), resident(w_att.shape),
            const((d, 2 * LANES)), const((1, LANES)),
            pl.BlockSpec((tm, LANES), lambda i: (late(i)[1], 0)),
            pl.BlockSpec((tm, LANES), lambda i: (late(i)[1], 0)),
            const((1, LANES)), const((1, LANES)), const((2 * LANES, 2 * LANES)),
            const((SUBLANES, d)), const((1, d)),
        ],
        out_specs=[tok(d), pl.BlockSpec((1, tm, 4 * d), lambda i: (cur(i)[0], cur(i)[1], 0)),
                   tok_t(da_w), tok(da_w), tok_t(DIFF_HEADS * V_ROWS), tok(LANES),
                   pl.BlockSpec((1, 1, 4 * MLSTM_HEADS, tm), lambda i: (late(i)[0], late(i)[1], 0, 0))],
        out_shape=[jax.ShapeDtypeStruct((b, t, d), BF16),
                   jax.ShapeDtypeStruct((b, t, 4 * d), BF16),
                   jax.ShapeDtypeStruct((b, da_w, t), BF16),
                   jax.ShapeDtypeStruct((b, t, da_w), BF16),
                   jax.ShapeDtypeStruct((b, DIFF_HEADS * V_ROWS, t), BF16),
                   jax.ShapeDtypeStruct((b, t, LANES), F32),
                   jax.ShapeDtypeStruct((b, n_tiles, 4 * MLSTM_HEADS, tm), F32)],
        scratch_shapes=[pltpu.VMEM((tm + 2 * SUBLANES, d), F32),
                        pltpu.VMEM((tm, da_w), F32), pltpu.VMEM((tm, da_w), F32), pltpu.VMEM((tm, da_w), F32),
                        pltpu.VMEM((tm, LANES), F32),
                        pltpu.VMEM((tm + 2 * SUBLANES, d), BF16),
                        pltpu.VMEM((tm, d), BF16)],
        compiler_params=_cparams(("arbitrary",)),
        name="proj",
    )(x, x, x, ctx, mods3, norm1, w_main, w_gab, w_att, wg, gate_bias, cos_t, sin_t, gq, gk, grp, conv_w8, conv_b)


def _mlstm_direction(reverse, q, k, v_aug, a_col, a_row, lf_cum_col, lf_total, c_ref, m_ref, hd):
    l = q.shape[0]
    dv = v_aug.shape[1] - LANES

    def wide(x, n_tiles):
        return jnp.concatenate([x] * n_tiles, axis=1)

    m_prev = m_ref[hd:hd + 1, :]
    c_aug = c_ref[hd]
    lf_cum_rep = jnp.broadcast_to(lf_cum_col, (l, LANES))
    a_rep = jnp.broadcast_to(a_col, (l, LANES))
    rows = lax.broadcasted_iota(jnp.int32, (l, l), 0)
    cols = lax.broadcasted_iota(jnp.int32, (l, l), 1)
    visible = (cols >= rows) if reverse else (cols <= rows)
    a_mat = jnp.where(visible, a_row, -jnp.inf)
    m_rep = jnp.maximum(jnp.broadcast_to(jnp.max(a_mat, axis=-1, keepdims=True), (l, LANES)), m_prev)
    decay = jnp.exp2(a_mat - wide(m_rep, l // LANES))
    s = lax.dot_general(q, k, (((1,), (1,)), ((), ())), preferred_element_type=F32)
    pmat = (s * decay).astype(BF16)
    n_wide = v_aug.shape[1] // LANES
    intra = jnp.dot(pmat, v_aug, preferred_element_type=F32)
    inter = jnp.dot(q, c_aug.astype(BF16), preferred_element_type=F32)
    numden = intra + wide(jnp.exp2(m_prev - m_rep), n_wide) * inter
    den = numden[:, dv:]
    floor = jnp.exp2(-(lf_cum_rep + m_rep))
    h = numden[:, :dv] * wide(1.0 / jnp.maximum(jnp.abs(den), floor), dv // LANES)
    m_last = jnp.maximum(jnp.max(a_rep, axis=0, keepdims=True), m_prev)
    w = jnp.exp2(a_rep - m_last)
    kw = (w * k.astype(F32)).astype(BF16)
    kv = lax.dot_general(kw, v_aug, (((0,), (0,)), ((), ())), preferred_element_type=F32)
    c_ref[hd] = wide(jnp.exp2(m_prev - m_last), n_wide) * c_aug + kv
    m_ref[hd:hd + 1, :] = lf_total + m_last
    return h


def _mlstm_kernel(qf_ref, kf_ref, vf_ref, gf_ref, gtf_ref, qb_ref, kb_ref, vb_ref, gb_ref, gtb_ref,
                  hf_ref, hb_ref, c_ref, m_ref):
    s_idx = pl.program_id(1)

    @pl.when(s_idx == 0)
    def _():
        c_ref[...] = jnp.zeros_like(c_ref)
        m_ref[...] = jnp.zeros_like(m_ref)

    nb, l = qf_ref.shape[0], qf_ref.shape[1]
    nh = MLSTM_HEADS
    ones_col = jnp.ones((l, LANES), BF16)
    for bb in range(nb):
        for reverse, q_ref, k_ref, v_ref, g_ref, gt_ref, h_ref in (
                (False, qf_ref, kf_ref, vf_ref, gf_ref, gtf_ref, hf_ref),
                (True, qb_ref, kb_ref, vb_ref, gb_ref, gtb_ref, hb_ref)):
            g = g_ref[bb]
            g_t = gt_ref[bb, 0]
            base = 2 * nh if reverse else 0
            for h in range(nh):
                ca, cc = base + h, base + nh + h
                hd = bb * 2 * nh + (nh if reverse else 0) + h
                lf_cum_col = g[:, cc:cc + 1]
                lf_total = g[0:1, cc:cc + 1] if reverse else g[l - 1:l, cc:cc + 1]
                q = q_ref[bb, :, h * MLSTM_DQK:(h + 1) * MLSTM_DQK]
                k = k_ref[bb, :, h * MLSTM_DQK:(h + 1) * MLSTM_DQK]
                v_aug = jnp.concatenate([v_ref[bb, :, h * MLSTM_DV:(h + 1) * MLSTM_DV], ones_col], axis=1)
                hval = _mlstm_direction(reverse, q, k, v_aug, g[:, ca:ca + 1], g_t[ca:ca + 1, :],
                                        lf_cum_col, lf_total, c_ref, m_ref, hd)
                h_ref[bb, :, h * MLSTM_DV:(h + 1) * MLSTM_DV] = hval.astype(h_ref.dtype)


def _mlstm(qk, pm, gates, gates_t, n_lat, n_ctx, v_region):
    b, t, _ = qk.shape
    l = CHUNK
    n_lat_c = n_lat // l
    n_ctx_c = n_ctx // l
    n_steps = n_lat_c + n_ctx_c
    qw = MLSTM_HEADS * MLSTM_DQK
    vw = MLSTM_HEADS * MLSTM_DV

    def fwd(si):
        return jnp.where(si < n_ctx_c, n_lat_c + si, si - n_ctx_c)

    def bwd(si):
        return jnp.where(si < n_ctx_c, n_steps - 1 - si, n_steps - 1 - si)

    nb = MLSTM_BATCH
    assert b % nb == 0
    in_specs = []
    for order in (fwd, bwd):
        in_specs += [
            pl.BlockSpec((nb, l, qw), lambda bi, si, o=order: (bi, o(si), 0)),
            pl.BlockSpec((nb, l, qw), lambda bi, si, o=order: (bi, o(si), 1)),
            pl.BlockSpec((nb, l, vw), lambda bi, si, o=order: (bi, o(si), v_region)),
            pl.BlockSpec((nb, l, LANES), lambda bi, si, o=order: (bi, o(si), 0)),
            pl.BlockSpec((nb, 1, 4 * MLSTM_HEADS, l), lambda bi, si, o=order: (bi, o(si), 0, 0)),
        ]
    out_specs = [pl.BlockSpec((nb, l, vw), lambda bi, si: (bi, fwd(si), 0)),
                 pl.BlockSpec((nb, l, vw), lambda bi, si: (bi, bwd(si), 0))]
    n_scans = nb * 2 * MLSTM_HEADS
    return pl.pallas_call(
        _mlstm_kernel,
        grid=(b // nb, n_steps),
        in_specs=in_specs,
        out_specs=out_specs,
        out_shape=[jax.ShapeDtypeStruct((b, t, vw), BF16)] * 2,
        scratch_shapes=[pltpu.VMEM((n_scans, MLSTM_DQK, MLSTM_DV + LANES), F32),
                        pltpu.VMEM((n_scans, LANES), F32)],
        compiler_params=_cparams(("parallel", "arbitrary")),
        name="mlstm",
    )(qk, qk, pm, gates, gates_t, qk, qk, pm, gates, gates_t)


def _attn_kernel(n_keys, tq, qt_ref, k_ref, vt_ref, lam_ref, gain_ref, o_ref,
                 s_ref, pm_ref, e_ref, al_ref, m_ref, acc_ref):
    tk = ATT_TK
    n = n_keys // tk
    nq = qt_ref.shape[2] // tq
    width = 2 * tq
    row = lax.broadcasted_iota(jnp.int32, (qt_ref.shape[1], tq), 0)
    first = (row % DIFF_HEAD_DIM) < (DIFF_HEAD_DIM // 2)

    acc_ref[...] = jnp.zeros_like(acc_ref)

    def scores(j, qb):
        off = pl.multiple_of(j * tk, tk)
        qt = qt_ref[0, :, pl.ds(pl.multiple_of(qb * tq, tq), tq)]
        zero = jnp.zeros_like(qt)
        qq = jnp.concatenate([jnp.where(first, qt, zero), jnp.where(first, zero, qt)], axis=1)
        s = jnp.dot(k_ref[0, pl.ds(off, tk), :], qq, preferred_element_type=F32)
        s_ref[...] = s
        pm_ref[...] = jnp.max(s.reshape(tk // SUBLANES, SUBLANES, width), axis=0)

    def softmax(j, qb):
        m_old = jnp.where(j == 0, -jnp.inf, m_ref[...])
        m_new = jnp.maximum(m_old, jnp.max(pm_ref[...], axis=0, keepdims=True))
        alpha = jnp.exp2(m_old - m_new)
        e = jnp.exp2(s_ref[...] - m_new)
        e_ref[...] = e.astype(BF16)
        al_ref[...] = alpha
        m_ref[...] = m_new

    def accumulate(j):
        off = pl.multiple_of(j * tk, tk)
        pv = jnp.dot(vt_ref[0, :, pl.ds(off, tk)], e_ref[...], preferred_element_type=F32)
        acc_ref[...] = al_ref[...] * acc_ref[...] + pv

    def finalize(qb):
        lv = lam_ref[...]
        lam = (jnp.exp(jnp.sum(lv[0:1] * lv[1:2], axis=-1, keepdims=True))
               - jnp.exp(jnp.sum(lv[2:3] * lv[3:4], axis=-1, keepdims=True)) + LAM_INIT_L0)
        hw = o_ref.shape[2]
        inv_l = 1.0 / acc_ref[hw:hw + 1, :]
        o_all = acc_ref[0:hw, :] * inv_l
        o_t = o_all[:, :tq] - lam * o_all[:, tq:]
        ms = jnp.mean(o_t * o_t, axis=0, keepdims=True)
        out = (o_t * lax.rsqrt(ms + RMS_EPS)).T * (gain_ref[...] * (1.0 - LAM_INIT_L0))
        o_ref[0, pl.ds(pl.multiple_of(qb * tq, tq), tq), :] = out.astype(o_ref.dtype)

    def advance(j, qb):
        wrap = j == n - 1
        return jnp.where(wrap, 0, j + 1), jnp.where(wrap, qb + 1, qb)

    zero_i = jnp.int32(0)
    item0 = (zero_i, zero_i)
    item1 = advance(*item0)
    scores(*item0)
    softmax(*item0)
    scores(*item1)

    def iteration(t, carry):
        ja, qa, jm, qm, js, qs = carry
        accumulate(ja)
        softmax(jm, qm)
        scores(js, qs)

        @pl.when(ja == n - 1)
        def _():
            finalize(qa)

        return (jm, qm, js, qs) + advance(js, qs)

    item2 = advance(*item1)
    lax.fori_loop(2, nq * n, iteration, item0 + item1 + item2)
    accumulate(n - 2)
    softmax(n - 1, nq - 1)
    accumulate(n - 1)
    finalize(nq - 1)


def _attention(qt, kr, vt, lam_vecs, diff_gain, n_lat):
    b, t, _ = kr.shape
    tq = ATT_TQ
    hw = 2 * DIFF_HEAD_DIM
    nq = n_lat // tq
    assert t % ATT_TK == 0 and t // ATT_TK >= 2
    return pl.pallas_call(
        functools.partial(_attn_kernel, t, tq),
        grid=(b, DIFF_HEADS),
        in_specs=[pl.BlockSpec((1, hw, n_lat), lambda bi, h: (bi, h, 0)),
                  pl.BlockSpec((1, t, hw), lambda bi, h: (bi, 0, h)),
                  pl.BlockSpec((1, V_ROWS, t), lambda bi, h: (bi, h, 0)),
                  pl.BlockSpec(lam_vecs.shape, lambda bi, h: (0, 0)),
                  pl.BlockSpec((1, hw), lambda bi, h: (0, 0))],
        out_specs=pl.BlockSpec((1, n_lat, hw), lambda bi, h: (bi, 0, h)),
        out_shape=jax.ShapeDtypeStruct((b, n_lat, DIFF_HEADS * hw), BF16),
        scratch_shapes=[pltpu.VMEM((ATT_TK, 2 * tq), F32),
                        pltpu.VMEM((SUBLANES, 2 * tq), F32),
                        pltpu.VMEM((ATT_TK, 2 * tq), BF16),
                        pltpu.VMEM((1, 2 * tq), F32),
                        pltpu.VMEM((1, 2 * tq), F32),
                        pltpu.VMEM((V_ROWS, 2 * tq), F32)],
        compiler_params=_cparams(("parallel", "arbitrary")),
        name="attn",
    )(qt, kr, vt, lam_vecs, diff_gain)


def _merge_kernel(d, hf_ref, hb_ref, mo_ref, ga_ref, gb_ref, hB_ref, x_ref, mod_ref, gain_ref,
                  wa_ref, wb_ref, wo_ref, o_ref):
    gain = gain_ref[...]
    parts = []
    for h in range(MLSTM_HEADS):
        sl = slice(h * MLSTM_DV, (h + 1) * MLSTM_DV)
        hs = hf_ref[0, :, sl].astype(F32) + hb_ref[0, :, sl].astype(F32)
        ms = jnp.mean(hs * hs, axis=-1, keepdims=True)
        hn = hs * lax.rsqrt(ms + RMS_EPS) * gain[:, sl]
        parts.append((_sigmoid(mo_ref[0, :, sl].astype(F32)) * hn).astype(BF16))
    h_a = jnp.concatenate(parts, axis=1)
    ya = jnp.dot(h_a, wa_ref[...], preferred_element_type=F32)
    yb = jnp.dot(hB_ref[0], wb_ref[...], preferred_element_type=F32)
    y = _sigmoid(ga_ref[0].astype(F32)) * ya + _sigmoid(gb_ref[0].astype(F32)) * yb
    yo = jnp.dot(y.astype(BF16), wo_ref[...], preferred_element_type=F32)
    g1 = mod_ref[0][:, 2 * d:3 * d]
    o_ref[0] = x_ref[0] + g1 * yo


def _merge(hf, hb, p, h_b, x, mods3, mlstm_gain, wa, wb, wo, mo_region, ga_region, gb_region):
    b, s, d = x.shape
    tm = LAT_TILE
    wspec = lambda shape: pl.BlockSpec(shape, lambda bi, i: (0, 0), pipeline_mode=pl.Buffered(1))
    tok = lambda region: pl.BlockSpec((1, tm, d), lambda bi, i, r=region: (bi, i, r))
    return pl.pallas_call(
        functools.partial(_merge_kernel, d),
        grid=(b, s // tm),
        in_specs=[tok(0), tok(0), tok(mo_region), tok(ga_region), tok(gb_region), tok(0), tok(0),
                  pl.BlockSpec((1, 1, N_MOD * d), lambda bi, i: (bi, 0, 0)),
                  pl.BlockSpec((1, d), lambda bi, i: (0, 0)),
                  wspec(wa.shape), wspec(wb.shape), wspec(wo.shape)],
        out_specs=pl.BlockSpec((1, tm, d), lambda bi, i: (bi, i, 0)),
        out_shape=jax.ShapeDtypeStruct((b, s, d), F32),
        compiler_params=_cparams(("parallel", "arbitrary")),
        name="merge",
    )(hf, hb, p, p, p, h_b, x, mods3, mlstm_gain, wa, wb, wo)


def _ffn_kernel(d, hidden, x_ref, mod_ref, g_ref, wi_ref, wo_ref, o_ref):
    x1 = x_ref[0]
    mod = mod_ref[0]
    xn = _rms_mod(x1, g_ref[...], mod[:, 3 * d:4 * d], mod[:, 4 * d:5 * d]).astype(BF16)
    a = jnp.dot(xn, wi_ref[:, :hidden], preferred_element_type=F32)
    bb = jnp.dot(xn, wi_ref[:, hidden:], preferred_element_type=F32)
    hmid = (_silu(a) * bb).astype(BF16)
    y = jnp.dot(hmid, wo_ref[...], preferred_element_type=F32)
    o_ref[0] = x1 + mod[:, 5 * d:6 * d] * y


def _ffn(x1, mods3, norm2, wi, wo):
    b, s, d = x1.shape
    hidden = wo.shape[0]
    tm = LAT_TILE
    wspec = lambda shape: pl.BlockSpec(shape, lambda bi, i: (0, 0), pipeline_mode=pl.Buffered(1))
    return pl.pallas_call(
        functools.partial(_ffn_kernel, d, hidden),
        grid=(b, s // tm),
        in_specs=[pl.BlockSpec((1, tm, d), lambda bi, i: (bi, i, 0)),
                  pl.BlockSpec((1, 1, N_MOD * d), lambda bi, i: (bi, 0, 0)),
                  pl.BlockSpec((1, d), lambda bi, i: (0, 0)),
                  wspec(wi.shape), wspec(wo.shape)],
        out_specs=pl.BlockSpec((1, tm, d), lambda bi, i: (bi, i, 0)),
        out_shape=jax.ShapeDtypeStruct((b, s, d), F32),
        compiler_params=_cparams(("parallel", "arbitrary")),
        name="ffn",
    )(x1, mods3, norm2, wi, wo)


def _diff_lane_perm():
    l = np.arange(2 * DIFF_HEAD_DIM)
    half, r = l // DIFF_HEAD_DIM, l % DIFF_HEAD_DIM
    qsel, axis, f = r // 32, (r % 32) // ROPE_FREQS, r % ROPE_FREQS
    return qsel * DIFF_HEAD_DIM + axis * 2 * ROPE_FREQS + half * ROPE_FREQS + f, qsel, axis, f, half


def _rope_tables(n_lat, n_ctx):
    _, _, axis, f, half = _diff_lane_perm()
    pos = np.arange(n_lat)
    row, col = pos // GRID_W, pos % GRID_W
    inv = (ROPE_BASE ** (-np.arange(ROPE_FREQS, dtype=np.float32) / ROPE_FREQS)).astype(np.float32)
    ang = (np.where(axis[None, :] == 0, row[:, None], col[:, None]).astype(np.float32) * inv[f][None, :])
    cos = np.concatenate([np.cos(ang), np.ones((n_ctx, LANES))], axis=0)
    sin = np.concatenate([np.sin(ang) * np.where(half == 0, -1.0, 1.0)[None, :], np.zeros((n_ctx, LANES))], axis=0)
    return jnp.asarray(cos, F32), jnp.asarray(sin, F32)


def kernel(x, c, ctx, c_ctx, w_mod, b_mod, norm1, norm2, w_in, b_gate, conv_w, conv_b, mlstm_norm, q_norm, k_norm,
           lam_vecs, diff_norm, w_branch_a, w_branch_b, w_out, w_ffn_in, w_ffn_out):
    b, s, d = x.shape
    n_ctx = ctx.shape[1]
    assert w_mod.shape[0] == 1, "single-layer block"
    assert s % CHUNK == 0 and n_ctx % CHUNK == 0 and s % ATT_TQ == 0 and s % LAT_TILE == 0
    qk_w = MLSTM_HEADS * MLSTM_DQK
    v_w = MLSTM_HEADS * MLSTM_DV
    da_w = DIFF_HEADS * 2 * DIFF_HEAD_DIM
    n_gates = 4 * MLSTM_HEADS
    sizes = (qk_w, qk_w, v_w, v_w, n_gates, da_w, da_w, da_w, d, d)
    offs = np.concatenate([[0], np.cumsum(sizes)])
    seg = lambda i: w_in[0][:, offs[i]:offs[i + 1]]

    perm, qsel, _, _, _ = _diff_lane_perm()
    head_perm = (np.arange(DIFF_HEADS)[:, None] * 2 * DIFF_HEAD_DIM + perm[None, :]).reshape(-1)
    w_main = w_in[0][:, offs[0]:offs[4]].astype(BF16)
    w_gab = w_in[0][:, offs[8]:offs[10]].astype(BF16)
    segb = lambda i: seg(i).astype(BF16)
    w_att = jnp.concatenate([segb(5)[:, head_perm], segb(6)[:, head_perm], segb(7)], axis=1)
    R_V, R_O, R_GA, R_GB = range(4)
    wg_f32 = jnp.pad(seg(4), ((0, 0), (0, LANES - n_gates)))
    wg_hi = wg_f32.astype(BF16)
    wg = jnp.concatenate([wg_hi, (wg_f32 - wg_hi.astype(F32)).astype(BF16)], axis=1)
    gate_bias = jnp.pad(b_gate[0], (0, LANES - n_gates)).reshape(1, LANES)
    conv_w8 = jnp.pad(conv_w[0], ((0, SUBLANES - CONV_K), (0, 0)))
    gq = q_norm[0][perm % DIFF_HEAD_DIM].reshape(1, LANES)
    gk = k_norm[0][perm % DIFF_HEAD_DIM].reshape(1, LANES)
    unit = np.concatenate([qsel, 2 + qsel])
    grp = jnp.asarray(unit[:, None] == unit[None, :], BF16)
    cos_t, sin_t = _rope_tables(s, n_ctx)

    cc = jnp.concatenate([c, c_ctx[None, :], jnp.zeros((SUBLANES - b - 1, d), F32)], axis=0)
    mods = _modulation(cc, w_mod[0], b_mod[0])
    mods3 = mods.reshape(SUBLANES, 1, N_MOD * d)

    qk, pm, qt, kr, vt, gates, gates_t = _projection(x, ctx, mods3, norm1, (w_main, w_gab, w_att), wg, gate_bias,
                                                     cos_t, sin_t, gq, gk, grp, conv_w8, conv_b)
    hf, hb = _mlstm(qk, pm, gates, gates_t, s, n_ctx, R_V)
    h_b = _attention(qt, kr, vt, lam_vecs[0], diff_norm, s)
    x1 = _merge(hf, hb, pm, h_b, x, mods3, mlstm_norm, w_branch_a[0].astype(BF16), w_branch_b[0].astype(BF16),
                w_out[0].astype(BF16), R_O, R_GA, R_GB)
    return _ffn(x1, mods3, norm2, w_ffn_in[0].astype(BF16), w_ffn_out[0].astype(BF16))
```

```python
import functools
import math

import jax
import jax.numpy as jnp
import numpy as np
from jax import lax
from jax.experimental import pallas as pl
from jax.experimental.pallas import tpu as pltpu

F32 = jnp.float32
BF16 = jnp.bfloat16
HIGHEST = lax.Precision.HIGHEST

GRID_W = 64
N_MOD = 6
MLSTM_HEADS = 4
MLSTM_DQK = 128
MLSTM_DV = 256
CONV_K = 5
DIFF_HEADS = 8
DIFF_HEAD_DIM = 64
ROPE_BASE = 10000.0
ROPE_FREQS = DIFF_HEAD_DIM // 4
RMS_EPS = 1e-6
LAM_INIT_L0 = 0.8 - 0.6 * math.exp(-0.3 * 0)
LOG2_E = math.log2(math.e)
Q_SCALE_LOG2 = DIFF_HEAD_DIM ** -0.5 * LOG2_E
LANES = 128
SUBLANES = 8
BF16_SUBLANES = 16
V_ROWS = 2 * DIFF_HEAD_DIM + BF16_SUBLANES
VMEM_LIMIT = 56 * 1024 * 1024

TOK_TILE = 256
LAT_TILE = 512
CHUNK = 256
MLSTM_BATCH = 4
ATT_TQ = 1024
ATT_TK = 2816


def _cparams(sem):
    return pltpu.CompilerParams(dimension_semantics=sem, vmem_limit_bytes=VMEM_LIMIT)


def _sigmoid(x):
    return 0.5 + 0.5 * jnp.tanh(0.5 * x)


def _silu(x):
    h = 0.5 * x
    return h + h * jnp.tanh(h)


def _log_sigmoid(x):
    return jnp.minimum(x, 0.0) - jnp.log(1.0 + jnp.exp(-jnp.abs(x)))


def _mod_kernel(c_ref, w_ref, b_ref, o_ref):
    a = _silu(c_ref[...])
    o_ref[...] = jnp.dot(a, w_ref[...], preferred_element_type=F32, precision=HIGHEST) + b_ref[...]


def _modulation(cc, w_mod, b_mod):
    rows, d = cc.shape
    n = w_mod.shape[1]
    tn = 1024
    return pl.pallas_call(
        _mod_kernel,
        grid=(n // tn,),
        in_specs=[pl.BlockSpec((rows, d), lambda j: (0, 0)),
                  pl.BlockSpec((d, tn), lambda j: (0, j)),
                  pl.BlockSpec((1, tn), lambda j: (0, j))],
        out_specs=pl.BlockSpec((rows, tn), lambda j: (0, j)),
        out_shape=jax.ShapeDtypeStruct((rows, n), F32),
        compiler_params=_cparams(("arbitrary",)),
        name="mod",
    )(cc, w_mod, b_mod.reshape(1, n))


def _rms_mod(xf, g, shift, scale):
    ms = jnp.mean(xf * xf, axis=-1, keepdims=True)
    return xf * lax.rsqrt(ms + RMS_EPS) * g * (1.0 + scale) + shift


def _proj_kernel(n_lat_tiles, n_tiles, d, x_ref, xprev_ref, xnext_ref, ctx_ref, mod_ref, g_ref,
                 wm_ref, wgab_ref, watt_ref, wg_ref, gbias_ref,
                 cos_ref, sin_ref, gq_ref, gk_ref, grp_ref, cw_ref, cb_ref, tri_ref,
                 qk_ref, pm_ref, qt_ref, kr_ref, vt_ref, gates_ref, gates_t_ref,
                 ext_ref, yq_ref, yk_ref, yv_ref, gpre_ref, xb_ref, xlo_ref):
    i = pl.program_id(0)
    tm = x_ref.shape[1]

    @pl.when(i == 0)
    def _():
        ext_ref[...] = jnp.zeros_like(ext_ref)
        yq_ref[...] = jnp.zeros_like(yq_ref)
        yk_ref[...] = jnp.zeros_like(yk_ref)
        yv_ref[...] = jnp.zeros_like(yv_ref)
        gpre_ref[...] = jnp.zeros_like(gpre_ref)

    ti = jnp.minimum(i, pl.num_programs(0) - 2) % n_tiles
    xf = jnp.where(ti < n_lat_tiles, x_ref[0], ctx_ref[0])
    mod = mod_ref[0]
    gain1 = g_ref[...]
    xn = _rms_mod(xf, gain1, mod[:, 0:d], mod[:, d:2 * d])
    xb = xn.astype(BF16)
    halo = jnp.concatenate([xprev_ref[0], xnext_ref[0]], axis=0)
    xb_ref[0:tm, :] = xb
    xb_ref[tm:, :] = _rms_mod(halo, gain1, mod[:, 0:d], mod[:, d:2 * d]).astype(BF16)
    xlo_ref[...] = (xn - xb.astype(F32)).astype(BF16)

    def project(w_ref, region):
        return jnp.dot(xb_ref[0:tm, :], w_ref[:, region * d:(region + 1) * d], preferred_element_type=F32)

    taps = cw_ref[...]
    conv_bias = cb_ref[...]
    n_q_cols = d // 2

    w2 = 2 * LANES

    def conv_columns(c_lo, c_hi):
        for c0 in range(c_lo, c_hi, w2):
            acc = jnp.zeros((tm, w2), F32) + conv_bias[:, c0:c0 + w2]
            for k in range(CONV_K):
                r0 = SUBLANES - CONV_K // 2 + k
                acc = acc + ext_ref[r0:r0 + tm, c0:c0 + w2] * taps[k:k + 1, c0:c0 + w2]
            out = _silu(acc)
            if c0 < n_q_cols:
                out = out * MLSTM_DQK ** -0.5
            qk_ref[0, :, c0:c0 + w2] = out.astype(BF16)

    cos = cos_ref[...]
    sin = sin_ref[...]
    grp = grp_ref[...]
    gq = gq_ref[...]
    gk = gk_ref[...]
    hw = 2 * DIFF_HEAD_DIM

    def scaled_sumsq(x2):
        ss = jnp.dot((x2 * x2).astype(BF16), grp, preferred_element_type=F32)
        return lax.rsqrt(ss * (1.0 / DIFF_HEAD_DIM) + RMS_EPS)

    def rotated(y):
        return y * cos + pltpu.roll(y, DIFF_HEAD_DIM, axis=1) * sin

    ones_rows = jnp.ones((V_ROWS - hw, tm), BF16)

    def attention_pair(pair):
        sl2 = slice(pair * 2 * hw, (pair + 1) * 2 * hw)
        yq2 = yq_ref[:, sl2]
        yk2 = yk_ref[:, sl2]
        qn = yq2 * scaled_sumsq(yq2)
        kn = yk2 * scaled_sumsq(yk2)
        for u in range(2):
            h = 2 * pair + u
            sl = slice(h * hw, (h + 1) * hw)
            su = slice(u * hw, (u + 1) * hw)
            qt_ref[0, sl, :] = (rotated(qn[:, su] * gq) * Q_SCALE_LOG2).T.astype(BF16)
            kr_ref[0, :, sl] = rotated(kn[:, su] * gk).astype(BF16)
            vt_ref[0, h * V_ROWS:h * V_ROWS + hw, :] = yv_ref[:, sl].T.astype(BF16)
            vt_ref[0, h * V_ROWS + hw:(h + 1) * V_ROWS, :] = ones_rows

    def gate_epilogue():
        n_g = 4 * MLSTM_HEADS
        g_t = gpre_ref[...].T[0:n_g, :]
        lf = _log_sigmoid(g_t)
        lf_hi = lf.astype(BF16)
        lf_lo = (lf - lf_hi.astype(F32)).astype(BF16)
        incl_f = tri_ref[0]
        incl_b = tri_ref[1]
        cum_f = (jnp.dot(lf_hi, incl_f, preferred_element_type=F32)
                 + jnp.dot(lf_lo, incl_f, preferred_element_type=F32))
        cum_b = (jnp.dot(lf_hi, incl_b, preferred_element_type=F32)
                 + jnp.dot(lf_lo, incl_b, preferred_element_type=F32))
        row = lax.broadcasted_iota(jnp.int32, (n_g, tm), 0)
        cum = jnp.where(row < 2 * MLSTM_HEADS, cum_f, cum_b)
        a = g_t - pltpu.roll(cum, n_g - MLSTM_HEADS, axis=0)
        out_t = jnp.where((row % (2 * MLSTM_HEADS)) < MLSTM_HEADS, a, cum) * LOG2_E
        gates_t_ref[0, 0] = out_t
        gates_ref[0] = jnp.concatenate([out_t, jnp.zeros((LANES - n_g, tm), F32)], axis=0).T

    for r, (w_ref, region) in enumerate(((wm_ref, 1), (wm_ref, 2), (wgab_ref, 0), (wgab_ref, 1))):
        pm_ref[0, :, r * d:(r + 1) * d] = project(w_ref, region).astype(BF16)
        attention_pair(r)
    yq_ref[...] = project(watt_ref, 0)
    conv_columns(0, d // 2)
    yk_ref[...] = project(watt_ref, 1)
    conv_columns(d // 2, d)
    yv_ref[...] = project(watt_ref, 2)
    gate_epilogue()
    y_qk = jnp.dot(xb_ref[...], wm_ref[:, 0:d], preferred_element_type=F32)
    has_prev = jnp.logical_and(ti > 0, ti < n_lat_tiles)
    has_next = ti < n_lat_tiles - 1
    ext_ref[0:SUBLANES, :] = jnp.where(has_prev, y_qk[tm:tm + SUBLANES], 0.0)
    ext_ref[SUBLANES:SUBLANES + tm, :] = y_qk[0:tm]
    ext_ref[SUBLANES + tm:, :] = jnp.where(has_next, y_qk[tm + SUBLANES:], 0.0)
    g2 = (jnp.dot(xb_ref[0:tm, :], wg_ref[...], preferred_element_type=F32)
          + jnp.dot(xlo_ref[...], wg_ref[...], preferred_element_type=F32))
    gpre_ref[...] = g2[:, :LANES] + g2[:, LANES:] + gbias_ref[...]


def _projection(x, ctx, mods3, norm1, weights, wg, gate_bias, cos_t, sin_t, gq, gk, grp, conv_w8, conv_b):
    b, s, d = x.shape
    n_ctx = ctx.shape[1]
    tm = TOK_TILE
    assert tm == CHUNK
    assert n_ctx == tm
    n_lat_tiles = s // tm
    n_tiles = (s + n_ctx) // tm
    t = s + n_ctx
    w_main, w_gab, w_att = weights
    resident = lambda shape: pl.BlockSpec(shape, lambda i: (0, 0), pipeline_mode=pl.Buffered(1))
    da_w = DIFF_HEADS * 2 * DIFF_HEAD_DIM
    halo_stride = tm // SUBLANES
    last_halo = s // SUBLANES - 1
    const = lambda shape: pl.BlockSpec(shape, lambda i: (0,) * len(shape))
    n_pairs = b * n_tiles
    pos = np.arange(tm)
    tri = jnp.asarray(np.stack([pos[:, None] <= pos[None, :], pos[:, None] >= pos[None, :]]), BF16)

    def cur(i):
        j = jnp.minimum(i, n_pairs - 1)
        return j // n_tiles, j % n_tiles

    def late(i):
        j = jnp.maximum(i - 1, 0)
        return j // n_tiles, j % n_tiles

    tok = lambda width: pl.BlockSpec((1, tm, width), lambda i: (late(i)[0], late(i)[1], 0))
    tok_t = lambda rows: pl.BlockSpec((1, rows, tm), lambda i: (late(i)[0], 0, late(i)[1]))
    n_ctx_tiles = n_ctx // tm
    return pl.pallas_call(
        functools.partial(_proj_kernel, n_lat_tiles, n_tiles, d),
        grid=(n_pairs + 1,),
        in_specs=[
            pl.BlockSpec((1, tm, d), lambda i: (cur(i)[0], jnp.minimum(cur(i)[1], n_lat_tiles - 1), 0)),
            pl.BlockSpec((1, SUBLANES, d),
                         lambda i: (cur(i)[0], jnp.clip(cur(i)[1] * halo_stride - 1, 0, last_halo), 0)),
            pl.BlockSpec((1, SUBLANES, d),
                         lambda i: (cur(i)[0], jnp.clip((cur(i)[1] + 1) * halo_stride, 0, last_halo), 0)),
            pl.BlockSpec((1, tm, d),
                         lambda i: (cur(i)[0], jnp.clip(cur(i)[1] - n_lat_tiles, 0, n_ctx_tiles - 1), 0)),
            pl.BlockSpec((1, 1, N_MOD * d),
                         lambda i: (jnp.where(cur(i)[1] < n_lat_tiles, cur(i)[0], b), 0, 0)),
            const((1, d)),
            resident(w_main.shape), resident(w_gab.shape), resident(w_att.shape),
            const((d, 2 * LANES)), const((1, LANES)),
            pl.BlockSpec((tm, LANES), lambda i: (late(i)[1], 0)),
            pl.BlockSpec((tm, LANES), lambda i: (late(i)[1], 0)),
            const((1, LANES)), const((1, LANES)), const((2 * LANES, 2 * LANES)),
            const((SUBLANES, d)), const((1, d)), const((2, tm, tm)),
        ],
        out_specs=[tok(d), pl.BlockSpec((1, tm, 4 * d), lambda i: (cur(i)[0], cur(i)[1], 0)),
                   tok_t(da_w), tok(da_w), tok_t(DIFF_HEADS * V_ROWS), tok(LANES),
                   pl.BlockSpec((1, 1, 4 * MLSTM_HEADS, tm), lambda i: (late(i)[0], late(i)[1], 0, 0))],
        out_shape=[jax.ShapeDtypeStruct((b, t, d), BF16),
                   jax.ShapeDtypeStruct((b, t, 4 * d), BF16),
                   jax.ShapeDtypeStruct((b, da_w, t), BF16),
                   jax.ShapeDtypeStruct((b, t, da_w), BF16),
                   jax.ShapeDtypeStruct((b, DIFF_HEADS * V_ROWS, t), BF16),
                   jax.ShapeDtypeStruct((b, t, LANES), F32),
                   jax.ShapeDtypeStruct((b, n_tiles, 4 * MLSTM_HEADS, tm), F32)],
        scratch_shapes=[pltpu.VMEM((tm + 2 * SUBLANES, d), F32),
                        pltpu.VMEM((tm, da_w), F32), pltpu.VMEM((tm, da_w), F32), pltpu.VMEM((tm, da_w), F32),
                        pltpu.VMEM((tm, LANES), F32),
                        pltpu.VMEM((tm + 2 * SUBLANES, d), BF16),
                        pltpu.VMEM((tm, d), BF16)],
        compiler_params=_cparams(("arbitrary",)),
        name="proj",
    )(x, x, x, ctx, mods3, norm1, w_main, w_gab, w_att, wg, gate_bias, cos_t, sin_t, gq, gk, grp, conv_w8, conv_b,
      tri)


def _mlstm_direction(reverse, q, k, v_aug, a_col, a_row, lf_cum_col, lf_total, c_ref, m_ref, hd):
    l = q.shape[0]
    dv = v_aug.shape[1] - LANES

    def wide(x, n_tiles):
        return jnp.concatenate([x] * n_tiles, axis=1)

    m_prev = m_ref[hd:hd + 1, :]
    c_aug = c_ref[hd]
    lf_cum_rep = jnp.broadcast_to(lf_cum_col, (l, LANES))
    a_rep = jnp.broadcast_to(a_col, (l, LANES))
    rows = lax.broadcasted_iota(jnp.int32, (l, l), 0)
    cols = lax.broadcasted_iota(jnp.int32, (l, l), 1)
    visible = (cols >= rows) if reverse else (cols <= rows)
    a_mat = jnp.where(visible, a_row, -jnp.inf)
    m_rep = jnp.maximum(jnp.broadcast_to(jnp.max(a_mat, axis=-1, keepdims=True), (l, LANES)), m_prev)
    decay = jnp.exp2(a_mat - wide(m_rep, l // LANES))
    s = lax.dot_general(q, k, (((1,), (1,)), ((), ())), preferred_element_type=F32)
    pmat = (s * decay).astype(BF16)
    n_wide = v_aug.shape[1] // LANES
    intra = jnp.dot(pmat, v_aug, preferred_element_type=F32)
    inter = jnp.dot(q, c_aug.astype(BF16), preferred_element_type=F32)
    numden = intra + wide(jnp.exp2(m_prev - m_rep), n_wide) * inter
    den = numden[:, dv:]
    floor = jnp.exp2(-(lf_cum_rep + m_rep))
    h = numden[:, :dv] * wide(1.0 / jnp.maximum(jnp.abs(den), floor), dv // LANES)
    m_last = jnp.maximum(jnp.max(a_rep, axis=0, keepdims=True), m_prev)
    w = jnp.exp2(a_rep - m_last)
    kw = (w * k.astype(F32)).astype(BF16)
    kv = lax.dot_general(kw, v_aug, (((0,), (0,)), ((), ())), preferred_element_type=F32)
    c_ref[hd] = wide(jnp.exp2(m_prev - m_last), n_wide) * c_aug + kv
    m_ref[hd:hd + 1, :] = lf_total + m_last
    return h


def _mlstm_kernel(qf_ref, kf_ref, vf_ref, gf_ref, gtf_ref, qb_ref, kb_ref, vb_ref, gb_ref, gtb_ref,
                  hf_ref, hb_ref, c_ref, m_ref):
    s_idx = pl.program_id(1)

    @pl.when(s_idx == 0)
    def _():
        c_ref[...] = jnp.zeros_like(c_ref)
        m_ref[...] = jnp.zeros_like(m_ref)

    nb, l = qf_ref.shape[0], qf_ref.shape[1]
    nh = MLSTM_HEADS
    ones_col = jnp.ones((l, LANES), BF16)
    for bb in range(nb):
        for reverse, q_ref, k_ref, v_ref, g_ref, gt_ref, h_ref in (
                (False, qf_ref, kf_ref, vf_ref, gf_ref, gtf_ref, hf_ref),
                (True, qb_ref, kb_ref, vb_ref, gb_ref, gtb_ref, hb_ref)):
            g = g_ref[bb]
            g_t = gt_ref[bb, 0]
            base = 2 * nh if reverse else 0
            for h in range(nh):
                ca, cc = base + h, base + nh + h
                hd = bb * 2 * nh + (nh if reverse else 0) + h
                lf_cum_col = g[:, cc:cc + 1]
                lf_total = g[0:1, cc:cc + 1] if reverse else g[l - 1:l, cc:cc + 1]
                q = q_ref[bb, :, h * MLSTM_DQK:(h + 1) * MLSTM_DQK]
                k = k_ref[bb, :, h * MLSTM_DQK:(h + 1) * MLSTM_DQK]
                v_aug = jnp.concatenate([v_ref[bb, :, h * MLSTM_DV:(h + 1) * MLSTM_DV], ones_col], axis=1)
                hval = _mlstm_direction(reverse, q, k, v_aug, g[:, ca:ca + 1], g_t[ca:ca + 1, :],
                                        lf_cum_col, lf_total, c_ref, m_ref, hd)
                h_ref[bb, :, h * MLSTM_DV:(h + 1) * MLSTM_DV] = hval.astype(h_ref.dtype)


def _mlstm(qk, pm, gates, gates_t, n_lat, n_ctx, v_region):
    b, t, _ = qk.shape
    l = CHUNK
    n_lat_c = n_lat // l
    n_ctx_c = n_ctx // l
    n_steps = n_lat_c + n_ctx_c
    qw = MLSTM_HEADS * MLSTM_DQK
    vw = MLSTM_HEADS * MLSTM_DV

    def fwd(si):
        return jnp.where(si < n_ctx_c, n_lat_c + si, si - n_ctx_c)

    def bwd(si):
        return jnp.where(si < n_ctx_c, n_steps - 1 - si, n_steps - 1 - si)

    nb = MLSTM_BATCH
    assert b % nb == 0
    in_specs = []
    for order in (fwd, bwd):
        in_specs += [
            pl.BlockSpec((nb, l, qw), lambda bi, si, o=order: (bi, o(si), 0)),
            pl.BlockSpec((nb, l, qw), lambda bi, si, o=order: (bi, o(si), 1)),
            pl.BlockSpec((nb, l, vw), lambda bi, si, o=order: (bi, o(si), v_region)),
            pl.BlockSpec((nb, l, LANES), lambda bi, si, o=order: (bi, o(si), 0)),
            pl.BlockSpec((nb, 1, 4 * MLSTM_HEADS, l), lambda bi, si, o=order: (bi, o(si), 0, 0)),
        ]
    out_specs = [pl.BlockSpec((nb, l, vw), lambda bi, si: (bi, fwd(si), 0)),
                 pl.BlockSpec((nb, l, vw), lambda bi, si: (bi, bwd(si), 0))]
    n_scans = nb * 2 * MLSTM_HEADS
    return pl.pallas_call(
        _mlstm_kernel,
        grid=(b // nb, n_steps),
        in_specs=in_specs,
        out_specs=out_specs,
        out_shape=[jax.ShapeDtypeStruct((b, t, vw), BF16)] * 2,
        scratch_shapes=[pltpu.VMEM((n_scans, MLSTM_DQK, MLSTM_DV + LANES), F32),
                        pltpu.VMEM((n_scans, LANES), F32)],
        compiler_params=_cparams(("parallel", "arbitrary")),
        name="mlstm",
    )(qk, qk, pm, gates, gates_t, qk, qk, pm, gates, gates_t)


def _attn_kernel(n_keys, tq, qt_ref, k_ref, vt_ref, lam_ref, gain_ref, o_ref,
                 s_ref, pm_ref, e_ref, al_ref, m_ref, acc_ref):
    tk = ATT_TK
    n = n_keys // tk
    nq = qt_ref.shape[2] // tq
    width = 2 * tq
    row = lax.broadcasted_iota(jnp.int32, (qt_ref.shape[1], tq), 0)
    first = (row % DIFF_HEAD_DIM) < (DIFF_HEAD_DIM // 2)

    acc_ref[...] = jnp.zeros_like(acc_ref)

    def scores(j, qb):
        off = pl.multiple_of(j * tk, tk)
        qt = qt_ref[0, :, pl.ds(pl.multiple_of(qb * tq, tq), tq)]
        zero = jnp.zeros_like(qt)
        qq = jnp.concatenate([jnp.where(first, qt, zero), jnp.where(first, zero, qt)], axis=1)
        s = jnp.dot(k_ref[0, pl.ds(off, tk), :], qq, preferred_element_type=F32)
        s_ref[...] = s
        pm_ref[...] = jnp.max(s.reshape(tk // SUBLANES, SUBLANES, width), axis=0)

    def softmax(j, qb):
        m_old = jnp.where(j == 0, -jnp.inf, m_ref[...])
        m_new = jnp.maximum(m_old, jnp.max(pm_ref[...], axis=0, keepdims=True))
        alpha = jnp.exp2(m_old - m_new)
        e = jnp.exp2(s_ref[...] - m_new)
        e_ref[...] = e.astype(BF16)
        al_ref[...] = alpha
        m_ref[...] = m_new

    def accumulate(j):
        off = pl.multiple_of(j * tk, tk)
        pv = jnp.dot(vt_ref[0, :, pl.ds(off, tk)], e_ref[...], preferred_element_type=F32)
        acc_ref[...] = al_ref[...] * acc_ref[...] + pv

    def finalize(qb):
        lv = lam_ref[...]
        lam = (jnp.exp(jnp.sum(lv[0:1] * lv[1:2], axis=-1, keepdims=True))
               - jnp.exp(jnp.sum(lv[2:3] * lv[3:4], axis=-1, keepdims=True)) + LAM_INIT_L0)
        hw = o_ref.shape[2]
        inv_l = 1.0 / acc_ref[hw:hw + 1, :]
        o_all = acc_ref[0:hw, :] * inv_l
        o_t = o_all[:, :tq] - lam * o_all[:, tq:]
        ms = jnp.mean(o_t * o_t, axis=0, keepdims=True)
        out = (o_t * lax.rsqrt(ms + RMS_EPS)).T * (gain_ref[...] * (1.0 - LAM_INIT_L0))
        o_ref[0, pl.ds(pl.multiple_of(qb * tq, tq), tq), :] = out.astype(o_ref.dtype)

    def advance(j, qb):
        wrap = j == n - 1
        return jnp.where(wrap, 0, j + 1), jnp.where(wrap, qb + 1, qb)

    zero_i = jnp.int32(0)
    item0 = (zero_i, zero_i)
    item1 = advance(*item0)
    scores(*item0)
    softmax(*item0)
    scores(*item1)

    def iteration(t, carry):
        ja, qa, jm, qm, js, qs = carry
        accumulate(ja)
        softmax(jm, qm)
        scores(js, qs)

        @pl.when(ja == n - 1)
        def _():
            finalize(qa)

        return (jm, qm, js, qs) + advance(js, qs)

    item2 = advance(*item1)
    lax.fori_loop(2, nq * n, iteration, item0 + item1 + item2)
    accumulate(n - 2)
    softmax(n - 1, nq - 1)
    accumulate(n - 1)
    finalize(nq - 1)


def _attention(qt, kr, vt, lam_vecs, diff_gain, n_lat):
    b, t, _ = kr.shape
    tq = ATT_TQ
    hw = 2 * DIFF_HEAD_DIM
    nq = n_lat // tq
    assert t % ATT_TK == 0 and t // ATT_TK >= 2
    return pl.pallas_call(
        functools.partial(_attn_kernel, t, tq),
        grid=(b, DIFF_HEADS),
        in_specs=[pl.BlockSpec((1, hw, n_lat), lambda bi, h: (bi, h, 0)),
                  pl.BlockSpec((1, t, hw), lambda bi, h: (bi, 0, h)),
                  pl.BlockSpec((1, V_ROWS, t), lambda bi, h: (bi, h, 0)),
                  pl.BlockSpec(lam_vecs.shape, lambda bi, h: (0, 0)),
                  pl.BlockSpec((1, hw), lambda bi, h: (0, 0))],
        out_specs=pl.BlockSpec((1, n_lat, hw), lambda bi, h: (bi, 0, h)),
        out_shape=jax.ShapeDtypeStruct((b, n_lat, DIFF_HEADS * hw), BF16),
        scratch_shapes=[pltpu.VMEM((ATT_TK, 2 * tq), F32),
                        pltpu.VMEM((SUBLANES, 2 * tq), F32),
                        pltpu.VMEM((ATT_TK, 2 * tq), BF16),
                        pltpu.VMEM((1, 2 * tq), F32),
                        pltpu.VMEM((1, 2 * tq), F32),
                        pltpu.VMEM((V_ROWS, 2 * tq), F32)],
        compiler_params=_cparams(("parallel", "arbitrary")),
        name="attn",
    )(qt, kr, vt, lam_vecs, diff_gain)


def _merge_kernel(d, hf_ref, hb_ref, mo_ref, ga_ref, gb_ref, hB_ref, x_ref, mod_ref, gain_ref,
                  wa_ref, wb_ref, wo_ref, o_ref):
    gain = gain_ref[...]
    parts = []
    for h in range(MLSTM_HEADS):
        sl = slice(h * MLSTM_DV, (h + 1) * MLSTM_DV)
        hs = hf_ref[0, :, sl].astype(F32) + hb_ref[0, :, sl].astype(F32)
        ms = jnp.mean(hs * hs, axis=-1, keepdims=True)
        hn = hs * lax.rsqrt(ms + RMS_EPS) * gain[:, sl]
        parts.append((_sigmoid(mo_ref[0, :, sl].astype(F32)) * hn).astype(BF16))
    h_a = jnp.concatenate(parts, axis=1)
    ya = jnp.dot(h_a, wa_ref[...], preferred_element_type=F32)
    yb = jnp.dot(hB_ref[0], wb_ref[...], preferred_element_type=F32)
    y = _sigmoid(ga_ref[0].astype(F32)) * ya + _sigmoid(gb_ref[0].astype(F32)) * yb
    yo = jnp.dot(y.astype(BF16), wo_ref[...], preferred_element_type=F32)
    g1 = mod_ref[0][:, 2 * d:3 * d]
    o_ref[0] = x_ref[0] + g1 * yo


def _merge(hf, hb, p, h_b, x, mods3, mlstm_gain, wa, wb, wo, mo_region, ga_region, gb_region):
    b, s, d = x.shape
    tm = LAT_TILE
    wspec = lambda shape: pl.BlockSpec(shape, lambda bi, i: (0, 0), pipeline_mode=pl.Buffered(1))
    tok = lambda region: pl.BlockSpec((1, tm, d), lambda bi, i, r=region: (bi, i, r))
    return pl.pallas_call(
        functools.partial(_merge_kernel, d),
        grid=(b, s // tm),
        in_specs=[tok(0), tok(0), tok(mo_region), tok(ga_region), tok(gb_region), tok(0), tok(0),
                  pl.BlockSpec((1, 1, N_MOD * d), lambda bi, i: (bi, 0, 0)),
                  pl.BlockSpec((1, d), lambda bi, i: (0, 0)),
                  wspec(wa.shape), wspec(wb.shape), wspec(wo.shape)],
        out_specs=pl.BlockSpec((1, tm, d), lambda bi, i: (bi, i, 0)),
        out_shape=jax.ShapeDtypeStruct((b, s, d), F32),
        compiler_params=_cparams(("parallel", "arbitrary")),
        name="merge",
    )(hf, hb, p, p, p, h_b, x, mods3, mlstm_gain, wa, wb, wo)


def _ffn_kernel(d, hidden, x_ref, mod_ref, g_ref, wi_ref, wo_ref, o_ref):
    x1 = x_ref[0]
    mod = mod_ref[0]
    xn = _rms_mod(x1, g_ref[...], mod[:, 3 * d:4 * d], mod[:, 4 * d:5 * d]).astype(BF16)
    a = jnp.dot(xn, wi_ref[:, :hidden], preferred_element_type=F32)
    bb = jnp.dot(xn, wi_ref[:, hidden:], preferred_element_type=F32)
    hmid = (_silu(a) * bb).astype(BF16)
    y = jnp.dot(hmid, wo_ref[...], preferred_element_type=F32)
    o_ref[0] = x1 + mod[:, 5 * d:6 * d] * y


def _ffn(x1, mods3, norm2, wi, wo):
    b, s, d = x1.shape
    hidden = wo.shape[0]
    tm = LAT_TILE
    wspec = lambda shape: pl.BlockSpec(shape, lambda bi, i: (0, 0), pipeline_mode=pl.Buffered(1))
    return pl.pallas_call(
        functools.partial(_ffn_kernel, d, hidden),
        grid=(b, s // tm),
        in_specs=[pl.BlockSpec((1, tm, d), lambda bi, i: (bi, i, 0)),
                  pl.BlockSpec((1, 1, N_MOD * d), lambda bi, i: (bi, 0, 0)),
                  pl.BlockSpec((1, d), lambda bi, i: (0, 0)),
                  wspec(wi.shape), wspec(wo.shape)],
        out_specs=pl.BlockSpec((1, tm, d), lambda bi, i: (bi, i, 0)),
        out_shape=jax.ShapeDtypeStruct((b, s, d), F32),
        compiler_params=_cparams(("parallel", "arbitrary")),
        name="ffn",
    )(x1, mods3, norm2, wi, wo)


def _diff_lane_perm():
    l = np.arange(2 * DIFF_HEAD_DIM)
    half, r = l // DIFF_HEAD_DIM, l % DIFF_HEAD_DIM
    qsel, axis, f = r // 32, (r % 32) // ROPE_FREQS, r % ROPE_FREQS
    return qsel * DIFF_HEAD_DIM + axis * 2 * ROPE_FREQS + half * ROPE_FREQS + f, qsel, axis, f, half


def _rope_tables(n_lat, n_ctx):
    _, _, axis, f, half = _diff_lane_perm()
    pos = np.arange(n_lat)
    row, col = pos // GRID_W, pos % GRID_W
    inv = (ROPE_BASE ** (-np.arange(ROPE_FREQS, dtype=np.float32) / ROPE_FREQS)).astype(np.float32)
    ang = (np.where(axis[None, :] == 0, row[:, None], col[:, None]).astype(np.float32) * inv[f][None, :])
    cos = np.concatenate([np.cos(ang), np.ones((n_ctx, LANES))], axis=0)
    sin = np.concatenate([np.sin(ang) * np.where(half == 0, -1.0, 1.0)[None, :], np.zeros((n_ctx, LANES))], axis=0)
    return jnp.asarray(cos, F32), jnp.asarray(sin, F32)


def kernel(x, c, ctx, c_ctx, w_mod, b_mod, norm1, norm2, w_in, b_gate, conv_w, conv_b, mlstm_norm, q_norm, k_norm,
           lam_vecs, diff_norm, w_branch_a, w_branch_b, w_out, w_ffn_in, w_ffn_out):
    b, s, d = x.shape
    n_ctx = ctx.shape[1]
    assert w_mod.shape[0] == 1, "single-layer block"
    assert s % CHUNK == 0 and n_ctx % CHUNK == 0 and s % ATT_TQ == 0 and s % LAT_TILE == 0
    qk_w = MLSTM_HEADS * MLSTM_DQK
    v_w = MLSTM_HEADS * MLSTM_DV
    da_w = DIFF_HEADS * 2 * DIFF_HEAD_DIM
    n_gates = 4 * MLSTM_HEADS
    sizes = (qk_w, qk_w, v_w, v_w, n_gates, da_w, da_w, da_w, d, d)
    offs = np.concatenate([[0], np.cumsum(sizes)])
    seg = lambda i: w_in[0][:, offs[i]:offs[i + 1]]

    perm, qsel, _, _, _ = _diff_lane_perm()
    head_perm = (np.arange(DIFF_HEADS)[:, None] * 2 * DIFF_HEAD_DIM + perm[None, :]).reshape(-1)
    w_main = w_in[0][:, offs[0]:offs[4]].astype(BF16)
    w_gab = w_in[0][:, offs[8]:offs[10]].astype(BF16)
    segb = lambda i: seg(i).astype(BF16)
    w_att = jnp.concatenate([segb(5)[:, head_perm], segb(6)[:, head_perm], segb(7)], axis=1)
    R_V, R_O, R_GA, R_GB = range(4)
    wg_f32 = jnp.pad(seg(4), ((0, 0), (0, LANES - n_gates)))
    wg_hi = wg_f32.astype(BF16)
    wg = jnp.concatenate([wg_hi, (wg_f32 - wg_hi.astype(F32)).astype(BF16)], axis=1)
    gate_bias = jnp.pad(b_gate[0], (0, LANES - n_gates)).reshape(1, LANES)
    conv_w8 = jnp.pad(conv_w[0], ((0, SUBLANES - CONV_K), (0, 0)))
    gq = q_norm[0][perm % DIFF_HEAD_DIM].reshape(1, LANES)
    gk = k_norm[0][perm % DIFF_HEAD_DIM].reshape(1, LANES)
    unit = np.concatenate([qsel, 2 + qsel])
    grp = jnp.asarray(unit[:, None] == unit[None, :], BF16)
    cos_t, sin_t = _rope_tables(s, n_ctx)

    cc = jnp.concatenate([c, c_ctx[None, :], jnp.zeros((SUBLANES - b - 1, d), F32)], axis=0)
    mods = _modulation(cc, w_mod[0], b_mod[0])
    mods3 = mods.reshape(SUBLANES, 1, N_MOD * d)

    qk, pm, qt, kr, vt, gates, gates_t = _projection(x, ctx, mods3, norm1, (w_main, w_gab, w_att), wg, gate_bias,
                                                     cos_t, sin_t, gq, gk, grp, conv_w8, conv_b)
    hf, hb = _mlstm(qk, pm, gates, gates_t, s, n_ctx, R_V)
    h_b = _attention(qt, kr, vt, lam_vecs[0], diff_norm, s)
    x1 = _merge(hf, hb, pm, h_b, x, mods3, mlstm_norm, w_branch_a[0].astype(BF16), w_branch_b[0].astype(BF16),
                w_out[0].astype(BF16), R_O, R_GA, R_GB)
    return _ffn(x1, mods3, norm2, w_ffn_in[0].astype(BF16), w_ffn_out[0].astype(BF16))
```

```python
import functools
import math

import jax
import jax.numpy as jnp
import numpy as np
from jax import lax
from jax.experimental import pallas as pl
from jax.experimental.pallas import tpu as pltpu

F32 = jnp.float32
BF16 = jnp.bfloat16
HIGHEST = lax.Precision.HIGHEST

GRID_W = 64
N_MOD = 6
MLSTM_HEADS = 4
MLSTM_DQK = 128
MLSTM_DV = 256
CONV_K = 5
DIFF_HEADS = 8
DIFF_HEAD_DIM = 64
ROPE_BASE = 10000.0
ROPE_FREQS = DIFF_HEAD_DIM // 4
RMS_EPS = 1e-6
LAM_INIT_L0 = 0.8 - 0.6 * math.exp(-0.3 * 0)
LOG2_E = math.log2(math.e)
Q_SCALE_LOG2 = DIFF_HEAD_DIM ** -0.5 * LOG2_E
LANES = 128
SUBLANES = 8
BF16_SUBLANES = 16
V_ROWS = 2 * DIFF_HEAD_DIM + BF16_SUBLANES
VMEM_LIMIT = 56 * 1024 * 1024

TOK_TILE = 256
LAT_TILE = 512
CHUNK = 256
MLSTM_BATCH = 4
MLSTM_GROUP = 2
ATT_TQ = 1024
ATT_TK = 2816


def _cparams(sem):
    return pltpu.CompilerParams(dimension_semantics=sem, vmem_limit_bytes=VMEM_LIMIT)


def _sigmoid(x):
    return 0.5 + 0.5 * jnp.tanh(0.5 * x)


def _silu(x):
    h = 0.5 * x
    return h + h * jnp.tanh(h)


def _log_sigmoid(x):
    return jnp.minimum(x, 0.0) - jnp.log(1.0 + jnp.exp(-jnp.abs(x)))


def _mod_kernel(c_ref, w_ref, b_ref, o_ref):
    a = _silu(c_ref[...])
    o_ref[...] = jnp.dot(a, w_ref[...], preferred_element_type=F32, precision=HIGHEST) + b_ref[...]


def _modulation(cc, w_mod, b_mod):
    rows, d = cc.shape
    n = w_mod.shape[1]
    tn = 1024
    return pl.pallas_call(
        _mod_kernel,
        grid=(n // tn,),
        in_specs=[pl.BlockSpec((rows, d), lambda j: (0, 0)),
                  pl.BlockSpec((d, tn), lambda j: (0, j)),
                  pl.BlockSpec((1, tn), lambda j: (0, j))],
        out_specs=pl.BlockSpec((rows, tn), lambda j: (0, j)),
        out_shape=jax.ShapeDtypeStruct((rows, n), F32),
        compiler_params=_cparams(("arbitrary",)),
        name="mod",
    )(cc, w_mod, b_mod.reshape(1, n))


def _rms_mod(xf, g, shift, scale):
    ms = jnp.mean(xf * xf, axis=-1, keepdims=True)
    return xf * lax.rsqrt(ms + RMS_EPS) * g * (1.0 + scale) + shift


def _proj_kernel(n_lat_tiles, n_tiles, d, x_ref, xprev_ref, xnext_ref, ctx_ref, mod_ref, g_ref,
                 wm_ref, wgab_ref, watt_ref, wg_ref, gbias_ref,
                 cos_ref, sin_ref, gq_ref, gk_ref, grp_ref, cw_ref, cb_ref,
                 qk_ref, pm_ref, qt_ref, kr_ref, vt_ref, gates_ref, gates_t_ref,
                 ext_ref, yq_ref, yk_ref, yv_ref, gpre_ref, xb_ref, xlo_ref):
    i = pl.program_id(0)
    tm = x_ref.shape[1]

    @pl.when(i == 0)
    def _():
        ext_ref[...] = jnp.zeros_like(ext_ref)
        yq_ref[...] = jnp.zeros_like(yq_ref)
        yk_ref[...] = jnp.zeros_like(yk_ref)
        yv_ref[...] = jnp.zeros_like(yv_ref)
        gpre_ref[...] = jnp.zeros_like(gpre_ref)

    ti = jnp.minimum(i, pl.num_programs(0) - 2) % n_tiles
    xf = jnp.where(ti < n_lat_tiles, x_ref[0], ctx_ref[0])
    mod = mod_ref[0]
    gain1 = g_ref[...]
    xn = _rms_mod(xf, gain1, mod[:, 0:d], mod[:, d:2 * d])
    xb = xn.astype(BF16)
    halo = jnp.concatenate([xprev_ref[0], xnext_ref[0]], axis=0)
    xb_ref[0:tm, :] = xb
    xb_ref[tm:, :] = _rms_mod(halo, gain1, mod[:, 0:d], mod[:, d:2 * d]).astype(BF16)
    xlo_ref[...] = (xn - xb.astype(F32)).astype(BF16)

    def project(w_ref, region):
        return jnp.dot(xb_ref[0:tm, :], w_ref[:, region * d:(region + 1) * d], preferred_element_type=F32)

    taps = cw_ref[...]
    conv_bias = cb_ref[...]
    n_q_cols = d // 2

    w2 = 2 * LANES

    def conv_columns(c_lo, c_hi):
        for c0 in range(c_lo, c_hi, w2):
            acc = jnp.zeros((tm, w2), F32) + conv_bias[:, c0:c0 + w2]
            for k in range(CONV_K):
                r0 = SUBLANES - CONV_K // 2 + k
                acc = acc + ext_ref[r0:r0 + tm, c0:c0 + w2] * taps[k:k + 1, c0:c0 + w2]
            out = _silu(acc)
            if c0 < n_q_cols:
                out = out * MLSTM_DQK ** -0.5
            qk_ref[0, :, c0:c0 + w2] = out.astype(BF16)

    cos = cos_ref[...]
    sin = sin_ref[...]
    grp = grp_ref[...]
    gq = gq_ref[...]
    gk = gk_ref[...]
    hw = 2 * DIFF_HEAD_DIM

    def scaled_sumsq(x2):
        ss = jnp.dot((x2 * x2).astype(BF16), grp, preferred_element_type=F32)
        return lax.rsqrt(ss * (1.0 / DIFF_HEAD_DIM) + RMS_EPS)

    def rotated(y):
        return y * cos + pltpu.roll(y, DIFF_HEAD_DIM, axis=1) * sin

    ones_rows = jnp.ones((V_ROWS - hw, tm), BF16)

    def attention_pair(pair):
        sl2 = slice(pair * 2 * hw, (pair + 1) * 2 * hw)
        yq2 = yq_ref[:, sl2]
        yk2 = yk_ref[:, sl2]
        qn = yq2 * scaled_sumsq(yq2)
        kn = yk2 * scaled_sumsq(yk2)
        for u in range(2):
            h = 2 * pair + u
            sl = slice(h * hw, (h + 1) * hw)
            su = slice(u * hw, (u + 1) * hw)
            qt_ref[0, sl, :] = (rotated(qn[:, su] * gq) * Q_SCALE_LOG2).T.astype(BF16)
            kr_ref[0, :, sl] = rotated(kn[:, su] * gk).astype(BF16)
            vt_ref[0, h * V_ROWS:h * V_ROWS + hw, :] = yv_ref[:, sl].T.astype(BF16)
            vt_ref[0, h * V_ROWS + hw:(h + 1) * V_ROWS, :] = ones_rows

    def gate_epilogue():
        g = gpre_ref[...]
        lf = _log_sigmoid(g)
        lf_hi = lf.astype(BF16)
        lf_lo = (lf - lf_hi.astype(F32)).astype(BF16)
        lf2 = jnp.concatenate([lf_hi, lf_lo], axis=1)
        rows = lax.broadcasted_iota(jnp.int32, (tm, tm), 0)
        cols = lax.broadcasted_iota(jnp.int32, (tm, tm), 1)
        lower = jnp.where(cols <= rows, 1.0, 0.0).astype(BF16)
        upper = jnp.where(cols >= rows, 1.0, 0.0).astype(BF16)
        cum_f2 = jnp.dot(lower, lf2, preferred_element_type=F32)
        cum_b2 = jnp.dot(upper, lf2, preferred_element_type=F32)
        cum_f = cum_f2[:, :LANES] + cum_f2[:, LANES:]
        cum_b = cum_b2[:, :LANES] + cum_b2[:, LANES:]
        lane = lax.broadcasted_iota(jnp.int32, (tm, LANES), 1)
        cum = jnp.where(lane < 2 * MLSTM_HEADS, cum_f, cum_b)
        a = g - pltpu.roll(cum, LANES - MLSTM_HEADS, axis=1)
        a = a * LOG2_E
        gates_ref[0] = jnp.where((lane % (2 * MLSTM_HEADS)) < MLSTM_HEADS, a, cum * LOG2_E)
        gates_t_ref[0, 0] = a.T[0:4 * MLSTM_HEADS, :]

    for r, (w_ref, region) in enumerate(((wm_ref, 1), (wm_ref, 2), (wgab_ref, 0), (wgab_ref, 1))):
        pm_ref[0, :, r * d:(r + 1) * d] = project(w_ref, region).astype(BF16)
        attention_pair(r)
    yq_ref[...] = project(watt_ref, 0)
    conv_columns(0, d // 2)
    yk_ref[...] = project(watt_ref, 1)
    conv_columns(d // 2, d)
    yv_ref[...] = project(watt_ref, 2)
    gate_epilogue()
    y_qk = jnp.dot(xb_ref[...], wm_ref[:, 0:d], preferred_element_type=F32)
    has_prev = jnp.logical_and(ti > 0, ti < n_lat_tiles)
    has_next = ti < n_lat_tiles - 1
    ext_ref[0:SUBLANES, :] = jnp.where(has_prev, y_qk[tm:tm + SUBLANES], 0.0)
    ext_ref[SUBLANES:SUBLANES + tm, :] = y_qk[0:tm]
    ext_ref[SUBLANES + tm:, :] = jnp.where(has_next, y_qk[tm + SUBLANES:], 0.0)
    g2 = (jnp.dot(xb_ref[0:tm, :], wg_ref[...], preferred_element_type=F32)
          + jnp.dot(xlo_ref[...], wg_ref[...], preferred_element_type=F32))
    gpre_ref[...] = g2[:, :LANES] + g2[:, LANES:] + gbias_ref[...]


def _projection(x, ctx, mods3, norm1, weights, wg, gate_bias, cos_t, sin_t, gq, gk, grp, conv_w8, conv_b):
    b, s, d = x.shape
    n_ctx = ctx.shape[1]
    tm = TOK_TILE
    assert tm == CHUNK
    assert n_ctx == tm
    n_lat_tiles = s // tm
    n_tiles = (s + n_ctx) // tm
    t = s + n_ctx
    w_main, w_gab, w_att = weights
    resident = lambda shape: pl.BlockSpec(shape, lambda i: (0, 0), pipeline_mode=pl.Buffered(1))
    da_w = DIFF_HEADS * 2 * DIFF_HEAD_DIM
    halo_stride = tm // SUBLANES
    last_halo = s // SUBLANES - 1
    const = lambda shape: pl.BlockSpec(shape, lambda i: (0,) * len(shape))
    n_pairs = b * n_tiles

    def cur(i):
        j = jnp.minimum(i, n_pairs - 1)
        return j // n_tiles, j % n_tiles

    def late(i):
        j = jnp.maximum(i - 1, 0)
        return j // n_tiles, j % n_tiles

    tok = lambda width: pl.BlockSpec((1, tm, width), lambda i: (late(i)[0], late(i)[1], 0))
    tok_t = lambda rows: pl.BlockSpec((1, rows, tm), lambda i: (late(i)[0], 0, late(i)[1]))
    n_ctx_tiles = n_ctx // tm
    return pl.pallas_call(
        functools.partial(_proj_kernel, n_lat_tiles, n_tiles, d),
        grid=(n_pairs + 1,),
        in_specs=[
            pl.BlockSpec((1, tm, d), lambda i: (cur(i)[0], jnp.minimum(cur(i)[1], n_lat_tiles - 1), 0)),
            pl.BlockSpec((1, SUBLANES, d),
                         lambda i: (cur(i)[0], jnp.clip(cur(i)[1] * halo_stride - 1, 0, last_halo), 0)),
            pl.BlockSpec((1, SUBLANES, d),
                         lambda i: (cur(i)[0], jnp.clip((cur(i)[1] + 1) * halo_stride, 0, last_halo), 0)),
            pl.BlockSpec((1, tm, d),
                         lambda i: (cur(i)[0], jnp.clip(cur(i)[1] - n_lat_tiles, 0, n_ctx_tiles - 1), 0)),
            pl.BlockSpec((1, 1, N_MOD * d),
                         lambda i: (jnp.where(cur(i)[1] < n_lat_tiles, cur(i)[0], b), 0, 0)),
            const((1, d)),
            resident(w_main.shape), resident(w_gab.shape), resident(w_att.shape),
            const((d, 2 * LANES)), const((1, LANES)),
            pl.BlockSpec((tm, LANES), lambda i: (late(i)[1], 0)),
            pl.BlockSpec((tm, LANES), lambda i: (late(i)[1], 0)),
            const((1, LANES)), const((1, LANES)), const((2 * LANES, 2 * LANES)),
            const((SUBLANES, d)), const((1, d)),
        ],
        out_specs=[tok(d), pl.BlockSpec((1, tm, 4 * d), lambda i: (cur(i)[0], cur(i)[1], 0)),
                   tok_t(da_w), tok(da_w), tok_t(DIFF_HEADS * V_ROWS), tok(LANES),
                   pl.BlockSpec((1, 1, 4 * MLSTM_HEADS, tm), lambda i: (late(i)[0], late(i)[1], 0, 0))],
        out_shape=[jax.ShapeDtypeStruct((b, t, d), BF16),
                   jax.ShapeDtypeStruct((b, t, 4 * d), BF16),
                   jax.ShapeDtypeStruct((b, da_w, t), BF16),
                   jax.ShapeDtypeStruct((b, t, da_w), BF16),
                   jax.ShapeDtypeStruct((b, DIFF_HEADS * V_ROWS, t), BF16),
                   jax.ShapeDtypeStruct((b, t, LANES), F32),
                   jax.ShapeDtypeStruct((b, n_tiles, 4 * MLSTM_HEADS, tm), F32)],
        scratch_shapes=[pltpu.VMEM((tm + 2 * SUBLANES, d), F32),
                        pltpu.VMEM((tm, da_w), F32), pltpu.VMEM((tm, da_w), F32), pltpu.VMEM((tm, da_w), F32),
                        pltpu.VMEM((tm, LANES), F32),
                        pltpu.VMEM((tm + 2 * SUBLANES, d), BF16),
                        pltpu.VMEM((tm, d), BF16)],
        compiler_params=_cparams(("arbitrary",)),
        name="proj",
    )(x, x, x, ctx, mods3, norm1, w_main, w_gab, w_att, wg, gate_bias, cos_t, sin_t, gq, gk, grp, conv_w8, conv_b)


def _wide(x, n_tiles):
    return jnp.concatenate([x] * n_tiles, axis=1)


def _mlstm_scores(q, k, c_aug):
    s = lax.dot_general(q, k, (((1,), (1,)), ((), ())), preferred_element_type=F32)
    inter = jnp.dot(q, c_aug.astype(BF16), preferred_element_type=F32)
    return s, inter


def _mlstm_decay(reverse, s, k, a_col, a_row, m_prev):
    l = s.shape[0]
    a_rep = jnp.broadcast_to(a_col, (l, LANES))
    rows = lax.broadcasted_iota(jnp.int32, (l, l), 0)
    cols = lax.broadcasted_iota(jnp.int32, (l, l), 1)
    visible = (cols >= rows) if reverse else (cols <= rows)
    a_mat = jnp.where(visible, a_row, -jnp.inf)
    m_rep = jnp.maximum(jnp.broadcast_to(jnp.max(a_mat, axis=-1, keepdims=True), (l, LANES)), m_prev)
    pmat = (s * jnp.exp2(a_mat - _wide(m_rep, l // LANES))).astype(BF16)
    m_last = jnp.maximum(jnp.max(a_rep, axis=0, keepdims=True), m_prev)
    kw = (jnp.exp2(a_rep - m_last) * k.astype(F32)).astype(BF16)
    return pmat, kw, m_rep, m_last


def _mlstm_finish(pmat, kw, v_aug, inter, m_rep, m_last, m_prev, lf_cum_col, lf_total, c_aug):
    l = pmat.shape[0]
    dv = v_aug.shape[1] - LANES
    n_wide = v_aug.shape[1] // LANES
    intra = jnp.dot(pmat, v_aug, preferred_element_type=F32)
    numden = intra + _wide(jnp.exp2(m_prev - m_rep), n_wide) * inter
    den = numden[:, dv:]
    floor = jnp.exp2(-(jnp.broadcast_to(lf_cum_col, (l, LANES)) + m_rep))
    h = numden[:, :dv] * _wide(1.0 / jnp.maximum(jnp.abs(den), floor), dv // LANES)
    kv = lax.dot_general(kw, v_aug, (((0,), (0,)), ((), ())), preferred_element_type=F32)
    c_new = _wide(jnp.exp2(m_prev - m_last), n_wide) * c_aug + kv
    return h, c_new, lf_total + m_last


def _mlstm_kernel(qf_ref, kf_ref, vf_ref, gf_ref, gtf_ref, qb_ref, kb_ref, vb_ref, gb_ref, gtb_ref,
                  hf_ref, hb_ref, c_ref, m_ref):
    s_idx = pl.program_id(1)

    @pl.when(s_idx == 0)
    def _():
        c_ref[...] = jnp.zeros_like(c_ref)
        m_ref[...] = jnp.zeros_like(m_ref)

    nb, l = qf_ref.shape[0], qf_ref.shape[1]
    nh = MLSTM_HEADS
    ones_col = jnp.ones((l, LANES), BF16)
    scans = []
    for bb in range(nb):
        for reverse, q_ref, k_ref, v_ref, g_ref, gt_ref, h_ref in (
                (False, qf_ref, kf_ref, vf_ref, gf_ref, gtf_ref, hf_ref),
                (True, qb_ref, kb_ref, vb_ref, gb_ref, gtb_ref, hb_ref)):
            base = 2 * nh if reverse else 0
            for h in range(nh):
                scans.append(dict(bb=bb, reverse=reverse, h=h, ca=base + h, cc=base + nh + h,
                                  hd=bb * 2 * nh + (nh if reverse else 0) + h,
                                  q_ref=q_ref, k_ref=k_ref, v_ref=v_ref, g_ref=g_ref, gt_ref=gt_ref, h_ref=h_ref))
    qk_cols = lambda sc: slice(sc["h"] * MLSTM_DQK, (sc["h"] + 1) * MLSTM_DQK)
    v_cols = lambda sc: slice(sc["h"] * MLSTM_DV, (sc["h"] + 1) * MLSTM_DV)
    for g0 in range(0, len(scans), MLSTM_GROUP):
        group = scans[g0:g0 + MLSTM_GROUP]
        for sc in group:
            sc["s"], sc["inter"] = _mlstm_scores(sc["q_ref"][sc["bb"], :, qk_cols(sc)],
                                                 sc["k_ref"][sc["bb"], :, qk_cols(sc)], c_ref[sc["hd"]])
        for sc in group:
            g = sc["g_ref"][sc["bb"]]
            g_t = sc["gt_ref"][sc["bb"], 0]
            m_prev = m_ref[sc["hd"]:sc["hd"] + 1, :]
            sc["pmat"], sc["kw"], sc["m_rep"], sc["m_last"] = _mlstm_decay(
                sc["reverse"], sc["s"], sc["k_ref"][sc["bb"], :, qk_cols(sc)],
                g[:, sc["ca"]:sc["ca"] + 1], g_t[sc["ca"]:sc["ca"] + 1, :], m_prev)
        for sc in group:
            g = sc["g_ref"][sc["bb"]]
            cc, hd = sc["cc"], sc["hd"]
            lf_total = g[0:1, cc:cc + 1] if sc["reverse"] else g[l - 1:l, cc:cc + 1]
            v_aug = jnp.concatenate([sc["v_ref"][sc["bb"], :, v_cols(sc)], ones_col], axis=1)
            hval, c_new, m_new = _mlstm_finish(sc["pmat"], sc["kw"], v_aug, sc["inter"], sc["m_rep"],
                                               sc["m_last"], m_ref[hd:hd + 1, :], g[:, cc:cc + 1], lf_total,
                                               c_ref[hd])
            sc["h_ref"][sc["bb"], :, v_cols(sc)] = hval.astype(sc["h_ref"].dtype)
            c_ref[hd] = c_new
            m_ref[hd:hd + 1, :] = m_new


def _mlstm(qk, pm, gates, gates_t, n_lat, n_ctx, v_region):
    b, t, _ = qk.shape
    l = CHUNK
    n_lat_c = n_lat // l
    n_ctx_c = n_ctx // l
    n_steps = n_lat_c + n_ctx_c
    qw = MLSTM_HEADS * MLSTM_DQK
    vw = MLSTM_HEADS * MLSTM_DV

    def fwd(si):
        return jnp.where(si < n_ctx_c, n_lat_c + si, si - n_ctx_c)

    def bwd(si):
        return jnp.where(si < n_ctx_c, n_steps - 1 - si, n_steps - 1 - si)

    nb = MLSTM_BATCH
    assert b % nb == 0
    in_specs = []
    for order in (fwd, bwd):
        in_specs += [
            pl.BlockSpec((nb, l, qw), lambda bi, si, o=order: (bi, o(si), 0)),
            pl.BlockSpec((nb, l, qw), lambda bi, si, o=order: (bi, o(si), 1)),
            pl.BlockSpec((nb, l, vw), lambda bi, si, o=order: (bi, o(si), v_region)),
            pl.BlockSpec((nb, l, LANES), lambda bi, si, o=order: (bi, o(si), 0)),
            pl.BlockSpec((nb, 1, 4 * MLSTM_HEADS, l), lambda bi, si, o=order: (bi, o(si), 0, 0)),
        ]
    out_specs = [pl.BlockSpec((nb, l, vw), lambda bi, si: (bi, fwd(si), 0)),
                 pl.BlockSpec((nb, l, vw), lambda bi, si: (bi, bwd(si), 0))]
    n_scans = nb * 2 * MLSTM_HEADS
    return pl.pallas_call(
        _mlstm_kernel,
        grid=(b // nb, n_steps),
        in_specs=in_specs,
        out_specs=out_specs,
        out_shape=[jax.ShapeDtypeStruct((b, t, vw), BF16)] * 2,
        scratch_shapes=[pltpu.VMEM((n_scans, MLSTM_DQK, MLSTM_DV + LANES), F32),
                        pltpu.VMEM((n_scans, LANES), F32)],
        compiler_params=_cparams(("parallel", "arbitrary")),
        name="mlstm",
    )(qk, qk, pm, gates, gates_t, qk, qk, pm, gates, gates_t)


def _attn_kernel(n_keys, tq, qt_ref, k_ref, vt_ref, lam_ref, gain_ref, o_ref,
                 s_ref, pm_ref, e_ref, al_ref, m_ref, acc_ref):
    tk = ATT_TK
    n = n_keys // tk
    nq = qt_ref.shape[2] // tq
    width = 2 * tq
    row = lax.broadcasted_iota(jnp.int32, (qt_ref.shape[1], tq), 0)
    first = (row % DIFF_HEAD_DIM) < (DIFF_HEAD_DIM // 2)

    acc_ref[...] = jnp.zeros_like(acc_ref)

    def scores(j, qb):
        off = pl.multiple_of(j * tk, tk)
        qt = qt_ref[0, :, pl.ds(pl.multiple_of(qb * tq, tq), tq)]
        zero = jnp.zeros_like(qt)
        qq = jnp.concatenate([jnp.where(first, qt, zero), jnp.where(first, zero, qt)], axis=1)
        s = jnp.dot(k_ref[0, pl.ds(off, tk), :], qq, preferred_element_type=F32)
        s_ref[...] = s
        pm_ref[...] = jnp.max(s.reshape(tk // SUBLANES, SUBLANES, width), axis=0)

    def softmax(j, qb):
        m_old = jnp.where(j == 0, -jnp.inf, m_ref[...])
        m_new = jnp.maximum(m_old, jnp.max(pm_ref[...], axis=0, keepdims=True))
        alpha = jnp.exp2(m_old - m_new)
        e = jnp.exp2(s_ref[...] - m_new)
        e_ref[...] = e.astype(BF16)
        al_ref[...] = alpha
        m_ref[...] = m_new

    def accumulate(j):
        off = pl.multiple_of(j * tk, tk)
        pv = jnp.dot(vt_ref[0, :, pl.ds(off, tk)], e_ref[...], preferred_element_type=F32)
        acc_ref[...] = al_ref[...] * acc_ref[...] + pv

    def finalize(qb):
        lv = lam_ref[...]
        lam = (jnp.exp(jnp.sum(lv[0:1] * lv[1:2], axis=-1, keepdims=True))
               - jnp.exp(jnp.sum(lv[2:3] * lv[3:4], axis=-1, keepdims=True)) + LAM_INIT_L0)
        hw = o_ref.shape[2]
        inv_l = 1.0 / acc_ref[hw:hw + 1, :]
        o_all = acc_ref[0:hw, :] * inv_l
        o_t = o_all[:, :tq] - lam * o_all[:, tq:]
        ms = jnp.mean(o_t * o_t, axis=0, keepdims=True)
        out = (o_t * lax.rsqrt(ms + RMS_EPS)).T * (gain_ref[...] * (1.0 - LAM_INIT_L0))
        o_ref[0, pl.ds(pl.multiple_of(qb * tq, tq), tq), :] = out.astype(o_ref.dtype)

    def advance(j, qb):
        wrap = j == n - 1
        return jnp.where(wrap, 0, j + 1), jnp.where(wrap, qb + 1, qb)

    zero_i = jnp.int32(0)
    item0 = (zero_i, zero_i)
    item1 = advance(*item0)
    scores(*item0)
    softmax(*item0)
    scores(*item1)

    def iteration(t, carry):
        ja, qa, jm, qm, js, qs = carry
        accumulate(ja)
        softmax(jm, qm)
        scores(js, qs)

        @pl.when(ja == n - 1)
        def _():
            finalize(qa)

        return (jm, qm, js, qs) + advance(js, qs)

    item2 = advance(*item1)
    lax.fori_loop(2, nq * n, iteration, item0 + item1 + item2)
    accumulate(n - 2)
    softmax(n - 1, nq - 1)
    accumulate(n - 1)
    finalize(nq - 1)


def _attention(qt, kr, vt, lam_vecs, diff_gain, n_lat):
    b, t, _ = kr.shape
    tq = ATT_TQ
    hw = 2 * DIFF_HEAD_DIM
    nq = n_lat // tq
    assert t % ATT_TK == 0 and t // ATT_TK >= 2
    return pl.pallas_call(
        functools.partial(_attn_kernel, t, tq),
        grid=(b, DIFF_HEADS),
        in_specs=[pl.BlockSpec((1, hw, n_lat), lambda bi, h: (bi, h, 0)),
                  pl.BlockSpec((1, t, hw), lambda bi, h: (bi, 0, h)),
                  pl.BlockSpec((1, V_ROWS, t), lambda bi, h: (bi, h, 0)),
                  pl.BlockSpec(lam_vecs.shape, lambda bi, h: (0, 0)),
                  pl.BlockSpec((1, hw), lambda bi, h: (0, 0))],
        out_specs=pl.BlockSpec((1, n_lat, hw), lambda bi, h: (bi, 0, h)),
        out_shape=jax.ShapeDtypeStruct((b, n_lat, DIFF_HEADS * hw), BF16),
        scratch_shapes=[pltpu.VMEM((ATT_TK, 2 * tq), F32),
                        pltpu.VMEM((SUBLANES, 2 * tq), F32),
                        pltpu.VMEM((ATT_TK, 2 * tq), BF16),
                        pltpu.VMEM((1, 2 * tq), F32),
                        pltpu.VMEM((1, 2 * tq), F32),
                        pltpu.VMEM((V_ROWS, 2 * tq), F32)],
        compiler_params=_cparams(("parallel", "arbitrary")),
        name="attn",
    )(qt, kr, vt, lam_vecs, diff_gain)


def _merge_kernel(d, hf_ref, hb_ref, mo_ref, ga_ref, gb_ref, hB_ref, x_ref, mod_ref, gain_ref,
                  wa_ref, wb_ref, wo_ref, o_ref):
    gain = gain_ref[...]
    yb = jnp.dot(hB_ref[0], wb_ref[...], preferred_element_type=F32)
    ya = None
    for h in range(MLSTM_HEADS):
        sl = slice(h * MLSTM_DV, (h + 1) * MLSTM_DV)
        hs = hf_ref[0, :, sl].astype(F32) + hb_ref[0, :, sl].astype(F32)
        ms = jnp.mean(hs * hs, axis=-1, keepdims=True)
        hn = hs * lax.rsqrt(ms + RMS_EPS) * gain[:, sl]
        h_a = (_sigmoid(mo_ref[0, :, sl].astype(F32)) * hn).astype(BF16)
        part = jnp.dot(h_a, wa_ref[sl, :], preferred_element_type=F32)
        ya = part if ya is None else ya + part
    y = _sigmoid(ga_ref[0].astype(F32)) * ya + _sigmoid(gb_ref[0].astype(F32)) * yb
    yo = jnp.dot(y.astype(BF16), wo_ref[...], preferred_element_type=F32)
    g1 = mod_ref[0][:, 2 * d:3 * d]
    o_ref[0] = x_ref[0] + g1 * yo


def _merge(hf, hb, p, h_b, x, mods3, mlstm_gain, wa, wb, wo, mo_region, ga_region, gb_region):
    b, s, d = x.shape
    tm = LAT_TILE
    wspec = lambda shape: pl.BlockSpec(shape, lambda bi, i: (0, 0), pipeline_mode=pl.Buffered(1))
    tok = lambda region: pl.BlockSpec((1, tm, d), lambda bi, i, r=region: (bi, i, r))
    return pl.pallas_call(
        functools.partial(_merge_kernel, d),
        grid=(b, s // tm),
        in_specs=[tok(0), tok(0), tok(mo_region), tok(ga_region), tok(gb_region), tok(0), tok(0),
                  pl.BlockSpec((1, 1, N_MOD * d), lambda bi, i: (bi, 0, 0)),
                  pl.BlockSpec((1, d), lambda bi, i: (0, 0)),
                  wspec(wa.shape), wspec(wb.shape), wspec(wo.shape)],
        out_specs=pl.BlockSpec((1, tm, d), lambda bi, i: (bi, i, 0)),
        out_shape=jax.ShapeDtypeStruct((b, s, d), F32),
        compiler_params=_cparams(("parallel", "arbitrary")),
        name="merge",
    )(hf, hb, p, p, p, h_b, x, mods3, mlstm_gain, wa, wb, wo)


def _ffn_kernel(d, hidden, x_ref, mod_ref, g_ref, wi_ref, wo_ref, o_ref):
    x1 = x_ref[0]
    mod = mod_ref[0]
    xn = _rms_mod(x1, g_ref[...], mod[:, 3 * d:4 * d], mod[:, 4 * d:5 * d]).astype(BF16)
    a = jnp.dot(xn, wi_ref[:, :hidden], preferred_element_type=F32)
    bb = jnp.dot(xn, wi_ref[:, hidden:], preferred_element_type=F32)
    hmid = (_silu(a) * bb).astype(BF16)
    y = jnp.dot(hmid, wo_ref[...], preferred_element_type=F32)
    o_ref[0] = x1 + mod[:, 5 * d:6 * d] * y


def _ffn(x1, mods3, norm2, wi, wo):
    b, s, d = x1.shape
    hidden = wo.shape[0]
    tm = LAT_TILE
    wspec = lambda shape: pl.BlockSpec(shape, lambda bi, i: (0, 0), pipeline_mode=pl.Buffered(1))
    return pl.pallas_call(
        functools.partial(_ffn_kernel, d, hidden),
        grid=(b, s // tm),
        in_specs=[pl.BlockSpec((1, tm, d), lambda bi, i: (bi, i, 0)),
                  pl.BlockSpec((1, 1, N_MOD * d), lambda bi, i: (bi, 0, 0)),
                  pl.BlockSpec((1, d), lambda bi, i: (0, 0)),
                  wspec(wi.shape), wspec(wo.shape)],
        out_specs=pl.BlockSpec((1, tm, d), lambda bi, i: (bi, i, 0)),
        out_shape=jax.ShapeDtypeStruct((b, s, d), F32),
        compiler_params=_cparams(("parallel", "arbitrary")),
        name="ffn",
    )(x1, mods3, norm2, wi, wo)


def _diff_lane_perm():
    l = np.arange(2 * DIFF_HEAD_DIM)
    half, r = l // DIFF_HEAD_DIM, l % DIFF_HEAD_DIM
    qsel, axis, f = r // 32, (r % 32) // ROPE_FREQS, r % ROPE_FREQS
    return qsel * DIFF_HEAD_DIM + axis * 2 * ROPE_FREQS + half * ROPE_FREQS + f, qsel, axis, f, half


def _rope_tables(n_lat, n_ctx):
    _, _, axis, f, half = _diff_lane_perm()
    pos = np.arange(n_lat)
    row, col = pos // GRID_W, pos % GRID_W
    inv = (ROPE_BASE ** (-np.arange(ROPE_FREQS, dtype=np.float32) / ROPE_FREQS)).astype(np.float32)
    ang = (np.where(axis[None, :] == 0, row[:, None], col[:, None]).astype(np.float32) * inv[f][None, :])
    cos = np.concatenate([np.cos(ang), np.ones((n_ctx, LANES))], axis=0)
    sin = np.concatenate([np.sin(ang) * np.where(half == 0, -1.0, 1.0)[None, :], np.zeros((n_ctx, LANES))], axis=0)
    return jnp.asarray(cos, F32), jnp.asarray(sin, F32)


def kernel(x, c, ctx, c_ctx, w_mod, b_mod, norm1, norm2, w_in, b_gate, conv_w, conv_b, mlstm_norm, q_norm, k_norm,
           lam_vecs, diff_norm, w_branch_a, w_branch_b, w_out, w_ffn_in, w_ffn_out):
    b, s, d = x.shape
    n_ctx = ctx.shape[1]
    assert w_mod.shape[0] == 1, "single-layer block"
    assert s % CHUNK == 0 and n_ctx % CHUNK == 0 and s % ATT_TQ == 0 and s % LAT_TILE == 0
    qk_w = MLSTM_HEADS * MLSTM_DQK
    v_w = MLSTM_HEADS * MLSTM_DV
    da_w = DIFF_HEADS * 2 * DIFF_HEAD_DIM
    n_gates = 4 * MLSTM_HEADS
    sizes = (qk_w, qk_w, v_w, v_w, n_gates, da_w, da_w, da_w, d, d)
    offs = np.concatenate([[0], np.cumsum(sizes)])
    seg = lambda i: w_in[0][:, offs[i]:offs[i + 1]]

    perm, qsel, _, _, _ = _diff_lane_perm()
    head_perm = (np.arange(DIFF_HEADS)[:, None] * 2 * DIFF_HEAD_DIM + perm[None, :]).reshape(-1)
    w_main = w_in[0][:, offs[0]:offs[4]].astype(BF16)
    w_gab = w_in[0][:, offs[8]:offs[10]].astype(BF16)
    segb = lambda i: seg(i).astype(BF16)
    w_att = jnp.concatenate([segb(5)[:, head_perm], segb(6)[:, head_perm], segb(7)], axis=1)
    R_V, R_O, R_GA, R_GB = range(4)
    wg_f32 = jnp.pad(seg(4), ((0, 0), (0, LANES - n_gates)))
    wg_hi = wg_f32.astype(BF16)
    wg = jnp.concatenate([wg_hi, (wg_f32 - wg_hi.astype(F32)).astype(BF16)], axis=1)
    gate_bias = jnp.pad(b_gate[0], (0, LANES - n_gates)).reshape(1, LANES)
    conv_w8 = jnp.pad(conv_w[0], ((0, SUBLANES - CONV_K), (0, 0)))
    gq = q_norm[0][perm % DIFF_HEAD_DIM].reshape(1, LANES)
    gk = k_norm[0][perm % DIFF_HEAD_DIM].reshape(1, LANES)
    unit = np.concatenate([qsel, 2 + qsel])
    grp = jnp.asarray(unit[:, None] == unit[None, :], BF16)
    cos_t, sin_t = _rope_tables(s, n_ctx)

    cc = jnp.concatenate([c, c_ctx[None, :], jnp.zeros((SUBLANES - b - 1, d), F32)], axis=0)
    mods = _modulation(cc, w_mod[0], b_mod[0])
    mods3 = mods.reshape(SUBLANES, 1, N_MOD * d)

    qk, pm, qt, kr, vt, gates, gates_t = _projection(x, ctx, mods3, norm1, (w_main, w_gab, w_att), wg, gate_bias,
                                                     cos_t, sin_t, gq, gk, grp, conv_w8, conv_b)
    hf, hb = _mlstm(qk, pm, gates, gates_t, s, n_ctx, R_V)
    h_b = _attention(qt, kr, vt, lam_vecs[0], diff_norm, s)
    x1 = _merge(hf, hb, pm, h_b, x, mods3, mlstm_norm, w_branch_a[0].astype(BF16), w_branch_b[0].astype(BF16),
                w_out[0].astype(BF16), R_O, R_GA, R_GB)
    return _ffn(x1, mods3, norm2, w_ffn_in[0].astype(BF16), w_ffn_out[0].astype(BF16))
```

```python
import functools
import math

import jax
import jax.numpy as jnp
import numpy as np
from jax import lax
from jax.experimental import pallas as pl
from jax.experimental.pallas import tpu as pltpu

F32 = jnp.float32
BF16 = jnp.bfloat16
HIGHEST = lax.Precision.HIGHEST

GRID_W = 64
N_MOD = 6
MLSTM_HEADS = 4
MLSTM_DQK = 128
MLSTM_DV = 256
CONV_K = 5
DIFF_HEADS = 8
DIFF_HEAD_DIM = 64
ROPE_BASE = 10000.0
ROPE_FREQS = DIFF_HEAD_DIM // 4
RMS_EPS = 1e-6
LAM_INIT_L0 = 0.8 - 0.6 * math.exp(-0.3 * 0)
LOG2_E = math.log2(math.e)
Q_SCALE_LOG2 = DIFF_HEAD_DIM ** -0.5 * LOG2_E
LANES = 128
SUBLANES = 8
BF16_SUBLANES = 16
V_ROWS = 2 * DIFF_HEAD_DIM + BF16_SUBLANES
VMEM_LIMIT = 56 * 1024 * 1024

TOK_TILE = 256
LAT_TILE = 512
CHUNK = 256
MLSTM_BATCH = 4
MLSTM_GROUP = 2
MLSTM_AHEAD = 1
ATT_TQ = 1024
ATT_TK = 2816


def _cparams(sem):
    return pltpu.CompilerParams(dimension_semantics=sem, vmem_limit_bytes=VMEM_LIMIT)


def _sigmoid(x):
    return 0.5 + 0.5 * jnp.tanh(0.5 * x)


def _silu(x):
    h = 0.5 * x
    return h + h * jnp.tanh(h)


def _log_sigmoid(x):
    return jnp.minimum(x, 0.0) - jnp.log(1.0 + jnp.exp(-jnp.abs(x)))


def _mod_kernel(c_ref, w_ref, b_ref, o_ref):
    a = _silu(c_ref[...])
    o_ref[...] = jnp.dot(a, w_ref[...], preferred_element_type=F32, precision=HIGHEST) + b_ref[...]


def _modulation(cc, w_mod, b_mod):
    rows, d = cc.shape
    n = w_mod.shape[1]
    tn = 1024
    return pl.pallas_call(
        _mod_kernel,
        grid=(n // tn,),
        in_specs=[pl.BlockSpec((rows, d), lambda j: (0, 0)),
                  pl.BlockSpec((d, tn), lambda j: (0, j)),
                  pl.BlockSpec((1, tn), lambda j: (0, j))],
        out_specs=pl.BlockSpec((rows, tn), lambda j: (0, j)),
        out_shape=jax.ShapeDtypeStruct((rows, n), F32),
        compiler_params=_cparams(("arbitrary",)),
        name="mod",
    )(cc, w_mod, b_mod.reshape(1, n))


def _rms_mod(xf, g, shift, scale):
    ms = jnp.mean(xf * xf, axis=-1, keepdims=True)
    return xf * lax.rsqrt(ms + RMS_EPS) * g * (1.0 + scale) + shift


def _proj_kernel(n_lat_tiles, n_tiles, d, x_ref, xprev_ref, xnext_ref, ctx_ref, mod_ref, g_ref,
                 wm_ref, wgab_ref, watt_ref, wg_ref, gbias_ref,
                 cos_ref, sin_ref, gq_ref, gk_ref, grp_ref, cw_ref, cb_ref,
                 qk_ref, pm_ref, qt_ref, kr_ref, vt_ref, gates_ref, gates_t_ref,
                 ext_ref, yq_ref, yk_ref, yv_ref, gpre_ref, xb_ref, xlo_ref):
    i = pl.program_id(0)
    tm = x_ref.shape[1]

    @pl.when(i == 0)
    def _():
        ext_ref[...] = jnp.zeros_like(ext_ref)
        yq_ref[...] = jnp.zeros_like(yq_ref)
        yk_ref[...] = jnp.zeros_like(yk_ref)
        yv_ref[...] = jnp.zeros_like(yv_ref)
        gpre_ref[...] = jnp.zeros_like(gpre_ref)

    ti = jnp.minimum(i, pl.num_programs(0) - 2) % n_tiles
    xf = jnp.where(ti < n_lat_tiles, x_ref[0], ctx_ref[0])
    mod = mod_ref[0]
    gain1 = g_ref[...]
    xn = _rms_mod(xf, gain1, mod[:, 0:d], mod[:, d:2 * d])
    xb = xn.astype(BF16)
    halo = jnp.concatenate([xprev_ref[0], xnext_ref[0]], axis=0)
    xb_ref[0:tm, :] = xb
    xb_ref[tm:, :] = _rms_mod(halo, gain1, mod[:, 0:d], mod[:, d:2 * d]).astype(BF16)
    xlo_ref[...] = (xn - xb.astype(F32)).astype(BF16)

    def project(w_ref, region):
        return jnp.dot(xb_ref[0:tm, :], w_ref[:, region * d:(region + 1) * d], preferred_element_type=F32)

    taps = cw_ref[...]
    conv_bias = cb_ref[...]
    n_q_cols = d // 2

    w2 = 2 * LANES

    def conv_columns(c_lo, c_hi):
        for c0 in range(c_lo, c_hi, w2):
            acc = jnp.zeros((tm, w2), F32) + conv_bias[:, c0:c0 + w2]
            for k in range(CONV_K):
                r0 = SUBLANES - CONV_K // 2 + k
                acc = acc + ext_ref[r0:r0 + tm, c0:c0 + w2] * taps[k:k + 1, c0:c0 + w2]
            out = _silu(acc)
            if c0 < n_q_cols:
                out = out * MLSTM_DQK ** -0.5
            qk_ref[0, :, c0:c0 + w2] = out.astype(BF16)

    cos = cos_ref[...]
    sin = sin_ref[...]
    grp = grp_ref[...]
    gq = gq_ref[...]
    gk = gk_ref[...]
    hw = 2 * DIFF_HEAD_DIM

    def scaled_sumsq(x2):
        ss = jnp.dot((x2 * x2).astype(BF16), grp, preferred_element_type=F32)
        return lax.rsqrt(ss * (1.0 / DIFF_HEAD_DIM) + RMS_EPS)

    def rotated(y):
        return y * cos + pltpu.roll(y, DIFF_HEAD_DIM, axis=1) * sin

    ones_rows = jnp.ones((V_ROWS - hw, tm), BF16)

    def attention_pair(pair):
        sl2 = slice(pair * 2 * hw, (pair + 1) * 2 * hw)
        yq2 = yq_ref[:, sl2]
        yk2 = yk_ref[:, sl2]
        qn = yq2 * scaled_sumsq(yq2)
        kn = yk2 * scaled_sumsq(yk2)
        for u in range(2):
            h = 2 * pair + u
            sl = slice(h * hw, (h + 1) * hw)
            su = slice(u * hw, (u + 1) * hw)
            qt_ref[0, sl, :] = (rotated(qn[:, su] * gq) * Q_SCALE_LOG2).T.astype(BF16)
            kr_ref[0, :, sl] = rotated(kn[:, su] * gk).astype(BF16)
            vt_ref[0, h * V_ROWS:h * V_ROWS + hw, :] = yv_ref[:, sl].T.astype(BF16)
            vt_ref[0, h * V_ROWS + hw:(h + 1) * V_ROWS, :] = ones_rows

    def gate_epilogue():
        g = gpre_ref[...]
        lf = _log_sigmoid(g)
        lf_hi = lf.astype(BF16)
        lf_lo = (lf - lf_hi.astype(F32)).astype(BF16)
        lf2 = jnp.concatenate([lf_hi, lf_lo], axis=1)
        rows = lax.broadcasted_iota(jnp.int32, (tm, tm), 0)
        cols = lax.broadcasted_iota(jnp.int32, (tm, tm), 1)
        lower = jnp.where(cols <= rows, 1.0, 0.0).astype(BF16)
        upper = jnp.where(cols >= rows, 1.0, 0.0).astype(BF16)
        cum_f2 = jnp.dot(lower, lf2, preferred_element_type=F32)
        cum_b2 = jnp.dot(upper, lf2, preferred_element_type=F32)
        cum_f = cum_f2[:, :LANES] + cum_f2[:, LANES:]
        cum_b = cum_b2[:, :LANES] + cum_b2[:, LANES:]
        lane = lax.broadcasted_iota(jnp.int32, (tm, LANES), 1)
        cum = jnp.where(lane < 2 * MLSTM_HEADS, cum_f, cum_b)
        a = g - pltpu.roll(cum, LANES - MLSTM_HEADS, axis=1)
        a = a * LOG2_E
        gates_ref[0] = jnp.where((lane % (2 * MLSTM_HEADS)) < MLSTM_HEADS, a, cum * LOG2_E)
        gates_t_ref[0, 0] = a.T[0:4 * MLSTM_HEADS, :]

    for r, (w_ref, region) in enumerate(((wm_ref, 1), (wm_ref, 2), (wgab_ref, 0), (wgab_ref, 1))):
        pm_ref[0, :, r * d:(r + 1) * d] = project(w_ref, region).astype(BF16)
        attention_pair(r)
    yq_ref[...] = project(watt_ref, 0)
    conv_columns(0, d // 2)
    yk_ref[...] = project(watt_ref, 1)
    conv_columns(d // 2, d)
    yv_ref[...] = project(watt_ref, 2)
    gate_epilogue()
    y_qk = jnp.dot(xb_ref[...], wm_ref[:, 0:d], preferred_element_type=F32)
    has_prev = jnp.logical_and(ti > 0, ti < n_lat_tiles)
    has_next = ti < n_lat_tiles - 1
    ext_ref[0:SUBLANES, :] = jnp.where(has_prev, y_qk[tm:tm + SUBLANES], 0.0)
    ext_ref[SUBLANES:SUBLANES + tm, :] = y_qk[0:tm]
    ext_ref[SUBLANES + tm:, :] = jnp.where(has_next, y_qk[tm + SUBLANES:], 0.0)
    g2 = (jnp.dot(xb_ref[0:tm, :], wg_ref[...], preferred_element_type=F32)
          + jnp.dot(xlo_ref[...], wg_ref[...], preferred_element_type=F32))
    gpre_ref[...] = g2[:, :LANES] + g2[:, LANES:] + gbias_ref[...]


def _projection(x, ctx, mods3, norm1, weights, wg, gate_bias, cos_t, sin_t, gq, gk, grp, conv_w8, conv_b):
    b, s, d = x.shape
    n_ctx = ctx.shape[1]
    tm = TOK_TILE
    assert tm == CHUNK
    assert n_ctx == tm
    n_lat_tiles = s // tm
    n_tiles = (s + n_ctx) // tm
    t = s + n_ctx
    w_main, w_gab, w_att = weights
    resident = lambda shape: pl.BlockSpec(shape, lambda i: (0, 0), pipeline_mode=pl.Buffered(1))
    da_w = DIFF_HEADS * 2 * DIFF_HEAD_DIM
    halo_stride = tm // SUBLANES
    last_halo = s // SUBLANES - 1
    const = lambda shape: pl.BlockSpec(shape, lambda i: (0,) * len(shape))
    n_pairs = b * n_tiles

    def cur(i):
        j = jnp.minimum(i, n_pairs - 1)
        return j // n_tiles, j % n_tiles

    def late(i):
        j = jnp.maximum(i - 1, 0)
        return j // n_tiles, j % n_tiles

    tok = lambda width: pl.BlockSpec((1, tm, width), lambda i: (late(i)[0], late(i)[1], 0))
    tok_t = lambda rows: pl.BlockSpec((1, rows, tm), lambda i: (late(i)[0], 0, late(i)[1]))
    n_ctx_tiles = n_ctx // tm
    return pl.pallas_call(
        functools.partial(_proj_kernel, n_lat_tiles, n_tiles, d),
        grid=(n_pairs + 1,),
        in_specs=[
            pl.BlockSpec((1, tm, d), lambda i: (cur(i)[0], jnp.minimum(cur(i)[1], n_lat_tiles - 1), 0)),
            pl.BlockSpec((1, SUBLANES, d),
                         lambda i: (cur(i)[0], jnp.clip(cur(i)[1] * halo_stride - 1, 0, last_halo), 0)),
            pl.BlockSpec((1, SUBLANES, d),
                         lambda i: (cur(i)[0], jnp.clip((cur(i)[1] + 1) * halo_stride, 0, last_halo), 0)),
            pl.BlockSpec((1, tm, d),
                         lambda i: (cur(i)[0], jnp.clip(cur(i)[1] - n_lat_tiles, 0, n_ctx_tiles - 1), 0)),
            pl.BlockSpec((1, 1, N_MOD * d),
                         lambda i: (jnp.where(cur(i)[1] < n_lat_tiles, cur(i)[0], b), 0, 0)),
            const((1, d)),
            resident(w_main.shape), resident(w_gab.shape), resident(w_att.shape),
            const((d, 2 * LANES)), const((1, LANES)),
            pl.BlockSpec((tm, LANES), lambda i: (late(i)[1], 0)),
            pl.BlockSpec((tm, LANES), lambda i: (late(i)[1], 0)),
            const((1, LANES)), const((1, LANES)), const((2 * LANES, 2 * LANES)),
            const((SUBLANES, d)), const((1, d)),
        ],
        out_specs=[tok(d), pl.BlockSpec((1, tm, 4 * d), lambda i: (cur(i)[0], cur(i)[1], 0)),
                   tok_t(da_w), tok(da_w), tok_t(DIFF_HEADS * V_ROWS), tok(LANES),
                   pl.BlockSpec((1, 1, 4 * MLSTM_HEADS, tm), lambda i: (late(i)[0], late(i)[1], 0, 0))],
        out_shape=[jax.ShapeDtypeStruct((b, t, d), BF16),
                   jax.ShapeDtypeStruct((b, t, 4 * d), BF16),
                   jax.ShapeDtypeStruct((b, da_w, t), BF16),
                   jax.ShapeDtypeStruct((b, t, da_w), BF16),
                   jax.ShapeDtypeStruct((b, DIFF_HEADS * V_ROWS, t), BF16),
                   jax.ShapeDtypeStruct((b, t, LANES), F32),
                   jax.ShapeDtypeStruct((b, n_tiles, 4 * MLSTM_HEADS, tm), F32)],
        scratch_shapes=[pltpu.VMEM((tm + 2 * SUBLANES, d), F32),
                        pltpu.VMEM((tm, da_w), F32), pltpu.VMEM((tm, da_w), F32), pltpu.VMEM((tm, da_w), F32),
                        pltpu.VMEM((tm, LANES), F32),
                        pltpu.VMEM((tm + 2 * SUBLANES, d), BF16),
                        pltpu.VMEM((tm, d), BF16)],
        compiler_params=_cparams(("arbitrary",)),
        name="proj",
    )(x, x, x, ctx, mods3, norm1, w_main, w_gab, w_att, wg, gate_bias, cos_t, sin_t, gq, gk, grp, conv_w8, conv_b)


def _wide(x, n_tiles):
    return jnp.concatenate([x] * n_tiles, axis=1)


def _mlstm_scores(q, k, c_aug):
    s = lax.dot_general(q, k, (((1,), (1,)), ((), ())), preferred_element_type=F32)
    inter = jnp.dot(q, c_aug.astype(BF16), preferred_element_type=F32)
    return s, inter


def _mlstm_decay(reverse, s, k, a_col, a_row, m_prev):
    l = s.shape[0]
    a_rep = jnp.broadcast_to(a_col, (l, LANES))
    rows = lax.broadcasted_iota(jnp.int32, (l, l), 0)
    cols = lax.broadcasted_iota(jnp.int32, (l, l), 1)
    visible = (cols >= rows) if reverse else (cols <= rows)
    a_mat = jnp.where(visible, a_row, -jnp.inf)
    m_rep = jnp.maximum(jnp.broadcast_to(jnp.max(a_mat, axis=-1, keepdims=True), (l, LANES)), m_prev)
    pmat = (s * jnp.exp2(a_mat - _wide(m_rep, l // LANES))).astype(BF16)
    m_last = jnp.maximum(jnp.max(a_rep, axis=0, keepdims=True), m_prev)
    kw = (jnp.exp2(a_rep - m_last) * k.astype(F32)).astype(BF16)
    return pmat, kw, m_rep, m_last


def _mlstm_finish(pmat, kw, v_aug, inter, m_rep, m_last, m_prev, lf_cum_col, lf_total, c_aug):
    l = pmat.shape[0]
    dv = v_aug.shape[1] - LANES
    n_wide = v_aug.shape[1] // LANES
    intra = jnp.dot(pmat, v_aug, preferred_element_type=F32)
    numden = intra + _wide(jnp.exp2(m_prev - m_rep), n_wide) * inter
    den = numden[:, dv:]
    floor = jnp.exp2(-(jnp.broadcast_to(lf_cum_col, (l, LANES)) + m_rep))
    h = numden[:, :dv] * _wide(1.0 / jnp.maximum(jnp.abs(den), floor), dv // LANES)
    kv = lax.dot_general(kw, v_aug, (((0,), (0,)), ((), ())), preferred_element_type=F32)
    c_new = _wide(jnp.exp2(m_prev - m_last), n_wide) * c_aug + kv
    return h, c_new, lf_total + m_last


def _mlstm_kernel(qf_ref, kf_ref, vf_ref, gf_ref, gtf_ref, qb_ref, kb_ref, vb_ref, gb_ref, gtb_ref,
                  hf_ref, hb_ref, c_ref, m_ref):
    s_idx = pl.program_id(1)

    @pl.when(s_idx == 0)
    def _():
        c_ref[...] = jnp.zeros_like(c_ref)
        m_ref[...] = jnp.zeros_like(m_ref)

    nb, l = qf_ref.shape[0], qf_ref.shape[1]
    nh = MLSTM_HEADS
    ones_col = jnp.ones((l, LANES), BF16)
    scans = []
    for bb in range(nb):
        for reverse, q_ref, k_ref, v_ref, g_ref, gt_ref, h_ref in (
                (False, qf_ref, kf_ref, vf_ref, gf_ref, gtf_ref, hf_ref),
                (True, qb_ref, kb_ref, vb_ref, gb_ref, gtb_ref, hb_ref)):
            base = 2 * nh if reverse else 0
            for h in range(nh):
                scans.append(dict(bb=bb, reverse=reverse, h=h, ca=base + h, cc=base + nh + h,
                                  hd=bb * 2 * nh + (nh if reverse else 0) + h,
                                  q_ref=q_ref, k_ref=k_ref, v_ref=v_ref, g_ref=g_ref, gt_ref=gt_ref, h_ref=h_ref))
    qk_cols = lambda sc: slice(sc["h"] * MLSTM_DQK, (sc["h"] + 1) * MLSTM_DQK)
    v_cols = lambda sc: slice(sc["h"] * MLSTM_DV, (sc["h"] + 1) * MLSTM_DV)
    groups = [scans[g0:g0 + MLSTM_GROUP] for g0 in range(0, len(scans), MLSTM_GROUP)]

    def issue_scores(group):
        for sc in group:
            sc["s"], sc["inter"] = _mlstm_scores(sc["q_ref"][sc["bb"], :, qk_cols(sc)],
                                                 sc["k_ref"][sc["bb"], :, qk_cols(sc)], c_ref[sc["hd"]])

    for ahead in range(MLSTM_AHEAD):
        issue_scores(groups[ahead])
    for gi, group in enumerate(groups):
        if gi + MLSTM_AHEAD < len(groups):
            issue_scores(groups[gi + MLSTM_AHEAD])
        for sc in group:
            g = sc["g_ref"][sc["bb"]]
            g_t = sc["gt_ref"][sc["bb"], 0]
            m_prev = m_ref[sc["hd"]:sc["hd"] + 1, :]
            sc["pmat"], sc["kw"], sc["m_rep"], sc["m_last"] = _mlstm_decay(
                sc["reverse"], sc["s"], sc["k_ref"][sc["bb"], :, qk_cols(sc)],
                g[:, sc["ca"]:sc["ca"] + 1], g_t[sc["ca"]:sc["ca"] + 1, :], m_prev)
        for sc in group:
            g = sc["g_ref"][sc["bb"]]
            cc, hd = sc["cc"], sc["hd"]
            lf_total = g[0:1, cc:cc + 1] if sc["reverse"] else g[l - 1:l, cc:cc + 1]
            v_aug = jnp.concatenate([sc["v_ref"][sc["bb"], :, v_cols(sc)], ones_col], axis=1)
            hval, c_new, m_new = _mlstm_finish(sc["pmat"], sc["kw"], v_aug, sc["inter"], sc["m_rep"],
                                               sc["m_last"], m_ref[hd:hd + 1, :], g[:, cc:cc + 1], lf_total,
                                               c_ref[hd])
            sc["h_ref"][sc["bb"], :, v_cols(sc)] = hval.astype(sc["h_ref"].dtype)
            c_ref[hd] = c_new
            m_ref[hd:hd + 1, :] = m_new


def _mlstm(qk, pm, gates, gates_t, n_lat, n_ctx, v_region):
    b, t, _ = qk.shape
    l = CHUNK
    n_lat_c = n_lat // l
    n_ctx_c = n_ctx // l
    n_steps = n_lat_c + n_ctx_c
    qw = MLSTM_HEADS * MLSTM_DQK
    vw = MLSTM_HEADS * MLSTM_DV

    def fwd(si):
        return jnp.where(si < n_ctx_c, n_lat_c + si, si - n_ctx_c)

    def bwd(si):
        return jnp.where(si < n_ctx_c, n_steps - 1 - si, n_steps - 1 - si)

    nb = MLSTM_BATCH
    assert b % nb == 0
    in_specs = []
    for order in (fwd, bwd):
        in_specs += [
            pl.BlockSpec((nb, l, qw), lambda bi, si, o=order: (bi, o(si), 0)),
            pl.BlockSpec((nb, l, qw), lambda bi, si, o=order: (bi, o(si), 1)),
            pl.BlockSpec((nb, l, vw), lambda bi, si, o=order: (bi, o(si), v_region)),
            pl.BlockSpec((nb, l, LANES), lambda bi, si, o=order: (bi, o(si), 0)),
            pl.BlockSpec((nb, 1, 4 * MLSTM_HEADS, l), lambda bi, si, o=order: (bi, o(si), 0, 0)),
        ]
    out_specs = [pl.BlockSpec((nb, l, vw), lambda bi, si: (bi, fwd(si), 0)),
                 pl.BlockSpec((nb, l, vw), lambda bi, si: (bi, bwd(si), 0))]
    n_scans = nb * 2 * MLSTM_HEADS
    return pl.pallas_call(
        _mlstm_kernel,
        grid=(b // nb, n_steps),
        in_specs=in_specs,
        out_specs=out_specs,
        out_shape=[jax.ShapeDtypeStruct((b, t, vw), BF16)] * 2,
        scratch_shapes=[pltpu.VMEM((n_scans, MLSTM_DQK, MLSTM_DV + LANES), F32),
                        pltpu.VMEM((n_scans, LANES), F32)],
        compiler_params=_cparams(("parallel", "arbitrary")),
        name="mlstm",
    )(qk, qk, pm, gates, gates_t, qk, qk, pm, gates, gates_t)


def _attn_kernel(n_keys, tq, qt_ref, k_ref, vt_ref, lam_ref, gain_ref, o_ref,
                 s_ref, pm_ref, e_ref, al_ref, m_ref, acc_ref):
    tk = ATT_TK
    n = n_keys // tk
    nq = qt_ref.shape[2] // tq
    width = 2 * tq
    row = lax.broadcasted_iota(jnp.int32, (qt_ref.shape[1], tq), 0)
    first = (row % DIFF_HEAD_DIM) < (DIFF_HEAD_DIM // 2)

    acc_ref[...] = jnp.zeros_like(acc_ref)

    def scores(j, qb):
        off = pl.multiple_of(j * tk, tk)
        qt = qt_ref[0, :, pl.ds(pl.multiple_of(qb * tq, tq), tq)]
        zero = jnp.zeros_like(qt)
        qq = jnp.concatenate([jnp.where(first, qt, zero), jnp.where(first, zero, qt)], axis=1)
        s = jnp.dot(k_ref[0, pl.ds(off, tk), :], qq, preferred_element_type=F32)
        s_ref[...] = s
        pm_ref[...] = jnp.max(s.reshape(tk // SUBLANES, SUBLANES, width), axis=0)

    def softmax(j, qb):
        m_old = jnp.where(j == 0, -jnp.inf, m_ref[...])
        m_new = jnp.maximum(m_old, jnp.max(pm_ref[...], axis=0, keepdims=True))
        alpha = jnp.exp2(m_old - m_new)
        e = jnp.exp2(s_ref[...] - m_new)
        e_ref[...] = e.astype(BF16)
        al_ref[...] = alpha
        m_ref[...] = m_new

    def accumulate(j):
        off = pl.multiple_of(j * tk, tk)
        pv = jnp.dot(vt_ref[0, :, pl.ds(off, tk)], e_ref[...], preferred_element_type=F32)
        acc_ref[...] = al_ref[...] * acc_ref[...] + pv

    def finalize(qb):
        lv = lam_ref[...]
        lam = (jnp.exp(jnp.sum(lv[0:1] * lv[1:2], axis=-1, keepdims=True))
               - jnp.exp(jnp.sum(lv[2:3] * lv[3:4], axis=-1, keepdims=True)) + LAM_INIT_L0)
        hw = o_ref.shape[2]
        inv_l = 1.0 / acc_ref[hw:hw + 1, :]
        o_all = acc_ref[0:hw, :] * inv_l
        o_t = o_all[:, :tq] - lam * o_all[:, tq:]
        ms = jnp.mean(o_t * o_t, axis=0, keepdims=True)
        out = (o_t * lax.rsqrt(ms + RMS_EPS)).T * (gain_ref[...] * (1.0 - LAM_INIT_L0))
        o_ref[0, pl.ds(pl.multiple_of(qb * tq, tq), tq), :] = out.astype(o_ref.dtype)

    def advance(j, qb):
        wrap = j == n - 1
        return jnp.where(wrap, 0, j + 1), jnp.where(wrap, qb + 1, qb)

    zero_i = jnp.int32(0)
    item0 = (zero_i, zero_i)
    item1 = advance(*item0)
    scores(*item0)
    softmax(*item0)
    scores(*item1)

    def iteration(t, carry):
        ja, qa, jm, qm, js, qs = carry
        accumulate(ja)
        softmax(jm, qm)
        scores(js, qs)

        @pl.when(ja == n - 1)
        def _():
            finalize(qa)

        return (jm, qm, js, qs) + advance(js, qs)

    item2 = advance(*item1)
    lax.fori_loop(2, nq * n, iteration, item0 + item1 + item2)
    accumulate(n - 2)
    softmax(n - 1, nq - 1)
    accumulate(n - 1)
    finalize(nq - 1)


def _attention(qt, kr, vt, lam_vecs, diff_gain, n_lat):
    b, t, _ = kr.shape
    tq = ATT_TQ
    hw = 2 * DIFF_HEAD_DIM
    nq = n_lat // tq
    assert t % ATT_TK == 0 and t // ATT_TK >= 2
    return pl.pallas_call(
        functools.partial(_attn_kernel, t, tq),
        grid=(b, DIFF_HEADS),
        in_specs=[pl.BlockSpec((1, hw, n_lat), lambda bi, h: (bi, h, 0)),
                  pl.BlockSpec((1, t, hw), lambda bi, h: (bi, 0, h)),
                  pl.BlockSpec((1, V_ROWS, t), lambda bi, h: (bi, h, 0)),
                  pl.BlockSpec(lam_vecs.shape, lambda bi, h: (0, 0)),
                  pl.BlockSpec((1, hw), lambda bi, h: (0, 0))],
        out_specs=pl.BlockSpec((1, n_lat, hw), lambda bi, h: (bi, 0, h)),
        out_shape=jax.ShapeDtypeStruct((b, n_lat, DIFF_HEADS * hw), BF16),
        scratch_shapes=[pltpu.VMEM((ATT_TK, 2 * tq), F32),
                        pltpu.VMEM((SUBLANES, 2 * tq), F32),
                        pltpu.VMEM((ATT_TK, 2 * tq), BF16),
                        pltpu.VMEM((1, 2 * tq), F32),
                        pltpu.VMEM((1, 2 * tq), F32),
                        pltpu.VMEM((V_ROWS, 2 * tq), F32)],
        compiler_params=_cparams(("parallel", "arbitrary")),
        name="attn",
    )(qt, kr, vt, lam_vecs, diff_gain)


def _merge_kernel(d, hf_ref, hb_ref, mo_ref, ga_ref, gb_ref, hB_ref, x_ref, mod_ref, gain_ref,
                  wa_ref, wb_ref, wo_ref, o_ref):
    gain = gain_ref[...]
    yb = jnp.dot(hB_ref[0], wb_ref[...], preferred_element_type=F32)
    ya = None
    for h in range(MLSTM_HEADS):
        sl = slice(h * MLSTM_DV, (h + 1) * MLSTM_DV)
        hs = hf_ref[0, :, sl].astype(F32) + hb_ref[0, :, sl].astype(F32)
        ms = jnp.mean(hs * hs, axis=-1, keepdims=True)
        hn = hs * lax.rsqrt(ms + RMS_EPS) * gain[:, sl]
        h_a = (_sigmoid(mo_ref[0, :, sl].astype(F32)) * hn).astype(BF16)
        part = jnp.dot(h_a, wa_ref[sl, :], preferred_element_type=F32)
        ya = part if ya is None else ya + part
    y = _sigmoid(ga_ref[0].astype(F32)) * ya + _sigmoid(gb_ref[0].astype(F32)) * yb
    yo = jnp.dot(y.astype(BF16), wo_ref[...], preferred_element_type=F32)
    g1 = mod_ref[0][:, 2 * d:3 * d]
    o_ref[0] = x_ref[0] + g1 * yo


def _merge(hf, hb, p, h_b, x, mods3, mlstm_gain, wa, wb, wo, mo_region, ga_region, gb_region):
    b, s, d = x.shape
    tm = LAT_TILE
    wspec = lambda shape: pl.BlockSpec(shape, lambda bi, i: (0, 0), pipeline_mode=pl.Buffered(1))
    tok = lambda region: pl.BlockSpec((1, tm, d), lambda bi, i, r=region: (bi, i, r))
    return pl.pallas_call(
        functools.partial(_merge_kernel, d),
        grid=(b, s // tm),
        in_specs=[tok(0), tok(0), tok(mo_region), tok(ga_region), tok(gb_region), tok(0), tok(0),
                  pl.BlockSpec((1, 1, N_MOD * d), lambda bi, i: (bi, 0, 0)),
                  pl.BlockSpec((1, d), lambda bi, i: (0, 0)),
                  wspec(wa.shape), wspec(wb.shape), wspec(wo.shape)],
        out_specs=pl.BlockSpec((1, tm, d), lambda bi, i: (bi, i, 0)),
        out_shape=jax.ShapeDtypeStruct((b, s, d), F32),
        compiler_params=_cparams(("parallel", "arbitrary")),
        name="merge",
    )(hf, hb, p, p, p, h_b, x, mods3, mlstm_gain, wa, wb, wo)


def _ffn_kernel(d, hidden, x_ref, mod_ref, g_ref, wi_ref, wo_ref, o_ref):
    x1 = x_ref[0]
    mod = mod_ref[0]
    xn = _rms_mod(x1, g_ref[...], mod[:, 3 * d:4 * d], mod[:, 4 * d:5 * d]).astype(BF16)
    a = jnp.dot(xn, wi_ref[:, :hidden], preferred_element_type=F32)
    bb = jnp.dot(xn, wi_ref[:, hidden:], preferred_element_type=F32)
    hmid = (_silu(a) * bb).astype(BF16)
    y = jnp.dot(hmid, wo_ref[...], preferred_element_type=F32)
    o_ref[0] = x1 + mod[:, 5 * d:6 * d] * y


def _ffn(x1, mods3, norm2, wi, wo):
    b, s, d = x1.shape
    hidden = wo.shape[0]
    tm = LAT_TILE
    wspec = lambda shape: pl.BlockSpec(shape, lambda bi, i: (0, 0), pipeline_mode=pl.Buffered(1))
    return pl.pallas_call(
        functools.partial(_ffn_kernel, d, hidden),
        grid=(b, s // tm),
        in_specs=[pl.BlockSpec((1, tm, d), lambda bi, i: (bi, i, 0)),
                  pl.BlockSpec((1, 1, N_MOD * d), lambda bi, i: (bi, 0, 0)),
                  pl.BlockSpec((1, d), lambda bi, i: (0, 0)),
                  wspec(wi.shape), wspec(wo.shape)],
        out_specs=pl.BlockSpec((1, tm, d), lambda bi, i: (bi, i, 0)),
        out_shape=jax.ShapeDtypeStruct((b, s, d), F32),
        compiler_params=_cparams(("parallel", "arbitrary")),
        name="ffn",
    )(x1, mods3, norm2, wi, wo)


def _diff_lane_perm():
    l = np.arange(2 * DIFF_HEAD_DIM)
    half, r = l // DIFF_HEAD_DIM, l % DIFF_HEAD_DIM
    qsel, axis, f = r // 32, (r % 32) // ROPE_FREQS, r % ROPE_FREQS
    return qsel * DIFF_HEAD_DIM + axis * 2 * ROPE_FREQS + half * ROPE_FREQS + f, qsel, axis, f, half


def _rope_tables(n_lat, n_ctx):
    _, _, axis, f, half = _diff_lane_perm()
    pos = np.arange(n_lat)
    row, col = pos // GRID_W, pos % GRID_W
    inv = (ROPE_BASE ** (-np.arange(ROPE_FREQS, dtype=np.float32) / ROPE_FREQS)).astype(np.float32)
    ang = (np.where(axis[None, :] == 0, row[:, None], col[:, None]).astype(np.float32) * inv[f][None, :])
    cos = np.concatenate([np.cos(ang), np.ones((n_ctx, LANES))], axis=0)
    sin = np.concatenate([np.sin(ang) * np.where(half == 0, -1.0, 1.0)[None, :], np.zeros((n_ctx, LANES))], axis=0)
    return jnp.asarray(cos, F32), jnp.asarray(sin, F32)


def kernel(x, c, ctx, c_ctx, w_mod, b_mod, norm1, norm2, w_in, b_gate, conv_w, conv_b, mlstm_norm, q_norm, k_norm,
           lam_vecs, diff_norm, w_branch_a, w_branch_b, w_out, w_ffn_in, w_ffn_out):
    b, s, d = x.shape
    n_ctx = ctx.shape[1]
    assert w_mod.shape[0] == 1, "single-layer block"
    assert s % CHUNK == 0 and n_ctx % CHUNK == 0 and s % ATT_TQ == 0 and s % LAT_TILE == 0
    qk_w = MLSTM_HEADS * MLSTM_DQK
    v_w = MLSTM_HEADS * MLSTM_DV
    da_w = DIFF_HEADS * 2 * DIFF_HEAD_DIM
    n_gates = 4 * MLSTM_HEADS
    sizes = (qk_w, qk_w, v_w, v_w, n_gates, da_w, da_w, da_w, d, d)
    offs = np.concatenate([[0], np.cumsum(sizes)])
    seg = lambda i: w_in[0][:, offs[i]:offs[i + 1]]

    perm, qsel, _, _, _ = _diff_lane_perm()
    head_perm = (np.arange(DIFF_HEADS)[:, None] * 2 * DIFF_HEAD_DIM + perm[None, :]).reshape(-1)
    w_main = w_in[0][:, offs[0]:offs[4]].astype(BF16)
    w_gab = w_in[0][:, offs[8]:offs[10]].astype(BF16)
    segb = lambda i: seg(i).astype(BF16)
    w_att = jnp.concatenate([segb(5)[:, head_perm], segb(6)[:, head_perm], segb(7)], axis=1)
    R_V, R_O, R_GA, R_GB = range(4)
    wg_f32 = jnp.pad(seg(4), ((0, 0), (0, LANES - n_gates)))
    wg_hi = wg_f32.astype(BF16)
    wg = jnp.concatenate([wg_hi, (wg_f32 - wg_hi.astype(F32)).astype(BF16)], axis=1)
    gate_bias = jnp.pad(b_gate[0], (0, LANES - n_gates)).reshape(1, LANES)
    conv_w8 = jnp.pad(conv_w[0], ((0, SUBLANES - CONV_K), (0, 0)))
    gq = q_norm[0][perm % DIFF_HEAD_DIM].reshape(1, LANES)
    gk = k_norm[0][perm % DIFF_HEAD_DIM].reshape(1, LANES)
    unit = np.concatenate([qsel, 2 + qsel])
    grp = jnp.asarray(unit[:, None] == unit[None, :], BF16)
    cos_t, sin_t = _rope_tables(s, n_ctx)

    cc = jnp.concatenate([c, c_ctx[None, :], jnp.zeros((SUBLANES - b - 1, d), F32)], axis=0)
    mods = _modulation(cc, w_mod[0], b_mod[0])
    mods3 = mods.reshape(SUBLANES, 1, N_MOD * d)

    qk, pm, qt, kr, vt, gates, gates_t = _projection(x, ctx, mods3, norm1, (w_main, w_gab, w_att), wg, gate_bias,
                                                     cos_t, sin_t, gq, gk, grp, conv_w8, conv_b)
    hf, hb = _mlstm(qk, pm, gates, gates_t, s, n_ctx, R_V)
    h_b = _attention(qt, kr, vt, lam_vecs[0], diff_norm, s)
    x1 = _merge(hf, hb, pm, h_b, x, mods3, mlstm_norm, w_branch_a[0].astype(BF16), w_branch_b[0].astype(BF16),
                w_out[0].astype(BF16), R_O, R_GA, R_GB)
    return _ffn(x1, mods3, norm2, w_ffn_in[0].astype(BF16), w_ffn_out[0].astype(BF16))
```

```python
import functools
import math

import jax
import jax.numpy as jnp
import numpy as np
from jax import lax
from jax.experimental import pallas as pl
from jax.experimental.pallas import tpu as pltpu

F32 = jnp.float32
BF16 = jnp.bfloat16
HIGHEST = lax.Precision.HIGHEST

GRID_W = 64
N_MOD = 6
MLSTM_HEADS = 4
MLSTM_DQK = 128
MLSTM_DV = 256
CONV_K = 5
DIFF_HEADS = 8
DIFF_HEAD_DIM = 64
ROPE_BASE = 10000.0
ROPE_FREQS = DIFF_HEAD_DIM // 4
RMS_EPS = 1e-6
LAM_INIT_L0 = 0.8 - 0.6 * math.exp(-0.3 * 0)
LOG2_E = math.log2(math.e)
Q_SCALE_LOG2 = DIFF_HEAD_DIM ** -0.5 * LOG2_E
LANES = 128
SUBLANES = 8
BF16_SUBLANES = 16
V_ROWS = 2 * DIFF_HEAD_DIM + BF16_SUBLANES
VMEM_LIMIT = 56 * 1024 * 1024

TOK_TILE = 256
LAT_TILE = 512
CHUNK = 256
MLSTM_BATCH = 4
MLSTM_GROUP = 2
ATT_TQ = 256


def _cparams(sem):
    return pltpu.CompilerParams(dimension_semantics=sem, vmem_limit_bytes=VMEM_LIMIT)


def _sigmoid(x):
    return 0.5 + 0.5 * jnp.tanh(0.5 * x)


def _silu(x):
    h = 0.5 * x
    return h + h * jnp.tanh(h)


def _log_sigmoid(x):
    return jnp.minimum(x, 0.0) - jnp.log(1.0 + jnp.exp(-jnp.abs(x)))


def _mod_kernel(c_ref, w_ref, b_ref, o_ref):
    a = _silu(c_ref[...])
    o_ref[...] = jnp.dot(a, w_ref[...], preferred_element_type=F32, precision=HIGHEST) + b_ref[...]


def _modulation(cc, w_mod, b_mod):
    rows, d = cc.shape
    n = w_mod.shape[1]
    tn = 1024
    return pl.pallas_call(
        _mod_kernel,
        grid=(n // tn,),
        in_specs=[pl.BlockSpec((rows, d), lambda j: (0, 0)),
                  pl.BlockSpec((d, tn), lambda j: (0, j)),
                  pl.BlockSpec((1, tn), lambda j: (0, j))],
        out_specs=pl.BlockSpec((rows, tn), lambda j: (0, j)),
        out_shape=jax.ShapeDtypeStruct((rows, n), F32),
        compiler_params=_cparams(("arbitrary",)),
        name="mod",
    )(cc, w_mod, b_mod.reshape(1, n))


def _rms_mod(xf, g, shift, scale):
    ms = jnp.mean(xf * xf, axis=-1, keepdims=True)
    return xf * lax.rsqrt(ms + RMS_EPS) * g * (1.0 + scale) + shift


def _proj_kernel(n_lat_tiles, n_tiles, d, x_ref, xprev_ref, xnext_ref, ctx_ref, mod_ref, g_ref,
                 wm_ref, wgab_ref, watt_ref, wg_ref, gbias_ref,
                 cos_ref, sin_ref, gq_ref, gk_ref, grp_ref, cw_ref, cb_ref,
                 qk_ref, pm_ref, qt_ref, kr_ref, vt_ref, gates_ref, gates_t_ref,
                 ext_ref, yq_ref, yk_ref, yv_ref, gpre_ref, xb_ref, xlo_ref):
    i = pl.program_id(0)
    tm = x_ref.shape[1]

    @pl.when(i == 0)
    def _():
        ext_ref[...] = jnp.zeros_like(ext_ref)
        yq_ref[...] = jnp.zeros_like(yq_ref)
        yk_ref[...] = jnp.zeros_like(yk_ref)
        yv_ref[...] = jnp.zeros_like(yv_ref)
        gpre_ref[...] = jnp.zeros_like(gpre_ref)

    ti = jnp.minimum(i, pl.num_programs(0) - 2) % n_tiles
    xf = jnp.where(ti < n_lat_tiles, x_ref[0], ctx_ref[0])
    mod = mod_ref[0]
    gain1 = g_ref[...]
    xn = _rms_mod(xf, gain1, mod[:, 0:d], mod[:, d:2 * d])
    xb = xn.astype(BF16)
    halo = jnp.concatenate([xprev_ref[0], xnext_ref[0]], axis=0)
    xb_ref[0:tm, :] = xb
    xb_ref[tm:, :] = _rms_mod(halo, gain1, mod[:, 0:d], mod[:, d:2 * d]).astype(BF16)
    xlo_ref[...] = (xn - xb.astype(F32)).astype(BF16)

    def project(w_ref, region):
        return jnp.dot(xb_ref[0:tm, :], w_ref[:, region * d:(region + 1) * d], preferred_element_type=F32)

    taps = cw_ref[...]
    conv_bias = cb_ref[...]
    n_q_cols = d // 2

    w2 = 2 * LANES

    def conv_columns(c_lo, c_hi):
        for c0 in range(c_lo, c_hi, w2):
            acc = jnp.zeros((tm, w2), F32) + conv_bias[:, c0:c0 + w2]
            for k in range(CONV_K):
                r0 = SUBLANES - CONV_K // 2 + k
                acc = acc + ext_ref[r0:r0 + tm, c0:c0 + w2] * taps[k:k + 1, c0:c0 + w2]
            out = _silu(acc)
            if c0 < n_q_cols:
                out = out * MLSTM_DQK ** -0.5
            qk_ref[0, :, c0:c0 + w2] = out.astype(BF16)

    cos = cos_ref[...]
    sin = sin_ref[...]
    grp = grp_ref[...]
    gq = gq_ref[...]
    gk = gk_ref[...]
    hw = 2 * DIFF_HEAD_DIM

    def scaled_sumsq(x2):
        ss = jnp.dot((x2 * x2).astype(BF16), grp, preferred_element_type=F32)
        return lax.rsqrt(ss * (1.0 / DIFF_HEAD_DIM) + RMS_EPS)

    def rotated(y):
        return y * cos + pltpu.roll(y, DIFF_HEAD_DIM, axis=1) * sin

    ones_rows = jnp.ones((V_ROWS - hw, tm), BF16)

    def attention_pair(pair):
        sl2 = slice(pair * 2 * hw, (pair + 1) * 2 * hw)
        yq2 = yq_ref[:, sl2]
        yk2 = yk_ref[:, sl2]
        qn = yq2 * scaled_sumsq(yq2)
        kn = yk2 * scaled_sumsq(yk2)
        for u in range(2):
            h = 2 * pair + u
            sl = slice(h * hw, (h + 1) * hw)
            su = slice(u * hw, (u + 1) * hw)
            qt_ref[0, sl, :] = (rotated(qn[:, su] * gq) * Q_SCALE_LOG2).T.astype(BF16)
            kr_ref[0, :, sl] = rotated(kn[:, su] * gk).astype(BF16)
            vt_ref[0, h * V_ROWS:h * V_ROWS + hw, :] = yv_ref[:, sl].T.astype(BF16)
            vt_ref[0, h * V_ROWS + hw:(h + 1) * V_ROWS, :] = ones_rows

    def gate_epilogue():
        g = gpre_ref[...]
        lf = _log_sigmoid(g)
        lf_hi = lf.astype(BF16)
        lf_lo = (lf - lf_hi.astype(F32)).astype(BF16)
        lf2 = jnp.concatenate([lf_hi, lf_lo], axis=1)
        rows = lax.broadcasted_iota(jnp.int32, (tm, tm), 0)
        cols = lax.broadcasted_iota(jnp.int32, (tm, tm), 1)
        lower = jnp.where(cols <= rows, 1.0, 0.0).astype(BF16)
        upper = jnp.where(cols >= rows, 1.0, 0.0).astype(BF16)
        cum_f2 = jnp.dot(lower, lf2, preferred_element_type=F32)
        cum_b2 = jnp.dot(upper, lf2, preferred_element_type=F32)
        cum_f = cum_f2[:, :LANES] + cum_f2[:, LANES:]
        cum_b = cum_b2[:, :LANES] + cum_b2[:, LANES:]
        lane = lax.broadcasted_iota(jnp.int32, (tm, LANES), 1)
        cum = jnp.where(lane < 2 * MLSTM_HEADS, cum_f, cum_b)
        a = g - pltpu.roll(cum, LANES - MLSTM_HEADS, axis=1)
        a = a * LOG2_E
        gates_ref[0] = jnp.where((lane % (2 * MLSTM_HEADS)) < MLSTM_HEADS, a, cum * LOG2_E)
        gates_t_ref[0, 0] = a.T[0:4 * MLSTM_HEADS, :]

    for r, (w_ref, region) in enumerate(((wm_ref, 1), (wm_ref, 2), (wgab_ref, 0), (wgab_ref, 1))):
        pm_ref[0, :, r * d:(r + 1) * d] = project(w_ref, region).astype(BF16)
        attention_pair(r)
    yq_ref[...] = project(watt_ref, 0)
    conv_columns(0, d // 2)
    yk_ref[...] = project(watt_ref, 1)
    conv_columns(d // 2, d)
    yv_ref[...] = project(watt_ref, 2)
    gate_epilogue()
    y_qk = jnp.dot(xb_ref[...], wm_ref[:, 0:d], preferred_element_type=F32)
    has_prev = jnp.logical_and(ti > 0, ti < n_lat_tiles)
    has_next = ti < n_lat_tiles - 1
    ext_ref[0:SUBLANES, :] = jnp.where(has_prev, y_qk[tm:tm + SUBLANES], 0.0)
    ext_ref[SUBLANES:SUBLANES + tm, :] = y_qk[0:tm]
    ext_ref[SUBLANES + tm:, :] = jnp.where(has_next, y_qk[tm + SUBLANES:], 0.0)
    g2 = (jnp.dot(xb_ref[0:tm, :], wg_ref[...], preferred_element_type=F32)
          + jnp.dot(xlo_ref[...], wg_ref[...], preferred_element_type=F32))
    gpre_ref[...] = g2[:, :LANES] + g2[:, LANES:] + gbias_ref[...]


def _projection(x, ctx, mods3, norm1, weights, wg, gate_bias, cos_t, sin_t, gq, gk, grp, conv_w8, conv_b):
    b, s, d = x.shape
    n_ctx = ctx.shape[1]
    tm = TOK_TILE
    assert tm == CHUNK
    assert n_ctx == tm
    n_lat_tiles = s // tm
    n_tiles = (s + n_ctx) // tm
    t = s + n_ctx
    w_main, w_gab, w_att = weights
    resident = lambda shape: pl.BlockSpec(shape, lambda i: (0, 0), pipeline_mode=pl.Buffered(1))
    da_w = DIFF_HEADS * 2 * DIFF_HEAD_DIM
    halo_stride = tm // SUBLANES
    last_halo = s // SUBLANES - 1
    const = lambda shape: pl.BlockSpec(shape, lambda i: (0,) * len(shape))
    n_pairs = b * n_tiles

    def cur(i):
        j = jnp.minimum(i, n_pairs - 1)
        return j // n_tiles, j % n_tiles

    def late(i):
        j = jnp.maximum(i - 1, 0)
        return j // n_tiles, j % n_tiles

    tok = lambda width: pl.BlockSpec((1, tm, width), lambda i: (late(i)[0], late(i)[1], 0))
    tok_t = lambda rows: pl.BlockSpec((1, rows, tm), lambda i: (late(i)[0], 0, late(i)[1]))
    n_ctx_tiles = n_ctx // tm
    return pl.pallas_call(
        functools.partial(_proj_kernel, n_lat_tiles, n_tiles, d),
        grid=(n_pairs + 1,),
        in_specs=[
            pl.BlockSpec((1, tm, d), lambda i: (cur(i)[0], jnp.minimum(cur(i)[1], n_lat_tiles - 1), 0)),
            pl.BlockSpec((1, SUBLANES, d),
                         lambda i: (cur(i)[0], jnp.clip(cur(i)[1] * halo_stride - 1, 0, last_halo), 0)),
            pl.BlockSpec((1, SUBLANES, d),
                         lambda i: (cur(i)[0], jnp.clip((cur(i)[1] + 1) * halo_stride, 0, last_halo), 0)),
            pl.BlockSpec((1, tm, d),
                         lambda i: (cur(i)[0], jnp.clip(cur(i)[1] - n_lat_tiles, 0, n_ctx_tiles - 1), 0)),
            pl.BlockSpec((1, 1, N_MOD * d),
                         lambda i: (jnp.where(cur(i)[1] < n_lat_tiles, cur(i)[0], b), 0, 0)),
            const((1, d)),
            resident(w_main.shape), resident(w_gab.shape), resident(w_att.shape),
            const((d, 2 * LANES)), const((1, LANES)),
            pl.BlockSpec((tm, LANES), lambda i: (late(i)[1], 0)),
            pl.BlockSpec((tm, LANES), lambda i: (late(i)[1], 0)),
            const((1, LANES)), const((1, LANES)), const((2 * LANES, 2 * LANES)),
            const((SUBLANES, d)), const((1, d)),
        ],
        out_specs=[tok(d), pl.BlockSpec((1, tm, 4 * d), lambda i: (cur(i)[0], cur(i)[1], 0)),
                   tok_t(da_w), tok(da_w), tok_t(DIFF_HEADS * V_ROWS), tok(LANES),
                   pl.BlockSpec((1, 1, 4 * MLSTM_HEADS, tm), lambda i: (late(i)[0], late(i)[1], 0, 0))],
        out_shape=[jax.ShapeDtypeStruct((b, t, d), BF16),
                   jax.ShapeDtypeStruct((b, t, 4 * d), BF16),
                   jax.ShapeDtypeStruct((b, da_w, t), BF16),
                   jax.ShapeDtypeStruct((b, t, da_w), BF16),
                   jax.ShapeDtypeStruct((b, DIFF_HEADS * V_ROWS, t), BF16),
                   jax.ShapeDtypeStruct((b, t, LANES), F32),
                   jax.ShapeDtypeStruct((b, n_tiles, 4 * MLSTM_HEADS, tm), F32)],
        scratch_shapes=[pltpu.VMEM((tm + 2 * SUBLANES, d), F32),
                        pltpu.VMEM((tm, da_w), F32), pltpu.VMEM((tm, da_w), F32), pltpu.VMEM((tm, da_w), F32),
                        pltpu.VMEM((tm, LANES), F32),
                        pltpu.VMEM((tm + 2 * SUBLANES, d), BF16),
                        pltpu.VMEM((tm, d), BF16)],
        compiler_params=_cparams(("arbitrary",)),
        name="proj",
    )(x, x, x, ctx, mods3, norm1, w_main, w_gab, w_att, wg, gate_bias, cos_t, sin_t, gq, gk, grp, conv_w8, conv_b)


def _wide(x, n_tiles):
    return jnp.concatenate([x] * n_tiles, axis=1)


def _mlstm_scores(q, k, c_aug):
    s = lax.dot_general(q, k, (((1,), (1,)), ((), ())), preferred_element_type=F32)
    inter = jnp.dot(q, c_aug.astype(BF16), preferred_element_type=F32)
    return s, inter


def _mlstm_decay(reverse, s, k, a_col, a_row, m_prev):
    l = s.shape[0]
    a_rep = jnp.broadcast_to(a_col, (l, LANES))
    rows = lax.broadcasted_iota(jnp.int32, (l, l), 0)
    cols = lax.broadcasted_iota(jnp.int32, (l, l), 1)
    visible = (cols >= rows) if reverse else (cols <= rows)
    a_mat = jnp.where(visible, a_row, -jnp.inf)
    m_rep = jnp.maximum(jnp.broadcast_to(jnp.max(a_mat, axis=-1, keepdims=True), (l, LANES)), m_prev)
    pmat = (s * jnp.exp2(a_mat - _wide(m_rep, l // LANES))).astype(BF16)
    m_last = jnp.maximum(jnp.max(a_rep, axis=0, keepdims=True), m_prev)
    kw = (jnp.exp2(a_rep - m_last) * k.astype(F32)).astype(BF16)
    return pmat, kw, m_rep, m_last


def _mlstm_finish(pmat, kw, v_aug, inter, m_rep, m_last, m_prev, lf_cum_col, lf_total, c_aug):
    l = pmat.shape[0]
    dv = v_aug.shape[1] - LANES
    n_wide = v_aug.shape[1] // LANES
    intra = jnp.dot(pmat, v_aug, preferred_element_type=F32)
    numden = intra + _wide(jnp.exp2(m_prev - m_rep), n_wide) * inter
    den = numden[:, dv:]
    floor = jnp.exp2(-(jnp.broadcast_to(lf_cum_col, (l, LANES)) + m_rep))
    h = numden[:, :dv] * _wide(1.0 / jnp.maximum(jnp.abs(den), floor), dv // LANES)
    kv = lax.dot_general(kw, v_aug, (((0,), (0,)), ((), ())), preferred_element_type=F32)
    c_new = _wide(jnp.exp2(m_prev - m_last), n_wide) * c_aug + kv
    return h, c_new, lf_total + m_last


def _mlstm_kernel(qf_ref, kf_ref, vf_ref, gf_ref, gtf_ref, qb_ref, kb_ref, vb_ref, gb_ref, gtb_ref,
                  hf_ref, hb_ref, c_ref, m_ref):
    s_idx = pl.program_id(1)

    @pl.when(s_idx == 0)
    def _():
        c_ref[...] = jnp.zeros_like(c_ref)
        m_ref[...] = jnp.zeros_like(m_ref)

    nb, l = qf_ref.shape[0], qf_ref.shape[1]
    nh = MLSTM_HEADS
    ones_col = jnp.ones((l, LANES), BF16)
    scans = []
    for bb in range(nb):
        for reverse, q_ref, k_ref, v_ref, g_ref, gt_ref, h_ref in (
                (False, qf_ref, kf_ref, vf_ref, gf_ref, gtf_ref, hf_ref),
                (True, qb_ref, kb_ref, vb_ref, gb_ref, gtb_ref, hb_ref)):
            base = 2 * nh if reverse else 0
            for h in range(nh):
                scans.append(dict(bb=bb, reverse=reverse, h=h, ca=base + h, cc=base + nh + h,
                                  hd=bb * 2 * nh + (nh if reverse else 0) + h,
                                  q_ref=q_ref, k_ref=k_ref, v_ref=v_ref, g_ref=g_ref, gt_ref=gt_ref, h_ref=h_ref))
    qk_cols = lambda sc: slice(sc["h"] * MLSTM_DQK, (sc["h"] + 1) * MLSTM_DQK)
    v_cols = lambda sc: slice(sc["h"] * MLSTM_DV, (sc["h"] + 1) * MLSTM_DV)
    for g0 in range(0, len(scans), MLSTM_GROUP):
        group = scans[g0:g0 + MLSTM_GROUP]
        for sc in group:
            sc["s"], sc["inter"] = _mlstm_scores(sc["q_ref"][sc["bb"], :, qk_cols(sc)],
                                                 sc["k_ref"][sc["bb"], :, qk_cols(sc)], c_ref[sc["hd"]])
        for sc in group:
            g = sc["g_ref"][sc["bb"]]
            g_t = sc["gt_ref"][sc["bb"], 0]
            m_prev = m_ref[sc["hd"]:sc["hd"] + 1, :]
            sc["pmat"], sc["kw"], sc["m_rep"], sc["m_last"] = _mlstm_decay(
                sc["reverse"], sc["s"], sc["k_ref"][sc["bb"], :, qk_cols(sc)],
                g[:, sc["ca"]:sc["ca"] + 1], g_t[sc["ca"]:sc["ca"] + 1, :], m_prev)
        for sc in group:
            g = sc["g_ref"][sc["bb"]]
            cc, hd = sc["cc"], sc["hd"]
            lf_total = g[0:1, cc:cc + 1] if sc["reverse"] else g[l - 1:l, cc:cc + 1]
            v_aug = jnp.concatenate([sc["v_ref"][sc["bb"], :, v_cols(sc)], ones_col], axis=1)
            hval, c_new, m_new = _mlstm_finish(sc["pmat"], sc["kw"], v_aug, sc["inter"], sc["m_rep"],
                                               sc["m_last"], m_ref[hd:hd + 1, :], g[:, cc:cc + 1], lf_total,
                                               c_ref[hd])
            sc["h_ref"][sc["bb"], :, v_cols(sc)] = hval.astype(sc["h_ref"].dtype)
            c_ref[hd] = c_new
            m_ref[hd:hd + 1, :] = m_new


def _mlstm(qk, pm, gates, gates_t, n_lat, n_ctx, v_region):
    b, t, _ = qk.shape
    l = CHUNK
    n_lat_c = n_lat // l
    n_ctx_c = n_ctx // l
    n_steps = n_lat_c + n_ctx_c
    qw = MLSTM_HEADS * MLSTM_DQK
    vw = MLSTM_HEADS * MLSTM_DV

    def fwd(si):
        return jnp.where(si < n_ctx_c, n_lat_c + si, si - n_ctx_c)

    def bwd(si):
        return jnp.where(si < n_ctx_c, n_steps - 1 - si, n_steps - 1 - si)

    nb = MLSTM_BATCH
    assert b % nb == 0
    in_specs = []
    for order in (fwd, bwd):
        in_specs += [
            pl.BlockSpec((nb, l, qw), lambda bi, si, o=order: (bi, o(si), 0)),
            pl.BlockSpec((nb, l, qw), lambda bi, si, o=order: (bi, o(si), 1)),
            pl.BlockSpec((nb, l, vw), lambda bi, si, o=order: (bi, o(si), v_region)),
            pl.BlockSpec((nb, l, LANES), lambda bi, si, o=order: (bi, o(si), 0)),
            pl.BlockSpec((nb, 1, 4 * MLSTM_HEADS, l), lambda bi, si, o=order: (bi, o(si), 0, 0)),
        ]
    out_specs = [pl.BlockSpec((nb, l, vw), lambda bi, si: (bi, fwd(si), 0)),
                 pl.BlockSpec((nb, l, vw), lambda bi, si: (bi, bwd(si), 0))]
    n_scans = nb * 2 * MLSTM_HEADS
    return pl.pallas_call(
        _mlstm_kernel,
        grid=(b // nb, n_steps),
        in_specs=in_specs,
        out_specs=out_specs,
        out_shape=[jax.ShapeDtypeStruct((b, t, vw), BF16)] * 2,
        scratch_shapes=[pltpu.VMEM((n_scans, MLSTM_DQK, MLSTM_DV + LANES), F32),
                        pltpu.VMEM((n_scans, LANES), F32)],
        compiler_params=_cparams(("parallel", "arbitrary")),
        name="mlstm",
    )(qk, qk, pm, gates, gates_t, qk, qk, pm, gates, gates_t)


def _attn_kernel(n_keys, tq, qt_ref, k_ref, vt_ref, lam_ref, gain_ref, o_ref, s_ref, pm_ref, e_ref):
    nq = qt_ref.shape[2] // tq
    width = 2 * tq
    hw = o_ref.shape[2]
    row = lax.broadcasted_iota(jnp.int32, (qt_ref.shape[1], tq), 0)
    first = (row % DIFF_HEAD_DIM) < (DIFF_HEAD_DIM // 2)

    def scores(qb):
        qt = qt_ref[0, :, pl.ds(pl.multiple_of(qb * tq, tq), tq)]
        zero = jnp.zeros_like(qt)
        qq = jnp.concatenate([jnp.where(first, qt, zero), jnp.where(first, zero, qt)], axis=1)
        s = jnp.dot(k_ref[0], qq, preferred_element_type=F32)
        s_ref[...] = s
        pm_ref[...] = jnp.max(s.reshape(n_keys // SUBLANES, SUBLANES, width), axis=0)

    def softmax():
        m = jnp.max(pm_ref[...], axis=0, keepdims=True)
        e_ref[...] = jnp.exp2(s_ref[...] - m).astype(BF16)

    def output(qb):
        lv = lam_ref[...]
        lam = (jnp.exp(jnp.sum(lv[0:1] * lv[1:2], axis=-1, keepdims=True))
               - jnp.exp(jnp.sum(lv[2:3] * lv[3:4], axis=-1, keepdims=True)) + LAM_INIT_L0)
        pv = jnp.dot(vt_ref[0], e_ref[...], preferred_element_type=F32)
        o_all = pv[0:hw, :] * (1.0 / pv[hw:hw + 1, :])
        o_t = o_all[:, :tq] - lam * o_all[:, tq:]
        ms = jnp.mean(o_t * o_t, axis=0, keepdims=True)
        out = (o_t * lax.rsqrt(ms + RMS_EPS)).T * (gain_ref[...] * (1.0 - LAM_INIT_L0))
        o_ref[0, pl.ds(pl.multiple_of(qb * tq, tq), tq), :] = out.astype(o_ref.dtype)

    scores(0)
    softmax()
    scores(1)

    def iteration(t, carry):
        output(t - 2)
        softmax()
        scores(t)
        return carry

    lax.fori_loop(2, nq, iteration, 0)
    output(nq - 2)
    softmax()
    output(nq - 1)


def _attention(qt, kr, vt, lam_vecs, diff_gain, n_lat):
    b, t, _ = kr.shape
    tq = ATT_TQ
    hw = 2 * DIFF_HEAD_DIM
    nq = n_lat // tq
    assert nq >= 2
    return pl.pallas_call(
        functools.partial(_attn_kernel, t, tq),
        grid=(b, DIFF_HEADS),
        in_specs=[pl.BlockSpec((1, hw, n_lat), lambda bi, h: (bi, h, 0)),
                  pl.BlockSpec((1, t, hw), lambda bi, h: (bi, 0, h)),
                  pl.BlockSpec((1, V_ROWS, t), lambda bi, h: (bi, h, 0)),
                  pl.BlockSpec(lam_vecs.shape, lambda bi, h: (0, 0)),
                  pl.BlockSpec((1, hw), lambda bi, h: (0, 0))],
        out_specs=pl.BlockSpec((1, n_lat, hw), lambda bi, h: (bi, 0, h)),
        out_shape=jax.ShapeDtypeStruct((b, n_lat, DIFF_HEADS * hw), BF16),
        scratch_shapes=[pltpu.VMEM((t, 2 * tq), F32),
                        pltpu.VMEM((SUBLANES, 2 * tq), F32),
                        pltpu.VMEM((t, 2 * tq), BF16)],
        compiler_params=_cparams(("parallel", "arbitrary")),
        name="attn",
    )(qt, kr, vt, lam_vecs, diff_gain)


def _merge_kernel(d, hf_ref, hb_ref, mo_ref, ga_ref, gb_ref, hB_ref, x_ref, mod_ref, gain_ref,
                  wa_ref, wb_ref, wo_ref, o_ref):
    gain = gain_ref[...]
    yb = jnp.dot(hB_ref[0], wb_ref[...], preferred_element_type=F32)
    ya = None
    for h in range(MLSTM_HEADS):
        sl = slice(h * MLSTM_DV, (h + 1) * MLSTM_DV)
        hs = hf_ref[0, :, sl].astype(F32) + hb_ref[0, :, sl].astype(F32)
        ms = jnp.mean(hs * hs, axis=-1, keepdims=True)
        hn = hs * lax.rsqrt(ms + RMS_EPS) * gain[:, sl]
        h_a = (_sigmoid(mo_ref[0, :, sl].astype(F32)) * hn).astype(BF16)
        part = jnp.dot(h_a, wa_ref[sl, :], preferred_element_type=F32)
        ya = part if ya is None else ya + part
    y = _sigmoid(ga_ref[0].astype(F32)) * ya + _sigmoid(gb_ref[0].astype(F32)) * yb
    yo = jnp.dot(y.astype(BF16), wo_ref[...], preferred_element_type=F32)
    g1 = mod_ref[0][:, 2 * d:3 * d]
    o_ref[0] = x_ref[0] + g1 * yo


def _merge(hf, hb, p, h_b, x, mods3, mlstm_gain, wa, wb, wo, mo_region, ga_region, gb_region):
    b, s, d = x.shape
    tm = LAT_TILE
    wspec = lambda shape: pl.BlockSpec(shape, lambda bi, i: (0, 0), pipeline_mode=pl.Buffered(1))
    tok = lambda region: pl.BlockSpec((1, tm, d), lambda bi, i, r=region: (bi, i, r))
    return pl.pallas_call(
        functools.partial(_merge_kernel, d),
        grid=(b, s // tm),
        in_specs=[tok(0), tok(0), tok(mo_region), tok(ga_region), tok(gb_region), tok(0), tok(0),
                  pl.BlockSpec((1, 1, N_MOD * d), lambda bi, i: (bi, 0, 0)),
                  pl.BlockSpec((1, d), lambda bi, i: (0, 0)),
                  wspec(wa.shape), wspec(wb.shape), wspec(wo.shape)],
        out_specs=pl.BlockSpec((1, tm, d), lambda bi, i: (bi, i, 0)),
        out_shape=jax.ShapeDtypeStruct((b, s, d), F32),
        compiler_params=_cparams(("parallel", "arbitrary")),
        name="merge",
    )(hf, hb, p, p, p, h_b, x, mods3, mlstm_gain, wa, wb, wo)


def _ffn_kernel(d, hidden, x_ref, mod_ref, g_ref, wi_ref, wo_ref, o_ref):
    x1 = x_ref[0]
    mod = mod_ref[0]
    xn = _rms_mod(x1, g_ref[...], mod[:, 3 * d:4 * d], mod[:, 4 * d:5 * d]).astype(BF16)
    a = jnp.dot(xn, wi_ref[:, :hidden], preferred_element_type=F32)
    bb = jnp.dot(xn, wi_ref[:, hidden:], preferred_element_type=F32)
    hmid = (_silu(a) * bb).astype(BF16)
    y = jnp.dot(hmid, wo_ref[...], preferred_element_type=F32)
    o_ref[0] = x1 + mod[:, 5 * d:6 * d] * y


def _ffn(x1, mods3, norm2, wi, wo):
    b, s, d = x1.shape
    hidden = wo.shape[0]
    tm = LAT_TILE
    wspec = lambda shape: pl.BlockSpec(shape, lambda bi, i: (0, 0), pipeline_mode=pl.Buffered(1))
    return pl.pallas_call(
        functools.partial(_ffn_kernel, d, hidden),
        grid=(b, s // tm),
        in_specs=[pl.BlockSpec((1, tm, d), lambda bi, i: (bi, i, 0)),
                  pl.BlockSpec((1, 1, N_MOD * d), lambda bi, i: (bi, 0, 0)),
                  pl.BlockSpec((1, d), lambda bi, i: (0, 0)),
                  wspec(wi.shape), wspec(wo.shape)],
        out_specs=pl.BlockSpec((1, tm, d), lambda bi, i: (bi, i, 0)),
        out_shape=jax.ShapeDtypeStruct((b, s, d), F32),
        compiler_params=_cparams(("parallel", "arbitrary")),
        name="ffn",
    )(x1, mods3, norm2, wi, wo)


def _diff_lane_perm():
    l = np.arange(2 * DIFF_HEAD_DIM)
    half, r = l // DIFF_HEAD_DIM, l % DIFF_HEAD_DIM
    qsel, axis, f = r // 32, (r % 32) // ROPE_FREQS, r % ROPE_FREQS
    return qsel * DIFF_HEAD_DIM + axis * 2 * ROPE_FREQS + half * ROPE_FREQS + f, qsel, axis, f, half


def _rope_tables(n_lat, n_ctx):
    _, _, axis, f, half = _diff_lane_perm()
    pos = np.arange(n_lat)
    row, col = pos // GRID_W, pos % GRID_W
    inv = (ROPE_BASE ** (-np.arange(ROPE_FREQS, dtype=np.float32) / ROPE_FREQS)).astype(np.float32)
    ang = (np.where(axis[None, :] == 0, row[:, None], col[:, None]).astype(np.float32) * inv[f][None, :])
    cos = np.concatenate([np.cos(ang), np.ones((n_ctx, LANES))], axis=0)
    sin = np.concatenate([np.sin(ang) * np.where(half == 0, -1.0, 1.0)[None, :], np.zeros((n_ctx, LANES))], axis=0)
    return jnp.asarray(cos, F32), jnp.asarray(sin, F32)


def kernel(x, c, ctx, c_ctx, w_mod, b_mod, norm1, norm2, w_in, b_gate, conv_w, conv_b, mlstm_norm, q_norm, k_norm,
           lam_vecs, diff_norm, w_branch_a, w_branch_b, w_out, w_ffn_in, w_ffn_out):
    b, s, d = x.shape
    n_ctx = ctx.shape[1]
    assert w_mod.shape[0] == 1, "single-layer block"
    assert s % CHUNK == 0 and n_ctx % CHUNK == 0 and s % ATT_TQ == 0 and s % LAT_TILE == 0
    qk_w = MLSTM_HEADS * MLSTM_DQK
    v_w = MLSTM_HEADS * MLSTM_DV
    da_w = DIFF_HEADS * 2 * DIFF_HEAD_DIM
    n_gates = 4 * MLSTM_HEADS
    sizes = (qk_w, qk_w, v_w, v_w, n_gates, da_w, da_w, da_w, d, d)
    offs = np.concatenate([[0], np.cumsum(sizes)])
    seg = lambda i: w_in[0][:, offs[i]:offs[i + 1]]

    perm, qsel, _, _, _ = _diff_lane_perm()
    head_perm = (np.arange(DIFF_HEADS)[:, None] * 2 * DIFF_HEAD_DIM + perm[None, :]).reshape(-1)
    w_main = w_in[0][:, offs[0]:offs[4]].astype(BF16)
    w_gab = w_in[0][:, offs[8]:offs[10]].astype(BF16)
    segb = lambda i: seg(i).astype(BF16)
    w_att = jnp.concatenate([segb(5)[:, head_perm], segb(6)[:, head_perm], segb(7)], axis=1)
    R_V, R_O, R_GA, R_GB = range(4)
    wg_f32 = jnp.pad(seg(4), ((0, 0), (0, LANES - n_gates)))
    wg_hi = wg_f32.astype(BF16)
    wg = jnp.concatenate([wg_hi, (wg_f32 - wg_hi.astype(F32)).astype(BF16)], axis=1)
    gate_bias = jnp.pad(b_gate[0], (0, LANES - n_gates)).reshape(1, LANES)
    conv_w8 = jnp.pad(conv_w[0], ((0, SUBLANES - CONV_K), (0, 0)))
    gq = q_norm[0][perm % DIFF_HEAD_DIM].reshape(1, LANES)
    gk = k_norm[0][perm % DIFF_HEAD_DIM].reshape(1, LANES)
    unit = np.concatenate([qsel, 2 + qsel])
    grp = jnp.asarray(unit[:, None] == unit[None, :], BF16)
    cos_t, sin_t = _rope_tables(s, n_ctx)

    cc = jnp.concatenate([c, c_ctx[None, :], jnp.zeros((SUBLANES - b - 1, d), F32)], axis=0)
    mods = _modulation(cc, w_mod[0], b_mod[0])
    mods3 = mods.reshape(SUBLANES, 1, N_MOD * d)

    qk, pm, qt, kr, vt, gates, gates_t = _projection(x, ctx, mods3, norm1, (w_main, w_gab, w_att), wg, gate_bias,
                                                     cos_t, sin_t, gq, gk, grp, conv_w8, conv_b)
    hf, hb = _mlstm(qk, pm, gates, gates_t, s, n_ctx, R_V)
    h_b = _attention(qt, kr, vt, lam_vecs[0], diff_norm, s)
    x1 = _merge(hf, hb, pm, h_b, x, mods3, mlstm_norm, w_branch_a[0].astype(BF16), w_branch_b[0].astype(BF16),
                w_out[0].astype(BF16), R_O, R_GA, R_GB)
    return _ffn(x1, mods3, norm2, w_ffn_in[0].astype(BF16), w_ffn_out[0].astype(BF16))
```

```python
import functools
import math

import jax
import jax.numpy as jnp
import numpy as np
from jax import lax
from jax.experimental import pallas as pl
from jax.experimental.pallas import tpu as pltpu

F32 = jnp.float32
BF16 = jnp.bfloat16
HIGHEST = lax.Precision.HIGHEST

GRID_W = 64
N_MOD = 6
MLSTM_HEADS = 4
MLSTM_DQK = 128
MLSTM_DV = 256
CONV_K = 5
DIFF_HEADS = 8
DIFF_HEAD_DIM = 64
ROPE_BASE = 10000.0
ROPE_FREQS = DIFF_HEAD_DIM // 4
RMS_EPS = 1e-6
LAM_INIT_L0 = 0.8 - 0.6 * math.exp(-0.3 * 0)
LOG2_E = math.log2(math.e)
Q_SCALE_LOG2 = DIFF_HEAD_DIM ** -0.5 * LOG2_E
LANES = 128
SUBLANES = 8
BF16_SUBLANES = 16
V_ROWS = 2 * DIFF_HEAD_DIM + BF16_SUBLANES
VMEM_LIMIT = 56 * 1024 * 1024

TOK_TILE = 256
LAT_TILE = 512
CHUNK = 256
MLSTM_BATCH = 4
MLSTM_GROUP = 2
ATT_TQ = 128


def _cparams(sem):
    return pltpu.CompilerParams(dimension_semantics=sem, vmem_limit_bytes=VMEM_LIMIT)


def _sigmoid(x):
    return 0.5 + 0.5 * jnp.tanh(0.5 * x)


def _silu(x):
    h = 0.5 * x
    return h + h * jnp.tanh(h)


def _log_sigmoid(x):
    return jnp.minimum(x, 0.0) - jnp.log(1.0 + jnp.exp(-jnp.abs(x)))


def _mod_kernel(c_ref, w_ref, b_ref, o_ref):
    a = _silu(c_ref[...])
    o_ref[...] = jnp.dot(a, w_ref[...], preferred_element_type=F32, precision=HIGHEST) + b_ref[...]


def _modulation(cc, w_mod, b_mod):
    rows, d = cc.shape
    n = w_mod.shape[1]
    tn = 1024
    return pl.pallas_call(
        _mod_kernel,
        grid=(n // tn,),
        in_specs=[pl.BlockSpec((rows, d), lambda j: (0, 0)),
                  pl.BlockSpec((d, tn), lambda j: (0, j)),
                  pl.BlockSpec((1, tn), lambda j: (0, j))],
        out_specs=pl.BlockSpec((rows, tn), lambda j: (0, j)),
        out_shape=jax.ShapeDtypeStruct((rows, n), F32),
        compiler_params=_cparams(("arbitrary",)),
        name="mod",
    )(cc, w_mod, b_mod.reshape(1, n))


def _rms_mod(xf, g, shift, scale):
    ms = jnp.mean(xf * xf, axis=-1, keepdims=True)
    return xf * lax.rsqrt(ms + RMS_EPS) * g * (1.0 + scale) + shift


def _proj_kernel(n_lat_tiles, n_tiles, d, x_ref, xprev_ref, xnext_ref, ctx_ref, mod_ref, g_ref,
                 wm_ref, wgab_ref, watt_ref, wg_ref, gbias_ref,
                 cos_ref, sin_ref, gq_ref, gk_ref, grp_ref, cw_ref, cb_ref,
                 qk_ref, pm_ref, qt_ref, kr_ref, vt_ref, gates_ref, gates_t_ref,
                 ext_ref, yq_ref, yk_ref, yv_ref, gpre_ref, xb_ref, xlo_ref):
    i = pl.program_id(0)
    tm = x_ref.shape[1]

    @pl.when(i == 0)
    def _():
        ext_ref[...] = jnp.zeros_like(ext_ref)
        yq_ref[...] = jnp.zeros_like(yq_ref)
        yk_ref[...] = jnp.zeros_like(yk_ref)
        yv_ref[...] = jnp.zeros_like(yv_ref)
        gpre_ref[...] = jnp.zeros_like(gpre_ref)

    ti = jnp.minimum(i, pl.num_programs(0) - 2) % n_tiles
    xf = jnp.where(ti < n_lat_tiles, x_ref[0], ctx_ref[0])
    mod = mod_ref[0]
    gain1 = g_ref[...]
    xn = _rms_mod(xf, gain1, mod[:, 0:d], mod[:, d:2 * d])
    xb = xn.astype(BF16)
    halo = jnp.concatenate([xprev_ref[0], xnext_ref[0]], axis=0)
    xb_ref[0:tm, :] = xb
    xb_ref[tm:, :] = _rms_mod(halo, gain1, mod[:, 0:d], mod[:, d:2 * d]).astype(BF16)
    xlo_ref[...] = (xn - xb.astype(F32)).astype(BF16)

    def project(w_ref, region):
        return jnp.dot(xb_ref[0:tm, :], w_ref[:, region * d:(region + 1) * d], preferred_element_type=F32)

    taps = cw_ref[...]
    conv_bias = cb_ref[...]
    n_q_cols = d // 2

    w2 = 2 * LANES

    def conv_columns(c_lo, c_hi):
        for c0 in range(c_lo, c_hi, w2):
            acc = jnp.zeros((tm, w2), F32) + conv_bias[:, c0:c0 + w2]
            for k in range(CONV_K):
                r0 = SUBLANES - CONV_K // 2 + k
                acc = acc + ext_ref[r0:r0 + tm, c0:c0 + w2] * taps[k:k + 1, c0:c0 + w2]
            out = _silu(acc)
            if c0 < n_q_cols:
                out = out * MLSTM_DQK ** -0.5
            qk_ref[0, :, c0:c0 + w2] = out.astype(BF16)

    cos = cos_ref[...]
    sin = sin_ref[...]
    grp = grp_ref[...]
    gq = gq_ref[...]
    gk = gk_ref[...]
    hw = 2 * DIFF_HEAD_DIM

    def scaled_sumsq(x2):
        ss = jnp.dot((x2 * x2).astype(BF16), grp, preferred_element_type=F32)
        return lax.rsqrt(ss * (1.0 / DIFF_HEAD_DIM) + RMS_EPS)

    def rotated(y):
        return y * cos + pltpu.roll(y, DIFF_HEAD_DIM, axis=1) * sin

    ones_rows = jnp.ones((V_ROWS - hw, tm), BF16)

    def attention_pair(pair):
        sl2 = slice(pair * 2 * hw, (pair + 1) * 2 * hw)
        yq2 = yq_ref[:, sl2]
        yk2 = yk_ref[:, sl2]
        qn = yq2 * scaled_sumsq(yq2)
        kn = yk2 * scaled_sumsq(yk2)
        for u in range(2):
            h = 2 * pair + u
            sl = slice(h * hw, (h + 1) * hw)
            su = slice(u * hw, (u + 1) * hw)
            qt_ref[0, sl, :] = (rotated(qn[:, su] * gq) * Q_SCALE_LOG2).T.astype(BF16)
            kr_ref[0, :, sl] = rotated(kn[:, su] * gk).astype(BF16)
            vt_ref[0, h * V_ROWS:h * V_ROWS + hw, :] = yv_ref[:, sl].T.astype(BF16)
            vt_ref[0, h * V_ROWS + hw:(h + 1) * V_ROWS, :] = ones_rows

    def gate_epilogue():
        g = gpre_ref[...]
        lf = _log_sigmoid(g)
        lf_hi = lf.astype(BF16)
        lf_lo = (lf - lf_hi.astype(F32)).astype(BF16)
        lf2 = jnp.concatenate([lf_hi, lf_lo], axis=1)
        rows = lax.broadcasted_iota(jnp.int32, (tm, tm), 0)
        cols = lax.broadcasted_iota(jnp.int32, (tm, tm), 1)
        lower = jnp.where(cols <= rows, 1.0, 0.0).astype(BF16)
        upper = jnp.where(cols >= rows, 1.0, 0.0).astype(BF16)
        cum_f2 = jnp.dot(lower, lf2, preferred_element_type=F32)
        cum_b2 = jnp.dot(upper, lf2, preferred_element_type=F32)
        cum_f = cum_f2[:, :LANES] + cum_f2[:, LANES:]
        cum_b = cum_b2[:, :LANES] + cum_b2[:, LANES:]
        lane = lax.broadcasted_iota(jnp.int32, (tm, LANES), 1)
        cum = jnp.where(lane < 2 * MLSTM_HEADS, cum_f, cum_b)
        a = g - pltpu.roll(cum, LANES - MLSTM_HEADS, axis=1)
        a = a * LOG2_E
        gates_ref[0] = jnp.where((lane % (2 * MLSTM_HEADS)) < MLSTM_HEADS, a, cum * LOG2_E)
        gates_t_ref[0, 0] = a.T[0:4 * MLSTM_HEADS, :]

    for r, (w_ref, region) in enumerate(((wm_ref, 1), (wm_ref, 2), (wgab_ref, 0), (wgab_ref, 1))):
        pm_ref[0, :, r * d:(r + 1) * d] = project(w_ref, region).astype(BF16)
        attention_pair(r)
    yq_ref[...] = project(watt_ref, 0)
    conv_columns(0, d // 2)
    yk_ref[...] = project(watt_ref, 1)
    conv_columns(d // 2, d)
    yv_ref[...] = project(watt_ref, 2)
    gate_epilogue()
    y_qk = jnp.dot(xb_ref[...], wm_ref[:, 0:d], preferred_element_type=F32)
    has_prev = jnp.logical_and(ti > 0, ti < n_lat_tiles)
    has_next = ti < n_lat_tiles - 1
    ext_ref[0:SUBLANES, :] = jnp.where(has_prev, y_qk[tm:tm + SUBLANES], 0.0)
    ext_ref[SUBLANES:SUBLANES + tm, :] = y_qk[0:tm]
    ext_ref[SUBLANES + tm:, :] = jnp.where(has_next, y_qk[tm + SUBLANES:], 0.0)
    g2 = (jnp.dot(xb_ref[0:tm, :], wg_ref[...], preferred_element_type=F32)
          + jnp.dot(xlo_ref[...], wg_ref[...], preferred_element_type=F32))
    gpre_ref[...] = g2[:, :LANES] + g2[:, LANES:] + gbias_ref[...]


def _projection(x, ctx, mods3, norm1, weights, wg, gate_bias, cos_t, sin_t, gq, gk, grp, conv_w8, conv_b):
    b, s, d = x.shape
    n_ctx = ctx.shape[1]
    tm = TOK_TILE
    assert tm == CHUNK
    assert n_ctx == tm
    n_lat_tiles = s // tm
    n_tiles = (s + n_ctx) // tm
    t = s + n_ctx
    w_main, w_gab, w_att = weights
    resident = lambda shape: pl.BlockSpec(shape, lambda i: (0, 0), pipeline_mode=pl.Buffered(1))
    da_w = DIFF_HEADS * 2 * DIFF_HEAD_DIM
    halo_stride = tm // SUBLANES
    last_halo = s // SUBLANES - 1
    const = lambda shape: pl.BlockSpec(shape, lambda i: (0,) * len(shape))
    n_pairs = b * n_tiles

    def cur(i):
        j = jnp.minimum(i, n_pairs - 1)
        return j // n_tiles, j % n_tiles

    def late(i):
        j = jnp.maximum(i - 1, 0)
        return j // n_tiles, j % n_tiles

    tok = lambda width: pl.BlockSpec((1, tm, width), lambda i: (late(i)[0], late(i)[1], 0))
    tok_t = lambda rows: pl.BlockSpec((1, rows, tm), lambda i: (late(i)[0], 0, late(i)[1]))
    n_ctx_tiles = n_ctx // tm
    return pl.pallas_call(
        functools.partial(_proj_kernel, n_lat_tiles, n_tiles, d),
        grid=(n_pairs + 1,),
        in_specs=[
            pl.BlockSpec((1, tm, d), lambda i: (cur(i)[0], jnp.minimum(cur(i)[1], n_lat_tiles - 1), 0)),
            pl.BlockSpec((1, SUBLANES, d),
                         lambda i: (cur(i)[0], jnp.clip(cur(i)[1] * halo_stride - 1, 0, last_halo), 0)),
            pl.BlockSpec((1, SUBLANES, d),
                         lambda i: (cur(i)[0], jnp.clip((cur(i)[1] + 1) * halo_stride, 0, last_halo), 0)),
            pl.BlockSpec((1, tm, d),
                         lambda i: (cur(i)[0], jnp.clip(cur(i)[1] - n_lat_tiles, 0, n_ctx_tiles - 1), 0)),
            pl.BlockSpec((1, 1, N_MOD * d),
                         lambda i: (jnp.where(cur(i)[1] < n_lat_tiles, cur(i)[0], b), 0, 0)),
            const((1, d)),
            resident(w_main.shape), resident(w_gab.shape), resident(w_att.shape),
            const((d, 2 * LANES)), const((1, LANES)),
            pl.BlockSpec((tm, LANES), lambda i: (late(i)[1], 0)),
            pl.BlockSpec((tm, LANES), lambda i: (late(i)[1], 0)),
            const((1, LANES)), const((1, LANES)), const((2 * LANES, 2 * LANES)),
            const((SUBLANES, d)), const((1, d)),
        ],
        out_specs=[tok(d), pl.BlockSpec((1, tm, 4 * d), lambda i: (cur(i)[0], cur(i)[1], 0)),
                   tok_t(da_w), tok(da_w), tok_t(DIFF_HEADS * V_ROWS), tok(LANES),
                   pl.BlockSpec((1, 1, 4 * MLSTM_HEADS, tm), lambda i: (late(i)[0], late(i)[1], 0, 0))],
        out_shape=[jax.ShapeDtypeStruct((b, t, d), BF16),
                   jax.ShapeDtypeStruct((b, t, 4 * d), BF16),
                   jax.ShapeDtypeStruct((b, da_w, t), BF16),
                   jax.ShapeDtypeStruct((b, t, da_w), BF16),
                   jax.ShapeDtypeStruct((b, DIFF_HEADS * V_ROWS, t), BF16),
                   jax.ShapeDtypeStruct((b, t, LANES), F32),
                   jax.ShapeDtypeStruct((b, n_tiles, 4 * MLSTM_HEADS, tm), F32)],
        scratch_shapes=[pltpu.VMEM((tm + 2 * SUBLANES, d), F32),
                        pltpu.VMEM((tm, da_w), F32), pltpu.VMEM((tm, da_w), F32), pltpu.VMEM((tm, da_w), F32),
                        pltpu.VMEM((tm, LANES), F32),
                        pltpu.VMEM((tm + 2 * SUBLANES, d), BF16),
                        pltpu.VMEM((tm, d), BF16)],
        compiler_params=_cparams(("arbitrary",)),
        name="proj",
    )(x, x, x, ctx, mods3, norm1, w_main, w_gab, w_att, wg, gate_bias, cos_t, sin_t, gq, gk, grp, conv_w8, conv_b)


def _wide(x, n_tiles):
    return jnp.concatenate([x] * n_tiles, axis=1)


def _mlstm_scores(q, k, c_aug):
    s = lax.dot_general(q, k, (((1,), (1,)), ((), ())), preferred_element_type=F32)
    inter = jnp.dot(q, c_aug.astype(BF16), preferred_element_type=F32)
    return s, inter


def _mlstm_decay(reverse, s, k, a_col, a_row, m_prev):
    l = s.shape[0]
    a_rep = jnp.broadcast_to(a_col, (l, LANES))
    rows = lax.broadcasted_iota(jnp.int32, (l, l), 0)
    cols = lax.broadcasted_iota(jnp.int32, (l, l), 1)
    visible = (cols >= rows) if reverse else (cols <= rows)
    a_mat = jnp.where(visible, a_row, -jnp.inf)
    m_rep = jnp.maximum(jnp.broadcast_to(jnp.max(a_mat, axis=-1, keepdims=True), (l, LANES)), m_prev)
    pmat = (s * jnp.exp2(a_mat - _wide(m_rep, l // LANES))).astype(BF16)
    m_last = jnp.maximum(jnp.max(a_rep, axis=0, keepdims=True), m_prev)
    kw = (jnp.exp2(a_rep - m_last) * k.astype(F32)).astype(BF16)
    return pmat, kw, m_rep, m_last


def _mlstm_finish(pmat, kw, v_aug, inter, m_rep, m_last, m_prev, lf_cum_col, lf_total, c_aug):
    l = pmat.shape[0]
    dv = v_aug.shape[1] - LANES
    n_wide = v_aug.shape[1] // LANES
    intra = jnp.dot(pmat, v_aug, preferred_element_type=F32)
    numden = intra + _wide(jnp.exp2(m_prev - m_rep), n_wide) * inter
    den = numden[:, dv:]
    floor = jnp.exp2(-(jnp.broadcast_to(lf_cum_col, (l, LANES)) + m_rep))
    h = numden[:, :dv] * _wide(1.0 / jnp.maximum(jnp.abs(den), floor), dv // LANES)
    kv = lax.dot_general(kw, v_aug, (((0,), (0,)), ((), ())), preferred_element_type=F32)
    c_new = _wide(jnp.exp2(m_prev - m_last), n_wide) * c_aug + kv
    return h, c_new, lf_total + m_last


def _mlstm_kernel(qf_ref, kf_ref, vf_ref, gf_ref, gtf_ref, qb_ref, kb_ref, vb_ref, gb_ref, gtb_ref,
                  hf_ref, hb_ref, c_ref, m_ref):
    s_idx = pl.program_id(1)

    @pl.when(s_idx == 0)
    def _():
        c_ref[...] = jnp.zeros_like(c_ref)
        m_ref[...] = jnp.zeros_like(m_ref)

    nb, l = qf_ref.shape[0], qf_ref.shape[1]
    nh = MLSTM_HEADS
    ones_col = jnp.ones((l, LANES), BF16)
    scans = []
    for bb in range(nb):
        for reverse, q_ref, k_ref, v_ref, g_ref, gt_ref, h_ref in (
                (False, qf_ref, kf_ref, vf_ref, gf_ref, gtf_ref, hf_ref),
                (True, qb_ref, kb_ref, vb_ref, gb_ref, gtb_ref, hb_ref)):
            base = 2 * nh if reverse else 0
            for h in range(nh):
                scans.append(dict(bb=bb, reverse=reverse, h=h, ca=base + h, cc=base + nh + h,
                                  hd=bb * 2 * nh + (nh if reverse else 0) + h,
                                  q_ref=q_ref, k_ref=k_ref, v_ref=v_ref, g_ref=g_ref, gt_ref=gt_ref, h_ref=h_ref))
    qk_cols = lambda sc: slice(sc["h"] * MLSTM_DQK, (sc["h"] + 1) * MLSTM_DQK)
    v_cols = lambda sc: slice(sc["h"] * MLSTM_DV, (sc["h"] + 1) * MLSTM_DV)
    for g0 in range(0, len(scans), MLSTM_GROUP):
        group = scans[g0:g0 + MLSTM_GROUP]
        for sc in group:
            sc["s"], sc["inter"] = _mlstm_scores(sc["q_ref"][sc["bb"], :, qk_cols(sc)],
                                                 sc["k_ref"][sc["bb"], :, qk_cols(sc)], c_ref[sc["hd"]])
        for sc in group:
            g = sc["g_ref"][sc["bb"]]
            g_t = sc["gt_ref"][sc["bb"], 0]
            m_prev = m_ref[sc["hd"]:sc["hd"] + 1, :]
            sc["pmat"], sc["kw"], sc["m_rep"], sc["m_last"] = _mlstm_decay(
                sc["reverse"], sc["s"], sc["k_ref"][sc["bb"], :, qk_cols(sc)],
                g[:, sc["ca"]:sc["ca"] + 1], g_t[sc["ca"]:sc["ca"] + 1, :], m_prev)
        for sc in group:
            g = sc["g_ref"][sc["bb"]]
            cc, hd = sc["cc"], sc["hd"]
            lf_total = g[0:1, cc:cc + 1] if sc["reverse"] else g[l - 1:l, cc:cc + 1]
            v_aug = jnp.concatenate([sc["v_ref"][sc["bb"], :, v_cols(sc)], ones_col], axis=1)
            hval, c_new, m_new = _mlstm_finish(sc["pmat"], sc["kw"], v_aug, sc["inter"], sc["m_rep"],
                                               sc["m_last"], m_ref[hd:hd + 1, :], g[:, cc:cc + 1], lf_total,
                                               c_ref[hd])
            sc["h_ref"][sc["bb"], :, v_cols(sc)] = hval.astype(sc["h_ref"].dtype)
            c_ref[hd] = c_new
            m_ref[hd:hd + 1, :] = m_new


def _mlstm(qk, pm, gates, gates_t, n_lat, n_ctx, v_region):
    b, t, _ = qk.shape
    l = CHUNK
    n_lat_c = n_lat // l
    n_ctx_c = n_ctx // l
    n_steps = n_lat_c + n_ctx_c
    qw = MLSTM_HEADS * MLSTM_DQK
    vw = MLSTM_HEADS * MLSTM_DV

    def fwd(si):
        return jnp.where(si < n_ctx_c, n_lat_c + si, si - n_ctx_c)

    def bwd(si):
        return jnp.where(si < n_ctx_c, n_steps - 1 - si, n_steps - 1 - si)

    nb = MLSTM_BATCH
    assert b % nb == 0
    in_specs = []
    for order in (fwd, bwd):
        in_specs += [
            pl.BlockSpec((nb, l, qw), lambda bi, si, o=order: (bi, o(si), 0)),
            pl.BlockSpec((nb, l, qw), lambda bi, si, o=order: (bi, o(si), 1)),
            pl.BlockSpec((nb, l, vw), lambda bi, si, o=order: (bi, o(si), v_region)),
            pl.BlockSpec((nb, l, LANES), lambda bi, si, o=order: (bi, o(si), 0)),
            pl.BlockSpec((nb, 1, 4 * MLSTM_HEADS, l), lambda bi, si, o=order: (bi, o(si), 0, 0)),
        ]
    out_specs = [pl.BlockSpec((nb, l, vw), lambda bi, si: (bi, fwd(si), 0)),
                 pl.BlockSpec((nb, l, vw), lambda bi, si: (bi, bwd(si), 0))]
    n_scans = nb * 2 * MLSTM_HEADS
    return pl.pallas_call(
        _mlstm_kernel,
        grid=(b // nb, n_steps),
        in_specs=in_specs,
        out_specs=out_specs,
        out_shape=[jax.ShapeDtypeStruct((b, t, vw), BF16)] * 2,
        scratch_shapes=[pltpu.VMEM((n_scans, MLSTM_DQK, MLSTM_DV + LANES), F32),
                        pltpu.VMEM((n_scans, LANES), F32)],
        compiler_params=_cparams(("parallel", "arbitrary")),
        name="mlstm",
    )(qk, qk, pm, gates, gates_t, qk, qk, pm, gates, gates_t)


def _attn_kernel(n_keys, tq, qt_ref, k_ref, vt_ref, lam_ref, gain_ref, o_ref, s_ref, pm_ref, e_ref):
    nq = qt_ref.shape[2] // tq
    width = 2 * tq
    hw = o_ref.shape[2]
    row = lax.broadcasted_iota(jnp.int32, (qt_ref.shape[1], tq), 0)
    first = (row % DIFF_HEAD_DIM) < (DIFF_HEAD_DIM // 2)

    def scores(qb):
        qt = qt_ref[0, :, pl.ds(pl.multiple_of(qb * tq, tq), tq)]
        zero = jnp.zeros_like(qt)
        qq = jnp.concatenate([jnp.where(first, qt, zero), jnp.where(first, zero, qt)], axis=1)
        s = jnp.dot(k_ref[0], qq, preferred_element_type=F32)
        s_ref[...] = s
        pm_ref[...] = jnp.max(s.reshape(n_keys // SUBLANES, SUBLANES, width), axis=0)

    def softmax():
        m = jnp.max(pm_ref[...], axis=0, keepdims=True)
        e_ref[...] = jnp.exp2(s_ref[...] - m).astype(BF16)

    def output(qb):
        lv = lam_ref[...]
        lam = (jnp.exp(jnp.sum(lv[0:1] * lv[1:2], axis=-1, keepdims=True))
               - jnp.exp(jnp.sum(lv[2:3] * lv[3:4], axis=-1, keepdims=True)) + LAM_INIT_L0)
        pv = jnp.dot(vt_ref[0], e_ref[...], preferred_element_type=F32)
        o_all = pv[0:hw, :] * (1.0 / pv[hw:hw + 1, :])
        o_t = o_all[:, :tq] - lam * o_all[:, tq:]
        ms = jnp.mean(o_t * o_t, axis=0, keepdims=True)
        out = (o_t * lax.rsqrt(ms + RMS_EPS)).T * (gain_ref[...] * (1.0 - LAM_INIT_L0))
        o_ref[0, pl.ds(pl.multiple_of(qb * tq, tq), tq), :] = out.astype(o_ref.dtype)

    scores(0)
    softmax()
    scores(1)

    def iteration(t, carry):
        output(t - 2)
        softmax()
        scores(t)
        return carry

    lax.fori_loop(2, nq, iteration, 0)
    output(nq - 2)
    softmax()
    output(nq - 1)


def _attention(qt, kr, vt, lam_vecs, diff_gain, n_lat):
    b, t, _ = kr.shape
    tq = ATT_TQ
    hw = 2 * DIFF_HEAD_DIM
    nq = n_lat // tq
    assert nq >= 2
    return pl.pallas_call(
        functools.partial(_attn_kernel, t, tq),
        grid=(b, DIFF_HEADS),
        in_specs=[pl.BlockSpec((1, hw, n_lat), lambda bi, h: (bi, h, 0)),
                  pl.BlockSpec((1, t, hw), lambda bi, h: (bi, 0, h)),
                  pl.BlockSpec((1, V_ROWS, t), lambda bi, h: (bi, h, 0)),
                  pl.BlockSpec(lam_vecs.shape, lambda bi, h: (0, 0)),
                  pl.BlockSpec((1, hw), lambda bi, h: (0, 0))],
        out_specs=pl.BlockSpec((1, n_lat, hw), lambda bi, h: (bi, 0, h)),
        out_shape=jax.ShapeDtypeStruct((b, n_lat, DIFF_HEADS * hw), BF16),
        scratch_shapes=[pltpu.VMEM((t, 2 * tq), F32),
                        pltpu.VMEM((SUBLANES, 2 * tq), F32),
                        pltpu.VMEM((t, 2 * tq), BF16)],
        compiler_params=_cparams(("parallel", "arbitrary")),
        name="attn",
    )(qt, kr, vt, lam_vecs, diff_gain)


def _merge_kernel(d, hf_ref, hb_ref, mo_ref, ga_ref, gb_ref, hB_ref, x_ref, mod_ref, gain_ref,
                  wa_ref, wb_ref, wo_ref, o_ref):
    gain = gain_ref[...]
    yb = jnp.dot(hB_ref[0], wb_ref[...], preferred_element_type=F32)
    ya = None
    for h in range(MLSTM_HEADS):
        sl = slice(h * MLSTM_DV, (h + 1) * MLSTM_DV)
        hs = hf_ref[0, :, sl].astype(F32) + hb_ref[0, :, sl].astype(F32)
        ms = jnp.mean(hs * hs, axis=-1, keepdims=True)
        hn = hs * lax.rsqrt(ms + RMS_EPS) * gain[:, sl]
        h_a = (_sigmoid(mo_ref[0, :, sl].astype(F32)) * hn).astype(BF16)
        part = jnp.dot(h_a, wa_ref[sl, :], preferred_element_type=F32)
        ya = part if ya is None else ya + part
    y = _sigmoid(ga_ref[0].astype(F32)) * ya + _sigmoid(gb_ref[0].astype(F32)) * yb
    yo = jnp.dot(y.astype(BF16), wo_ref[...], preferred_element_type=F32)
    g1 = mod_ref[0][:, 2 * d:3 * d]
    o_ref[0] = x_ref[0] + g1 * yo


def _merge(hf, hb, p, h_b, x, mods3, mlstm_gain, wa, wb, wo, mo_region, ga_region, gb_region):
    b, s, d = x.shape
    tm = LAT_TILE
    wspec = lambda shape: pl.BlockSpec(shape, lambda bi, i: (0, 0), pipeline_mode=pl.Buffered(1))
    tok = lambda region: pl.BlockSpec((1, tm, d), lambda bi, i, r=region: (bi, i, r))
    return pl.pallas_call(
        functools.partial(_merge_kernel, d),
        grid=(b, s // tm),
        in_specs=[tok(0), tok(0), tok(mo_region), tok(ga_region), tok(gb_region), tok(0), tok(0),
                  pl.BlockSpec((1, 1, N_MOD * d), lambda bi, i: (bi, 0, 0)),
                  pl.BlockSpec((1, d), lambda bi, i: (0, 0)),
                  wspec(wa.shape), wspec(wb.shape), wspec(wo.shape)],
        out_specs=pl.BlockSpec((1, tm, d), lambda bi, i: (bi, i, 0)),
        out_shape=jax.ShapeDtypeStruct((b, s, d), F32),
        compiler_params=_cparams(("parallel", "arbitrary")),
        name="merge",
    )(hf, hb, p, p, p, h_b, x, mods3, mlstm_gain, wa, wb, wo)


def _ffn_kernel(d, hidden, x_ref, mod_ref, g_ref, wi_ref, wo_ref, o_ref):
    x1 = x_ref[0]
    mod = mod_ref[0]
    xn = _rms_mod(x1, g_ref[...], mod[:, 3 * d:4 * d], mod[:, 4 * d:5 * d]).astype(BF16)
    a = jnp.dot(xn, wi_ref[:, :hidden], preferred_element_type=F32)
    bb = jnp.dot(xn, wi_ref[:, hidden:], preferred_element_type=F32)
    hmid = (_silu(a) * bb).astype(BF16)
    y = jnp.dot(hmid, wo_ref[...], preferred_element_type=F32)
    o_ref[0] = x1 + mod[:, 5 * d:6 * d] * y


def _ffn(x1, mods3, norm2, wi, wo):
    b, s, d = x1.shape
    hidden = wo.shape[0]
    tm = LAT_TILE
    wspec = lambda shape: pl.BlockSpec(shape, lambda bi, i: (0, 0), pipeline_mode=pl.Buffered(1))
    return pl.pallas_call(
        functools.partial(_ffn_kernel, d, hidden),
        grid=(b, s // tm),
        in_specs=[pl.BlockSpec((1, tm, d), lambda bi, i: (bi, i, 0)),
                  pl.BlockSpec((1, 1, N_MOD * d), lambda bi, i: (bi, 0, 0)),
                  pl.BlockSpec((1, d), lambda bi, i: (0, 0)),
                  wspec(wi.shape), wspec(wo.shape)],
        out_specs=pl.BlockSpec((1, tm, d), lambda bi, i: (bi, i, 0)),
        out_shape=jax.ShapeDtypeStruct((b, s, d), F32),
        compiler_params=_cparams(("parallel", "arbitrary")),
        name="ffn",
    )(x1, mods3, norm2, wi, wo)


def _diff_lane_perm():
    l = np.arange(2 * DIFF_HEAD_DIM)
    half, r = l // DIFF_HEAD_DIM, l % DIFF_HEAD_DIM
    qsel, axis, f = r // 32, (r % 32) // ROPE_FREQS, r % ROPE_FREQS
    return qsel * DIFF_HEAD_DIM + axis * 2 * ROPE_FREQS + half * ROPE_FREQS + f, qsel, axis, f, half


def _rope_tables(n_lat, n_ctx):
    _, _, axis, f, half = _diff_lane_perm()
    pos = np.arange(n_lat)
    row, col = pos // GRID_W, pos % GRID_W
    inv = (ROPE_BASE ** (-np.arange(ROPE_FREQS, dtype=np.float32) / ROPE_FREQS)).astype(np.float32)
    ang = (np.where(axis[None, :] == 0, row[:, None], col[:, None]).astype(np.float32) * inv[f][None, :])
    cos = np.concatenate([np.cos(ang), np.ones((n_ctx, LANES))], axis=0)
    sin = np.concatenate([np.sin(ang) * np.where(half == 0, -1.0, 1.0)[None, :], np.zeros((n_ctx, LANES))], axis=0)
    return jnp.asarray(cos, F32), jnp.asarray(sin, F32)


def kernel(x, c, ctx, c_ctx, w_mod, b_mod, norm1, norm2, w_in, b_gate, conv_w, conv_b, mlstm_norm, q_norm, k_norm,
           lam_vecs, diff_norm, w_branch_a, w_branch_b, w_out, w_ffn_in, w_ffn_out):
    b, s, d = x.shape
    n_ctx = ctx.shape[1]
    assert w_mod.shape[0] == 1, "single-layer block"
    assert s % CHUNK == 0 and n_ctx % CHUNK == 0 and s % ATT_TQ == 0 and s % LAT_TILE == 0
    qk_w = MLSTM_HEADS * MLSTM_DQK
    v_w = MLSTM_HEADS * MLSTM_DV
    da_w = DIFF_HEADS * 2 * DIFF_HEAD_DIM
    n_gates = 4 * MLSTM_HEADS
    sizes = (qk_w, qk_w, v_w, v_w, n_gates, da_w, da_w, da_w, d, d)
    offs = np.concatenate([[0], np.cumsum(sizes)])
    seg = lambda i: w_in[0][:, offs[i]:offs[i + 1]]

    perm, qsel, _, _, _ = _diff_lane_perm()
    head_perm = (np.arange(DIFF_HEADS)[:, None] * 2 * DIFF_HEAD_DIM + perm[None, :]).reshape(-1)
    w_main = w_in[0][:, offs[0]:offs[4]].astype(BF16)
    w_gab = w_in[0][:, offs[8]:offs[10]].astype(BF16)
    segb = lambda i: seg(i).astype(BF16)
    w_att = jnp.concatenate([segb(5)[:, head_perm], segb(6)[:, head_perm], segb(7)], axis=1)
    R_V, R_O, R_GA, R_GB = range(4)
    wg_f32 = jnp.pad(seg(4), ((0, 0), (0, LANES - n_gates)))
    wg_hi = wg_f32.astype(BF16)
    wg = jnp.concatenate([wg_hi, (wg_f32 - wg_hi.astype(F32)).astype(BF16)], axis=1)
    gate_bias = jnp.pad(b_gate[0], (0, LANES - n_gates)).reshape(1, LANES)
    conv_w8 = jnp.pad(conv_w[0], ((0, SUBLANES - CONV_K), (0, 0)))
    gq = q_norm[0][perm % DIFF_HEAD_DIM].reshape(1, LANES)
    gk = k_norm[0][perm % DIFF_HEAD_DIM].reshape(1, LANES)
    unit = np.concatenate([qsel, 2 + qsel])
    grp = jnp.asarray(unit[:, None] == unit[None, :], BF16)
    cos_t, sin_t = _rope_tables(s, n_ctx)

    cc = jnp.concatenate([c, c_ctx[None, :], jnp.zeros((SUBLANES - b - 1, d), F32)], axis=0)
    mods = _modulation(cc, w_mod[0], b_mod[0])
    mods3 = mods.reshape(SUBLANES, 1, N_MOD * d)

    qk, pm, qt, kr, vt, gates, gates_t = _projection(x, ctx, mods3, norm1, (w_main, w_gab, w_att), wg, gate_bias,
                                                     cos_t, sin_t, gq, gk, grp, conv_w8, conv_b)
    hf, hb = _mlstm(qk, pm, gates, gates_t, s, n_ctx, R_V)
    h_b = _attention(qt, kr, vt, lam_vecs[0], diff_norm, s)
    x1 = _merge(hf, hb, pm, h_b, x, mods3, mlstm_norm, w_branch_a[0].astype(BF16), w_branch_b[0].astype(BF16),
                w_out[0].astype(BF16), R_O, R_GA, R_GB)
    return _ffn(x1, mods3, norm2, w_ffn_in[0].astype(BF16), w_ffn_out[0].astype(BF16))
```

```python
import functools
import math

import jax
import jax.numpy as jnp
import numpy as np
from jax import lax
from jax.experimental import pallas as pl
from jax.experimental.pallas import tpu as pltpu

F32 = jnp.float32
BF16 = jnp.bfloat16
HIGHEST = lax.Precision.HIGHEST

GRID_W = 64
N_MOD = 6
MLSTM_HEADS = 4
MLSTM_DQK = 128
MLSTM_DV = 256
CONV_K = 5
DIFF_HEADS = 8
DIFF_HEAD_DIM = 64
ROPE_BASE = 10000.0
ROPE_FREQS = DIFF_HEAD_DIM // 4
RMS_EPS = 1e-6
LAM_INIT_L0 = 0.8 - 0.6 * math.exp(-0.3 * 0)
LOG2_E = math.log2(math.e)
Q_SCALE_LOG2 = DIFF_HEAD_DIM ** -0.5 * LOG2_E
LANES = 128
SUBLANES = 8
BF16_SUBLANES = 16
V_ROWS = 2 * DIFF_HEAD_DIM + BF16_SUBLANES
VMEM_LIMIT = 56 * 1024 * 1024

TOK_TILE = 256
LAT_TILE = 512
CHUNK = 256
MLSTM_BATCH = 4
MLSTM_GROUP = 2
ATT_TQ = 256


def _cparams(sem):
    return pltpu.CompilerParams(dimension_semantics=sem, vmem_limit_bytes=VMEM_LIMIT)


def _sigmoid(x):
    return 0.5 + 0.5 * jnp.tanh(0.5 * x)


def _silu(x):
    h = 0.5 * x
    return h + h * jnp.tanh(h)


def _log_sigmoid(x):
    return jnp.minimum(x, 0.0) - jnp.log(1.0 + jnp.exp(-jnp.abs(x)))


def _mod_kernel(c_ref, w_ref, b_ref, o_ref):
    a = _silu(c_ref[...])
    o_ref[...] = jnp.dot(a, w_ref[...], preferred_element_type=F32, precision=HIGHEST) + b_ref[...]


def _modulation(cc, w_mod, b_mod):
    rows, d = cc.shape
    n = w_mod.shape[1]
    tn = 1024
    return pl.pallas_call(
        _mod_kernel,
        grid=(n // tn,),
        in_specs=[pl.BlockSpec((rows, d), lambda j: (0, 0)),
                  pl.BlockSpec((d, tn), lambda j: (0, j)),
                  pl.BlockSpec((1, tn), lambda j: (0, j))],
        out_specs=pl.BlockSpec((rows, tn), lambda j: (0, j)),
        out_shape=jax.ShapeDtypeStruct((rows, n), F32),
        compiler_params=_cparams(("arbitrary",)),
        name="mod",
    )(cc, w_mod, b_mod.reshape(1, n))


def _rms_mod(xf, g, shift, scale):
    ms = jnp.mean(xf * xf, axis=-1, keepdims=True)
    return xf * lax.rsqrt(ms + RMS_EPS) * g * (1.0 + scale) + shift


def _proj_kernel(n_lat_tiles, n_tiles, d, x_ref, xprev_ref, xnext_ref, ctx_ref, mod_ref, g_ref,
                 wm_ref, wgab_ref, watt_ref, wg_ref, gbias_ref,
                 cos_ref, sin_ref, gq_ref, gk_ref, grp_ref, cw_ref, cb_ref,
                 qk_ref, pm_ref, qt_ref, kr_ref, vt_ref, gates_ref, gates_t_ref,
                 ext_ref, yq_ref, yk_ref, yv_ref, gpre_ref, xb_ref, xlo_ref):
    i = pl.program_id(0)
    tm = x_ref.shape[1]

    @pl.when(i == 0)
    def _():
        ext_ref[...] = jnp.zeros_like(ext_ref)
        yq_ref[...] = jnp.zeros_like(yq_ref)
        yk_ref[...] = jnp.zeros_like(yk_ref)
        yv_ref[...] = jnp.zeros_like(yv_ref)
        gpre_ref[...] = jnp.zeros_like(gpre_ref)

    ti = jnp.minimum(i, pl.num_programs(0) - 2) % n_tiles
    xf = jnp.where(ti < n_lat_tiles, x_ref[0], ctx_ref[0])
    mod = mod_ref[0]
    gain1 = g_ref[...]
    xn = _rms_mod(xf, gain1, mod[:, 0:d], mod[:, d:2 * d])
    xb = xn.astype(BF16)
    halo = jnp.concatenate([xprev_ref[0], xnext_ref[0]], axis=0)
    xb_ref[0:tm, :] = xb
    xb_ref[tm:, :] = _rms_mod(halo, gain1, mod[:, 0:d], mod[:, d:2 * d]).astype(BF16)
    xlo_ref[...] = (xn - xb.astype(F32)).astype(BF16)

    def project(w_ref, region):
        return jnp.dot(xb_ref[0:tm, :], w_ref[:, region * d:(region + 1) * d], preferred_element_type=F32)

    taps = cw_ref[...]
    conv_bias = cb_ref[...]
    n_q_cols = d // 2

    w2 = 2 * LANES

    def conv_columns(c_lo, c_hi):
        for c0 in range(c_lo, c_hi, w2):
            acc = jnp.zeros((tm, w2), F32) + conv_bias[:, c0:c0 + w2]
            for k in range(CONV_K):
                r0 = SUBLANES - CONV_K // 2 + k
                acc = acc + ext_ref[r0:r0 + tm, c0:c0 + w2] * taps[k:k + 1, c0:c0 + w2]
            out = _silu(acc)
            if c0 < n_q_cols:
                out = out * MLSTM_DQK ** -0.5
            qk_ref[0, :, c0:c0 + w2] = out.astype(BF16)

    cos = cos_ref[...]
    sin = sin_ref[...]
    grp = grp_ref[...]
    gq = gq_ref[...]
    gk = gk_ref[...]
    hw = 2 * DIFF_HEAD_DIM

    def scaled_sumsq(x2):
        ss = jnp.dot((x2 * x2).astype(BF16), grp, preferred_element_type=F32)
        return lax.rsqrt(ss * (1.0 / DIFF_HEAD_DIM) + RMS_EPS)

    def rotated(y):
        return y * cos + pltpu.roll(y, DIFF_HEAD_DIM, axis=1) * sin

    ones_rows = jnp.ones((V_ROWS - hw, tm), BF16)

    def attention_pair(pair):
        sl2 = slice(pair * 2 * hw, (pair + 1) * 2 * hw)
        yq2 = yq_ref[:, sl2]
        yk2 = yk_ref[:, sl2]
        qn = yq2 * scaled_sumsq(yq2)
        kn = yk2 * scaled_sumsq(yk2)
        for u in range(2):
            h = 2 * pair + u
            sl = slice(h * hw, (h + 1) * hw)
            su = slice(u * hw, (u + 1) * hw)
            qt_ref[0, sl, :] = (rotated(qn[:, su] * gq) * Q_SCALE_LOG2).T.astype(BF16)
            kr_ref[0, :, sl] = rotated(kn[:, su] * gk).astype(BF16)
            vt_ref[0, h * V_ROWS:h * V_ROWS + hw, :] = yv_ref[:, sl].T.astype(BF16)
            vt_ref[0, h * V_ROWS + hw:(h + 1) * V_ROWS, :] = ones_rows

    def gate_epilogue():
        g = gpre_ref[...]
        lf = _log_sigmoid(g)
        lf_hi = lf.astype(BF16)
        lf_lo = (lf - lf_hi.astype(F32)).astype(BF16)
        lf2 = jnp.concatenate([lf_hi, lf_lo], axis=1)
        rows = lax.broadcasted_iota(jnp.int32, (tm, tm), 0)
        cols = lax.broadcasted_iota(jnp.int32, (tm, tm), 1)
        lower = jnp.where(cols <= rows, 1.0, 0.0).astype(BF16)
        upper = jnp.where(cols >= rows, 1.0, 0.0).astype(BF16)
        cum_f2 = jnp.dot(lower, lf2, preferred_element_type=F32)
        cum_b2 = jnp.dot(upper, lf2, preferred_element_type=F32)
        cum_f = cum_f2[:, :LANES] + cum_f2[:, LANES:]
        cum_b = cum_b2[:, :LANES] + cum_b2[:, LANES:]
        lane = lax.broadcasted_iota(jnp.int32, (tm, LANES), 1)
        cum = jnp.where(lane < 2 * MLSTM_HEADS, cum_f, cum_b)
        a = g - pltpu.roll(cum, LANES - MLSTM_HEADS, axis=1)
        a = a * LOG2_E
        gates_ref[0] = jnp.where((lane % (2 * MLSTM_HEADS)) < MLSTM_HEADS, a, cum * LOG2_E)
        gates_t_ref[0, 0] = a.T[0:4 * MLSTM_HEADS, :]

    for r, (w_ref, region) in enumerate(((wm_ref, 1), (wm_ref, 2), (wgab_ref, 0), (wgab_ref, 1))):
        pm_ref[0, :, r * d:(r + 1) * d] = project(w_ref, region).astype(BF16)
        attention_pair(r)
    yq_ref[...] = project(watt_ref, 0)
    conv_columns(0, d // 2)
    yk_ref[...] = project(watt_ref, 1)
    conv_columns(d // 2, d)
    yv_ref[...] = project(watt_ref, 2)
    gate_epilogue()
    y_qk = jnp.dot(xb_ref[...], wm_ref[:, 0:d], preferred_element_type=F32)
    has_prev = jnp.logical_and(ti > 0, ti < n_lat_tiles)
    has_next = ti < n_lat_tiles - 1
    ext_ref[0:SUBLANES, :] = jnp.where(has_prev, y_qk[tm:tm + SUBLANES], 0.0)
    ext_ref[SUBLANES:SUBLANES + tm, :] = y_qk[0:tm]
    ext_ref[SUBLANES + tm:, :] = jnp.where(has_next, y_qk[tm + SUBLANES:], 0.0)
    g2 = (jnp.dot(xb_ref[0:tm, :], wg_ref[...], preferred_element_type=F32)
          + jnp.dot(xlo_ref[...], wg_ref[...], preferred_element_type=F32))
    gpre_ref[...] = g2[:, :LANES] + g2[:, LANES:] + gbias_ref[...]


def _projection(x, ctx, mods3, norm1, weights, wg, gate_bias, cos_t, sin_t, gq, gk, grp, conv_w8, conv_b):
    b, s, d = x.shape
    n_ctx = ctx.shape[1]
    tm = TOK_TILE
    assert tm == CHUNK
    assert n_ctx == tm
    n_lat_tiles = s // tm
    n_tiles = (s + n_ctx) // tm
    t = s + n_ctx
    w_main, w_gab, w_att = weights
    resident = lambda shape: pl.BlockSpec(shape, lambda i: (0, 0), pipeline_mode=pl.Buffered(1))
    da_w = DIFF_HEADS * 2 * DIFF_HEAD_DIM
    halo_stride = tm // SUBLANES
    last_halo = s // SUBLANES - 1
    const = lambda shape: pl.BlockSpec(shape, lambda i: (0,) * len(shape))
    n_pairs = b * n_tiles

    def cur(i):
        j = jnp.minimum(i, n_pairs - 1)
        return j // n_tiles, j % n_tiles

    def late(i):
        j = jnp.maximum(i - 1, 0)
        return j // n_tiles, j % n_tiles

    tok = lambda width: pl.BlockSpec((1, tm, width), lambda i: (late(i)[0], late(i)[1], 0))
    tok_t = lambda rows: pl.BlockSpec((1, rows, tm), lambda i: (late(i)[0], 0, late(i)[1]))
    n_ctx_tiles = n_ctx // tm
    return pl.pallas_call(
        functools.partial(_proj_kernel, n_lat_tiles, n_tiles, d),
        grid=(n_pairs + 1,),
        in_specs=[
            pl.BlockSpec((1, tm, d), lambda i: (cur(i)[0], jnp.minimum(cur(i)[1], n_lat_tiles - 1), 0)),
            pl.BlockSpec((1, SUBLANES, d),
                         lambda i: (cur(i)[0], jnp.clip(cur(i)[1] * halo_stride - 1, 0, last_halo), 0)),
            pl.BlockSpec((1, SUBLANES, d),
                         lambda i: (cur(i)[0], jnp.clip((cur(i)[1] + 1) * halo_stride, 0, last_halo), 0)),
            pl.BlockSpec((1, tm, d),
                         lambda i: (cur(i)[0], jnp.clip(cur(i)[1] - n_lat_tiles, 0, n_ctx_tiles - 1), 0)),
            pl.BlockSpec((1, 1, N_MOD * d),
                         lambda i: (jnp.where(cur(i)[1] < n_lat_tiles, cur(i)[0], b), 0, 0)),
            const((1, d)),
            resident(w_main.shape), resident(w_gab.shape), resident(w_att.shape),
            const((d, 2 * LANES)), const((1, LANES)),
            pl.BlockSpec((tm, LANES), lambda i: (late(i)[1], 0)),
            pl.BlockSpec((tm, LANES), lambda i: (late(i)[1], 0)),
            const((1, LANES)), const((1, LANES)), const((2 * LANES, 2 * LANES)),
            const((SUBLANES, d)), const((1, d)),
        ],
        out_specs=[tok(d), pl.BlockSpec((1, tm, 4 * d), lambda i: (cur(i)[0], cur(i)[1], 0)),
                   tok_t(da_w), tok(da_w), tok_t(DIFF_HEADS * V_ROWS), tok(LANES),
                   pl.BlockSpec((1, 1, 4 * MLSTM_HEADS, tm), lambda i: (late(i)[0], late(i)[1], 0, 0))],
        out_shape=[jax.ShapeDtypeStruct((b, t, d), BF16),
                   jax.ShapeDtypeStruct((b, t, 4 * d), BF16),
                   jax.ShapeDtypeStruct((b, da_w, t), BF16),
                   jax.ShapeDtypeStruct((b, t, da_w), BF16),
                   jax.ShapeDtypeStruct((b, DIFF_HEADS * V_ROWS, t), BF16),
                   jax.ShapeDtypeStruct((b, t, LANES), F32),
                   jax.ShapeDtypeStruct((b, n_tiles, 4 * MLSTM_HEADS, tm), F32)],
        scratch_shapes=[pltpu.VMEM((tm + 2 * SUBLANES, d), F32),
                        pltpu.VMEM((tm, da_w), F32), pltpu.VMEM((tm, da_w), F32), pltpu.VMEM((tm, da_w), F32),
                        pltpu.VMEM((tm, LANES), F32),
                        pltpu.VMEM((tm + 2 * SUBLANES, d), BF16),
                        pltpu.VMEM((tm, d), BF16)],
        compiler_params=_cparams(("arbitrary",)),
        name="proj",
    )(x, x, x, ctx, mods3, norm1, w_main, w_gab, w_att, wg, gate_bias, cos_t, sin_t, gq, gk, grp, conv_w8, conv_b)


def _wide(x, n_tiles):
    return jnp.concatenate([x] * n_tiles, axis=1)


def _mlstm_scores(q, k, c_aug):
    s = lax.dot_general(q, k, (((1,), (1,)), ((), ())), preferred_element_type=F32)
    inter = jnp.dot(q, c_aug.astype(BF16), preferred_element_type=F32)
    return s, inter


def _mlstm_decay(reverse, s, k, a_col, a_row, m_prev):
    l = s.shape[0]
    a_rep = jnp.broadcast_to(a_col, (l, LANES))
    rows = lax.broadcasted_iota(jnp.int32, (l, l), 0)
    cols = lax.broadcasted_iota(jnp.int32, (l, l), 1)
    visible = (cols >= rows) if reverse else (cols <= rows)
    a_mat = jnp.where(visible, a_row, -jnp.inf)
    m_rep = jnp.maximum(jnp.broadcast_to(jnp.max(a_mat, axis=-1, keepdims=True), (l, LANES)), m_prev)
    pmat = (s * jnp.exp2(a_mat - _wide(m_rep, l // LANES))).astype(BF16)
    m_last = jnp.maximum(jnp.max(a_rep, axis=0, keepdims=True), m_prev)
    kw = (jnp.exp2(a_rep - m_last) * k.astype(F32)).astype(BF16)
    return pmat, kw, m_rep, m_last


def _mlstm_finish(pmat, kw, v_aug, inter, m_rep, m_last, m_prev, lf_cum_col, lf_total, c_aug):
    l = pmat.shape[0]
    dv = v_aug.shape[1] - LANES
    n_wide = v_aug.shape[1] // LANES
    intra = jnp.dot(pmat, v_aug, preferred_element_type=F32)
    numden = intra + _wide(jnp.exp2(m_prev - m_rep), n_wide) * inter
    den = numden[:, dv:]
    floor = jnp.exp2(-(jnp.broadcast_to(lf_cum_col, (l, LANES)) + m_rep))
    h = numden[:, :dv] * _wide(1.0 / jnp.maximum(jnp.abs(den), floor), dv // LANES)
    kv = lax.dot_general(kw, v_aug, (((0,), (0,)), ((), ())), preferred_element_type=F32)
    c_new = _wide(jnp.exp2(m_prev - m_last), n_wide) * c_aug + kv
    return h, c_new, lf_total + m_last


def _mlstm_kernel(qf_ref, kf_ref, vf_ref, gf_ref, gtf_ref, qb_ref, kb_ref, vb_ref, gb_ref, gtb_ref,
                  hf_ref, hb_ref, c_ref, m_ref):
    s_idx = pl.program_id(1)

    @pl.when(s_idx == 0)
    def _():
        c_ref[...] = jnp.zeros_like(c_ref)
        m_ref[...] = jnp.zeros_like(m_ref)

    nb, l = qf_ref.shape[0], qf_ref.shape[1]
    nh = MLSTM_HEADS
    ones_col = jnp.ones((l, LANES), BF16)
    scans = []
    for bb in range(nb):
        for reverse, q_ref, k_ref, v_ref, g_ref, gt_ref, h_ref in (
                (False, qf_ref, kf_ref, vf_ref, gf_ref, gtf_ref, hf_ref),
                (True, qb_ref, kb_ref, vb_ref, gb_ref, gtb_ref, hb_ref)):
            base = 2 * nh if reverse else 0
            for h in range(nh):
                scans.append(dict(bb=bb, reverse=reverse, h=h, ca=base + h, cc=base + nh + h,
                                  hd=bb * 2 * nh + (nh if reverse else 0) + h,
                                  q_ref=q_ref, k_ref=k_ref, v_ref=v_ref, g_ref=g_ref, gt_ref=gt_ref, h_ref=h_ref))
    qk_cols = lambda sc: slice(sc["h"] * MLSTM_DQK, (sc["h"] + 1) * MLSTM_DQK)
    v_cols = lambda sc: slice(sc["h"] * MLSTM_DV, (sc["h"] + 1) * MLSTM_DV)
    for g0 in range(0, len(scans), MLSTM_GROUP):
        group = scans[g0:g0 + MLSTM_GROUP]
        for sc in group:
            sc["s"], sc["inter"] = _mlstm_scores(sc["q_ref"][sc["bb"], :, qk_cols(sc)],
                                                 sc["k_ref"][sc["bb"], :, qk_cols(sc)], c_ref[sc["hd"]])
        for sc in group:
            g = sc["g_ref"][sc["bb"]]
            g_t = sc["gt_ref"][sc["bb"], 0]
            m_prev = m_ref[sc["hd"]:sc["hd"] + 1, :]
            sc["pmat"], sc["kw"], sc["m_rep"], sc["m_last"] = _mlstm_decay(
                sc["reverse"], sc["s"], sc["k_ref"][sc["bb"], :, qk_cols(sc)],
                g[:, sc["ca"]:sc["ca"] + 1], g_t[sc["ca"]:sc["ca"] + 1, :], m_prev)
        for sc in group:
            g = sc["g_ref"][sc["bb"]]
            cc, hd = sc["cc"], sc["hd"]
            lf_total = g[0:1, cc:cc + 1] if sc["reverse"] else g[l - 1:l, cc:cc + 1]
            v_aug = jnp.concatenate([sc["v_ref"][sc["bb"], :, v_cols(sc)], ones_col], axis=1)
            hval, c_new, m_new = _mlstm_finish(sc["pmat"], sc["kw"], v_aug, sc["inter"], sc["m_rep"],
                                               sc["m_last"], m_ref[hd:hd + 1, :], g[:, cc:cc + 1], lf_total,
                                               c_ref[hd])
            sc["h_ref"][sc["bb"], :, v_cols(sc)] = hval.astype(sc["h_ref"].dtype)
            c_ref[hd] = c_new
            m_ref[hd:hd + 1, :] = m_new


def _mlstm(qk, pm, gates, gates_t, n_lat, n_ctx, v_region):
    b, t, _ = qk.shape
    l = CHUNK
    n_lat_c = n_lat // l
    n_ctx_c = n_ctx // l
    n_steps = n_lat_c + n_ctx_c
    qw = MLSTM_HEADS * MLSTM_DQK
    vw = MLSTM_HEADS * MLSTM_DV

    def fwd(si):
        return jnp.where(si < n_ctx_c, n_lat_c + si, si - n_ctx_c)

    def bwd(si):
        return jnp.where(si < n_ctx_c, n_steps - 1 - si, n_steps - 1 - si)

    nb = MLSTM_BATCH
    assert b % nb == 0
    in_specs = []
    for order in (fwd, bwd):
        in_specs += [
            pl.BlockSpec((nb, l, qw), lambda bi, si, o=order: (bi, o(si), 0)),
            pl.BlockSpec((nb, l, qw), lambda bi, si, o=order: (bi, o(si), 1)),
            pl.BlockSpec((nb, l, vw), lambda bi, si, o=order: (bi, o(si), v_region)),
            pl.BlockSpec((nb, l, LANES), lambda bi, si, o=order: (bi, o(si), 0)),
            pl.BlockSpec((nb, 1, 4 * MLSTM_HEADS, l), lambda bi, si, o=order: (bi, o(si), 0, 0)),
        ]
    out_specs = [pl.BlockSpec((nb, l, vw), lambda bi, si: (bi, fwd(si), 0)),
                 pl.BlockSpec((nb, l, vw), lambda bi, si: (bi, bwd(si), 0))]
    n_scans = nb * 2 * MLSTM_HEADS
    return pl.pallas_call(
        _mlstm_kernel,
        grid=(b // nb, n_steps),
        in_specs=in_specs,
        out_specs=out_specs,
        out_shape=[jax.ShapeDtypeStruct((b, t, vw), BF16)] * 2,
        scratch_shapes=[pltpu.VMEM((n_scans, MLSTM_DQK, MLSTM_DV + LANES), F32),
                        pltpu.VMEM((n_scans, LANES), F32)],
        compiler_params=_cparams(("parallel", "arbitrary")),
        name="mlstm",
    )(qk, qk, pm, gates, gates_t, qk, qk, pm, gates, gates_t)


def _attn_kernel(n_keys, tq, qt_ref, k_ref, vt_ref, lam_ref, gain_ref, o_ref, s_ref, pm_ref, e_ref):
    nq = qt_ref.shape[2] // tq
    width = 2 * tq
    hw = o_ref.shape[2]
    row = lax.broadcasted_iota(jnp.int32, (qt_ref.shape[1], tq), 0)
    first = (row % DIFF_HEAD_DIM) < (DIFF_HEAD_DIM // 2)

    def scores(qb):
        qt = qt_ref[0, :, pl.ds(pl.multiple_of(qb * tq, tq), tq)]
        zero = jnp.zeros_like(qt)
        qq = jnp.concatenate([jnp.where(first, qt, zero), jnp.where(first, zero, qt)], axis=1)
        s = jnp.dot(k_ref[0], qq, preferred_element_type=F32)
        s_ref[...] = s
        pm_ref[...] = jnp.max(s.reshape(n_keys // SUBLANES, SUBLANES, width), axis=0)

    def softmax():
        m = jnp.max(pm_ref[...], axis=0, keepdims=True)
        e_ref[...] = jnp.exp2(s_ref[...] - m).astype(BF16)

    def output(qb):
        lv = lam_ref[...]
        lam = (jnp.exp(jnp.sum(lv[0:1] * lv[1:2], axis=-1, keepdims=True))
               - jnp.exp(jnp.sum(lv[2:3] * lv[3:4], axis=-1, keepdims=True)) + LAM_INIT_L0)
        pv = jnp.dot(vt_ref[0], e_ref[...], preferred_element_type=F32)
        o_all = pv[0:hw, :] * (1.0 / pv[hw:hw + 1, :])
        o_t = o_all[:, :tq] - lam * o_all[:, tq:]
        ms = jnp.mean(o_t * o_t, axis=0, keepdims=True)
        out = (o_t * lax.rsqrt(ms + RMS_EPS)).T * (gain_ref[...] * (1.0 - LAM_INIT_L0))
        o_ref[0, pl.ds(pl.multiple_of(qb * tq, tq), tq), :] = out.astype(o_ref.dtype)

    scores(0)

    def iteration(t, carry):
        softmax()
        output(t - 1)
        scores(t)
        return carry

    lax.fori_loop(1, nq, iteration, 0)
    softmax()
    output(nq - 1)


def _attention(qt, kr, vt, lam_vecs, diff_gain, n_lat):
    b, t, _ = kr.shape
    tq = ATT_TQ
    hw = 2 * DIFF_HEAD_DIM
    nq = n_lat // tq
    assert nq >= 2
    return pl.pallas_call(
        functools.partial(_attn_kernel, t, tq),
        grid=(b, DIFF_HEADS),
        in_specs=[pl.BlockSpec((1, hw, n_lat), lambda bi, h: (bi, h, 0)),
                  pl.BlockSpec((1, t, hw), lambda bi, h: (bi, 0, h)),
                  pl.BlockSpec((1, V_ROWS, t), lambda bi, h: (bi, h, 0)),
                  pl.BlockSpec(lam_vecs.shape, lambda bi, h: (0, 0)),
                  pl.BlockSpec((1, hw), lambda bi, h: (0, 0))],
        out_specs=pl.BlockSpec((1, n_lat, hw), lambda bi, h: (bi, 0, h)),
        out_shape=jax.ShapeDtypeStruct((b, n_lat, DIFF_HEADS * hw), BF16),
        scratch_shapes=[pltpu.VMEM((t, 2 * tq), F32),
                        pltpu.VMEM((SUBLANES, 2 * tq), F32),
                        pltpu.VMEM((t, 2 * tq), BF16)],
        compiler_params=_cparams(("parallel", "arbitrary")),
        name="attn",
    )(qt, kr, vt, lam_vecs, diff_gain)


def _merge_kernel(d, hf_ref, hb_ref, mo_ref, ga_ref, gb_ref, hB_ref, x_ref, mod_ref, gain_ref,
                  wa_ref, wb_ref, wo_ref, o_ref):
    gain = gain_ref[...]
    yb = jnp.dot(hB_ref[0], wb_ref[...], preferred_element_type=F32)
    ya = None
    for h in range(MLSTM_HEADS):
        sl = slice(h * MLSTM_DV, (h + 1) * MLSTM_DV)
        hs = hf_ref[0, :, sl].astype(F32) + hb_ref[0, :, sl].astype(F32)
        ms = jnp.mean(hs * hs, axis=-1, keepdims=True)
        hn = hs * lax.rsqrt(ms + RMS_EPS) * gain[:, sl]
        h_a = (_sigmoid(mo_ref[0, :, sl].astype(F32)) * hn).astype(BF16)
        part = jnp.dot(h_a, wa_ref[sl, :], preferred_element_type=F32)
        ya = part if ya is None else ya + part
    y = _sigmoid(ga_ref[0].astype(F32)) * ya + _sigmoid(gb_ref[0].astype(F32)) * yb
    yo = jnp.dot(y.astype(BF16), wo_ref[...], preferred_element_type=F32)
    g1 = mod_ref[0][:, 2 * d:3 * d]
    o_ref[0] = x_ref[0] + g1 * yo


def _merge(hf, hb, p, h_b, x, mods3, mlstm_gain, wa, wb, wo, mo_region, ga_region, gb_region):
    b, s, d = x.shape
    tm = LAT_TILE
    wspec = lambda shape: pl.BlockSpec(shape, lambda bi, i: (0, 0), pipeline_mode=pl.Buffered(1))
    tok = lambda region: pl.BlockSpec((1, tm, d), lambda bi, i, r=region: (bi, i, r))
    return pl.pallas_call(
        functools.partial(_merge_kernel, d),
        grid=(b, s // tm),
        in_specs=[tok(0), tok(0), tok(mo_region), tok(ga_region), tok(gb_region), tok(0), tok(0),
                  pl.BlockSpec((1, 1, N_MOD * d), lambda bi, i: (bi, 0, 0)),
                  pl.BlockSpec((1, d), lambda bi, i: (0, 0)),
                  wspec(wa.shape), wspec(wb.shape), wspec(wo.shape)],
        out_specs=pl.BlockSpec((1, tm, d), lambda bi, i: (bi, i, 0)),
        out_shape=jax.ShapeDtypeStruct((b, s, d), F32),
        compiler_params=_cparams(("parallel", "arbitrary")),
        name="merge",
    )(hf, hb, p, p, p, h_b, x, mods3, mlstm_gain, wa, wb, wo)


def _ffn_kernel(d, hidden, x_ref, mod_ref, g_ref, wi_ref, wo_ref, o_ref):
    x1 = x_ref[0]
    mod = mod_ref[0]
    xn = _rms_mod(x1, g_ref[...], mod[:, 3 * d:4 * d], mod[:, 4 * d:5 * d]).astype(BF16)
    a = jnp.dot(xn, wi_ref[:, :hidden], preferred_element_type=F32)
    bb = jnp.dot(xn, wi_ref[:, hidden:], preferred_element_type=F32)
    hmid = (_silu(a) * bb).astype(BF16)
    y = jnp.dot(hmid, wo_ref[...], preferred_element_type=F32)
    o_ref[0] = x1 + mod[:, 5 * d:6 * d] * y


def _ffn(x1, mods3, norm2, wi, wo):
    b, s, d = x1.shape
    hidden = wo.shape[0]
    tm = LAT_TILE
    wspec = lambda shape: pl.BlockSpec(shape, lambda bi, i: (0, 0), pipeline_mode=pl.Buffered(1))
    return pl.pallas_call(
        functools.partial(_ffn_kernel, d, hidden),
        grid=(b, s // tm),
        in_specs=[pl.BlockSpec((1, tm, d), lambda bi, i: (bi, i, 0)),
                  pl.BlockSpec((1, 1, N_MOD * d), lambda bi, i: (bi, 0, 0)),
                  pl.BlockSpec((1, d), lambda bi, i: (0, 0)),
                  wspec(wi.shape), wspec(wo.shape)],
        out_specs=pl.BlockSpec((1, tm, d), lambda bi, i: (bi, i, 0)),
        out_shape=jax.ShapeDtypeStruct((b, s, d), F32),
        compiler_params=_cparams(("parallel", "arbitrary")),
        name="ffn",
    )(x1, mods3, norm2, wi, wo)


def _diff_lane_perm():
    l = np.arange(2 * DIFF_HEAD_DIM)
    half, r = l // DIFF_HEAD_DIM, l % DIFF_HEAD_DIM
    qsel, axis, f = r // 32, (r % 32) // ROPE_FREQS, r % ROPE_FREQS
    return qsel * DIFF_HEAD_DIM + axis * 2 * ROPE_FREQS + half * ROPE_FREQS + f, qsel, axis, f, half


def _rope_tables(n_lat, n_ctx):
    _, _, axis, f, half = _diff_lane_perm()
    pos = np.arange(n_lat)
    row, col = pos // GRID_W, pos % GRID_W
    inv = (ROPE_BASE ** (-np.arange(ROPE_FREQS, dtype=np.float32) / ROPE_FREQS)).astype(np.float32)
    ang = (np.where(axis[None, :] == 0, row[:, None], col[:, None]).astype(np.float32) * inv[f][None, :])
    cos = np.concatenate([np.cos(ang), np.ones((n_ctx, LANES))], axis=0)
    sin = np.concatenate([np.sin(ang) * np.where(half == 0, -1.0, 1.0)[None, :], np.zeros((n_ctx, LANES))], axis=0)
    return jnp.asarray(cos, F32), jnp.asarray(sin, F32)


def kernel(x, c, ctx, c_ctx, w_mod, b_mod, norm1, norm2, w_in, b_gate, conv_w, conv_b, mlstm_norm, q_norm, k_norm,
           lam_vecs, diff_norm, w_branch_a, w_branch_b, w_out, w_ffn_in, w_ffn_out):
    b, s, d = x.shape
    n_ctx = ctx.shape[1]
    assert w_mod.shape[0] == 1, "single-layer block"
    assert s % CHUNK == 0 and n_ctx % CHUNK == 0 and s % ATT_TQ == 0 and s % LAT_TILE == 0
    qk_w = MLSTM_HEADS * MLSTM_DQK
    v_w = MLSTM_HEADS * MLSTM_DV
    da_w = DIFF_HEADS * 2 * DIFF_HEAD_DIM
    n_gates = 4 * MLSTM_HEADS
    sizes = (qk_w, qk_w, v_w, v_w, n_gates, da_w, da_w, da_w, d, d)
    offs = np.concatenate([[0], np.cumsum(sizes)])
    seg = lambda i: w_in[0][:, offs[i]:offs[i + 1]]

    perm, qsel, _, _, _ = _diff_lane_perm()
    head_perm = (np.arange(DIFF_HEADS)[:, None] * 2 * DIFF_HEAD_DIM + perm[None, :]).reshape(-1)
    w_main = w_in[0][:, offs[0]:offs[4]].astype(BF16)
    w_gab = w_in[0][:, offs[8]:offs[10]].astype(BF16)
    segb = lambda i: seg(i).astype(BF16)
    w_att = jnp.concatenate([segb(5)[:, head_perm], segb(6)[:, head_perm], segb(7)], axis=1)
    R_V, R_O, R_GA, R_GB = range(4)
    wg_f32 = jnp.pad(seg(4), ((0, 0), (0, LANES - n_gates)))
    wg_hi = wg_f32.astype(BF16)
    wg = jnp.concatenate([wg_hi, (wg_f32 - wg_hi.astype(F32)).astype(BF16)], axis=1)
    gate_bias = jnp.pad(b_gate[0], (0, LANES - n_gates)).reshape(1, LANES)
    conv_w8 = jnp.pad(conv_w[0], ((0, SUBLANES - CONV_K), (0, 0)))
    gq = q_norm[0][perm % DIFF_HEAD_DIM].reshape(1, LANES)
    gk = k_norm[0][perm % DIFF_HEAD_DIM].reshape(1, LANES)
    unit = np.concatenate([qsel, 2 + qsel])
    grp = jnp.asarray(unit[:, None] == unit[None, :], BF16)
    cos_t, sin_t = _rope_tables(s, n_ctx)

    cc = jnp.concatenate([c, c_ctx[None, :], jnp.zeros((SUBLANES - b - 1, d), F32)], axis=0)
    mods = _modulation(cc, w_mod[0], b_mod[0])
    mods3 = mods.reshape(SUBLANES, 1, N_MOD * d)

    qk, pm, qt, kr, vt, gates, gates_t = _projection(x, ctx, mods3, norm1, (w_main, w_gab, w_att), wg, gate_bias,
                                                     cos_t, sin_t, gq, gk, grp, conv_w8, conv_b)
    hf, hb = _mlstm(qk, pm, gates, gates_t, s, n_ctx, R_V)
    h_b = _attention(qt, kr, vt, lam_vecs[0], diff_norm, s)
    x1 = _merge(hf, hb, pm, h_b, x, mods3, mlstm_norm, w_branch_a[0].astype(BF16), w_branch_b[0].astype(BF16),
                w_out[0].astype(BF16), R_O, R_GA, R_GB)
    return _ffn(x1, mods3, norm2, w_ffn_in[0].astype(BF16), w_ffn_out[0].astype(BF16))
```
